```python
import jax, jax.numpy as jnp
from jax import lax
import numpy as np

D_MODEL = 1024
BATCH = 2
SEQ = 8192
DEPTH = 2
DEC_BATCH = 32
DEC_SEQ = 8
PAST_LEN = 16384
PAGE_SIZE = 128

A_HEADS = 8
A_HEAD_DIM = D_MODEL // 16
A_WIDTH = A_HEADS * A_HEAD_DIM
DILATED_GROUPS = ((128, 1), (512, 4), (2048, 16))
MAX_WINDOW = max(w for w, _ in DILATED_GROUPS)
Q_BLOCK = 128
N_BUCKETS = 32
CONV_WIDTH = 31
CONV_CH = D_MODEL // 2
C_HEADS = 4
C_DK = D_MODEL // 2
C_DV = D_MODEL
C_DK_HEAD = C_DK // C_HEADS
C_DV_HEAD = C_DV // C_HEADS
GATE_RANK = 16
GATE_TAU = 16.0
GLA_CHUNK = 64
D_FF = 256 * (-(-(8 * D_MODEL // 3) // 256))
EPS = 1e-6
NEG_INF = -1e30

kernel_name = 'hybrid_dilated_conformer_gla_decoder_step'


def _rms_norm(x, g):
    xf = x.astype(jnp.float32)
    y = xf * lax.rsqrt(jnp.mean(xf * xf, axis=-1, keepdims=True) + EPS)
    return (y * g.astype(jnp.float32)).astype(x.dtype)


def _layer_norm(x, g, b):
    xf = x.astype(jnp.float32)
    xc = xf - jnp.mean(xf, axis=-1, keepdims=True)
    y = xc * lax.rsqrt(jnp.mean(xc * xc, axis=-1, keepdims=True) + EPS)
    return (y * g.astype(jnp.float32) + b.astype(jnp.float32)).astype(x.dtype)


def _swiglu(x, w_in, w_out):
    a, b = jnp.split(x @ w_in, 2, axis=-1)
    return (jax.nn.silu(a) * b) @ w_out


def _t5_bucket(dist):
    max_exact = N_BUCKETS // 2
    d = np.asarray(dist, dtype=np.int32)
    df = np.maximum(d, 1).astype(np.float32)
    large = max_exact + (np.log(df / max_exact) / np.log(MAX_WINDOW / max_exact)
                         * (N_BUCKETS - max_exact)).astype(np.int32)
    large = np.minimum(large, N_BUCKETS - 1)
    return np.where(d < max_exact, d, large).astype(np.int32)


def _dilated_core(q, k_slab, v_slab, q_pos, rel_bias):
    n_q = q.shape[1]
    outs, lses = [], []
    for window, dil in DILATED_GROUPS:
        dist = (np.arange(window // dil + 1) * dil).astype(np.int32)
        idx = MAX_WINDOW + np.arange(n_q)[:, None] - dist[None, :]
        k_g = k_slab[:, idx]
        v_g = v_slab[:, idx]
        s = jnp.einsum('bqhd,bqjhd->bhqj', q, k_g).astype(jnp.float32)
        bias = rel_bias[_t5_bucket(dist)].astype(jnp.float32)
        s = s + bias.T[None, :, None, :]
        valid = (q_pos[:, None] - dist[None, :]) >= 0
        s = jnp.where(valid[None, None], s, NEG_INF)
        lse = jax.nn.logsumexp(s, axis=-1)
        p = jnp.exp(s - lse[..., None]).astype(v_g.dtype)
        outs.append(jnp.einsum('bhqj,bqjhd->bqhd', p, v_g))
        lses.append(lse)
    mix = jax.nn.softmax(jnp.stack(lses), axis=0)
    out = jnp.einsum('gbhq,gbqhd->bqhd', mix, jnp.stack(outs).astype(jnp.float32))
    return out.astype(q.dtype)


def _dilated_prompt(q, k, v, rel_bias):
    b, t, h, dh = q.shape
    n_blk = t // Q_BLOCK
    pad = jnp.zeros((b, MAX_WINDOW, h, dh), k.dtype)
    k_pad = jnp.concatenate([pad, k], axis=1)
    v_pad = jnp.concatenate([pad, v], axis=1)
    q_blk = q.reshape(b, n_blk, Q_BLOCK, h, dh).transpose(1, 0, 2, 3, 4)

    def one_block(args):
        blk, qb = args
        start = blk * Q_BLOCK
        k_s = lax.dynamic_slice_in_dim(k_pad, start, MAX_WINDOW + Q_BLOCK, axis=1)
        v_s = lax.dynamic_slice_in_dim(v_pad, start, MAX_WINDOW + Q_BLOCK, axis=1)
        return _dilated_core(qb, k_s, v_s, start + jnp.arange(Q_BLOCK), rel_bias)

    out = lax.map(one_block, (jnp.arange(n_blk), q_blk))
    return out.transpose(1, 0, 2, 3, 4).reshape(b, t, h, dh)


def _conformer_conv(u_hist, u, conv_w, conv_b, ln_g, ln_b):
    ext = jnp.concatenate([u_hist.astype(u.dtype), u], axis=1)
    y = lax.conv_general_dilated(ext, conv_w[:, None, :].astype(ext.dtype), (1,), 'VALID',
                                 dimension_numbers=('NWC', 'WIO', 'NWC'),
                                 feature_group_count=CONV_CH) + conv_b
    y = jax.nn.silu(_layer_norm(y, ln_g, ln_b))
    return y, ext[:, -(CONV_WIDTH - 1):]


def _even_mixer(h, past, rel_bias, w_in, q_gain, k_gain, conv_w, conv_b, ln_g, ln_b, w_out):
    b, t, _ = h.shape
    q, k, v, glu_v, glu_g = jnp.split(h @ w_in, [A_WIDTH, 2 * A_WIDTH, 3 * A_WIDTH, 3 * A_WIDTH + CONV_CH], axis=-1)
    q = _rms_norm(q.reshape(b, t, A_HEADS, A_HEAD_DIM), q_gain) * (A_HEAD_DIM ** -0.5)
    k = _rms_norm(k.reshape(b, t, A_HEADS, A_HEAD_DIM), k_gain)
    v = v.reshape(b, t, A_HEADS, A_HEAD_DIM)
    u = glu_v * jax.nn.sigmoid(glu_g)
    if past is None:
        a = _dilated_prompt(q, k, v, rel_bias)
        keep = min(MAX_WINDOW, t)
        new_k, new_v = k[:, t - keep:], v[:, t - keep:]
        u_hist = jnp.zeros((b, CONV_WIDTH - 1, CONV_CH), u.dtype)
    else:
        k_buf, v_buf, u_hist = past
        n_buf = k_buf.shape[1]
        pad = jnp.zeros((b, MAX_WINDOW - n_buf, A_HEADS, A_HEAD_DIM), k.dtype)
        k_all = jnp.concatenate([k_buf.astype(k.dtype), k], axis=1)
        v_all = jnp.concatenate([v_buf.astype(v.dtype), v], axis=1)
        a = _dilated_core(q, jnp.concatenate([pad, k_all], axis=1), jnp.concatenate([pad, v_all], axis=1),
                          PAST_LEN + jnp.arange(t), rel_bias)
        new_k, new_v = k_all[:, t:], v_all[:, t:]
    c, new_u = _conformer_conv(u_hist, u, conv_w, conv_b, ln_g, ln_b)
    y = jnp.concatenate([a.reshape(b, t, A_WIDTH), c], axis=-1) @ w_out
    return y, new_k, new_v, new_u


def _gla_scan(q, k, v, log_a, s0):
    b, t, h, _ = q.shape
    dv = v.shape[-1]
    c = min(GLA_CHUNK, t)
    n = -(-t // c)
    pad = n * c - t

    def blocks(z):
        z = jnp.pad(z, ((0, 0), (0, pad), (0, 0), (0, 0)))
        return z.reshape(b, n, c, h, z.shape[-1]).transpose(1, 0, 3, 2, 4)

    causal = np.tril(np.ones((c, c), dtype=bool))

    def step(state, xs):
        qc, kc, vc, lac = xs
        qf, kf, vf = qc.astype(jnp.float32), kc.astype(jnp.float32), vc.astype(jnp.float32)
        cum = jnp.cumsum(lac, axis=2)
        rel = jnp.where(causal[None, None, :, :, None], cum[:, :, :, None, :] - cum[:, :, None, :, :], -jnp.inf)
        att = jnp.einsum('bhtk,bhsk,bhtsk->bhts', qf, kf, jnp.exp(rel))
        sf = state.astype(jnp.float32)
        o = jnp.einsum('bhts,bhsv->bhtv', att, vf) + jnp.einsum('bhtk,bhkv->bhtv', qf * jnp.exp(cum), sf)
        last = cum[:, :, -1]
        new = jnp.exp(last)[..., None] * sf + jnp.einsum('bhsk,bhsv->bhkv', kf * jnp.exp(last[:, :, None] - cum), vf)
        return new.astype(state.dtype), o.astype(vc.dtype)

    s, o = lax.scan(step, s0, (blocks(q), blocks(k), blocks(v), blocks(log_a)))
    o = o.transpose(1, 0, 3, 2, 4).reshape(b, n * c, h, dv)[:, :t]
    return o, s


def _gla_mixer(h, s0, w_in, gate_w_up, gate_b, o_gain, w_out):
    b, t, _ = h.shape
    q, k, v, r, g_low = jnp.split(h @ w_in, [C_DK, 2 * C_DK, 2 * C_DK + C_DV, 2 * C_DK + 2 * C_DV], axis=-1)
    q = q.reshape(b, t, C_HEADS, C_DK_HEAD) * (C_DK_HEAD ** -0.5)
    k = k.reshape(b, t, C_HEADS, C_DK_HEAD)
    v = v.reshape(b, t, C_HEADS, C_DV_HEAD)
    log_a = jax.nn.log_sigmoid((g_low @ gate_w_up + gate_b).astype(jnp.float32)) / GATE_TAU
    log_a = log_a.reshape(b, t, C_HEADS, C_DK_HEAD)
    o, s = _gla_scan(q, k, v, log_a, s0)
    o = _rms_norm(o, o_gain) * jax.nn.silu(r).reshape(b, t, C_HEADS, C_DV_HEAD)
    return o.reshape(b, t, C_DV) @ w_out, s


def _trunk(x, past, P):
    new_k, new_v, new_u, new_s = [], [], [], []
    for layer in range(DEPTH):
        i = layer // 2
        x = x + 0.5 * _swiglu(_rms_norm(x, P['norm_ffn1'][layer]), P['ffn1_w_in'][layer], P['ffn1_w_out'][layer])
        h = _rms_norm(x, P['norm_mix'][layer])
        if layer % 2 == 0:
            layer_past = None if past is None else (past[0][i], past[1][i], past[2][i])
            y, k_i, v_i, u_i = _even_mixer(h, layer_past, P['rel_bias'], P['ev_w_in'][i], P['ev_q_gain'][i],
                                           P['ev_k_gain'][i], P['ev_conv_w'][i], P['ev_conv_b'][i],
                                           P['ev_conv_ln_g'][i], P['ev_conv_ln_b'][i], P['ev_w_out'][i])
            new_k.append(k_i)
            new_v.append(v_i)
            new_u.append(u_i)
        else:
            if past is None:
                s0 = jnp.zeros((h.shape[0], C_HEADS, C_DK_HEAD, C_DV_HEAD), h.dtype)
            else:
                s0 = past[3][i]
            y, s_i = _gla_mixer(h, s0, P['od_w_in'][i], P['od_gate_w_up'][i], P['od_gate_b'][i],
                                P['od_o_gain'][i], P['od_w_out'][i])
            new_s.append(s_i)
        x = x + y
        x = x + 0.5 * _swiglu(_rms_norm(x, P['norm_ffn2'][layer]), P['ffn2_w_in'][layer], P['ffn2_w_out'][layer])
    return x, jnp.stack(new_k), jnp.stack(new_v), jnp.stack(new_u), jnp.stack(new_s)


def setup_inputs(seed: int = 0) -> dict:
    key = jax.random.key(seed)
    keys = iter(jax.random.split(key, 32))

    def nrm(shape, scale):
        return scale * jax.random.normal(next(keys), shape, jnp.float32)

    n_even = (DEPTH + 1) // 2
    n_odd = DEPTH // 2
    n_buf = min(MAX_WINDOW, PAST_LEN)
    in_even = 3 * A_WIDTH + 2 * CONV_CH
    in_odd = 2 * C_DK + 2 * C_DV + GATE_RANK
    return {
        'x_prompt': nrm((BATCH, SEQ, D_MODEL), 1.0),
        'x_sample': nrm((DEC_BATCH, DEC_SEQ, D_MODEL), 1.0),
        'cache_k': nrm((n_even, DEC_BATCH, n_buf, A_HEADS, A_HEAD_DIM), 1.0),
        'cache_v': nrm((n_even, DEC_BATCH, n_buf, A_HEADS, A_HEAD_DIM), 1.0),
        'cache_conv': nrm((n_even, DEC_BATCH, CONV_WIDTH - 1, CONV_CH), 0.5),
        'state_gla': nrm((n_odd, DEC_BATCH, C_HEADS, C_DK_HEAD, C_DV_HEAD), 0.5),
        'rel_bias': nrm((N_BUCKETS, A_HEADS), 0.5),
        'norm_ffn1': 1.0 + nrm((DEPTH, D_MODEL), 0.05),
        'ffn1_w_in': nrm((DEPTH, D_MODEL, 2 * D_FF), D_MODEL ** -0.5),
        'ffn1_w_out': nrm((DEPTH, D_FF, D_MODEL), D_FF ** -0.5),
        'norm_mix': 1.0 + nrm((DEPTH, D_MODEL), 0.05),
        'norm_ffn2': 1.0 + nrm((DEPTH, D_MODEL), 0.05),
        'ffn2_w_in': nrm((DEPTH, D_MODEL, 2 * D_FF), D_MODEL ** -0.5),
        'ffn2_w_out': nrm((DEPTH, D_FF, D_MODEL), D_FF ** -0.5),
        'ev_w_in': nrm((n_even, D_MODEL, in_even), D_MODEL ** -0.5),
        'ev_q_gain': 1.0 + nrm((n_even, A_HEAD_DIM), 0.05),
        'ev_k_gain': 1.0 + nrm((n_even, A_HEAD_DIM), 0.05),
        'ev_conv_w': nrm((n_even, CONV_WIDTH, CONV_CH), CONV_WIDTH ** -0.5),
        'ev_conv_b': nrm((n_even, CONV_CH), 0.02),
        'ev_conv_ln_g': 1.0 + nrm((n_even, CONV_CH), 0.05),
        'ev_conv_ln_b': nrm((n_even, CONV_CH), 0.02),
        'ev_w_out': nrm((n_even, A_WIDTH + CONV_CH, D_MODEL), (A_WIDTH + CONV_CH) ** -0.5),
        'od_w_in': nrm((n_odd, D_MODEL, in_odd), D_MODEL ** -0.5),
        'od_gate_w_up': nrm((n_odd, GATE_RANK, C_DK), GATE_RANK ** -0.5),
        'od_gate_b': nrm((n_odd, C_DK), 0.1),
        'od_o_gain': 1.0 + nrm((n_odd, C_DV_HEAD), 0.05),
        'od_w_out': nrm((n_odd, C_DV, D_MODEL), C_DV ** -0.5),
    }


def reference(x_prompt, x_sample, cache_k, cache_v, cache_conv, state_gla, rel_bias,
              norm_ffn1, ffn1_w_in, ffn1_w_out, norm_mix, norm_ffn2, ffn2_w_in, ffn2_w_out,
              ev_w_in, ev_q_gain, ev_k_gain, ev_conv_w, ev_conv_b, ev_conv_ln_g, ev_conv_ln_b, ev_w_out,
              od_w_in, od_gate_w_up, od_gate_b, od_o_gain, od_w_out):
    params = {
        'rel_bias': rel_bias,
        'norm_ffn1': norm_ffn1, 'ffn1_w_in': ffn1_w_in, 'ffn1_w_out': ffn1_w_out,
        'norm_mix': norm_mix,
        'norm_ffn2': norm_ffn2, 'ffn2_w_in': ffn2_w_in, 'ffn2_w_out': ffn2_w_out,
        'ev_w_in': ev_w_in, 'ev_q_gain': ev_q_gain, 'ev_k_gain': ev_k_gain,
        'ev_conv_w': ev_conv_w, 'ev_conv_b': ev_conv_b, 'ev_conv_ln_g': ev_conv_ln_g,
        'ev_conv_ln_b': ev_conv_ln_b, 'ev_w_out': ev_w_out,
        'od_w_in': od_w_in, 'od_gate_w_up': od_gate_w_up, 'od_gate_b': od_gate_b,
        'od_o_gain': od_o_gain, 'od_w_out': od_w_out,
    }
    y_prompt, new_k_prompt, new_v_prompt, new_conv_prompt, new_gla_prompt = _trunk(x_prompt, None, params)
    y_sample, new_k_sample, new_v_sample, new_conv_sample, new_gla_sample = _trunk(
        x_sample, (cache_k, cache_v, cache_conv, state_gla), params)
    return (y_prompt, y_sample, new_k_prompt, new_v_prompt, new_conv_prompt, new_gla_prompt,
            new_k_sample, new_v_sample, new_conv_sample, new_gla_sample)
```

```python
import functools

import numpy as np
import jax
import jax.numpy as jnp
from jax import lax
from jax.experimental import pallas as pl
from jax.experimental.pallas import tpu as pltpu

F32 = jnp.float32
BF16 = jnp.bfloat16

D_MODEL = 1024
DEPTH = 2
PAST_LEN = 16384
A_HEADS = 8
A_HEAD_DIM = 64
A_WIDTH = A_HEADS * A_HEAD_DIM
DILATED_GROUPS = ((128, 1), (512, 4), (2048, 16))
MAX_WINDOW = 2048
N_BUCKETS = 32
CONV_WIDTH = 31
CONV_CH = 512
C_HEADS = 4
C_DK = 512
C_DV = 1024
C_DK_HEAD = 128
C_DV_HEAD = 256
GATE_RANK = 16
GATE_TAU = 16.0
D_FF = 2816
EPS = 1e-6
NEG_INF = -1e30

LANES = 128
WIN_KEYS = 128
ATT_BLOCK = 2048
VMEM_LIMIT = 56 * 1024 * 1024


def _params(*sem):
    return pltpu.CompilerParams(dimension_semantics=sem, vmem_limit_bytes=VMEM_LIMIT)


def _dot(a, b):
    return jnp.dot(a, b, preferred_element_type=F32)


def _dot_nt(a, b):
    return lax.dot_general(a, b, (((1,), (1,)), ((), ())), preferred_element_type=F32)


def _dot_tn(a, b):
    return lax.dot_general(a, b, (((0,), (0,)), ((), ())), preferred_element_type=F32)


def _rms_rows(x, g):
    y = x * lax.rsqrt(jnp.mean(x * x, axis=-1, keepdims=True) + EPS)
    return y * g


def _sigmoid(x):
    return 1.0 / (1.0 + jnp.exp(-x))


def _full(shape):
    return pl.BlockSpec(shape, lambda *_: (0,) * len(shape))


FF_CHUNK = 256


def _ffn_kernel(x_ref, g_ref, wi_ref, wo_ref, o_ref, act_ref):
    x = x_ref[...]
    h = _rms_rows(x, g_ref[...]).astype(BF16)
    for c in range(D_FF // FF_CHUNK):
        lo = c * FF_CHUNK
        a = _dot(h, wi_ref[:, lo:lo + FF_CHUNK])
        b = _dot(h, wi_ref[:, D_FF + lo:D_FF + lo + FF_CHUNK])
        act_ref[:, lo:lo + FF_CHUNK] = (a * _sigmoid(a) * b).astype(BF16)
    o_ref[...] = x + 0.5 * _dot(act_ref[...], wo_ref[...])


def _ffn(x, g, wi, wo, tm):
    n = x.shape[0]
    return pl.pallas_call(
        _ffn_kernel,
        grid=(n // tm,),
        in_specs=[pl.BlockSpec((tm, D_MODEL), lambda i: (i, 0)),
                  _full((1, D_MODEL)),
                  _full((D_MODEL, 2 * D_FF)),
                  _full((D_FF, D_MODEL))],
        out_specs=pl.BlockSpec((tm, D_MODEL), lambda i: (i, 0)),
        out_shape=jax.ShapeDtypeStruct((n, D_MODEL), F32),
        scratch_shapes=[pltpu.VMEM((tm, D_FF), BF16)],
        compiler_params=_params("parallel"),
        name="ffn",
    )(x, g, wi, wo)


def _outproj_kernel(n_in, x_ref, *refs):
    o_ref = refs[2 * n_in]
    acc = x_ref[...]
    for i in range(n_in):
        acc = acc + _dot(refs[i][...], refs[n_in + i][...])
    o_ref[...] = acc


def _outproj(x, ms, ws, tm):
    n = x.shape[0]
    in_specs = [pl.BlockSpec((tm, D_MODEL), lambda i: (i, 0))]
    in_specs += [pl.BlockSpec((tm, m.shape[1]), lambda i: (i, 0)) for m in ms]
    in_specs += [_full(w.shape) for w in ws]
    return pl.pallas_call(
        functools.partial(_outproj_kernel, len(ms)),
        grid=(n // tm,),
        in_specs=in_specs,
        out_specs=pl.BlockSpec((tm, D_MODEL), lambda i: (i, 0)),
        out_shape=jax.ShapeDtypeStruct((n, D_MODEL), F32),
        compiler_params=_params("parallel"),
        name="outproj",
    )(x, *ms, *ws)


def _head_norm(z, gain, bd):
    zz = z * z
    hi = zz.astype(BF16)
    lo = (zz - hi.astype(F32)).astype(BF16)
    ss = _dot(hi, bd) + _dot(lo, bd)
    return z * lax.rsqrt(ss * (1.0 / A_HEAD_DIM) + EPS) * gain


def _even_in_kernel(x_ref, g_ref, w_ref, qg_ref, kg_ref, bd_ref, q_ref, k_ref, v_ref, u_ref):
    h = _rms_rows(x_ref[...], g_ref[...]).astype(BF16)
    p = _dot(h, w_ref[...])
    bd = bd_ref[...]
    q_ref[...] = _head_norm(p[:, 0:A_WIDTH], qg_ref[...], bd) * (A_HEAD_DIM ** -0.5)
    k_ref[...] = _head_norm(p[:, A_WIDTH:2 * A_WIDTH], kg_ref[...], bd)
    v_ref[...] = p[:, 2 * A_WIDTH:3 * A_WIDTH]
    gv = p[:, 3 * A_WIDTH:3 * A_WIDTH + CONV_CH]
    gg = p[:, 3 * A_WIDTH + CONV_CH:3 * A_WIDTH + 2 * CONV_CH]
    u_ref[...] = gv * _sigmoid(gg)


def _even_in(x, g, w, qg, kg, bd, tm):
    n = x.shape[0]
    row = lambda width: pl.BlockSpec((tm, width), lambda i: (i, 0))
    out = jax.ShapeDtypeStruct((n, A_WIDTH), F32)
    return pl.pallas_call(
        _even_in_kernel,
        grid=(n // tm,),
        in_specs=[row(D_MODEL), _full((1, D_MODEL)), _full(w.shape), _full((1, A_WIDTH)),
                  _full((1, A_WIDTH)), _full((A_WIDTH, A_WIDTH))],
        out_specs=[row(A_WIDTH)] * 4,
        out_shape=[out] * 4,
        compiler_params=_params("parallel"),
        name="even_in",
    )(x, g, w, qg, kg, bd)


def _t5_bucket(dist):
    max_exact = N_BUCKETS // 2
    d = np.asarray(dist, dtype=np.int32)
    df = np.maximum(d, 1).astype(np.float32)
    large = max_exact + (np.log(df / max_exact) / np.log(MAX_WINDOW / max_exact)
                         * (N_BUCKETS - max_exact)).astype(np.int32)
    large = np.minimum(large, N_BUCKETS - 1)
    return np.where(d < max_exact, d, large).astype(np.int32)


def _band_tables(rel_bias):
    i = np.arange(WIN_KEYS)[:, None]
    m = np.arange(2 * WIN_KEYS)[None, :]
    j = i + WIN_KEYS - m
    inside = (j >= 0) & (j <= WIN_KEYS)
    jc = np.clip(j, 0, WIN_KEYS)
    tabs = []
    for first in (False, True):
        valid = inside & (m >= WIN_KEYS) if first else inside
        for _, dil in DILATED_GROUPS:
            bias = rel_bias[_t5_bucket(jc * dil)]
            t = jnp.where(valid[:, :, None], bias, NEG_INF)
            t = t.transpose(2, 0, 1).reshape(A_HEADS // 2, 2 * WIN_KEYS, 2 * WIN_KEYS)
            tabs.append(t)
    return jnp.stack(tabs).astype(F32)


def _attn_prompt_kernel(q_ref, kp_ref, kc_ref, vp_ref, vc_ref, tab_ref, o_ref,
                        kbuf, vbuf, og, lg):
    blk = pl.program_id(2)
    kbuf[0:ATT_BLOCK, :] = kp_ref[...]
    kbuf[ATT_BLOCK:2 * ATT_BLOCK, :] = kc_ref[...]
    vbuf[0:ATT_BLOCK, :] = vp_ref[...]
    vbuf[ATT_BLOCK:2 * ATT_BLOCK, :] = vc_ref[...]
    first = (blk == 0).astype(jnp.int32)
    lane = lax.broadcasted_iota(jnp.int32, (WIN_KEYS, LANES), 1)
    low = lane < A_HEAD_DIM
    ones = jnp.ones((2 * WIN_KEYS, LANES), BF16)

    def block(g, dil, start, tab):
        qs = q_ref[pl.ds(start, WIN_KEYS, stride=dil), :].astype(BF16)
        key_rows = pl.ds(ATT_BLOCK + start - WIN_KEYS * dil, 2 * WIN_KEYS, stride=dil)
        kk = kbuf[key_rows, :].astype(BF16)
        vv = vbuf[key_rows, :].astype(BF16)
        zero = jnp.zeros_like(qs)
        qst = jnp.concatenate([jnp.where(low, qs, zero), jnp.where(low, zero, qs)], axis=0)
        s = _dot_nt(qst, kk) + tab
        mx = jnp.max(s, axis=-1, keepdims=True)
        p = jnp.exp(s - mx).astype(BF16)
        r = _dot(p, jnp.concatenate([vv, ones], axis=1))
        o2 = jnp.where(low, r[0:WIN_KEYS, 0:LANES], r[WIN_KEYS:, 0:LANES])
        l2 = jnp.where(low, r[0:WIN_KEYS, LANES:], r[WIN_KEYS:, LANES:])
        m2 = jnp.where(low, jnp.broadcast_to(mx[0:WIN_KEYS], (WIN_KEYS, LANES)),
                       jnp.broadcast_to(mx[WIN_KEYS:], (WIN_KEYS, LANES)))
        rows = pl.ds(start, WIN_KEYS, stride=dil)
        og[g, rows, :] = o2 / l2
        lg[g, rows, :] = m2 + jnp.log(l2)

    for g, (window, dil) in enumerate(DILATED_GROUPS):
        n_sub = ATT_BLOCK // window

        def body(pb, carry, g=g, window=window, dil=dil):
            sub = pb // dil
            start = sub * window + (pb - sub * dil)
            head = jnp.logical_and(sub == 0, first == 1)
            tab = tab_ref[jnp.where(head, 3 + g, g), 0]
            block(g, dil, start, tab)
            return carry

        lax.fori_loop(0, n_sub * dil, body, 0)

    l0, l1, l2 = lg[0], lg[1], lg[2]
    mx = jnp.maximum(jnp.maximum(l0, l1), l2)
    w0, w1, w2 = jnp.exp(l0 - mx), jnp.exp(l1 - mx), jnp.exp(l2 - mx)
    o_ref[...] = ((w0 * og[0] + w1 * og[1] + w2 * og[2]) / (w0 + w1 + w2)).astype(o_ref.dtype)


def _attn_prompt(q, k, v, tabs, batch, seq):
    nb = seq // ATT_BLOCK
    cur = lambda b, p, t: (b * nb + t, p)
    prev = lambda b, p, t: (b * nb + jnp.maximum(t - 1, 0), p)
    blk = lambda imap: pl.BlockSpec((ATT_BLOCK, LANES), imap)
    return pl.pallas_call(
        _attn_prompt_kernel,
        grid=(batch, A_HEADS // 2, nb),
        in_specs=[blk(cur), blk(prev), blk(cur), blk(prev), blk(cur),
                  pl.BlockSpec((6, 1, 2 * WIN_KEYS, 2 * WIN_KEYS), lambda b, p, t: (0, p, 0, 0))],
        out_specs=blk(cur),
        out_shape=jax.ShapeDtypeStruct((batch * seq, A_WIDTH), BF16),
        scratch_shapes=[pltpu.VMEM((2 * ATT_BLOCK, LANES), F32),
                        pltpu.VMEM((2 * ATT_BLOCK, LANES), F32),
                        pltpu.VMEM((3, ATT_BLOCK, LANES), F32),
                        pltpu.VMEM((3, ATT_BLOCK, LANES), F32)],
        compiler_params=_params("parallel", "parallel", "arbitrary"),
        name="attn_prompt",
    )(q, k, k, v, v, tabs)


DECODE_TAIL = LANES


def _decode_tables(rel_bias, n_new):
    n_buf = MAX_WINDOW
    i = np.arange(n_new)[:, None]
    col = np.arange(n_buf + DECODE_TAIL)[None, :]
    dist = n_buf + i - col
    live = col < n_buf + n_new
    cnt = np.zeros(dist.shape, np.float32)
    for window, dil in DILATED_GROUPS:
        cnt += (live & (dist >= 0) & (dist <= window) & (dist % dil == 0)).astype(np.float32)
    bias = rel_bias[_t5_bucket(np.clip(dist, 0, MAX_WINDOW))]
    bias = jnp.where((cnt > 0)[:, :, None], bias, NEG_INF)
    bias = bias.transpose(2, 0, 1).reshape(A_HEADS * n_new, dist.shape[1])
    return bias.astype(F32), jnp.asarray(np.tile(cnt, (A_HEADS, 1)))


def _attn_sample_kernel(n_new, q_ref, kn_ref, vn_ref, kc_ref, vc_ref, bias_ref, cnt_ref,
                        o_ref, ko_ref, vo_ref, kall, vall):
    n_buf = MAX_WINDOW
    rows = A_HEADS * n_new
    kc = kc_ref[0]
    vc = vc_ref[0]
    kn = kn_ref[0]
    vn = vn_ref[0]
    ko_ref[0, 0:n_buf - n_new, :] = kc[n_new:, :]
    ko_ref[0, n_buf - n_new:, :] = kn
    vo_ref[0, 0:n_buf - n_new, :] = vc[n_new:, :]
    vo_ref[0, n_buf - n_new:, :] = vn
    unused = jnp.zeros((DECODE_TAIL - n_new, A_WIDTH), F32)
    kall[0:n_buf, :] = kc.astype(BF16)
    kall[n_buf:, :] = jnp.concatenate([kn, unused], axis=0).astype(BF16)
    vall[0:n_buf, :] = vc.astype(BF16)
    vall[n_buf:, :] = jnp.concatenate([vn, unused], axis=0).astype(BF16)

    q = q_ref[0]
    row_head = lax.broadcasted_iota(jnp.int32, (A_HEADS, n_new, A_WIDTH), 0).reshape(rows, A_WIDTH)
    lane = lax.broadcasted_iota(jnp.int32, (rows, A_WIDTH), 1)
    own = jnp.logical_and(lane >= row_head * A_HEAD_DIM, lane < (row_head + 1) * A_HEAD_DIM)
    qblk = jnp.where(own, jnp.concatenate([q] * A_HEADS, axis=0), 0.0).astype(BF16)
    s = _dot_nt(qblk, kall[...]) + bias_ref[...]
    mx = jnp.max(s, axis=-1, keepdims=True)
    p = cnt_ref[...] * jnp.exp(s - mx)
    den = jnp.sum(p, axis=-1, keepdims=True)
    acc = jnp.where(own, _dot(p.astype(BF16), vall[...]) / den, 0.0)
    out = acc[0:n_new]
    for h in range(1, A_HEADS):
        out = out + acc[h * n_new:(h + 1) * n_new]
    o_ref[0] = out.astype(o_ref.dtype)


def _attn_sample(q, k_new, v_new, cache_k, cache_v, tables):
    b, n_new, _ = q.shape
    n_buf = cache_k.shape[1]
    rows = A_HEADS * n_new
    new = pl.BlockSpec((1, n_new, A_WIDTH), lambda i: (i, 0, 0))
    buf = pl.BlockSpec((1, n_buf, A_WIDTH), lambda i: (i, 0, 0))
    tab = _full((rows, n_buf + DECODE_TAIL))
    return pl.pallas_call(
        functools.partial(_attn_sample_kernel, n_new),
        grid=(b,),
        in_specs=[new, new, new, buf, buf, tab, tab],
        out_specs=[new, buf, buf],
        out_shape=[jax.ShapeDtypeStruct((b, n_new, A_WIDTH), BF16),
                   jax.ShapeDtypeStruct((b, n_buf, A_WIDTH), F32),
                   jax.ShapeDtypeStruct((b, n_buf, A_WIDTH), F32)],
        scratch_shapes=[pltpu.VMEM((n_buf + DECODE_TAIL, A_WIDTH), BF16),
                        pltpu.VMEM((n_buf + DECODE_TAIL, A_WIDTH), BF16)],
        compiler_params=_params("parallel"),
        name="attn_sample",
    )(q, k_new, v_new, cache_k, cache_v, *tables)


CONV_PAD = 32


def _conv_kernel(tc, u_ref, up_ref, hist_ref, w_ref, b_ref, g_ref, beta_ref, o_ref, win):
    t = pl.program_id(1)
    win[CONV_PAD:CONV_PAD + tc, :] = u_ref[0]

    @pl.when(t == 0)
    def _():
        win[0:CONV_PAD, :] = hist_ref[0]

    @pl.when(t > 0)
    def _():
        win[0:CONV_PAD, :] = up_ref[0]

    off = CONV_PAD - (CONV_WIDTH - 1)
    acc = jnp.zeros((tc, CONV_CH), F32) + b_ref[...]
    for k in range(CONV_WIDTH):
        acc = acc + win[off + k:off + k + tc, :] * w_ref[k:k + 1, :]
    xc = acc - jnp.mean(acc, axis=-1, keepdims=True)
    y = xc * lax.rsqrt(jnp.mean(xc * xc, axis=-1, keepdims=True) + EPS)
    y = y * g_ref[...] + beta_ref[...]
    o_ref[0] = (y * _sigmoid(y)).astype(o_ref.dtype)


def _conv(u, hist, w, b, g, beta, tc):
    bsz, t, _ = u.shape
    per = tc // CONV_PAD
    if t >= CONV_PAD:
        prev = pl.BlockSpec((1, CONV_PAD, CONV_CH), lambda i, j: (i, jnp.maximum(j * per - 1, 0), 0))
        u_prev = u
    else:
        prev = pl.BlockSpec((1, CONV_PAD, CONV_CH), lambda i, j: (i, 0, 0))
        u_prev = hist
    return pl.pallas_call(
        functools.partial(_conv_kernel, tc),
        grid=(bsz, t // tc),
        in_specs=[pl.BlockSpec((1, tc, CONV_CH), lambda i, j: (i, j, 0)),
                  prev,
                  pl.BlockSpec((1, CONV_PAD, CONV_CH), lambda i, j: (i, 0, 0)),
                  _full((CONV_PAD, CONV_CH)), _full((1, CONV_CH)), _full((1, CONV_CH)),
                  _full((1, CONV_CH))],
        out_specs=pl.BlockSpec((1, tc, CONV_CH), lambda i, j: (i, j, 0)),
        out_shape=jax.ShapeDtypeStruct((bsz, t, CONV_CH), BF16),
        scratch_shapes=[pltpu.VMEM((CONV_PAD + tc, CONV_CH), F32)],
        compiler_params=_params("parallel", "arbitrary"),
        name="conv",
    )(u, u_prev, hist, w, b, g, beta)


def _gla_in_kernel(x_ref, g_ref, w_ref, wl_ref, wu_ref, bu_ref, q_ref, k_ref, v_ref, r_ref, la_ref):
    h = _rms_rows(x_ref[...], g_ref[...]).astype(BF16)
    p = _dot(h, w_ref[...])
    q_ref[...] = p[:, 0:C_DK] * (C_DK_HEAD ** -0.5)
    k_ref[...] = p[:, C_DK:2 * C_DK]
    v_ref[...] = p[:, 2 * C_DK:2 * C_DK + C_DV]
    r = p[:, 2 * C_DK + C_DV:2 * C_DK + 2 * C_DV]
    r_ref[...] = r * _sigmoid(r)
    low = _dot(h, wl_ref[...]).astype(BF16)
    z = _dot(low, wu_ref[...]) + bu_ref[...]
    log_sig = jnp.minimum(z, 0.0) - jnp.log1p(jnp.exp(-jnp.abs(z)))
    la_ref[...] = log_sig * (1.0 / GATE_TAU)


def _gla_in(x, g, w, wl, wu, bu, tm):
    n = x.shape[0]
    row = lambda width: pl.BlockSpec((tm, width), lambda i: (i, 0))
    sds = lambda width: jax.ShapeDtypeStruct((n, width), F32)
    return pl.pallas_call(
        _gla_in_kernel,
        grid=(n // tm,),
        in_specs=[row(D_MODEL), _full((1, D_MODEL)), _full(w.shape), _full(wl.shape),
                  _full(wu.shape), _full((1, C_DK))],
        out_specs=[row(C_DK), row(C_DK), row(C_DV), row(C_DV), row(C_DK)],
        out_shape=[sds(C_DK), sds(C_DK), sds(C_DV), sds(C_DV), sds(C_DK)],
        compiler_params=_params("parallel"),
        name="gla_in",
    )(x, g, w, wl, wu, bu)


def _gla_kernel(chunk, n_chunks, q_ref, k_ref, v_ref, r_ref, la_ref, s0_ref, gain_ref,
                o_ref, s_ref, st_ref):
    t = pl.program_id(1)

    @pl.when(t == 0)
    def _():
        for h in range(C_HEADS):
            st_ref[h] = s0_ref[0, h].T

    ri = lax.broadcasted_iota(jnp.int32, (chunk, chunk), 0)
    ci = lax.broadcasted_iota(jnp.int32, (chunk, chunk), 1)
    causal = ci <= ri
    tri = jnp.where(causal, 1.0, 0.0).astype(BF16)
    gain = gain_ref[...]

    def body(c, carry):
        rows = pl.ds(pl.multiple_of(c * chunk, chunk), chunk)
        la = la_ref[0, rows, :]
        la_hi = la.astype(BF16)
        rem = la - la_hi.astype(F32)
        la_mid = rem.astype(BF16)
        la_lo = (rem - la_mid.astype(F32)).astype(BF16)
        cum = _dot(tri, la_hi) + _dot(tri, la_mid) + _dot(tri, la_lo)
        mid = cum[chunk // 2 - 1:chunk // 2, :]
        last = cum[chunk - 1:chunk, :]
        eq = jnp.exp(cum - mid)
        ek = jnp.exp(mid - cum)
        q = q_ref[0, rows, :]
        k = k_ref[0, rows, :]
        qe = q * eq
        ke = k * ek
        q_in = qe.astype(BF16)
        k_in = ke.astype(BF16)
        q_st = (qe * jnp.exp(mid)).astype(BF16)
        k_st = (ke * jnp.exp(last - mid)).astype(BF16)
        dec = jnp.exp(last)
        for h in range(C_HEADS):
            ks = slice(h * C_DK_HEAD, (h + 1) * C_DK_HEAD)
            vs = slice(h * C_DV_HEAD, (h + 1) * C_DV_HEAD)
            vh = v_ref[0, rows, vs].astype(BF16)
            att = jnp.where(causal, _dot_nt(q_in[:, ks], k_in[:, ks]), 0.0)
            st = st_ref[h]
            o = _dot(att.astype(BF16), vh) + _dot_nt(q_st[:, ks], st.astype(BF16))
            st_ref[h] = st * dec[:, ks] + _dot_tn(vh, k_st[:, ks])
            y = o * lax.rsqrt(jnp.mean(o * o, axis=-1, keepdims=True) + EPS) * gain
            o_ref[0, rows, vs] = (y * r_ref[0, rows, vs]).astype(o_ref.dtype)
        return carry

    lax.fori_loop(0, n_chunks, body, 0)

    @pl.when(t == pl.num_programs(1) - 1)
    def _():
        for h in range(C_HEADS):
            s_ref[0, h] = st_ref[h].T


def _gla(q, k, v, r, la, s0, gain, chunk, tb):
    b, t, _ = q.shape
    seq = lambda width: pl.BlockSpec((1, tb, width), lambda i, j: (i, j, 0))
    state = pl.BlockSpec((1, C_HEADS, C_DK_HEAD, C_DV_HEAD), lambda i, j: (i, 0, 0, 0))
    return pl.pallas_call(
        functools.partial(_gla_kernel, chunk, tb // chunk),
        grid=(b, t // tb),
        in_specs=[seq(C_DK), seq(C_DK), seq(C_DV), seq(C_DV), seq(C_DK), state,
                  _full((1, C_DV_HEAD))],
        out_specs=[seq(C_DV), state],
        out_shape=[jax.ShapeDtypeStruct((b, t, C_DV), BF16),
                   jax.ShapeDtypeStruct(s0.shape, F32)],
        scratch_shapes=[pltpu.VMEM((C_HEADS, C_DV_HEAD, C_DK_HEAD), F32)],
        compiler_params=_params("parallel", "arbitrary"),
        name="gla",
    )(q, k, v, r, la, s0, gain)


GLA_CHUNK = 64
SAMPLE_PAD = 16


def _trunk(x, past, P, tm, conv_tc, gla_tb):
    bsz, t, _ = x.shape
    n = bsz * t
    x = x.reshape(n, D_MODEL)
    outs = {}
    for layer in range(DEPTH):
        i = layer // 2
        x = _ffn(x, P['norm_ffn1'][layer], P['ffn1_w_in'][layer], P['ffn1_w_out'][layer], tm)
        if layer % 2 == 0:
            q, k, v, u = _even_in(x, P['norm_mix'][layer], P['ev_w_in'][i], P['ev_q_gain'][i],
                                  P['ev_k_gain'][i], P['head_ones'], tm)
            u3 = u.reshape(bsz, t, CONV_CH)
            if past is None:
                a = _attn_prompt(q, k, v, P['band_tables'], bsz, t)
                keep = min(MAX_WINDOW, t)
                new_k = k.reshape(bsz, t, A_HEADS, A_HEAD_DIM)[:, t - keep:]
                new_v = v.reshape(bsz, t, A_HEADS, A_HEAD_DIM)[:, t - keep:]
                hist = jnp.zeros((bsz, CONV_PAD, CONV_CH), F32)
                new_u = u3[:, t - (CONV_WIDTH - 1):]
            else:
                n_buf = past[0].shape[2]
                a, new_k, new_v = _attn_sample(
                    q.reshape(bsz, t, A_WIDTH), k.reshape(bsz, t, A_WIDTH), v.reshape(bsz, t, A_WIDTH),
                    past[0][i].reshape(bsz, n_buf, A_WIDTH), past[1][i].reshape(bsz, n_buf, A_WIDTH),
                    P['decode_tables'])
                a = a.reshape(n, A_WIDTH)
                new_k = new_k.reshape(bsz, n_buf, A_HEADS, A_HEAD_DIM)
                new_v = new_v.reshape(bsz, n_buf, A_HEADS, A_HEAD_DIM)
                hist = jnp.pad(past[2][i], ((0, 0), (CONV_PAD - (CONV_WIDTH - 1), 0), (0, 0)))
                new_u = jnp.concatenate([past[2][i], u3], axis=1)[:, -(CONV_WIDTH - 1):]
            c = _conv(u3, hist, P['ev_conv_w'][i], P['ev_conv_b'][i], P['ev_conv_ln_g'][i],
                      P['ev_conv_ln_b'][i], conv_tc).reshape(n, CONV_CH)
            x = _outproj(x, [a, c], [P['ev_w_out_a'][i], P['ev_w_out_c'][i]], tm)
            outs.setdefault('k', []).append(new_k)
            outs.setdefault('v', []).append(new_v)
            outs.setdefault('u', []).append(new_u)
        else:
            q, k, v, r, la = _gla_in(x, P['norm_mix'][layer], P['od_w_main'][i], P['od_w_low'][i],
                                     P['od_gate_w_up'][i], P['od_gate_b'][i], tm)
            if past is None:
                s0 = jnp.zeros((bsz, C_HEADS, C_DK_HEAD, C_DV_HEAD), F32)
                tp, chunk = t, GLA_CHUNK
            else:
                s0 = past[3][i]
                tp, chunk = SAMPLE_PAD, SAMPLE_PAD
            seq = lambda z: jnp.pad(z.reshape(bsz, t, -1), ((0, 0), (0, tp - t), (0, 0)))
            o, s = _gla(seq(q), seq(k), seq(v), seq(r), seq(la), s0, P['od_o_gain'][i], chunk,
                        min(gla_tb, tp))
            x = _outproj(x, [o[:, :t].reshape(n, C_DV)], [P['od_w_out'][i]], tm)
            outs.setdefault('s', []).append(s)
        x = _ffn(x, P['norm_ffn2'][layer], P['ffn2_w_in'][layer], P['ffn2_w_out'][layer], tm)
    return (x.reshape(bsz, t, D_MODEL), jnp.stack(outs['k']), jnp.stack(outs['v']),
            jnp.stack(outs['u']), jnp.stack(outs['s']))


def kernel(x_prompt, x_sample, cache_k, cache_v, cache_conv, state_gla, rel_bias, norm_ffn1, ffn1_w_in, ffn1_w_out, norm_mix, norm_ffn2, ffn2_w_in, ffn2_w_out, ev_w_in, ev_q_gain, ev_k_gain, ev_conv_w, ev_conv_b, ev_conv_ln_g, ev_conv_ln_b, ev_w_out, od_w_in, od_gate_w_up, od_gate_b, od_o_gain, od_w_out):
    n_even = ev_w_in.shape[0]
    n_odd = od_w_in.shape[0]
    main = 2 * C_DK + 2 * C_DV
    head_ids = np.arange(A_WIDTH) // A_HEAD_DIM
    P = {
        'norm_ffn1': norm_ffn1[:, None, :], 'norm_mix': norm_mix[:, None, :],
        'norm_ffn2': norm_ffn2[:, None, :],
        'ffn1_w_in': ffn1_w_in.astype(BF16), 'ffn1_w_out': ffn1_w_out.astype(BF16),
        'ffn2_w_in': ffn2_w_in.astype(BF16), 'ffn2_w_out': ffn2_w_out.astype(BF16),
        'ev_w_in': ev_w_in.astype(BF16),
        'ev_q_gain': jnp.tile(ev_q_gain, (1, A_HEADS))[:, None, :],
        'ev_k_gain': jnp.tile(ev_k_gain, (1, A_HEADS))[:, None, :],
        'head_ones': jnp.asarray(head_ids[:, None] == head_ids[None, :], BF16),
        'ev_conv_w': jnp.pad(ev_conv_w, ((0, 0), (0, CONV_PAD - CONV_WIDTH), (0, 0))),
        'ev_conv_b': ev_conv_b[:, None, :], 'ev_conv_ln_g': ev_conv_ln_g[:, None, :],
        'ev_conv_ln_b': ev_conv_ln_b[:, None, :],
        'ev_w_out_a': ev_w_out[:, :A_WIDTH].astype(BF16),
        'ev_w_out_c': ev_w_out[:, A_WIDTH:].astype(BF16),
        'od_w_main': od_w_in[:, :, :main].astype(BF16),
        'od_w_low': jnp.pad(od_w_in[:, :, main:], ((0, 0), (0, 0), (0, LANES - GATE_RANK))).astype(BF16),
        'od_gate_w_up': jnp.pad(od_gate_w_up, ((0, 0), (0, LANES - GATE_RANK), (0, 0))).astype(BF16),
        'od_gate_b': od_gate_b[:, None, :],
        'od_o_gain': od_o_gain[:, None, :],
        'od_w_out': od_w_out.astype(BF16),
        'band_tables': _band_tables(rel_bias),
        'decode_tables': _decode_tables(rel_bias, x_sample.shape[1]),
    }
    del n_even, n_odd
    y_p, k_p, v_p, u_p, s_p = _trunk(x_prompt, None, P, tm=512, conv_tc=512, gla_tb=512)
    y_s, k_s, v_s, u_s, s_s = _trunk(x_sample, (cache_k, cache_v, cache_conv, state_gla), P,
                                     tm=256, conv_tc=x_sample.shape[1], gla_tb=SAMPLE_PAD)
    return (y_p, y_s, k_p, v_p, u_p, s_p, k_s, v_s, u_s, s_s)
```

```python
import functools

import numpy as np
import jax
import jax.numpy as jnp
from jax import lax
from jax.experimental import pallas as pl
from jax.experimental.pallas import tpu as pltpu

F32 = jnp.float32
BF16 = jnp.bfloat16

D_MODEL = 1024
DEPTH = 2
PAST_LEN = 16384
A_HEADS = 8
A_HEAD_DIM = 64
A_WIDTH = A_HEADS * A_HEAD_DIM
DILATED_GROUPS = ((128, 1), (512, 4), (2048, 16))
MAX_WINDOW = 2048
N_BUCKETS = 32
CONV_WIDTH = 31
CONV_CH = 512
C_HEADS = 4
C_DK = 512
C_DV = 1024
C_DK_HEAD = 128
C_DV_HEAD = 256
GATE_RANK = 16
GATE_TAU = 16.0
D_FF = 2816
EPS = 1e-6
NEG_INF = -1e30

LANES = 128
SUBLANES = 8
MXU_DIM = 256
WIN_KEYS = 128
ATT_BLOCK = 2048
ATT_UNROLL = 8
ATT_PHASES = 4
VMEM_LIMIT = 56 * 1024 * 1024


def _params(*sem):
    return pltpu.CompilerParams(dimension_semantics=sem, vmem_limit_bytes=VMEM_LIMIT)


def _dot(a, b):
    return jnp.dot(a, b, preferred_element_type=F32)


def _dot_nt(a, b):
    return lax.dot_general(a, b, (((1,), (1,)), ((), ())), preferred_element_type=F32)


def _dot_tn(a, b):
    return lax.dot_general(a, b, (((0,), (0,)), ((), ())), preferred_element_type=F32)


def _rms_rows(x, g):
    y = x * lax.rsqrt(jnp.mean(x * x, axis=-1, keepdims=True) + EPS)
    return y * g


def _sigmoid(x):
    return 1.0 / (1.0 + jnp.exp(-x))


def _full(shape):
    return pl.BlockSpec(shape, lambda *_: (0,) * len(shape))


FF_CHUNK = 256


def _ffn_kernel(x_ref, g_ref, wi_ref, wo_ref, o_ref, act_ref):
    x = x_ref[...]
    h = _rms_rows(x, g_ref[...]).astype(BF16)
    for c in range(D_FF // FF_CHUNK):
        lo = c * FF_CHUNK
        a = _dot(h, wi_ref[:, lo:lo + FF_CHUNK])
        b = _dot(h, wi_ref[:, D_FF + lo:D_FF + lo + FF_CHUNK])
        act_ref[:, lo:lo + FF_CHUNK] = (a * _sigmoid(a) * b).astype(BF16)
    o_ref[...] = x + 0.5 * _dot(act_ref[...], wo_ref[...])


def _ffn(x, g, wi, wo, tm):
    n = x.shape[0]
    return pl.pallas_call(
        _ffn_kernel,
        grid=(n // tm,),
        in_specs=[pl.BlockSpec((tm, D_MODEL), lambda i: (i, 0)),
                  _full((1, D_MODEL)),
                  _full((D_MODEL, 2 * D_FF)),
                  _full((D_FF, D_MODEL))],
        out_specs=pl.BlockSpec((tm, D_MODEL), lambda i: (i, 0)),
        out_shape=jax.ShapeDtypeStruct((n, D_MODEL), F32),
        scratch_shapes=[pltpu.VMEM((tm, D_FF), BF16)],
        compiler_params=_params("parallel"),
        name="ffn",
    )(x, g, wi, wo)


def _outproj_kernel(n_in, x_ref, *refs):
    o_ref = refs[2 * n_in]
    acc = x_ref[...]
    for i in range(n_in):
        acc = acc + _dot(refs[i][...], refs[n_in + i][...])
    o_ref[...] = acc


def _outproj(x, ms, ws, tm):
    n = x.shape[0]
    in_specs = [pl.BlockSpec((tm, D_MODEL), lambda i: (i, 0))]
    in_specs += [pl.BlockSpec((tm, m.shape[1]), lambda i: (i, 0)) for m in ms]
    in_specs += [_full(w.shape) for w in ws]
    return pl.pallas_call(
        functools.partial(_outproj_kernel, len(ms)),
        grid=(n // tm,),
        in_specs=in_specs,
        out_specs=pl.BlockSpec((tm, D_MODEL), lambda i: (i, 0)),
        out_shape=jax.ShapeDtypeStruct((n, D_MODEL), F32),
        compiler_params=_params("parallel"),
        name="outproj",
    )(x, *ms, *ws)


def _head_norm(z, gain, bd):
    zz = z * z
    hi = zz.astype(BF16)
    lo = (zz - hi.astype(F32)).astype(BF16)
    width = bd.shape[0]
    ss = jnp.concatenate(
        [_dot(hi[:, c:c + width], bd) + _dot(lo[:, c:c + width], bd)
         for c in range(0, z.shape[1], width)], axis=1)
    return z * lax.rsqrt(ss * (1.0 / A_HEAD_DIM) + EPS) * gain


def _even_in_kernel(x_ref, g_ref, w_ref, qg_ref, kg_ref, bd_ref, q_ref, k_ref, v_ref, u_ref):
    h = _rms_rows(x_ref[...], g_ref[...]).astype(BF16)
    p = _dot(h, w_ref[...])
    bd = bd_ref[...]
    q_ref[...] = _head_norm(p[:, 0:A_WIDTH], qg_ref[...], bd) * (A_HEAD_DIM ** -0.5)
    k_ref[...] = _head_norm(p[:, A_WIDTH:2 * A_WIDTH], kg_ref[...], bd)
    v_ref[...] = p[:, 2 * A_WIDTH:3 * A_WIDTH]
    gv = p[:, 3 * A_WIDTH:3 * A_WIDTH + CONV_CH]
    gg = p[:, 3 * A_WIDTH + CONV_CH:3 * A_WIDTH + 2 * CONV_CH]
    u_ref[...] = gv * _sigmoid(gg)


def _even_in(x, g, w, qg, kg, bd, tm):
    n = x.shape[0]
    row = lambda width: pl.BlockSpec((tm, width), lambda i: (i, 0))
    out = jax.ShapeDtypeStruct((n, A_WIDTH), F32)
    return pl.pallas_call(
        _even_in_kernel,
        grid=(n // tm,),
        in_specs=[row(D_MODEL), _full((1, D_MODEL)), _full(w.shape), _full((1, A_WIDTH)),
                  _full((1, A_WIDTH)), _full(bd.shape)],
        out_specs=[row(A_WIDTH)] * 4,
        out_shape=[out] * 4,
        compiler_params=_params("parallel"),
        name="even_in",
    )(x, g, w, qg, kg, bd)


def _t5_bucket(dist):
    max_exact = N_BUCKETS // 2
    d = np.asarray(dist, dtype=np.int32)
    df = np.maximum(d, 1).astype(np.float32)
    large = max_exact + (np.log(df / max_exact) / np.log(MAX_WINDOW / max_exact)
                         * (N_BUCKETS - max_exact)).astype(np.int32)
    large = np.minimum(large, N_BUCKETS - 1)
    return np.where(d < max_exact, d, large).astype(np.int32)


def _select_bias(rel_bias, dist, valid):
    onehot = (_t5_bucket(dist)[None, :] == np.arange(N_BUCKETS)[:, None]) & valid[None, :]
    picked = jnp.einsum('bh,bc->hc', rel_bias, jnp.asarray(onehot, F32),
                        precision=lax.Precision.HIGHEST)
    return picked + jnp.asarray(np.where(valid, 0.0, NEG_INF), F32)[None, :]


def _band_vectors(rel_bias):
    c = np.arange(2 * WIN_KEYS)
    valid = c <= WIN_KEYS
    vecs = [_select_bias(rel_bias, np.where(valid, (WIN_KEYS - c) * dil, 0), valid)
            for _, dil in DILATED_GROUPS]
    return jnp.stack(vecs).reshape(len(DILATED_GROUPS), A_HEADS // 2, 2, 2 * WIN_KEYS)


def _attn_prompt_kernel(q_ref, kp_ref, kc_ref, vp_ref, vc_ref, vec_ref, o_ref,
                        knat, vnat, k4, v4, q4, og1, lg1, og4, lg4, tab_ref):
    blk = pl.program_id(2)
    n_groups = len(DILATED_GROUPS)
    nph = ATT_PHASES
    per = ATT_BLOCK // nph

    @pl.when(blk == 0)
    def _():
        col = lax.broadcasted_iota(jnp.int32, (WIN_KEYS, 2 * WIN_KEYS), 1)
        for g in range(n_groups):
            for hh in range(2):
                vec = jnp.broadcast_to(vec_ref[g, 0, hh:hh + 1, :], (WIN_KEYS, 2 * WIN_KEYS))
                band = pltpu.roll(vec, 0, 1, stride=1, stride_axis=0)
                rows = slice(hh * WIN_KEYS, (hh + 1) * WIN_KEYS)
                tab_ref[g, rows, :] = band
                tab_ref[n_groups + g, rows, :] = jnp.where(col >= WIN_KEYS, band, NEG_INF)

    knat[0:WIN_KEYS, :] = kp_ref[ATT_BLOCK - WIN_KEYS:, :]
    knat[WIN_KEYS:, :] = kc_ref[...]
    vnat[0:WIN_KEYS, :] = vp_ref[ATT_BLOCK - WIN_KEYS:, :]
    vnat[WIN_KEYS:, :] = vc_ref[...]
    for r in range(nph):
        phase = pl.ds(r, per, stride=nph)
        k4[r, 0:per, :] = kp_ref[phase, :]
        k4[r, per:, :] = kc_ref[phase, :]
        v4[r, 0:per, :] = vp_ref[phase, :]
        v4[r, per:, :] = vc_ref[phase, :]
        q4[r] = q_ref[phase, :]
    first = blk == 0
    lane = lax.broadcasted_iota(jnp.int32, (WIN_KEYS, LANES), 1)
    low = lane < A_HEAD_DIM
    ones = jnp.ones((2 * WIN_KEYS, LANES), BF16)

    def block(qs, kk, vv, tab):
        qs = qs.astype(BF16)
        zero = jnp.zeros_like(qs)
        qst = jnp.concatenate([jnp.where(low, qs, zero), jnp.where(low, zero, qs)], axis=0)
        s = _dot_nt(qst, kk.astype(BF16)) + tab
        mx = jnp.max(s, axis=-1, keepdims=True)
        p = jnp.exp(s - mx).astype(BF16)
        r = _dot(p, jnp.concatenate([vv.astype(BF16), ones], axis=1))
        o2 = jnp.where(low, r[0:WIN_KEYS, 0:LANES], r[WIN_KEYS:, 0:LANES])
        l2 = jnp.where(low, r[0:WIN_KEYS, LANES:], r[WIN_KEYS:, LANES:])
        m2 = jnp.where(low, jnp.broadcast_to(mx[0:WIN_KEYS], (WIN_KEYS, LANES)),
                       jnp.broadcast_to(mx[WIN_KEYS:], (WIN_KEYS, LANES)))
        return o2 / l2, m2 + jnp.log(l2)

    def table(g, at_start):
        return tab_ref[jnp.where(jnp.logical_and(at_start, first), n_groups + g, g)]

    def body1(sub, carry):
        i0 = pl.multiple_of(sub * WIN_KEYS, WIN_KEYS)
        o, l = block(q_ref[pl.ds(i0, WIN_KEYS), :], knat[pl.ds(i0, 2 * WIN_KEYS), :],
                     vnat[pl.ds(i0, 2 * WIN_KEYS), :], table(0, sub == 0))
        og1[pl.ds(i0, WIN_KEYS), :] = o
        lg1[pl.ds(i0, WIN_KEYS), :] = l
        return carry

    def body2(pb, carry):
        sub = pb // nph
        r = pb - sub * nph
        i0 = pl.multiple_of(sub * WIN_KEYS, WIN_KEYS)
        keys = pl.ds(i0 + (per - WIN_KEYS), 2 * WIN_KEYS)
        o, l = block(q4[r, pl.ds(i0, WIN_KEYS), :], k4[r, keys, :], v4[r, keys, :],
                     table(1, sub == 0))
        og4[0, r, pl.ds(i0, WIN_KEYS), :] = o
        lg4[0, r, pl.ds(i0, WIN_KEYS), :] = l
        return carry

    def body3(pb, carry):
        a = pb // nph
        r = pb - a * nph
        rows = pl.ds(a, WIN_KEYS, stride=nph)
        keys = pl.ds(a, 2 * WIN_KEYS, stride=nph)
        o, l = block(q4[r, rows, :], k4[r, keys, :], v4[r, keys, :], table(2, True))
        og4[1, r, rows, :] = o
        lg4[1, r, rows, :] = l
        return carry

    n_blocks = ATT_BLOCK // WIN_KEYS
    for body in (body1, body2, body3):
        lax.fori_loop(0, n_blocks, body, 0, unroll=ATT_UNROLL)

    for r in range(nph):
        phase = pl.ds(r, per, stride=nph)
        la, lb, lc = lg1[phase, :], lg4[0, r], lg4[1, r]
        mx = jnp.maximum(jnp.maximum(la, lb), lc)
        wa, wb, wc = jnp.exp(la - mx), jnp.exp(lb - mx), jnp.exp(lc - mx)
        og1[phase, :] = (wa * og1[phase, :] + wb * og4[0, r] + wc * og4[1, r]) / (wa + wb + wc)
    o_ref[...] = og1[...].astype(o_ref.dtype)


def _attn_prompt(q, k, v, vecs, batch, seq):
    assert [d for _, d in DILATED_GROUPS] == [1, ATT_PHASES, ATT_PHASES ** 2]
    nb = seq // ATT_BLOCK
    n_groups = len(DILATED_GROUPS)
    per = ATT_BLOCK // ATT_PHASES
    cur = lambda b, p, t: (b * nb + t, p)
    prev = lambda b, p, t: (b * nb + jnp.maximum(t - 1, 0), p)
    blk = lambda imap: pl.BlockSpec((ATT_BLOCK, LANES), imap)
    vmem = lambda *shape: pltpu.VMEM(shape, F32)
    return pl.pallas_call(
        _attn_prompt_kernel,
        grid=(batch, A_HEADS // 2, nb),
        in_specs=[blk(cur), blk(prev), blk(cur), blk(prev), blk(cur),
                  pl.BlockSpec((n_groups, 1, 2, 2 * WIN_KEYS), lambda b, p, t: (0, p, 0, 0))],
        out_specs=blk(cur),
        out_shape=jax.ShapeDtypeStruct((batch * seq, A_WIDTH), BF16),
        scratch_shapes=[vmem(WIN_KEYS + ATT_BLOCK, LANES), vmem(WIN_KEYS + ATT_BLOCK, LANES),
                        vmem(ATT_PHASES, 2 * per, LANES), vmem(ATT_PHASES, 2 * per, LANES),
                        vmem(ATT_PHASES, per, LANES),
                        vmem(ATT_BLOCK, LANES), vmem(ATT_BLOCK, LANES),
                        vmem(2, ATT_PHASES, per, LANES), vmem(2, ATT_PHASES, per, LANES),
                        vmem(2 * n_groups, 2 * WIN_KEYS, 2 * WIN_KEYS)],
        compiler_params=_params("arbitrary", "arbitrary", "arbitrary"),
        name="attn_prompt",
    )(q, k, k, v, v, vecs)


DECODE_TAIL = LANES


def _decode_vectors(rel_bias):
    c = np.arange(MAX_WINDOW + DECODE_TAIL)
    dist = MAX_WINDOW - c
    cnt = np.zeros(c.shape, np.float32)
    for window, dil in DILATED_GROUPS:
        cnt += ((dist >= 0) & (dist <= window) & (dist % dil == 0)).astype(np.float32)
    bias = _select_bias(rel_bias, np.clip(dist, 0, MAX_WINDOW), cnt > 0)
    return bias, jnp.asarray(cnt[None, :])


def _attn_sample_kernel(n_new, q_ref, kn_ref, vn_ref, kc_ref, vc_ref, bvec_ref, cvec_ref,
                        o_ref, ko_ref, vo_ref, kall, vall, bias_ref, cnt_ref):
    n_buf = MAX_WINDOW
    rows = A_HEADS * n_new

    @pl.when(pl.program_id(0) == 0)
    def _():
        shape = (n_new, n_buf + DECODE_TAIL)
        cnt = pltpu.roll(jnp.broadcast_to(cvec_ref[...], shape), 0, 1, stride=1, stride_axis=0)
        for h in range(A_HEADS):
            bvec = jnp.broadcast_to(bvec_ref[h:h + 1, :], shape)
            bias_ref[h * n_new:(h + 1) * n_new, :] = pltpu.roll(bvec, 0, 1, stride=1, stride_axis=0)
            cnt_ref[h * n_new:(h + 1) * n_new, :] = cnt

    kc = kc_ref[0]
    vc = vc_ref[0]
    kn = kn_ref[0]
    vn = vn_ref[0]
    ko_ref[0, 0:n_buf - n_new, :] = kc[n_new:, :]
    ko_ref[0, n_buf - n_new:, :] = kn
    vo_ref[0, 0:n_buf - n_new, :] = vc[n_new:, :]
    vo_ref[0, n_buf - n_new:, :] = vn
    unused = jnp.zeros((DECODE_TAIL - n_new, A_WIDTH), F32)
    kall[0:n_buf, :] = kc.astype(BF16)
    kall[n_buf:, :] = jnp.concatenate([kn, unused], axis=0).astype(BF16)
    vall[0:n_buf, :] = vc.astype(BF16)
    vall[n_buf:, :] = jnp.concatenate([vn, unused], axis=0).astype(BF16)

    q = q_ref[0]
    row_head = lax.broadcasted_iota(jnp.int32, (A_HEADS, n_new, A_WIDTH), 0).reshape(rows, A_WIDTH)
    lane = lax.broadcasted_iota(jnp.int32, (rows, A_WIDTH), 1)
    own = jnp.logical_and(lane >= row_head * A_HEAD_DIM, lane < (row_head + 1) * A_HEAD_DIM)
    qblk = jnp.where(own, jnp.concatenate([q] * A_HEADS, axis=0), 0.0).astype(BF16)
    s = _dot_nt(qblk, kall[...]) + bias_ref[...]
    mx = jnp.max(s, axis=-1, keepdims=True)
    p = cnt_ref[...] * jnp.exp(s - mx)
    den = jnp.sum(p, axis=-1, keepdims=True)
    acc = jnp.where(own, _dot(p.astype(BF16), vall[...]) / den, 0.0)
    out = acc[0:n_new]
    for h in range(1, A_HEADS):
        out = out + acc[h * n_new:(h + 1) * n_new]
    o_ref[0] = out.astype(o_ref.dtype)


def _attn_sample(q, k_new, v_new, cache_k, cache_v, tables):
    b, n_new, _ = q.shape
    n_buf = cache_k.shape[1]
    rows = A_HEADS * n_new
    new = pl.BlockSpec((1, n_new, A_WIDTH), lambda i: (i, 0, 0))
    buf = pl.BlockSpec((1, n_buf, A_WIDTH), lambda i: (i, 0, 0))
    cols = n_buf + DECODE_TAIL
    return pl.pallas_call(
        functools.partial(_attn_sample_kernel, n_new),
        grid=(b,),
        in_specs=[new, new, new, buf, buf, _full((A_HEADS, cols)), _full((1, cols))],
        out_specs=[new, buf, buf],
        out_shape=[jax.ShapeDtypeStruct((b, n_new, A_WIDTH), BF16),
                   jax.ShapeDtypeStruct((b, n_buf, A_WIDTH), F32),
                   jax.ShapeDtypeStruct((b, n_buf, A_WIDTH), F32)],
        scratch_shapes=[pltpu.VMEM((cols, A_WIDTH), BF16),
                        pltpu.VMEM((cols, A_WIDTH), BF16),
                        pltpu.VMEM((rows, cols), F32),
                        pltpu.VMEM((rows, cols), F32)],
        compiler_params=_params("arbitrary"),
        name="attn_sample",
    )(q, k_new, v_new, cache_k, cache_v, *tables)


CONV_PAD = 32
CONV_ROWS = 32


def _conv_kernel(tc, u_ref, up_ref, hist_ref, w_ref, b_ref, g_ref, beta_ref, o_ref, win):
    t = pl.program_id(1)
    win[0, CONV_PAD:CONV_PAD + tc, :] = u_ref[0]

    @pl.when(t == 0)
    def _():
        win[0, 0:CONV_PAD, :] = hist_ref[0]

    @pl.when(t > 0)
    def _():
        win[0, 0:CONV_PAD, :] = up_ref[0]

    rows = CONV_PAD + tc - SUBLANES
    for s in range(1, SUBLANES):
        win[s, 0:rows, :] = win[0, s:s + rows, :]

    off = CONV_PAD - (CONV_WIDTH - 1)
    rc = min(CONV_ROWS, tc)

    def body(c, carry):
        r0 = pl.multiple_of(c * rc, rc)
        acc = jnp.zeros((rc, CONV_CH), F32) + b_ref[...]
        for k in range(CONV_WIDTH):
            s = (off + k) % SUBLANES
            tap = pltpu.repeat(w_ref[k], rc // SUBLANES, axis=0)
            acc = acc + win[s, pl.ds(r0 + (off + k - s), rc), :] * tap
        xc = acc - jnp.mean(acc, axis=-1, keepdims=True)
        y = xc * lax.rsqrt(jnp.mean(xc * xc, axis=-1, keepdims=True) + EPS)
        y = y * g_ref[...] + beta_ref[...]
        o_ref[0, pl.ds(r0, rc), :] = (y * _sigmoid(y)).astype(o_ref.dtype)
        return carry

    if tc == rc:
        body(0, 0)
    else:
        lax.fori_loop(0, tc // rc, body, 0, unroll=2)


def _conv(u, hist, w, b, g, beta, tc):
    bsz, t, _ = u.shape
    per = tc // CONV_PAD
    if t >= CONV_PAD:
        prev = pl.BlockSpec((1, CONV_PAD, CONV_CH), lambda i, j: (i, jnp.maximum(j * per - 1, 0), 0))
        u_prev = u
    else:
        prev = pl.BlockSpec((1, CONV_PAD, CONV_CH), lambda i, j: (i, 0, 0))
        u_prev = hist
    return pl.pallas_call(
        functools.partial(_conv_kernel, tc),
        grid=(bsz, t // tc),
        in_specs=[pl.BlockSpec((1, tc, CONV_CH), lambda i, j: (i, j, 0)),
                  prev,
                  pl.BlockSpec((1, CONV_PAD, CONV_CH), lambda i, j: (i, 0, 0)),
                  _full(w.shape), _full((1, CONV_CH)), _full((1, CONV_CH)),
                  _full((1, CONV_CH))],
        out_specs=pl.BlockSpec((1, tc, CONV_CH), lambda i, j: (i, j, 0)),
        out_shape=jax.ShapeDtypeStruct((bsz, t, CONV_CH), BF16),
        scratch_shapes=[pltpu.VMEM((SUBLANES, CONV_PAD + tc, CONV_CH), F32)],
        compiler_params=_params("parallel", "arbitrary"),
        name="conv",
    )(u, u_prev, hist, w, b, g, beta)


def _gla_in_kernel(x_ref, g_ref, w_ref, wl_ref, wu_ref, bu_ref, q_ref, k_ref, v_ref, r_ref, la_ref):
    h = _rms_rows(x_ref[...], g_ref[...]).astype(BF16)
    p = _dot(h, w_ref[...])
    q_ref[...] = p[:, 0:C_DK] * (C_DK_HEAD ** -0.5)
    k_ref[...] = p[:, C_DK:2 * C_DK]
    v_ref[...] = p[:, 2 * C_DK:2 * C_DK + C_DV]
    r = p[:, 2 * C_DK + C_DV:2 * C_DK + 2 * C_DV]
    r_ref[...] = r * _sigmoid(r)
    low = _dot(h, wl_ref[...]).astype(BF16)
    z = _dot(low, wu_ref[...]) + bu_ref[...]
    log_sig = jnp.minimum(z, 0.0) - jnp.log1p(jnp.exp(-jnp.abs(z)))
    la_ref[...] = log_sig * (1.0 / GATE_TAU)


def _gla_in(x, g, w, wl, wu, bu, tm):
    n = x.shape[0]
    row = lambda width: pl.BlockSpec((tm, width), lambda i: (i, 0))
    sds = lambda width: jax.ShapeDtypeStruct((n, width), F32)
    return pl.pallas_call(
        _gla_in_kernel,
        grid=(n // tm,),
        in_specs=[row(D_MODEL), _full((1, D_MODEL)), _full(w.shape), _full(wl.shape),
                  _full(wu.shape), _full((1, C_DK))],
        out_specs=[row(C_DK), row(C_DK), row(C_DV), row(C_DV), row(C_DK)],
        out_shape=[sds(C_DK), sds(C_DK), sds(C_DV), sds(C_DV), sds(C_DK)],
        compiler_params=_params("parallel"),
        name="gla_in",
    )(x, g, w, wl, wu, bu)


def _gla_kernel(chunk, n_chunks, q_ref, k_ref, v_ref, r_ref, la_ref, s0_ref, gain_ref,
                o_ref, s_ref, st_ref):
    t = pl.program_id(1)

    @pl.when(t == 0)
    def _():
        for h in range(C_HEADS):
            st_ref[h] = s0_ref[0, h].T

    ri = lax.broadcasted_iota(jnp.int32, (chunk, chunk), 0)
    ci = lax.broadcasted_iota(jnp.int32, (chunk, chunk), 1)
    causal = ci <= ri
    tri = jnp.where(causal, 1.0, 0.0).astype(BF16)
    gain = gain_ref[...]

    def body(c, carry):
        rows = pl.ds(pl.multiple_of(c * chunk, chunk), chunk)
        la = la_ref[0, rows, :]
        la_hi = la.astype(BF16)
        rem = la - la_hi.astype(F32)
        la_mid = rem.astype(BF16)
        la_lo = (rem - la_mid.astype(F32)).astype(BF16)
        cum = _dot(tri, la_hi) + _dot(tri, la_mid) + _dot(tri, la_lo)
        mid = cum[chunk // 2 - 1:chunk // 2, :]
        last = cum[chunk - 1:chunk, :]
        eq = jnp.exp(cum - mid)
        ek = jnp.exp(mid - cum)
        q = q_ref[0, rows, :]
        k = k_ref[0, rows, :]
        qe = q * eq
        ke = k * ek
        q_in = qe.astype(BF16)
        k_in = ke.astype(BF16)
        q_st = (qe * jnp.exp(mid)).astype(BF16)
        k_st = (ke * jnp.exp(last - mid)).astype(BF16)
        dec = jnp.exp(last)
        for h in range(C_HEADS):
            ks = slice(h * C_DK_HEAD, (h + 1) * C_DK_HEAD)
            vs = slice(h * C_DV_HEAD, (h + 1) * C_DV_HEAD)
            vh = v_ref[0, rows, vs].astype(BF16)
            att = jnp.where(causal, _dot_nt(q_in[:, ks], k_in[:, ks]), 0.0)
            st = st_ref[h]
            o = _dot(att.astype(BF16), vh) + _dot_nt(q_st[:, ks], st.astype(BF16))
            st_ref[h] = st * dec[:, ks] + _dot_tn(vh, k_st[:, ks])
            y = o * lax.rsqrt(jnp.mean(o * o, axis=-1, keepdims=True) + EPS) * gain
            o_ref[0, rows, vs] = (y * r_ref[0, rows, vs]).astype(o_ref.dtype)
        return carry

    lax.fori_loop(0, n_chunks, body, 0, unroll=min(n_chunks, GLA_UNROLL))

    @pl.when(t == pl.num_programs(1) - 1)
    def _():
        for h in range(C_HEADS):
            s_ref[0, h] = st_ref[h].T


def _gla(q, k, v, r, la, s0, gain, chunk, tb):
    b, t, _ = q.shape
    seq = lambda width: pl.BlockSpec((1, tb, width), lambda i, j: (i, j, 0))
    state = pl.BlockSpec((1, C_HEADS, C_DK_HEAD, C_DV_HEAD), lambda i, j: (i, 0, 0, 0))
    return pl.pallas_call(
        functools.partial(_gla_kernel, chunk, tb // chunk),
        grid=(b, t // tb),
        in_specs=[seq(C_DK), seq(C_DK), seq(C_DV), seq(C_DV), seq(C_DK), state,
                  _full((1, C_DV_HEAD))],
        out_specs=[seq(C_DV), state],
        out_shape=[jax.ShapeDtypeStruct((b, t, C_DV), BF16),
                   jax.ShapeDtypeStruct(s0.shape, F32)],
        scratch_shapes=[pltpu.VMEM((C_HEADS, C_DV_HEAD, C_DK_HEAD), F32)],
        compiler_params=_params("parallel", "arbitrary"),
        name="gla",
    )(q, k, v, r, la, s0, gain)


GLA_CHUNK = 64
GLA_UNROLL = 2
SAMPLE_PAD = 16


def _trunk(x, past, P, tm, conv_tc, gla_tb):
    bsz, t, _ = x.shape
    n = bsz * t
    x = x.reshape(n, D_MODEL)
    outs = {}
    for layer in range(DEPTH):
        i = layer // 2
        x = _ffn(x, P['norm_ffn1'][layer], P['ffn1_w_in'][layer], P['ffn1_w_out'][layer], tm)
        if layer % 2 == 0:
            q, k, v, u = _even_in(x, P['norm_mix'][layer], P['ev_w_in'][i], P['ev_q_gain'][i],
                                  P['ev_k_gain'][i], P['head_ones'], tm)
            u3 = u.reshape(bsz, t, CONV_CH)
            if past is None:
                a = _attn_prompt(q, k, v, P['band_vectors'], bsz, t)
                keep = min(MAX_WINDOW, t)
                tail = lambda z: z.reshape(bsz, t, A_WIDTH)[:, t - keep:].reshape(
                    bsz, keep, A_HEADS, A_HEAD_DIM)
                new_k, new_v = tail(k), tail(v)
                hist = jnp.zeros((bsz, CONV_PAD, CONV_CH), F32)
                new_u = u3[:, t - (CONV_WIDTH - 1):]
            else:
                n_buf = past[0].shape[2]
                a, new_k, new_v = _attn_sample(
                    q.reshape(bsz, t, A_WIDTH), k.reshape(bsz, t, A_WIDTH), v.reshape(bsz, t, A_WIDTH),
                    past[0][i].reshape(bsz, n_buf, A_WIDTH), past[1][i].reshape(bsz, n_buf, A_WIDTH),
                    P['decode_vectors'])
                a = a.reshape(n, A_WIDTH)
                new_k = new_k.reshape(bsz, n_buf, A_HEADS, A_HEAD_DIM)
                new_v = new_v.reshape(bsz, n_buf, A_HEADS, A_HEAD_DIM)
                hist = jnp.pad(past[2][i], ((0, 0), (CONV_PAD - (CONV_WIDTH - 1), 0), (0, 0)))
                new_u = jnp.concatenate([past[2][i], u3], axis=1)[:, -(CONV_WIDTH - 1):]
            c = _conv(u3, hist, P['ev_conv_w'][i], P['ev_conv_b'][i], P['ev_conv_ln_g'][i],
                      P['ev_conv_ln_b'][i], conv_tc).reshape(n, CONV_CH)
            x = _outproj(x, [a, c], [P['ev_w_out_a'][i], P['ev_w_out_c'][i]], tm)
            outs.setdefault('k', []).append(new_k)
            outs.setdefault('v', []).append(new_v)
            outs.setdefault('u', []).append(new_u)
        else:
            q, k, v, r, la = _gla_in(x, P['norm_mix'][layer], P['od_w_main'][i], P['od_w_low'][i],
                                     P['od_gate_w_up'][i], P['od_gate_b'][i], tm)
            if past is None:
                s0 = jnp.zeros((bsz, C_HEADS, C_DK_HEAD, C_DV_HEAD), F32)
                tp, chunk = t, GLA_CHUNK
            else:
                s0 = past[3][i]
                tp, chunk = SAMPLE_PAD, SAMPLE_PAD
            seq = lambda z: jnp.pad(z.reshape(bsz, t, -1), ((0, 0), (0, tp - t), (0, 0)))
            o, s = _gla(seq(q), seq(k), seq(v), seq(r), seq(la), s0, P['od_o_gain'][i], chunk,
                        min(gla_tb, tp))
            x = _outproj(x, [o[:, :t].reshape(n, C_DV)], [P['od_w_out'][i]], tm)
            outs.setdefault('s', []).append(s)
        x = _ffn(x, P['norm_ffn2'][layer], P['ffn2_w_in'][layer], P['ffn2_w_out'][layer], tm)
    return (x.reshape(bsz, t, D_MODEL), jnp.stack(outs['k']), jnp.stack(outs['v']),
            jnp.stack(outs['u']), jnp.stack(outs['s']))


def kernel(x_prompt, x_sample, cache_k, cache_v, cache_conv, state_gla, rel_bias, norm_ffn1, ffn1_w_in, ffn1_w_out, norm_mix, norm_ffn2, ffn2_w_in, ffn2_w_out, ev_w_in, ev_q_gain, ev_k_gain, ev_conv_w, ev_conv_b, ev_conv_ln_g, ev_conv_ln_b, ev_w_out, od_w_in, od_gate_w_up, od_gate_b, od_o_gain, od_w_out):
    n_even = ev_w_in.shape[0]
    n_odd = od_w_in.shape[0]
    main = 2 * C_DK + 2 * C_DV
    head_ids = np.arange(MXU_DIM) // A_HEAD_DIM
    per = lambda n, f: [f(j) for j in range(n)]
    row = lambda z: z[None, :]
    P = {
        'norm_ffn1': per(DEPTH, lambda j: row(norm_ffn1[j])),
        'norm_mix': per(DEPTH, lambda j: row(norm_mix[j])),
        'norm_ffn2': per(DEPTH, lambda j: row(norm_ffn2[j])),
        'ffn1_w_in': per(DEPTH, lambda j: ffn1_w_in[j].astype(BF16)),
        'ffn1_w_out': per(DEPTH, lambda j: ffn1_w_out[j].astype(BF16)),
        'ffn2_w_in': per(DEPTH, lambda j: ffn2_w_in[j].astype(BF16)),
        'ffn2_w_out': per(DEPTH, lambda j: ffn2_w_out[j].astype(BF16)),
        'ev_w_in': per(n_even, lambda j: ev_w_in[j].astype(BF16)),
        'ev_q_gain': per(n_even, lambda j: row(jnp.tile(ev_q_gain[j], A_HEADS))),
        'ev_k_gain': per(n_even, lambda j: row(jnp.tile(ev_k_gain[j], A_HEADS))),
        'head_ones': jnp.asarray(head_ids[:, None] == head_ids[None, :], BF16),
        'ev_conv_w': per(n_even, lambda j: jnp.broadcast_to(
            ev_conv_w[j][:, None, :], (CONV_WIDTH, SUBLANES, CONV_CH))),
        'ev_conv_b': per(n_even, lambda j: row(ev_conv_b[j])),
        'ev_conv_ln_g': per(n_even, lambda j: row(ev_conv_ln_g[j])),
        'ev_conv_ln_b': per(n_even, lambda j: row(ev_conv_ln_b[j])),
        'ev_w_out_a': per(n_even, lambda j: ev_w_out[j, :A_WIDTH].astype(BF16)),
        'ev_w_out_c': per(n_even, lambda j: ev_w_out[j, A_WIDTH:].astype(BF16)),
        'od_w_main': per(n_odd, lambda j: od_w_in[j, :, :main].astype(BF16)),
        'od_w_low': per(n_odd, lambda j: jnp.pad(od_w_in[j, :, main:],
                                                 ((0, 0), (0, LANES - GATE_RANK))).astype(BF16)),
        'od_gate_w_up': per(n_odd, lambda j: jnp.pad(od_gate_w_up[j],
                                                     ((0, LANES - GATE_RANK), (0, 0))).astype(BF16)),
        'od_gate_b': per(n_odd, lambda j: row(od_gate_b[j])),
        'od_o_gain': per(n_odd, lambda j: row(od_o_gain[j])),
        'od_w_out': per(n_odd, lambda j: od_w_out[j].astype(BF16)),
        'band_vectors': _band_vectors(rel_bias),
        'decode_vectors': _decode_vectors(rel_bias),
    }
    y_p, k_p, v_p, u_p, s_p = _trunk(x_prompt, None, P, tm=512, conv_tc=512, gla_tb=512)
    y_s, k_s, v_s, u_s, s_s = _trunk(x_sample, (cache_k, cache_v, cache_conv, state_gla), P,
                                     tm=256, conv_tc=x_sample.shape[1], gla_tb=SAMPLE_PAD)
    return (y_p, y_s, k_p, v_p, u_p, s_p, k_s, v_s, u_s, s_s)
```

```python
import functools

import numpy as np
import jax
import jax.numpy as jnp
from jax import lax
from jax.experimental import pallas as pl
from jax.experimental.pallas import tpu as pltpu

F32 = jnp.float32
BF16 = jnp.bfloat16

D_MODEL = 1024
DEPTH = 2
PAST_LEN = 16384
A_HEADS = 8
A_HEAD_DIM = 64
A_WIDTH = A_HEADS * A_HEAD_DIM
DILATED_GROUPS = ((128, 1), (512, 4), (2048, 16))
MAX_WINDOW = 2048
N_BUCKETS = 32
CONV_WIDTH = 31
CONV_CH = 512
C_HEADS = 4
C_DK = 512
C_DV = 1024
C_DK_HEAD = 128
C_DV_HEAD = 256
GATE_RANK = 16
GATE_TAU = 16.0
D_FF = 2816
EPS = 1e-6
NEG_INF = -1e30

LANES = 128
SUBLANES = 8
MXU_DIM = 256
WIN_KEYS = 128
ATT_BLOCK = 2048
ATT_UNROLL = 8
ATT_PHASES = 4
VMEM_LIMIT = 56 * 1024 * 1024


def _params(*sem):
    return pltpu.CompilerParams(dimension_semantics=sem, vmem_limit_bytes=VMEM_LIMIT)


def _dot(a, b):
    return jnp.dot(a, b, preferred_element_type=F32)


def _dot_nt(a, b):
    return lax.dot_general(a, b, (((1,), (1,)), ((), ())), preferred_element_type=F32)


def _dot_tn(a, b):
    return lax.dot_general(a, b, (((0,), (0,)), ((), ())), preferred_element_type=F32)


def _rms_rows(x, g):
    y = x * lax.rsqrt(jnp.mean(x * x, axis=-1, keepdims=True) + EPS)
    return y * g


def _sigmoid(x):
    return 1.0 / (1.0 + jnp.exp(-x))


def _full(shape):
    return pl.BlockSpec(shape, lambda *_: (0,) * len(shape), pipeline_mode=pl.Buffered(1))


FF_CHUNK = 256


def _ffn_kernel(n_pre, x_ref, *refs):
    m_refs, w_refs = refs[:n_pre], refs[n_pre:2 * n_pre]
    g_ref, wi_ref, wo_ref, o_ref, act_ref = refs[2 * n_pre:]
    x = x_ref[...]
    for m_ref, w_ref in zip(m_refs, w_refs):
        x = x + _dot(m_ref[...], w_ref[...])
    h = _rms_rows(x, g_ref[...]).astype(BF16)
    for c in range(D_FF // FF_CHUNK):
        lo = c * FF_CHUNK
        a = _dot(h, wi_ref[:, lo:lo + FF_CHUNK])
        b = _dot(h, wi_ref[:, D_FF + lo:D_FF + lo + FF_CHUNK])
        act_ref[:, lo:lo + FF_CHUNK] = (a * _sigmoid(a) * b).astype(BF16)
    o_ref[...] = x + 0.5 * _dot(act_ref[...], wo_ref[...])


def _ffn(x, g, wi, wo, tm, ms=(), ws=()):
    n = x.shape[0]
    row = lambda width: pl.BlockSpec((tm, width), lambda i: (i, 0))
    return pl.pallas_call(
        functools.partial(_ffn_kernel, len(ms)),
        grid=(n // tm,),
        in_specs=[row(D_MODEL)] + [row(m.shape[1]) for m in ms] + [_full(w.shape) for w in ws]
                 + [_full((1, D_MODEL)), _full((D_MODEL, 2 * D_FF)), _full((D_FF, D_MODEL))],
        out_specs=row(D_MODEL),
        out_shape=jax.ShapeDtypeStruct((n, D_MODEL), F32),
        scratch_shapes=[pltpu.VMEM((tm, D_FF), BF16)],
        compiler_params=_params("parallel"),
        name="ffn",
    )(x, *ms, *ws, g, wi, wo)


CAST_ROWS = 256


def _cast_kernel(w_ref, o_ref):
    o_ref[...] = w_ref[...].astype(o_ref.dtype)


def _weight_bf16(w, layer, row0=0, n_rows=None, n_cols=None):
    n_rows = w.shape[1] - row0 if n_rows is None else n_rows
    n_cols = w.shape[2] if n_cols is None else n_cols
    first = row0 // CAST_ROWS
    return pl.pallas_call(
        _cast_kernel,
        grid=(n_rows // CAST_ROWS,),
        in_specs=[pl.BlockSpec((None, CAST_ROWS, n_cols), lambda i: (layer, first + i, 0))],
        out_specs=pl.BlockSpec((CAST_ROWS, n_cols), lambda i: (i, 0)),
        out_shape=jax.ShapeDtypeStruct((n_rows, n_cols), BF16),
        compiler_params=_params("parallel"),
        name="weight_bf16",
    )(w)


def _head_norm(z, gain, bd):
    zz = z * z
    hi = zz.astype(BF16)
    lo = (zz - hi.astype(F32)).astype(BF16)
    width = bd.shape[0]
    ss = jnp.concatenate(
        [_dot(hi[:, c:c + width], bd) + _dot(lo[:, c:c + width], bd)
         for c in range(0, z.shape[1], width)], axis=1)
    return z * lax.rsqrt(ss * (1.0 / A_HEAD_DIM) + EPS) * gain


def _even_in_kernel(x_ref, g_ref, w_ref, qg_ref, kg_ref, bd_ref, q_ref, k_ref, v_ref, u_ref):
    h = _rms_rows(x_ref[...], g_ref[...]).astype(BF16)
    p = _dot(h, w_ref[...])
    bd = bd_ref[...]
    q_ref[...] = _head_norm(p[:, 0:A_WIDTH], qg_ref[...], bd) * (A_HEAD_DIM ** -0.5)
    k_ref[...] = _head_norm(p[:, A_WIDTH:2 * A_WIDTH], kg_ref[...], bd)
    v_ref[...] = p[:, 2 * A_WIDTH:3 * A_WIDTH]
    gv = p[:, 3 * A_WIDTH:3 * A_WIDTH + CONV_CH]
    gg = p[:, 3 * A_WIDTH + CONV_CH:3 * A_WIDTH + 2 * CONV_CH]
    u_ref[...] = gv * _sigmoid(gg)


def _even_in(x, g, w, qg, kg, bd, tm):
    n = x.shape[0]
    row = lambda width: pl.BlockSpec((tm, width), lambda i: (i, 0))
    out = jax.ShapeDtypeStruct((n, A_WIDTH), F32)
    return pl.pallas_call(
        _even_in_kernel,
        grid=(n // tm,),
        in_specs=[row(D_MODEL), _full((1, D_MODEL)), _full(w.shape), _full((1, A_WIDTH)),
                  _full((1, A_WIDTH)), _full(bd.shape)],
        out_specs=[row(A_WIDTH)] * 4,
        out_shape=[out] * 4,
        compiler_params=_params("parallel"),
        name="even_in",
    )(x, g, w, qg, kg, bd)


def _t5_bucket(dist):
    max_exact = N_BUCKETS // 2
    d = np.asarray(dist, dtype=np.int32)
    df = np.maximum(d, 1).astype(np.float32)
    large = max_exact + (np.log(df / max_exact) / np.log(MAX_WINDOW / max_exact)
                         * (N_BUCKETS - max_exact)).astype(np.int32)
    large = np.minimum(large, N_BUCKETS - 1)
    return np.where(d < max_exact, d, large).astype(np.int32)


def _select_bias(rel_bias, dist, valid):
    onehot = (_t5_bucket(dist)[None, :] == np.arange(N_BUCKETS)[:, None]) & valid[None, :]
    picked = jnp.einsum('bh,bc->hc', rel_bias, jnp.asarray(onehot, F32),
                        precision=lax.Precision.HIGHEST)
    return picked + jnp.asarray(np.where(valid, 0.0, NEG_INF), F32)[None, :]


def _band_vectors(rel_bias):
    c = np.arange(2 * WIN_KEYS)
    valid = c <= WIN_KEYS
    vecs = [_select_bias(rel_bias, np.where(valid, (WIN_KEYS - c) * dil, 0), valid)
            for _, dil in DILATED_GROUPS]
    return jnp.stack(vecs).reshape(len(DILATED_GROUPS), A_HEADS // 2, 2, 2 * WIN_KEYS)


def _attn_prompt_kernel(q_ref, kp_ref, kc_ref, vp_ref, vc_ref, vec_ref, o_ref,
                        knat, vnat, k4, v4, q4, og1, lg1, og4, lg4, tab_ref):
    blk = pl.program_id(2)
    n_groups = len(DILATED_GROUPS)
    nph = ATT_PHASES
    per = ATT_BLOCK // nph

    @pl.when(blk == 0)
    def _():
        col = lax.broadcasted_iota(jnp.int32, (WIN_KEYS, 2 * WIN_KEYS), 1)
        for g in range(n_groups):
            for hh in range(2):
                vec = jnp.broadcast_to(vec_ref[g, 0, hh:hh + 1, :], (WIN_KEYS, 2 * WIN_KEYS))
                band = pltpu.roll(vec, 0, 1, stride=1, stride_axis=0)
                rows = slice(hh * WIN_KEYS, (hh + 1) * WIN_KEYS)
                tab_ref[g, rows, :] = band
                tab_ref[n_groups + g, rows, :] = jnp.where(col >= WIN_KEYS, band, NEG_INF)

    knat[0:WIN_KEYS, :] = kp_ref[ATT_BLOCK - WIN_KEYS:, :]
    knat[WIN_KEYS:, :] = kc_ref[...]
    vnat[0:WIN_KEYS, :] = vp_ref[ATT_BLOCK - WIN_KEYS:, :]
    vnat[WIN_KEYS:, :] = vc_ref[...]
    for r in range(nph):
        phase = pl.ds(r, per, stride=nph)
        k4[r, 0:per, :] = kp_ref[phase, :]
        k4[r, per:, :] = kc_ref[phase, :]
        v4[r, 0:per, :] = vp_ref[phase, :]
        v4[r, per:, :] = vc_ref[phase, :]
        q4[r] = q_ref[phase, :]
    first = blk == 0
    lane = lax.broadcasted_iota(jnp.int32, (WIN_KEYS, LANES), 1)
    low = lane < A_HEAD_DIM
    ones = jnp.ones((2 * WIN_KEYS, LANES), BF16)

    def block(qs, kk, vv, tab):
        qs = qs.astype(BF16)
        zero = jnp.zeros_like(qs)
        qst = jnp.concatenate([jnp.where(low, qs, zero), jnp.where(low, zero, qs)], axis=0)
        s = _dot_nt(qst, kk.astype(BF16)) + tab
        mx = jnp.max(s, axis=-1, keepdims=True)
        p = jnp.exp(s - mx).astype(BF16)
        r = _dot(p, jnp.concatenate([vv.astype(BF16), ones], axis=1))
        o2 = jnp.where(low, r[0:WIN_KEYS, 0:LANES], r[WIN_KEYS:, 0:LANES])
        l2 = jnp.where(low, r[0:WIN_KEYS, LANES:], r[WIN_KEYS:, LANES:])
        m2 = jnp.where(low, jnp.broadcast_to(mx[0:WIN_KEYS], (WIN_KEYS, LANES)),
                       jnp.broadcast_to(mx[WIN_KEYS:], (WIN_KEYS, LANES)))
        return o2 / l2, m2 + jnp.log(l2)

    def table(g, at_start):
        return tab_ref[jnp.where(jnp.logical_and(at_start, first), n_groups + g, g)]

    def body1(sub, carry):
        i0 = pl.multiple_of(sub * WIN_KEYS, WIN_KEYS)
        o, l = block(q_ref[pl.ds(i0, WIN_KEYS), :], knat[pl.ds(i0, 2 * WIN_KEYS), :],
                     vnat[pl.ds(i0, 2 * WIN_KEYS), :], table(0, sub == 0))
        og1[pl.ds(i0, WIN_KEYS), :] = o
        lg1[pl.ds(i0, WIN_KEYS), :] = l
        return carry

    def body2(pb, carry):
        sub = pb // nph
        r = pb - sub * nph
        i0 = pl.multiple_of(sub * WIN_KEYS, WIN_KEYS)
        keys = pl.ds(i0 + (per - WIN_KEYS), 2 * WIN_KEYS)
        o, l = block(q4[r, pl.ds(i0, WIN_KEYS), :], k4[r, keys, :], v4[r, keys, :],
                     table(1, sub == 0))
        og4[0, r, pl.ds(i0, WIN_KEYS), :] = o
        lg4[0, r, pl.ds(i0, WIN_KEYS), :] = l
        return carry

    def body3(pb, carry):
        a = pb // nph
        r = pb - a * nph
        rows = pl.ds(a, WIN_KEYS, stride=nph)
        keys = pl.ds(a, 2 * WIN_KEYS, stride=nph)
        o, l = block(q4[r, rows, :], k4[r, keys, :], v4[r, keys, :], table(2, True))
        og4[1, r, rows, :] = o
        lg4[1, r, rows, :] = l
        return carry

    n_blocks = ATT_BLOCK // WIN_KEYS
    for body in (body1, body2, body3):
        lax.fori_loop(0, n_blocks, body, 0, unroll=ATT_UNROLL)

    for r in range(nph):
        phase = pl.ds(r, per, stride=nph)
        la, lb, lc = lg1[phase, :], lg4[0, r], lg4[1, r]
        mx = jnp.maximum(jnp.maximum(la, lb), lc)
        wa, wb, wc = jnp.exp(la - mx), jnp.exp(lb - mx), jnp.exp(lc - mx)
        og1[phase, :] = (wa * og1[phase, :] + wb * og4[0, r] + wc * og4[1, r]) / (wa + wb + wc)
    o_ref[...] = og1[...].astype(o_ref.dtype)


def _attn_prompt(q, k, v, vecs, batch, seq):
    assert [d for _, d in DILATED_GROUPS] == [1, ATT_PHASES, ATT_PHASES ** 2]
    nb = seq // ATT_BLOCK
    n_groups = len(DILATED_GROUPS)
    per = ATT_BLOCK // ATT_PHASES
    cur = lambda b, p, t: (b * nb + t, p)
    prev = lambda b, p, t: (b * nb + jnp.maximum(t - 1, 0), p)
    blk = lambda imap: pl.BlockSpec((ATT_BLOCK, LANES), imap)
    vmem = lambda *shape: pltpu.VMEM(shape, F32)
    return pl.pallas_call(
        _attn_prompt_kernel,
        grid=(batch, A_HEADS // 2, nb),
        in_specs=[blk(cur), blk(prev), blk(cur), blk(prev), blk(cur),
                  pl.BlockSpec((n_groups, 1, 2, 2 * WIN_KEYS), lambda b, p, t: (0, p, 0, 0))],
        out_specs=blk(cur),
        out_shape=jax.ShapeDtypeStruct((batch * seq, A_WIDTH), BF16),
        scratch_shapes=[vmem(WIN_KEYS + ATT_BLOCK, LANES), vmem(WIN_KEYS + ATT_BLOCK, LANES),
                        vmem(ATT_PHASES, 2 * per, LANES), vmem(ATT_PHASES, 2 * per, LANES),
                        vmem(ATT_PHASES, per, LANES),
                        vmem(ATT_BLOCK, LANES), vmem(ATT_BLOCK, LANES),
                        vmem(2, ATT_PHASES, per, LANES), vmem(2, ATT_PHASES, per, LANES),
                        vmem(2 * n_groups, 2 * WIN_KEYS, 2 * WIN_KEYS)],
        compiler_params=_params("arbitrary", "arbitrary", "arbitrary"),
        name="attn_prompt",
    )(q, k, k, v, v, vecs)


DECODE_TAIL = LANES


def _decode_vectors(rel_bias):
    c = np.arange(MAX_WINDOW + DECODE_TAIL)
    dist = MAX_WINDOW - c
    cnt = np.zeros(c.shape, np.float32)
    for window, dil in DILATED_GROUPS:
        cnt += ((dist >= 0) & (dist <= window) & (dist % dil == 0)).astype(np.float32)
    bias = _select_bias(rel_bias, np.clip(dist, 0, MAX_WINDOW), cnt > 0)
    return bias, jnp.asarray(cnt[None, :])


def _attn_sample_kernel(n_new, q_ref, kn_ref, vn_ref, kc_ref, vc_ref, bvec_ref, cvec_ref,
                        o_ref, ko_ref, vo_ref, kall, vall, bias_ref, cnt_ref):
    n_buf = MAX_WINDOW
    rows = A_HEADS * n_new

    @pl.when(pl.program_id(0) == 0)
    def _():
        shape = (n_new, n_buf + DECODE_TAIL)
        cnt = pltpu.roll(jnp.broadcast_to(cvec_ref[...], shape), 0, 1, stride=1, stride_axis=0)
        for h in range(A_HEADS):
            bvec = jnp.broadcast_to(bvec_ref[h:h + 1, :], shape)
            bias_ref[h * n_new:(h + 1) * n_new, :] = pltpu.roll(bvec, 0, 1, stride=1, stride_axis=0)
            cnt_ref[h * n_new:(h + 1) * n_new, :] = cnt

    kc = kc_ref[0]
    vc = vc_ref[0]
    kn = kn_ref[0]
    vn = vn_ref[0]
    ko_ref[0, 0:n_buf - n_new, :] = kc[n_new:, :]
    ko_ref[0, n_buf - n_new:, :] = kn
    vo_ref[0, 0:n_buf - n_new, :] = vc[n_new:, :]
    vo_ref[0, n_buf - n_new:, :] = vn
    unused = jnp.zeros((DECODE_TAIL - n_new, A_WIDTH), F32)
    kall[0:n_buf, :] = kc.astype(BF16)
    kall[n_buf:, :] = jnp.concatenate([kn, unused], axis=0).astype(BF16)
    vall[0:n_buf, :] = vc.astype(BF16)
    vall[n_buf:, :] = jnp.concatenate([vn, unused], axis=0).astype(BF16)

    q = q_ref[0]
    row_head = lax.broadcasted_iota(jnp.int32, (A_HEADS, n_new, A_WIDTH), 0).reshape(rows, A_WIDTH)
    lane = lax.broadcasted_iota(jnp.int32, (rows, A_WIDTH), 1)
    own = jnp.logical_and(lane >= row_head * A_HEAD_DIM, lane < (row_head + 1) * A_HEAD_DIM)
    qblk = jnp.where(own, jnp.concatenate([q] * A_HEADS, axis=0), 0.0).astype(BF16)
    s = _dot_nt(qblk, kall[...]) + bias_ref[...]
    mx = jnp.max(s, axis=-1, keepdims=True)
    p = cnt_ref[...] * jnp.exp(s - mx)
    den = jnp.sum(p, axis=-1, keepdims=True)
    acc = jnp.where(own, _dot(p.astype(BF16), vall[...]) / den, 0.0)
    out = acc[0:n_new]
    for h in range(1, A_HEADS):
        out = out + acc[h * n_new:(h + 1) * n_new]
    o_ref[0] = out.astype(o_ref.dtype)


def _attn_sample(q, k_new, v_new, cache_k, cache_v, tables):
    b, n_new, _ = q.shape
    n_buf = cache_k.shape[1]
    rows = A_HEADS * n_new
    new = pl.BlockSpec((1, n_new, A_WIDTH), lambda i: (i, 0, 0))
    buf = pl.BlockSpec((1, n_buf, A_WIDTH), lambda i: (i, 0, 0))
    cols = n_buf + DECODE_TAIL
    return pl.pallas_call(
        functools.partial(_attn_sample_kernel, n_new),
        grid=(b,),
        in_specs=[new, new, new, buf, buf, _full((A_HEADS, cols)), _full((1, cols))],
        out_specs=[new, buf, buf],
        out_shape=[jax.ShapeDtypeStruct((b, n_new, A_WIDTH), BF16),
                   jax.ShapeDtypeStruct((b, n_buf, A_WIDTH), F32),
                   jax.ShapeDtypeStruct((b, n_buf, A_WIDTH), F32)],
        scratch_shapes=[pltpu.VMEM((cols, A_WIDTH), BF16),
                        pltpu.VMEM((cols, A_WIDTH), BF16),
                        pltpu.VMEM((rows, cols), F32),
                        pltpu.VMEM((rows, cols), F32)],
        compiler_params=_params("arbitrary"),
        name="attn_sample",
    )(q, k_new, v_new, cache_k, cache_v, *tables)


CONV_PAD = 32
CONV_ROWS = 32
CONV_UNROLL = 2


def _conv_kernel(tc, u_ref, up_ref, hist_ref, w_ref, b_ref, g_ref, beta_ref, o_ref, win, stage):
    t = pl.program_id(1)
    n_slab = CONV_CH // LANES
    slab = lambda c: slice(c * LANES, (c + 1) * LANES)
    for c in range(n_slab):
        win[c, CONV_PAD:CONV_PAD + tc, :] = u_ref[0, :, slab(c)]

    @pl.when(t == 0)
    def _():
        for c in range(n_slab):
            win[c, 0:CONV_PAD, :] = hist_ref[0, :, slab(c)]

    @pl.when(t > 0)
    def _():
        for c in range(n_slab):
            win[c, 0:CONV_PAD, :] = up_ref[0, :, slab(c)]

    off = CONV_PAD - (CONV_WIDTH - 1)
    rc = min(CONV_ROWS, tc)
    half = rc // 2

    def tap(k, c):
        w = w_ref[k, :, slab(c)]
        if half < SUBLANES:
            return w[0:half]
        return jnp.concatenate([w] * (half // SUBLANES), axis=0)

    def body(j, carry):
        r0 = j * rc
        for c in range(n_slab):
            for par in range(2):
                acc = jnp.zeros((half, LANES), F32) + b_ref[:, slab(c)]
                for k in range(CONV_WIDTH):
                    rows = pl.ds(r0 + off + k + par, half, stride=2)
                    acc = acc + win[c, rows, :] * tap(k, c)
                stage[c, pl.ds(r0 + par, half, stride=2), :] = acc
        return carry

    if tc == rc:
        body(0, 0)
    else:
        lax.fori_loop(0, tc // rc, body, 0, unroll=CONV_UNROLL)
    y = jnp.concatenate([stage[c] for c in range(n_slab)], axis=1)
    xc = y - jnp.mean(y, axis=-1, keepdims=True)
    y = xc * lax.rsqrt(jnp.mean(xc * xc, axis=-1, keepdims=True) + EPS)
    y = y * g_ref[...] + beta_ref[...]
    o_ref[0] = (y * _sigmoid(y)).astype(o_ref.dtype)


def _conv(u, hist, w, b, g, beta, tc):
    bsz, t, _ = u.shape
    per = tc // CONV_PAD
    if t >= CONV_PAD:
        prev = pl.BlockSpec((1, CONV_PAD, CONV_CH), lambda i, j: (i, jnp.maximum(j * per - 1, 0), 0))
        u_prev = u
    else:
        prev = pl.BlockSpec((1, CONV_PAD, CONV_CH), lambda i, j: (i, 0, 0))
        u_prev = hist
    return pl.pallas_call(
        functools.partial(_conv_kernel, tc),
        grid=(bsz, t // tc),
        in_specs=[pl.BlockSpec((1, tc, CONV_CH), lambda i, j: (i, j, 0)),
                  prev,
                  pl.BlockSpec((1, CONV_PAD, CONV_CH), lambda i, j: (i, 0, 0)),
                  _full(w.shape), _full((1, CONV_CH)), _full((1, CONV_CH)),
                  _full((1, CONV_CH))],
        out_specs=pl.BlockSpec((1, tc, CONV_CH), lambda i, j: (i, j, 0)),
        out_shape=jax.ShapeDtypeStruct((bsz, t, CONV_CH), BF16),
        scratch_shapes=[pltpu.VMEM((CONV_CH // LANES, CONV_PAD + tc, LANES), F32),
                        pltpu.VMEM((CONV_CH // LANES, tc, LANES), F32)],
        compiler_params=_params("parallel", "arbitrary"),
        name="conv",
    )(u, u_prev, hist, w, b, g, beta)


def _gla_in_kernel(x_ref, g_ref, w_ref, wl_ref, wu_ref, bu_ref, q_ref, k_ref, v_ref, r_ref, la_ref):
    h = _rms_rows(x_ref[...], g_ref[...]).astype(BF16)
    p = _dot(h, w_ref[...])
    q_ref[...] = p[:, 0:C_DK] * (C_DK_HEAD ** -0.5)
    k_ref[...] = p[:, C_DK:2 * C_DK]
    v_ref[...] = p[:, 2 * C_DK:2 * C_DK + C_DV]
    r = p[:, 2 * C_DK + C_DV:2 * C_DK + 2 * C_DV]
    r_ref[...] = r * _sigmoid(r)
    low = _dot(h, wl_ref[...]).astype(BF16)
    z = _dot(low, wu_ref[...]) + bu_ref[...]
    log_sig = jnp.minimum(z, 0.0) - jnp.log1p(jnp.exp(-jnp.abs(z)))
    la_ref[...] = log_sig * (1.0 / GATE_TAU)


def _gla_in(x, g, w, wl, wu, bu, tm):
    n = x.shape[0]
    row = lambda width: pl.BlockSpec((tm, width), lambda i: (i, 0))
    sds = lambda width: jax.ShapeDtypeStruct((n, width), F32)
    return pl.pallas_call(
        _gla_in_kernel,
        grid=(n // tm,),
        in_specs=[row(D_MODEL), _full((1, D_MODEL)), _full(w.shape), _full(wl.shape),
                  _full(wu.shape), _full((1, C_DK))],
        out_specs=[row(C_DK), row(C_DK), row(C_DV), row(C_DV), row(C_DK)],
        out_shape=[sds(C_DK), sds(C_DK), sds(C_DV), sds(C_DV), sds(C_DK)],
        compiler_params=_params("parallel"),
        name="gla_in",
    )(x, g, w, wl, wu, bu)


def _gla_kernel(chunk, n_chunks, q_ref, k_ref, v_ref, r_ref, la_ref, s0_ref, gain_ref,
                o_ref, s_ref, st_ref):
    t = pl.program_id(1)

    @pl.when(t == 0)
    def _():
        for h in range(C_HEADS):
            st_ref[h] = s0_ref[0, h].T

    ri = lax.broadcasted_iota(jnp.int32, (chunk, chunk), 0)
    ci = lax.broadcasted_iota(jnp.int32, (chunk, chunk), 1)
    causal = ci <= ri
    tri = jnp.where(causal, 1.0, 0.0).astype(BF16)
    gain = gain_ref[...]

    def body(c, carry):
        rows = pl.ds(pl.multiple_of(c * chunk, chunk), chunk)
        la = la_ref[0, rows, :]
        la_hi = la.astype(BF16)
        rem = la - la_hi.astype(F32)
        la_mid = rem.astype(BF16)
        la_lo = (rem - la_mid.astype(F32)).astype(BF16)
        cum = _dot(tri, la_hi) + _dot(tri, la_mid) + _dot(tri, la_lo)
        mid = cum[chunk // 2 - 1:chunk // 2, :]
        last = cum[chunk - 1:chunk, :]
        eq = jnp.exp(cum - mid)
        ek = jnp.exp(mid - cum)
        q = q_ref[0, rows, :]
        k = k_ref[0, rows, :]
        qe = q * eq
        ke = k * ek
        q_in = qe.astype(BF16)
        k_in = ke.astype(BF16)
        q_st = (qe * jnp.exp(mid)).astype(BF16)
        k_st = (ke * jnp.exp(last - mid)).astype(BF16)
        dec = jnp.exp(last)
        for h in range(C_HEADS):
            ks = slice(h * C_DK_HEAD, (h + 1) * C_DK_HEAD)
            vs = slice(h * C_DV_HEAD, (h + 1) * C_DV_HEAD)
            vh = v_ref[0, rows, vs].astype(BF16)
            att = jnp.where(causal, _dot_nt(q_in[:, ks], k_in[:, ks]), 0.0)
            st = st_ref[h]
            o = _dot(att.astype(BF16), vh) + _dot_nt(q_st[:, ks], st.astype(BF16))
            st_ref[h] = st * dec[:, ks] + _dot_tn(vh, k_st[:, ks])
            y = o * lax.rsqrt(jnp.mean(o * o, axis=-1, keepdims=True) + EPS) * gain
            o_ref[0, rows, vs] = (y * r_ref[0, rows, vs]).astype(o_ref.dtype)
        return carry

    lax.fori_loop(0, n_chunks, body, 0, unroll=min(n_chunks, GLA_UNROLL))

    @pl.when(t == pl.num_programs(1) - 1)
    def _():
        for h in range(C_HEADS):
            s_ref[0, h] = st_ref[h].T


def _gla(q, k, v, r, la, s0, gain, chunk, tb):
    b, t, _ = q.shape
    seq = lambda width: pl.BlockSpec((1, tb, width), lambda i, j: (i, j, 0))
    state = pl.BlockSpec((1, C_HEADS, C_DK_HEAD, C_DV_HEAD), lambda i, j: (i, 0, 0, 0))
    return pl.pallas_call(
        functools.partial(_gla_kernel, chunk, tb // chunk),
        grid=(b, t // tb),
        in_specs=[seq(C_DK), seq(C_DK), seq(C_DV), seq(C_DV), seq(C_DK), state,
                  _full((1, C_DV_HEAD))],
        out_specs=[seq(C_DV), state],
        out_shape=[jax.ShapeDtypeStruct((b, t, C_DV), BF16),
                   jax.ShapeDtypeStruct(s0.shape, F32)],
        scratch_shapes=[pltpu.VMEM((C_HEADS, C_DV_HEAD, C_DK_HEAD), F32)],
        compiler_params=_params("parallel", "arbitrary"),
        name="gla",
    )(q, k, v, r, la, s0, gain)


GLA_CHUNK = 64
GLA_UNROLL = 2
SAMPLE_PAD = 16


def _trunk(x, past, P, tm, conv_tc, gla_tb):
    bsz, t, _ = x.shape
    n = bsz * t
    x = x.reshape(n, D_MODEL)
    outs = {}
    for layer in range(DEPTH):
        i = layer // 2
        x = _ffn(x, P['norm_ffn1'][layer], P['ffn1_w_in'][layer], P['ffn1_w_out'][layer], tm)
        if layer % 2 == 0:
            q, k, v, u = _even_in(x, P['norm_mix'][layer], P['ev_w_in'][i], P['ev_q_gain'][i],
                                  P['ev_k_gain'][i], P['head_ones'], tm)
            u3 = u.reshape(bsz, t, CONV_CH)
            if past is None:
                a = _attn_prompt(q, k, v, P['band_vectors'], bsz, t)
                keep = min(MAX_WINDOW, t)
                tail = lambda z: z.reshape(bsz, t, A_WIDTH)[:, t - keep:].reshape(
                    bsz, keep, A_HEADS, A_HEAD_DIM)
                new_k, new_v = tail(k), tail(v)
                hist = jnp.zeros((bsz, CONV_PAD, CONV_CH), F32)
                new_u = u3[:, t - (CONV_WIDTH - 1):]
            else:
                n_buf = past[0].shape[2]
                a, new_k, new_v = _attn_sample(
                    q.reshape(bsz, t, A_WIDTH), k.reshape(bsz, t, A_WIDTH), v.reshape(bsz, t, A_WIDTH),
                    past[0][i].reshape(bsz, n_buf, A_WIDTH), past[1][i].reshape(bsz, n_buf, A_WIDTH),
                    P['decode_vectors'])
                a = a.reshape(n, A_WIDTH)
                new_k = new_k.reshape(bsz, n_buf, A_HEADS, A_HEAD_DIM)
                new_v = new_v.reshape(bsz, n_buf, A_HEADS, A_HEAD_DIM)
                hist = jnp.pad(past[2][i], ((0, 0), (CONV_PAD - (CONV_WIDTH - 1), 0), (0, 0)))
                new_u = jnp.concatenate([past[2][i], u3], axis=1)[:, -(CONV_WIDTH - 1):]
            c = _conv(u3, hist, P['ev_conv_w'][i], P['ev_conv_b'][i], P['ev_conv_ln_g'][i],
                      P['ev_conv_ln_b'][i], conv_tc).reshape(n, CONV_CH)
            mixed, w_mix = [a, c], [P['ev_w_out_a'][i], P['ev_w_out_c'][i]]
            outs.setdefault('k', []).append(new_k)
            outs.setdefault('v', []).append(new_v)
            outs.setdefault('u', []).append(new_u)
        else:
            q, k, v, r, la = _gla_in(x, P['norm_mix'][layer], P['od_w_main'][i], P['od_w_low'][i],
                                     P['od_gate_w_up'][i], P['od_gate_b'][i], tm)
            if past is None:
                s0 = jnp.zeros((bsz, C_HEADS, C_DK_HEAD, C_DV_HEAD), F32)
                tp, chunk = t, GLA_CHUNK
            else:
                s0 = past[3][i]
                tp, chunk = SAMPLE_PAD, SAMPLE_PAD
            seq = lambda z: jnp.pad(z.reshape(bsz, t, -1), ((0, 0), (0, tp - t), (0, 0)))
            o, s = _gla(seq(q), seq(k), seq(v), seq(r), seq(la), s0, P['od_o_gain'][i], chunk,
                        min(gla_tb, tp))
            mixed, w_mix = [o[:, :t].reshape(n, C_DV)], [P['od_w_out'][i]]
            outs.setdefault('s', []).append(s)
        x = _ffn(x, P['norm_ffn2'][layer], P['ffn2_w_in'][layer], P['ffn2_w_out'][layer], tm,
                 mixed, w_mix)
    return (x.reshape(bsz, t, D_MODEL), jnp.stack(outs['k']), jnp.stack(outs['v']),
            jnp.stack(outs['u']), jnp.stack(outs['s']))


def kernel(x_prompt, x_sample, cache_k, cache_v, cache_conv, state_gla, rel_bias, norm_ffn1, ffn1_w_in, ffn1_w_out, norm_mix, norm_ffn2, ffn2_w_in, ffn2_w_out, ev_w_in, ev_q_gain, ev_k_gain, ev_conv_w, ev_conv_b, ev_conv_ln_g, ev_conv_ln_b, ev_w_out, od_w_in, od_gate_w_up, od_gate_b, od_o_gain, od_w_out):
    n_even = ev_w_in.shape[0]
    n_odd = od_w_in.shape[0]
    main = 2 * C_DK + 2 * C_DV
    head_ids = np.arange(MXU_DIM) // A_HEAD_DIM
    per = lambda n, f: [f(j) for j in range(n)]
    row = lambda z: z[None, :]
    P = {
        'norm_ffn1': per(DEPTH, lambda j: row(norm_ffn1[j])),
        'norm_mix': per(DEPTH, lambda j: row(norm_mix[j])),
        'norm_ffn2': per(DEPTH, lambda j: row(norm_ffn2[j])),
        'ffn1_w_in': per(DEPTH, lambda j: _weight_bf16(ffn1_w_in, j)),
        'ffn1_w_out': per(DEPTH, lambda j: _weight_bf16(ffn1_w_out, j)),
        'ffn2_w_in': per(DEPTH, lambda j: _weight_bf16(ffn2_w_in, j)),
        'ffn2_w_out': per(DEPTH, lambda j: _weight_bf16(ffn2_w_out, j)),
        'ev_w_in': per(n_even, lambda j: _weight_bf16(ev_w_in, j)),
        'ev_q_gain': per(n_even, lambda j: row(jnp.tile(ev_q_gain[j], A_HEADS))),
        'ev_k_gain': per(n_even, lambda j: row(jnp.tile(ev_k_gain[j], A_HEADS))),
        'head_ones': jnp.asarray(head_ids[:, None] == head_ids[None, :], BF16),
        'ev_conv_w': per(n_even, lambda j: jnp.broadcast_to(
            ev_conv_w[j][:, None, :], (CONV_WIDTH, SUBLANES, CONV_CH))),
        'ev_conv_b': per(n_even, lambda j: row(ev_conv_b[j])),
        'ev_conv_ln_g': per(n_even, lambda j: row(ev_conv_ln_g[j])),
        'ev_conv_ln_b': per(n_even, lambda j: row(ev_conv_ln_b[j])),
        'ev_w_out_a': per(n_even, lambda j: _weight_bf16(ev_w_out, j, 0, A_WIDTH)),
        'ev_w_out_c': per(n_even, lambda j: _weight_bf16(ev_w_out, j, A_WIDTH, CONV_CH)),
        'od_w_main': per(n_odd, lambda j: _weight_bf16(od_w_in, j, n_cols=main)),
        'od_w_low': per(n_odd, lambda j: jnp.pad(od_w_in[j, :, main:],
                                                 ((0, 0), (0, LANES - GATE_RANK))).astype(BF16)),
        'od_gate_w_up': per(n_odd, lambda j: jnp.pad(od_gate_w_up[j],
                                                     ((0, LANES - GATE_RANK), (0, 0))).astype(BF16)),
        'od_gate_b': per(n_odd, lambda j: row(od_gate_b[j])),
        'od_o_gain': per(n_odd, lambda j: row(od_o_gain[j])),
        'od_w_out': per(n_odd, lambda j: _weight_bf16(od_w_out, j)),
        'band_vectors': _band_vectors(rel_bias),
        'decode_vectors': _decode_vectors(rel_bias),
    }
    y_p, k_p, v_p, u_p, s_p = _trunk(x_prompt, None, P, tm=512, conv_tc=512, gla_tb=512)
    y_s, k_s, v_s, u_s, s_s = _trunk(x_sample, (cache_k, cache_v, cache_conv, state_gla), P,
                                     tm=256, conv_tc=x_sample.shape[1], gla_tb=SAMPLE_PAD)
    return (y_p, y_s, k_p, v_p, u_p, s_p, k_s, v_s, u_s, s_s)
```

```python
import functools

import numpy as np
import jax
import jax.numpy as jnp
from jax import lax
from jax.experimental import pallas as pl
from jax.experimental.pallas import tpu as pltpu

F32 = jnp.float32
BF16 = jnp.bfloat16

D_MODEL = 1024
DEPTH = 2
PAST_LEN = 16384
A_HEADS = 8
A_HEAD_DIM = 64
A_WIDTH = A_HEADS * A_HEAD_DIM
DILATED_GROUPS = ((128, 1), (512, 4), (2048, 16))
MAX_WINDOW = 2048
N_BUCKETS = 32
CONV_WIDTH = 31
CONV_CH = 512
C_HEADS = 4
C_DK = 512
C_DV = 1024
C_DK_HEAD = 128
C_DV_HEAD = 256
GATE_RANK = 16
GATE_TAU = 16.0
D_FF = 2816
EPS = 1e-6
NEG_INF = -1e30

LANES = 128
SUBLANES = 8
MXU_DIM = 256
WIN_KEYS = 128
ATT_BLOCK = 2048
ATT_UNROLL = 8
ATT_PHASES = 4
VMEM_LIMIT = 56 * 1024 * 1024


def _params(*sem):
    return pltpu.CompilerParams(dimension_semantics=sem, vmem_limit_bytes=VMEM_LIMIT)


def _dot(a, b):
    return jnp.dot(a, b, preferred_element_type=F32)


def _dot_nt(a, b):
    return lax.dot_general(a, b, (((1,), (1,)), ((), ())), preferred_element_type=F32)


def _dot_tn(a, b):
    return lax.dot_general(a, b, (((0,), (0,)), ((), ())), preferred_element_type=F32)


def _rms_rows(x, g):
    y = x * lax.rsqrt(jnp.mean(x * x, axis=-1, keepdims=True) + EPS)
    return y * g


def _sigmoid(x):
    return 1.0 / (1.0 + jnp.exp(-x))


def _full(shape):
    return pl.BlockSpec(shape, lambda *_: (0,) * len(shape), pipeline_mode=pl.Buffered(1))


FF_CHUNK = 256


def _ffn_kernel(n_pre, x_ref, *refs):
    m_refs, w_refs = refs[:n_pre], refs[n_pre:2 * n_pre]
    g_ref, wi_ref, wo_ref, o_ref, act_ref = refs[2 * n_pre:]
    x = x_ref[...]
    for m_ref, w_ref in zip(m_refs, w_refs):
        x = x + _dot(m_ref[...], w_ref[...])
    h = _rms_rows(x, g_ref[...]).astype(BF16)
    for c in range(D_FF // FF_CHUNK):
        lo = c * FF_CHUNK
        a = _dot(h, wi_ref[:, lo:lo + FF_CHUNK])
        b = _dot(h, wi_ref[:, D_FF + lo:D_FF + lo + FF_CHUNK])
        act_ref[:, lo:lo + FF_CHUNK] = (a * _sigmoid(a) * b).astype(BF16)
    o_ref[...] = x + 0.5 * _dot(act_ref[...], wo_ref[...])


def _ffn(x, g, wi, wo, tm, ms=(), ws=()):
    n = x.shape[0]
    row = lambda width: pl.BlockSpec((tm, width), lambda i: (i, 0))
    return pl.pallas_call(
        functools.partial(_ffn_kernel, len(ms)),
        grid=(n // tm,),
        in_specs=[row(D_MODEL)] + [row(m.shape[1]) for m in ms] + [_full(w.shape) for w in ws]
                 + [_full((1, D_MODEL)), _full((D_MODEL, 2 * D_FF)), _full((D_FF, D_MODEL))],
        out_specs=row(D_MODEL),
        out_shape=jax.ShapeDtypeStruct((n, D_MODEL), F32),
        scratch_shapes=[pltpu.VMEM((tm, D_FF), BF16)],
        compiler_params=_params("parallel"),
        name="ffn",
    )(x, *ms, *ws, g, wi, wo)


CAST_BLOCK_BYTES = 2 * 1024 * 1024
BF16_ROWS = 16


def _cast_kernel(w_ref, o_ref):
    o_ref[...] = w_ref[...].astype(o_ref.dtype)


def _weight_bf16(w, layer, row0=0, n_rows=None, n_cols=None):
    n_rows = w.shape[1] - row0 if n_rows is None else n_rows
    n_cols = w.shape[2] if n_cols is None else n_cols
    fits = [r for r in range(BF16_ROWS, n_rows + 1, BF16_ROWS)
            if n_rows % r == 0 and row0 % r == 0 and r * n_cols * 4 <= CAST_BLOCK_BYTES]
    rows = max(fits)
    first = row0 // rows
    return pl.pallas_call(
        _cast_kernel,
        grid=(n_rows // rows,),
        in_specs=[pl.BlockSpec((None, rows, n_cols), lambda i: (layer, first + i, 0))],
        out_specs=pl.BlockSpec((rows, n_cols), lambda i: (i, 0)),
        out_shape=jax.ShapeDtypeStruct((n_rows, n_cols), BF16),
        compiler_params=_params("parallel"),
        name="weight_bf16",
    )(w)


def _head_norm(z, gain, bd):
    zz = z * z
    hi = zz.astype(BF16)
    lo = (zz - hi.astype(F32)).astype(BF16)
    width = bd.shape[0]
    ss = jnp.concatenate(
        [_dot(hi[:, c:c + width], bd) + _dot(lo[:, c:c + width], bd)
         for c in range(0, z.shape[1], width)], axis=1)
    return z * lax.rsqrt(ss * (1.0 / A_HEAD_DIM) + EPS) * gain


def _even_in_kernel(x_ref, g_ref, w_ref, qg_ref, kg_ref, bd_ref, q_ref, k_ref, v_ref, u_ref):
    h = _rms_rows(x_ref[...], g_ref[...]).astype(BF16)
    p = _dot(h, w_ref[...])
    bd = bd_ref[...]
    q_ref[...] = _head_norm(p[:, 0:A_WIDTH], qg_ref[...], bd) * (A_HEAD_DIM ** -0.5)
    k_ref[...] = _head_norm(p[:, A_WIDTH:2 * A_WIDTH], kg_ref[...], bd)
    v_ref[...] = p[:, 2 * A_WIDTH:3 * A_WIDTH]
    gv = p[:, 3 * A_WIDTH:3 * A_WIDTH + CONV_CH]
    gg = p[:, 3 * A_WIDTH + CONV_CH:3 * A_WIDTH + 2 * CONV_CH]
    u_ref[...] = gv * _sigmoid(gg)


def _even_in(x, g, w, qg, kg, bd, tm):
    n = x.shape[0]
    row = lambda width: pl.BlockSpec((tm, width), lambda i: (i, 0))
    out = jax.ShapeDtypeStruct((n, A_WIDTH), F32)
    return pl.pallas_call(
        _even_in_kernel,
        grid=(n // tm,),
        in_specs=[row(D_MODEL), _full((1, D_MODEL)), _full(w.shape), _full((1, A_WIDTH)),
                  _full((1, A_WIDTH)), _full(bd.shape)],
        out_specs=[row(A_WIDTH)] * 4,
        out_shape=[out] * 4,
        compiler_params=_params("parallel"),
        name="even_in",
    )(x, g, w, qg, kg, bd)


def _t5_bucket(dist):
    max_exact = N_BUCKETS // 2
    d = np.asarray(dist, dtype=np.int32)
    df = np.maximum(d, 1).astype(np.float32)
    large = max_exact + (np.log(df / max_exact) / np.log(MAX_WINDOW / max_exact)
                         * (N_BUCKETS - max_exact)).astype(np.int32)
    large = np.minimum(large, N_BUCKETS - 1)
    return np.where(d < max_exact, d, large).astype(np.int32)


def _select_bias(rel_bias, dist, valid):
    onehot = (_t5_bucket(dist)[None, :] == np.arange(N_BUCKETS)[:, None]) & valid[None, :]
    picked = jnp.einsum('bh,bc->hc', rel_bias, jnp.asarray(onehot, F32),
                        precision=lax.Precision.HIGHEST)
    return picked + jnp.asarray(np.where(valid, 0.0, NEG_INF), F32)[None, :]


def _band_vectors(rel_bias):
    c = np.arange(2 * WIN_KEYS)
    valid = c <= WIN_KEYS
    vecs = [_select_bias(rel_bias, np.where(valid, (WIN_KEYS - c) * dil, 0), valid)
            for _, dil in DILATED_GROUPS]
    return jnp.stack(vecs).reshape(len(DILATED_GROUPS), A_HEADS // 2, 2, 2 * WIN_KEYS)


def _attn_prompt_kernel(q_ref, kp_ref, kc_ref, vp_ref, vc_ref, vec_ref, o_ref,
                        knat, vnat, k4, v4, q4, og1, lg1, og4, lg4, tab_ref):
    blk = pl.program_id(2)
    n_groups = len(DILATED_GROUPS)
    nph = ATT_PHASES
    per = ATT_BLOCK // nph

    @pl.when(blk == 0)
    def _():
        col = lax.broadcasted_iota(jnp.int32, (WIN_KEYS, 2 * WIN_KEYS), 1)
        for g in range(n_groups):
            for hh in range(2):
                vec = jnp.broadcast_to(vec_ref[g, 0, hh:hh + 1, :], (WIN_KEYS, 2 * WIN_KEYS))
                band = pltpu.roll(vec, 0, 1, stride=1, stride_axis=0)
                rows = slice(hh * WIN_KEYS, (hh + 1) * WIN_KEYS)
                tab_ref[g, rows, :] = band
                tab_ref[n_groups + g, rows, :] = jnp.where(col >= WIN_KEYS, band, NEG_INF)

    knat[0:WIN_KEYS, :] = kp_ref[ATT_BLOCK - WIN_KEYS:, :]
    knat[WIN_KEYS:, :] = kc_ref[...]
    vnat[0:WIN_KEYS, :] = vp_ref[ATT_BLOCK - WIN_KEYS:, :]
    vnat[WIN_KEYS:, :] = vc_ref[...]
    for r in range(nph):
        phase = pl.ds(r, per, stride=nph)
        k4[r, 0:per, :] = kp_ref[phase, :]
        k4[r, per:, :] = kc_ref[phase, :]
        v4[r, 0:per, :] = vp_ref[phase, :]
        v4[r, per:, :] = vc_ref[phase, :]
        q4[r] = q_ref[phase, :]
    first = blk == 0
    lane = lax.broadcasted_iota(jnp.int32, (WIN_KEYS, LANES), 1)
    low = lane < A_HEAD_DIM
    ones = jnp.ones((2 * WIN_KEYS, LANES), BF16)

    def block(qs, kk, vv, tab):
        qs = qs.astype(BF16)
        zero = jnp.zeros_like(qs)
        qst = jnp.concatenate([jnp.where(low, qs, zero), jnp.where(low, zero, qs)], axis=0)
        s = _dot_nt(qst, kk.astype(BF16)) + tab
        mx = jnp.max(s, axis=-1, keepdims=True)
        p = jnp.exp(s - mx).astype(BF16)
        r = _dot(p, jnp.concatenate([vv.astype(BF16), ones], axis=1))
        o2 = jnp.where(low, r[0:WIN_KEYS, 0:LANES], r[WIN_KEYS:, 0:LANES])
        l2 = jnp.where(low, r[0:WIN_KEYS, LANES:], r[WIN_KEYS:, LANES:])
        m2 = jnp.where(low, jnp.broadcast_to(mx[0:WIN_KEYS], (WIN_KEYS, LANES)),
                       jnp.broadcast_to(mx[WIN_KEYS:], (WIN_KEYS, LANES)))
        return o2 / l2, m2 + jnp.log(l2)

    def table(g, at_start):
        return tab_ref[jnp.where(jnp.logical_and(at_start, first), n_groups + g, g)]

    def body1(sub, carry):
        i0 = pl.multiple_of(sub * WIN_KEYS, WIN_KEYS)
        o, l = block(q_ref[pl.ds(i0, WIN_KEYS), :], knat[pl.ds(i0, 2 * WIN_KEYS), :],
                     vnat[pl.ds(i0, 2 * WIN_KEYS), :], table(0, sub == 0))
        og1[pl.ds(i0, WIN_KEYS), :] = o
        lg1[pl.ds(i0, WIN_KEYS), :] = l
        return carry

    def body2(pb, carry):
        sub = pb // nph
        r = pb - sub * nph
        i0 = pl.multiple_of(sub * WIN_KEYS, WIN_KEYS)
        keys = pl.ds(i0 + (per - WIN_KEYS), 2 * WIN_KEYS)
        o, l = block(q4[r, pl.ds(i0, WIN_KEYS), :], k4[r, keys, :], v4[r, keys, :],
                     table(1, sub == 0))
        og4[0, r, pl.ds(i0, WIN_KEYS), :] = o
        lg4[0, r, pl.ds(i0, WIN_KEYS), :] = l
        return carry

    def body3(pb, carry):
        a = pb // nph
        r = pb - a * nph
        rows = pl.ds(a, WIN_KEYS, stride=nph)
        keys = pl.ds(a, 2 * WIN_KEYS, stride=nph)
        o, l = block(q4[r, rows, :], k4[r, keys, :], v4[r, keys, :], table(2, True))
        og4[1, r, rows, :] = o
        lg4[1, r, rows, :] = l
        return carry

    n_blocks = ATT_BLOCK // WIN_KEYS
    for body in (body1, body2, body3):
        lax.fori_loop(0, n_blocks, body, 0, unroll=ATT_UNROLL)

    for r in range(nph):
        phase = pl.ds(r, per, stride=nph)
        la, lb, lc = lg1[phase, :], lg4[0, r], lg4[1, r]
        mx = jnp.maximum(jnp.maximum(la, lb), lc)
        wa, wb, wc = jnp.exp(la - mx), jnp.exp(lb - mx), jnp.exp(lc - mx)
        og1[phase, :] = (wa * og1[phase, :] + wb * og4[0, r] + wc * og4[1, r]) / (wa + wb + wc)
    o_ref[...] = og1[...].astype(o_ref.dtype)


def _attn_prompt(q, k, v, vecs, batch, seq):
    assert [d for _, d in DILATED_GROUPS] == [1, ATT_PHASES, ATT_PHASES ** 2]
    nb = seq // ATT_BLOCK
    n_groups = len(DILATED_GROUPS)
    per = ATT_BLOCK // ATT_PHASES
    cur = lambda b, p, t: (b * nb + t, p)
    prev = lambda b, p, t: (b * nb + jnp.maximum(t - 1, 0), p)
    blk = lambda imap: pl.BlockSpec((ATT_BLOCK, LANES), imap)
    vmem = lambda *shape: pltpu.VMEM(shape, F32)
    return pl.pallas_call(
        _attn_prompt_kernel,
        grid=(batch, A_HEADS // 2, nb),
        in_specs=[blk(cur), blk(prev), blk(cur), blk(prev), blk(cur),
                  pl.BlockSpec((n_groups, 1, 2, 2 * WIN_KEYS), lambda b, p, t: (0, p, 0, 0))],
        out_specs=blk(cur),
        out_shape=jax.ShapeDtypeStruct((batch * seq, A_WIDTH), BF16),
        scratch_shapes=[vmem(WIN_KEYS + ATT_BLOCK, LANES), vmem(WIN_KEYS + ATT_BLOCK, LANES),
                        vmem(ATT_PHASES, 2 * per, LANES), vmem(ATT_PHASES, 2 * per, LANES),
                        vmem(ATT_PHASES, per, LANES),
                        vmem(ATT_BLOCK, LANES), vmem(ATT_BLOCK, LANES),
                        vmem(2, ATT_PHASES, per, LANES), vmem(2, ATT_PHASES, per, LANES),
                        vmem(2 * n_groups, 2 * WIN_KEYS, 2 * WIN_KEYS)],
        compiler_params=_params("arbitrary", "arbitrary", "arbitrary"),
        name="attn_prompt",
    )(q, k, k, v, v, vecs)


DECODE_TAIL = LANES


def _decode_vectors(rel_bias):
    c = np.arange(MAX_WINDOW + DECODE_TAIL)
    dist = MAX_WINDOW - c
    cnt = np.zeros(c.shape, np.float32)
    for window, dil in DILATED_GROUPS:
        cnt += ((dist >= 0) & (dist <= window) & (dist % dil == 0)).astype(np.float32)
    bias = _select_bias(rel_bias, np.clip(dist, 0, MAX_WINDOW), cnt > 0)
    return bias, jnp.asarray(cnt[None, :])


def _attn_sample_kernel(n_new, q_ref, kn_ref, vn_ref, kc_ref, vc_ref, bvec_ref, cvec_ref,
                        o_ref, ko_ref, vo_ref, kall, vall, bias_ref, cnt_ref):
    n_buf = MAX_WINDOW
    rows = A_HEADS * n_new

    @pl.when(pl.program_id(0) == 0)
    def _():
        shape = (n_new, n_buf + DECODE_TAIL)
        cnt = pltpu.roll(jnp.broadcast_to(cvec_ref[...], shape), 0, 1, stride=1, stride_axis=0)
        for h in range(A_HEADS):
            bvec = jnp.broadcast_to(bvec_ref[h:h + 1, :], shape)
            bias_ref[h * n_new:(h + 1) * n_new, :] = pltpu.roll(bvec, 0, 1, stride=1, stride_axis=0)
            cnt_ref[h * n_new:(h + 1) * n_new, :] = cnt

    kc = kc_ref[0]
    vc = vc_ref[0]
    kn = kn_ref[0]
    vn = vn_ref[0]
    ko_ref[0, 0:n_buf - n_new, :] = kc[n_new:, :]
    ko_ref[0, n_buf - n_new:, :] = kn
    vo_ref[0, 0:n_buf - n_new, :] = vc[n_new:, :]
    vo_ref[0, n_buf - n_new:, :] = vn
    unused = jnp.zeros((DECODE_TAIL - n_new, A_WIDTH), F32)
    kall[0:n_buf, :] = kc.astype(BF16)
    kall[n_buf:, :] = jnp.concatenate([kn, unused], axis=0).astype(BF16)
    vall[0:n_buf, :] = vc.astype(BF16)
    vall[n_buf:, :] = jnp.concatenate([vn, unused], axis=0).astype(BF16)

    q = q_ref[0]
    row_head = lax.broadcasted_iota(jnp.int32, (A_HEADS, n_new, A_WIDTH), 0).reshape(rows, A_WIDTH)
    lane = lax.broadcasted_iota(jnp.int32, (rows, A_WIDTH), 1)
    own = jnp.logical_and(lane >= row_head * A_HEAD_DIM, lane < (row_head + 1) * A_HEAD_DIM)
    qblk = jnp.where(own, jnp.concatenate([q] * A_HEADS, axis=0), 0.0).astype(BF16)
    s = _dot_nt(qblk, kall[...]) + bias_ref[...]
    mx = jnp.max(s, axis=-1, keepdims=True)
    p = cnt_ref[...] * jnp.exp(s - mx)
    den = jnp.sum(p, axis=-1, keepdims=True)
    acc = jnp.where(own, _dot(p.astype(BF16), vall[...]) / den, 0.0)
    out = acc[0:n_new]
    for h in range(1, A_HEADS):
        out = out + acc[h * n_new:(h + 1) * n_new]
    o_ref[0] = out.astype(o_ref.dtype)


def _attn_sample(q, k_new, v_new, cache_k, cache_v, layer, tables):
    b, n_new, _ = q.shape
    n_buf = cache_k.shape[2]
    rows = A_HEADS * n_new
    new = pl.BlockSpec((1, n_new, A_WIDTH), lambda i: (i, 0, 0))
    buf = pl.BlockSpec((1, n_buf, A_WIDTH), lambda i: (i, 0, 0))
    past = pl.BlockSpec((None, 1, n_buf, A_WIDTH), lambda i: (layer, i, 0, 0))
    cols = n_buf + DECODE_TAIL
    return pl.pallas_call(
        functools.partial(_attn_sample_kernel, n_new),
        grid=(b,),
        in_specs=[new, new, new, past, past, _full((A_HEADS, cols)), _full((1, cols))],
        out_specs=[new, buf, buf],
        out_shape=[jax.ShapeDtypeStruct((b, n_new, A_WIDTH), BF16),
                   jax.ShapeDtypeStruct((b, n_buf, A_WIDTH), F32),
                   jax.ShapeDtypeStruct((b, n_buf, A_WIDTH), F32)],
        scratch_shapes=[pltpu.VMEM((cols, A_WIDTH), BF16),
                        pltpu.VMEM((cols, A_WIDTH), BF16),
                        pltpu.VMEM((rows, cols), F32),
                        pltpu.VMEM((rows, cols), F32)],
        compiler_params=_params("arbitrary"),
        name="attn_sample",
    )(q, k_new, v_new, cache_k, cache_v, *tables)


CONV_PAD = 32
CONV_ROWS = 32
CONV_UNROLL = 2


def _conv_kernel(tc, u_ref, up_ref, hist_ref, w_ref, b_ref, g_ref, beta_ref, o_ref, win, stage):
    t = pl.program_id(1)
    n_slab = CONV_CH // LANES
    slab = lambda c: slice(c * LANES, (c + 1) * LANES)
    for c in range(n_slab):
        win[c, CONV_PAD:CONV_PAD + tc, :] = u_ref[0, :, slab(c)]

    @pl.when(t == 0)
    def _():
        for c in range(n_slab):
            win[c, 0:CONV_PAD, :] = hist_ref[0, :, slab(c)]

    @pl.when(t > 0)
    def _():
        for c in range(n_slab):
            win[c, 0:CONV_PAD, :] = up_ref[0, :, slab(c)]

    off = CONV_PAD - (CONV_WIDTH - 1)
    rc = min(CONV_ROWS, tc)
    half = rc // 2

    def tap(k, c):
        w = w_ref[k, :, slab(c)]
        if half < SUBLANES:
            return w[0:half]
        return jnp.concatenate([w] * (half // SUBLANES), axis=0)

    def body(j, carry):
        r0 = j * rc
        for c in range(n_slab):
            for par in range(2):
                acc = jnp.zeros((half, LANES), F32) + b_ref[:, slab(c)]
                for k in range(CONV_WIDTH):
                    rows = pl.ds(r0 + off + k + par, half, stride=2)
                    acc = acc + win[c, rows, :] * tap(k, c)
                stage[c, pl.ds(r0 + par, half, stride=2), :] = acc
        return carry

    if tc == rc:
        body(0, 0)
    else:
        lax.fori_loop(0, tc // rc, body, 0, unroll=CONV_UNROLL)
    y = jnp.concatenate([stage[c] for c in range(n_slab)], axis=1)
    xc = y - jnp.mean(y, axis=-1, keepdims=True)
    y = xc * lax.rsqrt(jnp.mean(xc * xc, axis=-1, keepdims=True) + EPS)
    y = y * g_ref[...] + beta_ref[...]
    o_ref[0] = (y * _sigmoid(y)).astype(o_ref.dtype)


def _conv(u, hist, w, b, g, beta, tc):
    bsz, t, _ = u.shape
    per = tc // CONV_PAD
    if t >= CONV_PAD:
        prev = pl.BlockSpec((1, CONV_PAD, CONV_CH), lambda i, j: (i, jnp.maximum(j * per - 1, 0), 0))
        u_prev = u
    else:
        prev = pl.BlockSpec((1, CONV_PAD, CONV_CH), lambda i, j: (i, 0, 0))
        u_prev = hist
    return pl.pallas_call(
        functools.partial(_conv_kernel, tc),
        grid=(bsz, t // tc),
        in_specs=[pl.BlockSpec((1, tc, CONV_CH), lambda i, j: (i, j, 0)),
                  prev,
                  pl.BlockSpec((1, CONV_PAD, CONV_CH), lambda i, j: (i, 0, 0)),
                  _full(w.shape), _full((1, CONV_CH)), _full((1, CONV_CH)),
                  _full((1, CONV_CH))],
        out_specs=pl.BlockSpec((1, tc, CONV_CH), lambda i, j: (i, j, 0)),
        out_shape=jax.ShapeDtypeStruct((bsz, t, CONV_CH), BF16),
        scratch_shapes=[pltpu.VMEM((CONV_CH // LANES, CONV_PAD + tc, LANES), F32),
                        pltpu.VMEM((CONV_CH // LANES, tc, LANES), F32)],
        compiler_params=_params("parallel", "arbitrary"),
        name="conv",
    )(u, u_prev, hist, w, b, g, beta)


def _gla_in_kernel(x_ref, g_ref, w_ref, wl_ref, wu_ref, bu_ref, q_ref, k_ref, v_ref, r_ref, la_ref):
    h = _rms_rows(x_ref[...], g_ref[...]).astype(BF16)
    p = _dot(h, w_ref[...])
    q_ref[...] = p[:, 0:C_DK] * (C_DK_HEAD ** -0.5)
    k_ref[...] = p[:, C_DK:2 * C_DK]
    v_ref[...] = p[:, 2 * C_DK:2 * C_DK + C_DV]
    r = p[:, 2 * C_DK + C_DV:2 * C_DK + 2 * C_DV]
    r_ref[...] = r * _sigmoid(r)
    low = _dot(h, wl_ref[...]).astype(BF16)
    z = _dot(low, wu_ref[...]) + bu_ref[...]
    log_sig = jnp.minimum(z, 0.0) - jnp.log1p(jnp.exp(-jnp.abs(z)))
    la_ref[...] = log_sig * (1.0 / GATE_TAU)


def _gla_in(x, g, w, wl, wu, bu, tm):
    n = x.shape[0]
    row = lambda width: pl.BlockSpec((tm, width), lambda i: (i, 0))
    sds = lambda width: jax.ShapeDtypeStruct((n, width), F32)
    return pl.pallas_call(
        _gla_in_kernel,
        grid=(n // tm,),
        in_specs=[row(D_MODEL), _full((1, D_MODEL)), _full(w.shape), _full(wl.shape),
                  _full(wu.shape), _full((1, C_DK))],
        out_specs=[row(C_DK), row(C_DK), row(C_DV), row(C_DV), row(C_DK)],
        out_shape=[sds(C_DK), sds(C_DK), sds(C_DV), sds(C_DV), sds(C_DK)],
        compiler_params=_params("parallel"),
        name="gla_in",
    )(x, g, w, wl, wu, bu)


def _gla_kernel(chunk, n_chunks, q_ref, k_ref, v_ref, r_ref, la_ref, s0_ref, gain_ref,
                o_ref, s_ref, st_ref):
    t = pl.program_id(1)

    @pl.when(t == 0)
    def _():
        for h in range(C_HEADS):
            st_ref[h] = s0_ref[0, h].T

    half = chunk // 2
    ri = lax.broadcasted_iota(jnp.int32, (chunk, chunk), 0)
    ci = lax.broadcasted_iota(jnp.int32, (chunk, chunk), 1)
    causal = ci <= ri
    tri = jnp.where(causal, 1.0, 0.0).astype(BF16)
    cross = jnp.logical_and(ri >= half, ci < half)
    same_half = jnp.logical_and(causal, jnp.logical_not(cross))
    in_first = lax.broadcasted_iota(jnp.int32, (chunk, C_DK), 0) < half
    gain = gain_ref[...]

    def body(c, carry):
        rows = pl.ds(pl.multiple_of(c * chunk, chunk), chunk)
        la = la_ref[0, rows, :]
        la_hi = la.astype(BF16)
        rem = la - la_hi.astype(F32)
        la_mid = rem.astype(BF16)
        la_lo = (rem - la_mid.astype(F32)).astype(BF16)
        cum = _dot(tri, la_hi) + _dot(tri, la_mid) + _dot(tri, la_lo)
        row = lambda i: cum[i:i + 1, :]
        last = row(chunk - 1)
        edge = row(half - 1)
        mid = jnp.where(in_first, row(half // 2 - 1), row(half + half // 2 - 1))
        q = q_ref[0, rows, :]
        k = k_ref[0, rows, :]
        q_in = (q * jnp.exp(cum - mid)).astype(BF16)
        k_in = (k * jnp.exp(mid - cum)).astype(BF16)
        q_x = (q[half:] * jnp.exp(cum[half:] - edge)).astype(BF16)
        k_x = (k[:half] * jnp.exp(edge - cum[:half])).astype(BF16)
        q_st = (q * jnp.exp(cum)).astype(BF16)
        k_st = (k * jnp.exp(last - cum)).astype(BF16)
        dec = jnp.exp(last)
        for h in range(C_HEADS):
            ks = slice(h * C_DK_HEAD, (h + 1) * C_DK_HEAD)
            vs = slice(h * C_DV_HEAD, (h + 1) * C_DV_HEAD)
            vh = v_ref[0, rows, vs].astype(BF16)
            att_x = jnp.pad(_dot_nt(q_x[:, ks], k_x[:, ks]), ((half, 0), (0, half)))
            att = jnp.where(same_half, _dot_nt(q_in[:, ks], k_in[:, ks]), att_x)
            st = st_ref[h]
            o = _dot(att.astype(BF16), vh) + _dot_nt(q_st[:, ks], st.astype(BF16))
            st_ref[h] = st * dec[:, ks] + _dot_tn(vh, k_st[:, ks])
            y = o * lax.rsqrt(jnp.mean(o * o, axis=-1, keepdims=True) + EPS) * gain
            o_ref[0, rows, vs] = (y * r_ref[0, rows, vs]).astype(o_ref.dtype)
        return carry

    lax.fori_loop(0, n_chunks, body, 0, unroll=min(n_chunks, GLA_UNROLL))

    @pl.when(t == pl.num_programs(1) - 1)
    def _():
        for h in range(C_HEADS):
            s_ref[0, h] = st_ref[h].T


def _gla(q, k, v, r, la, s0, gain, chunk, tb):
    b, t, _ = q.shape
    seq = lambda width: pl.BlockSpec((1, tb, width), lambda i, j: (i, j, 0))
    state = pl.BlockSpec((1, C_HEADS, C_DK_HEAD, C_DV_HEAD), lambda i, j: (i, 0, 0, 0))
    return pl.pallas_call(
        functools.partial(_gla_kernel, chunk, tb // chunk),
        grid=(b, t // tb),
        in_specs=[seq(C_DK), seq(C_DK), seq(C_DV), seq(C_DV), seq(C_DK), state,
                  _full((1, C_DV_HEAD))],
        out_specs=[seq(C_DV), state],
        out_shape=[jax.ShapeDtypeStruct((b, t, C_DV), BF16),
                   jax.ShapeDtypeStruct(s0.shape, F32)],
        scratch_shapes=[pltpu.VMEM((C_HEADS, C_DV_HEAD, C_DK_HEAD), F32)],
        compiler_params=_params("parallel", "arbitrary"),
        name="gla",
    )(q, k, v, r, la, s0, gain)


GLA_CHUNK = 128
GLA_UNROLL = 2
SAMPLE_PAD = 16


def _trunk(x, past, P, tm, conv_tc, gla_tb):
    bsz, t, _ = x.shape
    n = bsz * t
    x = x.reshape(n, D_MODEL)
    outs = {}
    for layer in range(DEPTH):
        i = layer // 2
        x = _ffn(x, P['norm_ffn1'][layer], P['ffn1_w_in'][layer], P['ffn1_w_out'][layer], tm)
        if layer % 2 == 0:
            q, k, v, u = _even_in(x, P['norm_mix'][layer], P['ev_w_in'][i], P['ev_q_gain'][i],
                                  P['ev_k_gain'][i], P['head_ones'], tm)
            u3 = u.reshape(bsz, t, CONV_CH)
            if past is None:
                a = _attn_prompt(q, k, v, P['band_vectors'], bsz, t)
                keep = min(MAX_WINDOW, t)
                tail = lambda z: z.reshape(bsz, t, A_WIDTH)[:, t - keep:].reshape(
                    bsz, keep, A_HEADS, A_HEAD_DIM)
                new_k, new_v = tail(k), tail(v)
                hist = jnp.zeros((bsz, CONV_PAD, CONV_CH), F32)
                new_u = u3[:, t - (CONV_WIDTH - 1):]
            else:
                n_buf = past[0].shape[2]
                flat = lambda z: z.reshape(z.shape[0], bsz, n_buf, A_WIDTH)
                a, new_k, new_v = _attn_sample(
                    q.reshape(bsz, t, A_WIDTH), k.reshape(bsz, t, A_WIDTH), v.reshape(bsz, t, A_WIDTH),
                    flat(past[0]), flat(past[1]), i, P['decode_vectors'])
                a = a.reshape(n, A_WIDTH)
                new_k = new_k.reshape(bsz, n_buf, A_HEADS, A_HEAD_DIM)
                new_v = new_v.reshape(bsz, n_buf, A_HEADS, A_HEAD_DIM)
                hist = jnp.pad(past[2][i], ((0, 0), (CONV_PAD - (CONV_WIDTH - 1), 0), (0, 0)))
                new_u = jnp.concatenate([past[2][i], u3], axis=1)[:, -(CONV_WIDTH - 1):]
            c = _conv(u3, hist, P['ev_conv_w'][i], P['ev_conv_b'][i], P['ev_conv_ln_g'][i],
                      P['ev_conv_ln_b'][i], conv_tc).reshape(n, CONV_CH)
            mixed, w_mix = [a, c], [P['ev_w_out_a'][i], P['ev_w_out_c'][i]]
            outs.setdefault('k', []).append(new_k)
            outs.setdefault('v', []).append(new_v)
            outs.setdefault('u', []).append(new_u)
        else:
            q, k, v, r, la = _gla_in(x, P['norm_mix'][layer], P['od_w_main'][i], P['od_w_low'][i],
                                     P['od_gate_w_up'][i], P['od_gate_b'][i], tm)
            if past is None:
                s0 = jnp.zeros((bsz, C_HEADS, C_DK_HEAD, C_DV_HEAD), F32)
                tp, chunk = t, GLA_CHUNK
            else:
                s0 = past[3][i]
                tp, chunk = SAMPLE_PAD, SAMPLE_PAD
            seq = lambda z: jnp.pad(z.reshape(bsz, t, -1), ((0, 0), (0, tp - t), (0, 0)))
            o, s = _gla(seq(q), seq(k), seq(v), seq(r), seq(la), s0, P['od_o_gain'][i], chunk,
                        min(gla_tb, tp))
            mixed, w_mix = [o[:, :t].reshape(n, C_DV)], [P['od_w_out'][i]]
            outs.setdefault('s', []).append(s)
        x = _ffn(x, P['norm_ffn2'][layer], P['ffn2_w_in'][layer], P['ffn2_w_out'][layer], tm,
                 mixed, w_mix)
    return (x.reshape(bsz, t, D_MODEL), jnp.stack(outs['k']), jnp.stack(outs['v']),
            jnp.stack(outs['u']), jnp.stack(outs['s']))


def kernel(x_prompt, x_sample, cache_k, cache_v, cache_conv, state_gla, rel_bias, norm_ffn1, ffn1_w_in, ffn1_w_out, norm_mix, norm_ffn2, ffn2_w_in, ffn2_w_out, ev_w_in, ev_q_gain, ev_k_gain, ev_conv_w, ev_conv_b, ev_conv_ln_g, ev_conv_ln_b, ev_w_out, od_w_in, od_gate_w_up, od_gate_b, od_o_gain, od_w_out):
    n_even = ev_w_in.shape[0]
    n_odd = od_w_in.shape[0]
    main = 2 * C_DK + 2 * C_DV
    head_ids = np.arange(MXU_DIM) // A_HEAD_DIM
    per = lambda n, f: [f(j) for j in range(n)]
    row = lambda z: z[None, :]
    P = {
        'norm_ffn1': per(DEPTH, lambda j: row(norm_ffn1[j])),
        'norm_mix': per(DEPTH, lambda j: row(norm_mix[j])),
        'norm_ffn2': per(DEPTH, lambda j: row(norm_ffn2[j])),
        'ffn1_w_in': per(DEPTH, lambda j: _weight_bf16(ffn1_w_in, j)),
        'ffn1_w_out': per(DEPTH, lambda j: _weight_bf16(ffn1_w_out, j)),
        'ffn2_w_in': per(DEPTH, lambda j: _weight_bf16(ffn2_w_in, j)),
        'ffn2_w_out': per(DEPTH, lambda j: _weight_bf16(ffn2_w_out, j)),
        'ev_w_in': per(n_even, lambda j: _weight_bf16(ev_w_in, j)),
        'ev_q_gain': per(n_even, lambda j: row(jnp.tile(ev_q_gain[j], A_HEADS))),
        'ev_k_gain': per(n_even, lambda j: row(jnp.tile(ev_k_gain[j], A_HEADS))),
        'head_ones': jnp.asarray(head_ids[:, None] == head_ids[None, :], BF16),
        'ev_conv_w': per(n_even, lambda j: jnp.broadcast_to(
            ev_conv_w[j][:, None, :], (CONV_WIDTH, SUBLANES, CONV_CH))),
        'ev_conv_b': per(n_even, lambda j: row(ev_conv_b[j])),
        'ev_conv_ln_g': per(n_even, lambda j: row(ev_conv_ln_g[j])),
        'ev_conv_ln_b': per(n_even, lambda j: row(ev_conv_ln_b[j])),
        'ev_w_out_a': per(n_even, lambda j: _weight_bf16(ev_w_out, j, 0, A_WIDTH)),
        'ev_w_out_c': per(n_even, lambda j: _weight_bf16(ev_w_out, j, A_WIDTH, CONV_CH)),
        'od_w_main': per(n_odd, lambda j: _weight_bf16(od_w_in, j, n_cols=main)),
        'od_w_low': per(n_odd, lambda j: jnp.pad(od_w_in[j, :, main:],
                                                 ((0, 0), (0, LANES - GATE_RANK))).astype(BF16)),
        'od_gate_w_up': per(n_odd, lambda j: jnp.pad(od_gate_w_up[j],
                                                     ((0, LANES - GATE_RANK), (0, 0))).astype(BF16)),
        'od_gate_b': per(n_odd, lambda j: row(od_gate_b[j])),
        'od_o_gain': per(n_odd, lambda j: row(od_o_gain[j])),
        'od_w_out': per(n_odd, lambda j: _weight_bf16(od_w_out, j)),
        'band_vectors': _band_vectors(rel_bias),
        'decode_vectors': _decode_vectors(rel_bias),
    }
    y_p, k_p, v_p, u_p, s_p = _trunk(x_prompt, None, P, tm=512, conv_tc=512, gla_tb=512)
    y_s, k_s, v_s, u_s, s_s = _trunk(x_sample, (cache_k, cache_v, cache_conv, state_gla), P,
                                     tm=256, conv_tc=x_sample.shape[1], gla_tb=SAMPLE_PAD)
    return (y_p, y_s, k_p, v_p, u_p, s_p, k_s, v_s, u_s, s_s)
```

```python
import functools

import numpy as np
import jax
import jax.numpy as jnp
from jax import lax
from jax.experimental import pallas as pl
from jax.experimental.pallas import tpu as pltpu

F32 = jnp.float32
BF16 = jnp.bfloat16

D_MODEL = 1024
DEPTH = 2
PAST_LEN = 16384
A_HEADS = 8
A_HEAD_DIM = 64
A_WIDTH = A_HEADS * A_HEAD_DIM
DILATED_GROUPS = ((128, 1), (512, 4), (2048, 16))
MAX_WINDOW = 2048
N_BUCKETS = 32
CONV_WIDTH = 31
CONV_CH = 512
C_HEADS = 4
C_DK = 512
C_DV = 1024
C_DK_HEAD = 128
C_DV_HEAD = 256
GATE_RANK = 16
GATE_TAU = 16.0
D_FF = 2816
EPS = 1e-6
NEG_INF = -1e30

LANES = 128
SUBLANES = 8
MXU_DIM = 256
WIN_KEYS = 128
ATT_BLOCK = 2048
ATT_UNROLL = 8
ATT_PHASES = 4
VMEM_LIMIT = 56 * 1024 * 1024


def _params(*sem):
    return pltpu.CompilerParams(dimension_semantics=sem, vmem_limit_bytes=VMEM_LIMIT)


def _dot(a, b):
    return jnp.dot(a, b, preferred_element_type=F32)


def _dot_nt(a, b):
    return lax.dot_general(a, b, (((1,), (1,)), ((), ())), preferred_element_type=F32)


def _dot_tn(a, b):
    return lax.dot_general(a, b, (((0,), (0,)), ((), ())), preferred_element_type=F32)


def _rms_rows(x, g):
    y = x * lax.rsqrt(jnp.mean(x * x, axis=-1, keepdims=True) + EPS)
    return y * g


def _sigmoid(x):
    return 1.0 / (1.0 + jnp.exp(-x))


def _full(shape):
    return pl.BlockSpec(shape, lambda *_: (0,) * len(shape), pipeline_mode=pl.Buffered(1))


FF_CHUNK = 256


def _ffn_kernel(n_pre, x_ref, *refs):
    m_refs, w_refs = refs[:n_pre], refs[n_pre:2 * n_pre]
    g_ref, wi_ref, wo_ref, o_ref, act_ref = refs[2 * n_pre:]
    x = x_ref[...]
    for m_ref, w_ref in zip(m_refs, w_refs):
        x = x + _dot(m_ref[...], w_ref[...])
    h = _rms_rows(x, g_ref[...]).astype(BF16)
    for c in range(D_FF // FF_CHUNK):
        lo = c * FF_CHUNK
        a = _dot(h, wi_ref[:, lo:lo + FF_CHUNK])
        b = _dot(h, wi_ref[:, D_FF + lo:D_FF + lo + FF_CHUNK])
        act_ref[:, lo:lo + FF_CHUNK] = (a * _sigmoid(a) * b).astype(BF16)
    o_ref[...] = x + 0.5 * _dot(act_ref[...], wo_ref[...])


def _ffn(x, g, wi, wo, tm, ms=(), ws=()):
    n = x.shape[0]
    row = lambda width: pl.BlockSpec((tm, width), lambda i: (i, 0))
    return pl.pallas_call(
        functools.partial(_ffn_kernel, len(ms)),
        grid=(n // tm,),
        in_specs=[row(D_MODEL)] + [row(m.shape[1]) for m in ms] + [_full(w.shape) for w in ws]
                 + [_full((1, D_MODEL)), _full((D_MODEL, 2 * D_FF)), _full((D_FF, D_MODEL))],
        out_specs=row(D_MODEL),
        out_shape=jax.ShapeDtypeStruct((n, D_MODEL), F32),
        scratch_shapes=[pltpu.VMEM((tm, D_FF), BF16)],
        compiler_params=_params("parallel"),
        name="ffn",
    )(x, *ms, *ws, g, wi, wo)


CAST_BLOCK_BYTES = 2 * 1024 * 1024
BF16_ROWS = 16


def _cast_kernel(w_ref, o_ref):
    o_ref[...] = w_ref[...].astype(o_ref.dtype)


def _weight_bf16(w, layer, row0=0, n_rows=None, n_cols=None):
    n_rows = w.shape[1] - row0 if n_rows is None else n_rows
    n_cols = w.shape[2] if n_cols is None else n_cols
    fits = [r for r in range(BF16_ROWS, n_rows + 1, BF16_ROWS)
            if n_rows % r == 0 and row0 % r == 0 and r * n_cols * 4 <= CAST_BLOCK_BYTES]
    rows = max(fits)
    first = row0 // rows
    return pl.pallas_call(
        _cast_kernel,
        grid=(n_rows // rows,),
        in_specs=[pl.BlockSpec((None, rows, n_cols), lambda i: (layer, first + i, 0))],
        out_specs=pl.BlockSpec((rows, n_cols), lambda i: (i, 0)),
        out_shape=jax.ShapeDtypeStruct((n_rows, n_cols), BF16),
        compiler_params=_params("parallel"),
        name="weight_bf16",
    )(w)


def _head_norm(z, gain, bd):
    zz = z * z
    hi = zz.astype(BF16)
    lo = (zz - hi.astype(F32)).astype(BF16)
    width = bd.shape[0]
    ss = jnp.concatenate(
        [_dot(hi[:, c:c + width], bd) + _dot(lo[:, c:c + width], bd)
         for c in range(0, z.shape[1], width)], axis=1)
    return z * lax.rsqrt(ss * (1.0 / A_HEAD_DIM) + EPS) * gain


def _even_in_kernel(x_ref, g_ref, w_ref, qg_ref, kg_ref, bd_ref, q_ref, k_ref, v_ref, u_ref):
    h = _rms_rows(x_ref[...], g_ref[...]).astype(BF16)
    p = _dot(h, w_ref[...])
    bd = bd_ref[...]
    q_ref[...] = _head_norm(p[:, 0:A_WIDTH], qg_ref[...], bd) * (A_HEAD_DIM ** -0.5)
    k_ref[...] = _head_norm(p[:, A_WIDTH:2 * A_WIDTH], kg_ref[...], bd)
    v_ref[...] = p[:, 2 * A_WIDTH:3 * A_WIDTH]
    gv = p[:, 3 * A_WIDTH:3 * A_WIDTH + CONV_CH]
    gg = p[:, 3 * A_WIDTH + CONV_CH:3 * A_WIDTH + 2 * CONV_CH]
    u_ref[...] = gv * _sigmoid(gg)


def _even_in(x, g, w, qg, kg, bd, tm):
    n = x.shape[0]
    row = lambda width: pl.BlockSpec((tm, width), lambda i: (i, 0))
    out = jax.ShapeDtypeStruct((n, A_WIDTH), F32)
    return pl.pallas_call(
        _even_in_kernel,
        grid=(n // tm,),
        in_specs=[row(D_MODEL), _full((1, D_MODEL)), _full(w.shape), _full((1, A_WIDTH)),
                  _full((1, A_WIDTH)), _full(bd.shape)],
        out_specs=[row(A_WIDTH)] * 4,
        out_shape=[out] * 4,
        compiler_params=_params("parallel"),
        name="even_in",
    )(x, g, w, qg, kg, bd)


def _t5_bucket(dist):
    max_exact = N_BUCKETS // 2
    d = np.asarray(dist, dtype=np.int32)
    df = np.maximum(d, 1).astype(np.float32)
    large = max_exact + (np.log(df / max_exact) / np.log(MAX_WINDOW / max_exact)
                         * (N_BUCKETS - max_exact)).astype(np.int32)
    large = np.minimum(large, N_BUCKETS - 1)
    return np.where(d < max_exact, d, large).astype(np.int32)


def _select_bias(rel_bias, dist, valid):
    onehot = (_t5_bucket(dist)[None, :] == np.arange(N_BUCKETS)[:, None]) & valid[None, :]
    picked = jnp.einsum('bh,bc->hc', rel_bias, jnp.asarray(onehot, F32),
                        precision=lax.Precision.HIGHEST)
    return picked + jnp.asarray(np.where(valid, 0.0, NEG_INF), F32)[None, :]


def _band_vectors(rel_bias):
    c = np.arange(2 * WIN_KEYS)
    valid = c <= WIN_KEYS
    vecs = [_select_bias(rel_bias, np.where(valid, (WIN_KEYS - c) * dil, 0), valid)
            for _, dil in DILATED_GROUPS]
    return jnp.stack(vecs).reshape(len(DILATED_GROUPS), A_HEADS // 2, 2, 2 * WIN_KEYS)


def _attn_prompt_kernel(q_ref, kp_ref, kc_ref, vp_ref, vc_ref, vec_ref, o_ref,
                        knat, vnat, k4, v4, q4, og1, lg1, og4, lg4, tab_ref):
    blk = pl.program_id(2)
    n_groups = len(DILATED_GROUPS)
    nph = ATT_PHASES
    per = ATT_BLOCK // nph

    @pl.when(blk == 0)
    def _():
        col = lax.broadcasted_iota(jnp.int32, (WIN_KEYS, 2 * WIN_KEYS), 1)
        for g in range(n_groups):
            for hh in range(2):
                vec = jnp.broadcast_to(vec_ref[g, 0, hh:hh + 1, :], (WIN_KEYS, 2 * WIN_KEYS))
                band = pltpu.roll(vec, 0, 1, stride=1, stride_axis=0)
                rows = slice(hh * WIN_KEYS, (hh + 1) * WIN_KEYS)
                tab_ref[g, rows, :] = band
                tab_ref[n_groups + g, rows, :] = jnp.where(col >= WIN_KEYS, band, NEG_INF)

    knat[0:WIN_KEYS, :] = kp_ref[ATT_BLOCK - WIN_KEYS:, :]
    knat[WIN_KEYS:, :] = kc_ref[...]
    vnat[0:WIN_KEYS, :] = vp_ref[ATT_BLOCK - WIN_KEYS:, :]
    vnat[WIN_KEYS:, :] = vc_ref[...]
    for r in range(nph):
        phase = pl.ds(r, per, stride=nph)
        k4[r, 0:per, :] = kp_ref[phase, :]
        k4[r, per:, :] = kc_ref[phase, :]
        v4[r, 0:per, :] = vp_ref[phase, :]
        v4[r, per:, :] = vc_ref[phase, :]
        q4[r] = q_ref[phase, :]
    first = blk == 0
    lane = lax.broadcasted_iota(jnp.int32, (WIN_KEYS, LANES), 1)
    low = lane < A_HEAD_DIM
    ones = jnp.ones((2 * WIN_KEYS, LANES), BF16)

    def block(qs, kk, vv, tab):
        qs = qs.astype(BF16)
        zero = jnp.zeros_like(qs)
        qst = jnp.concatenate([jnp.where(low, qs, zero), jnp.where(low, zero, qs)], axis=0)
        s = _dot_nt(qst, kk.astype(BF16)) + tab
        mx = jnp.max(s, axis=-1, keepdims=True)
        p = jnp.exp(s - mx).astype(BF16)
        r = _dot(p, jnp.concatenate([vv.astype(BF16), ones], axis=1))
        o2 = jnp.where(low, r[0:WIN_KEYS, 0:LANES], r[WIN_KEYS:, 0:LANES])
        l2 = jnp.where(low, r[0:WIN_KEYS, LANES:], r[WIN_KEYS:, LANES:])
        m2 = jnp.where(low, jnp.broadcast_to(mx[0:WIN_KEYS], (WIN_KEYS, LANES)),
                       jnp.broadcast_to(mx[WIN_KEYS:], (WIN_KEYS, LANES)))
        return o2 / l2, m2 + jnp.log(l2)

    def table(g, at_start):
        return tab_ref[jnp.where(jnp.logical_and(at_start, first), n_groups + g, g)]

    def body1(sub, carry):
        i0 = pl.multiple_of(sub * WIN_KEYS, WIN_KEYS)
        o, l = block(q_ref[pl.ds(i0, WIN_KEYS), :], knat[pl.ds(i0, 2 * WIN_KEYS), :],
                     vnat[pl.ds(i0, 2 * WIN_KEYS), :], table(0, sub == 0))
        og1[pl.ds(i0, WIN_KEYS), :] = o
        lg1[pl.ds(i0, WIN_KEYS), :] = l
        return carry

    def body2(pb, carry):
        sub = pb // nph
        r = pb - sub * nph
        i0 = pl.multiple_of(sub * WIN_KEYS, WIN_KEYS)
        keys = pl.ds(i0 + (per - WIN_KEYS), 2 * WIN_KEYS)
        o, l = block(q4[r, pl.ds(i0, WIN_KEYS), :], k4[r, keys, :], v4[r, keys, :],
                     table(1, sub == 0))
        og4[0, r, pl.ds(i0, WIN_KEYS), :] = o
        lg4[0, r, pl.ds(i0, WIN_KEYS), :] = l
        return carry

    def body3(pb, carry):
        a = pb // nph
        r = pb - a * nph
        rows = pl.ds(a, WIN_KEYS, stride=nph)
        keys = pl.ds(a, 2 * WIN_KEYS, stride=nph)
        o, l = block(q4[r, rows, :], k4[r, keys, :], v4[r, keys, :], table(2, True))
        og4[1, r, rows, :] = o
        lg4[1, r, rows, :] = l
        return carry

    n_blocks = ATT_BLOCK // WIN_KEYS
    for body in (body1, body2, body3):
        lax.fori_loop(0, n_blocks, body, 0, unroll=ATT_UNROLL)

    for r in range(nph):
        phase = pl.ds(r, per, stride=nph)
        la, lb, lc = lg1[phase, :], lg4[0, r], lg4[1, r]
        mx = jnp.maximum(jnp.maximum(la, lb), lc)
        wa, wb, wc = jnp.exp(la - mx), jnp.exp(lb - mx), jnp.exp(lc - mx)
        og1[phase, :] = (wa * og1[phase, :] + wb * og4[0, r] + wc * og4[1, r]) / (wa + wb + wc)
    o_ref[...] = og1[...].astype(o_ref.dtype)


def _attn_prompt(q, k, v, vecs, batch, seq):
    assert [d for _, d in DILATED_GROUPS] == [1, ATT_PHASES, ATT_PHASES ** 2]
    nb = seq // ATT_BLOCK
    n_groups = len(DILATED_GROUPS)
    per = ATT_BLOCK // ATT_PHASES
    cur = lambda b, p, t: (b * nb + t, p)
    prev = lambda b, p, t: (b * nb + jnp.maximum(t - 1, 0), p)
    blk = lambda imap: pl.BlockSpec((ATT_BLOCK, LANES), imap)
    vmem = lambda *shape: pltpu.VMEM(shape, F32)
    return pl.pallas_call(
        _attn_prompt_kernel,
        grid=(batch, A_HEADS // 2, nb),
        in_specs=[blk(cur), blk(prev), blk(cur), blk(prev), blk(cur),
                  pl.BlockSpec((n_groups, 1, 2, 2 * WIN_KEYS), lambda b, p, t: (0, p, 0, 0))],
        out_specs=blk(cur),
        out_shape=jax.ShapeDtypeStruct((batch * seq, A_WIDTH), BF16),
        scratch_shapes=[vmem(WIN_KEYS + ATT_BLOCK, LANES), vmem(WIN_KEYS + ATT_BLOCK, LANES),
                        vmem(ATT_PHASES, 2 * per, LANES), vmem(ATT_PHASES, 2 * per, LANES),
                        vmem(ATT_PHASES, per, LANES),
                        vmem(ATT_BLOCK, LANES), vmem(ATT_BLOCK, LANES),
                        vmem(2, ATT_PHASES, per, LANES), vmem(2, ATT_PHASES, per, LANES),
                        vmem(2 * n_groups, 2 * WIN_KEYS, 2 * WIN_KEYS)],
        compiler_params=_params("arbitrary", "arbitrary", "arbitrary"),
        name="attn_prompt",
    )(q, k, k, v, v, vecs)


DECODE_TAIL = LANES


def _decode_vectors(rel_bias):
    c = np.arange(MAX_WINDOW + DECODE_TAIL)
    dist = MAX_WINDOW - c
    cnt = np.zeros(c.shape, np.float32)
    for window, dil in DILATED_GROUPS:
        cnt += ((dist >= 0) & (dist <= window) & (dist % dil == 0)).astype(np.float32)
    bias = _select_bias(rel_bias, np.clip(dist, 0, MAX_WINDOW), cnt > 0)
    return bias, jnp.asarray(cnt[None, :])


def _attn_sample_kernel(n_new, q_ref, kn_ref, vn_ref, kc_ref, vc_ref, bvec_ref, cvec_ref,
                        o_ref, ko_ref, vo_ref, kall, vall, bias_ref, cnt_ref):
    n_buf = MAX_WINDOW
    rows = A_HEADS * n_new

    @pl.when(pl.program_id(0) == 0)
    def _():
        shape = (n_new, n_buf + DECODE_TAIL)
        cnt = pltpu.roll(jnp.broadcast_to(cvec_ref[...], shape), 0, 1, stride=1, stride_axis=0)
        for h in range(A_HEADS):
            bvec = jnp.broadcast_to(bvec_ref[h:h + 1, :], shape)
            bias_ref[h * n_new:(h + 1) * n_new, :] = pltpu.roll(bvec, 0, 1, stride=1, stride_axis=0)
            cnt_ref[h * n_new:(h + 1) * n_new, :] = cnt

    kc = kc_ref[0]
    vc = vc_ref[0]
    kn = kn_ref[0]
    vn = vn_ref[0]
    ko_ref[0, 0:n_buf - n_new, :] = kc[n_new:, :]
    ko_ref[0, n_buf - n_new:, :] = kn
    vo_ref[0, 0:n_buf - n_new, :] = vc[n_new:, :]
    vo_ref[0, n_buf - n_new:, :] = vn
    unused = jnp.zeros((DECODE_TAIL - n_new, A_WIDTH), F32)
    kall[0:n_buf, :] = kc.astype(BF16)
    kall[n_buf:, :] = jnp.concatenate([kn, unused], axis=0).astype(BF16)
    vall[0:n_buf, :] = vc.astype(BF16)
    vall[n_buf:, :] = jnp.concatenate([vn, unused], axis=0).astype(BF16)

    q = q_ref[0]
    row_head = lax.broadcasted_iota(jnp.int32, (A_HEADS, n_new, A_WIDTH), 0).reshape(rows, A_WIDTH)
    lane = lax.broadcasted_iota(jnp.int32, (rows, A_WIDTH), 1)
    own = jnp.logical_and(lane >= row_head * A_HEAD_DIM, lane < (row_head + 1) * A_HEAD_DIM)
    qblk = jnp.where(own, jnp.concatenate([q] * A_HEADS, axis=0), 0.0).astype(BF16)
    s = _dot_nt(qblk, kall[...]) + bias_ref[...]
    mx = jnp.max(s, axis=-1, keepdims=True)
    p = cnt_ref[...] * jnp.exp(s - mx)
    den = jnp.sum(p, axis=-1, keepdims=True)
    acc = jnp.where(own, _dot(p.astype(BF16), vall[...]) / den, 0.0)
    out = acc[0:n_new]
    for h in range(1, A_HEADS):
        out = out + acc[h * n_new:(h + 1) * n_new]
    o_ref[0] = out.astype(o_ref.dtype)


def _attn_sample(q, k_new, v_new, cache_k, cache_v, layer, tables):
    b, n_new, _ = q.shape
    n_buf = cache_k.shape[2]
    rows = A_HEADS * n_new
    new = pl.BlockSpec((1, n_new, A_WIDTH), lambda i: (i, 0, 0))
    buf = pl.BlockSpec((1, n_buf, A_WIDTH), lambda i: (i, 0, 0))
    past = pl.BlockSpec((None, 1, n_buf, A_WIDTH), lambda i: (layer, i, 0, 0))
    cols = n_buf + DECODE_TAIL
    return pl.pallas_call(
        functools.partial(_attn_sample_kernel, n_new),
        grid=(b,),
        in_specs=[new, new, new, past, past, _full((A_HEADS, cols)), _full((1, cols))],
        out_specs=[new, buf, buf],
        out_shape=[jax.ShapeDtypeStruct((b, n_new, A_WIDTH), BF16),
                   jax.ShapeDtypeStruct((b, n_buf, A_WIDTH), F32),
                   jax.ShapeDtypeStruct((b, n_buf, A_WIDTH), F32)],
        scratch_shapes=[pltpu.VMEM((cols, A_WIDTH), BF16),
                        pltpu.VMEM((cols, A_WIDTH), BF16),
                        pltpu.VMEM((rows, cols), F32),
                        pltpu.VMEM((rows, cols), F32)],
        compiler_params=_params("arbitrary"),
        name="attn_sample",
    )(q, k_new, v_new, cache_k, cache_v, *tables)


CONV_PAD = 32
CONV_ROWS = 32
CONV_UNROLL = 2


def _conv_kernel(tc, u_ref, up_ref, hist_ref, w_ref, b_ref, g_ref, beta_ref, o_ref, win, stage):
    t = pl.program_id(1)
    n_slab = CONV_CH // LANES
    slab = lambda c: slice(c * LANES, (c + 1) * LANES)
    for c in range(n_slab):
        win[c, CONV_PAD:CONV_PAD + tc, :] = u_ref[0, :, slab(c)]

    @pl.when(t == 0)
    def _():
        for c in range(n_slab):
            win[c, 0:CONV_PAD, :] = hist_ref[0, :, slab(c)]

    @pl.when(t > 0)
    def _():
        for c in range(n_slab):
            win[c, 0:CONV_PAD, :] = up_ref[0, :, slab(c)]

    off = CONV_PAD - (CONV_WIDTH - 1)
    rc = min(CONV_ROWS, tc)
    half = rc // 2

    def tap(k, c):
        w = w_ref[k, :, slab(c)]
        if half < SUBLANES:
            return w[0:half]
        return jnp.concatenate([w] * (half // SUBLANES), axis=0)

    def body(j, carry):
        r0 = j * rc
        for c in range(n_slab):
            for par in range(2):
                acc = jnp.zeros((half, LANES), F32) + b_ref[:, slab(c)]
                for k in range(CONV_WIDTH):
                    rows = pl.ds(r0 + off + k + par, half, stride=2)
                    acc = acc + win[c, rows, :] * tap(k, c)
                stage[c, pl.ds(r0 + par, half, stride=2), :] = acc
        return carry

    if tc == rc:
        body(0, 0)
    else:
        lax.fori_loop(0, tc // rc, body, 0, unroll=CONV_UNROLL)
    y = jnp.concatenate([stage[c] for c in range(n_slab)], axis=1)
    xc = y - jnp.mean(y, axis=-1, keepdims=True)
    y = xc * lax.rsqrt(jnp.mean(xc * xc, axis=-1, keepdims=True) + EPS)
    y = y * g_ref[...] + beta_ref[...]
    o_ref[0] = (y * _sigmoid(y)).astype(o_ref.dtype)


def _conv(u, hist, w, b, g, beta, tc):
    bsz, t, _ = u.shape
    per = tc // CONV_PAD
    if t >= CONV_PAD:
        prev = pl.BlockSpec((1, CONV_PAD, CONV_CH), lambda i, j: (i, jnp.maximum(j * per - 1, 0), 0))
        u_prev = u
    else:
        prev = pl.BlockSpec((1, CONV_PAD, CONV_CH), lambda i, j: (i, 0, 0))
        u_prev = hist
    return pl.pallas_call(
        functools.partial(_conv_kernel, tc),
        grid=(bsz, t // tc),
        in_specs=[pl.BlockSpec((1, tc, CONV_CH), lambda i, j: (i, j, 0)),
                  prev,
                  pl.BlockSpec((1, CONV_PAD, CONV_CH), lambda i, j: (i, 0, 0)),
                  _full(w.shape), _full((1, CONV_CH)), _full((1, CONV_CH)),
                  _full((1, CONV_CH))],
        out_specs=pl.BlockSpec((1, tc, CONV_CH), lambda i, j: (i, j, 0)),
        out_shape=jax.ShapeDtypeStruct((bsz, t, CONV_CH), BF16),
        scratch_shapes=[pltpu.VMEM((CONV_CH // LANES, CONV_PAD + tc, LANES), F32),
                        pltpu.VMEM((CONV_CH // LANES, tc, LANES), F32)],
        compiler_params=_params("parallel", "arbitrary"),
        name="conv",
    )(u, u_prev, hist, w, b, g, beta)


def _gla_in_kernel(x_ref, g_ref, w_ref, wl_ref, wu_ref, bu_ref, q_ref, k_ref, v_ref, r_ref, la_ref):
    h = _rms_rows(x_ref[...], g_ref[...]).astype(BF16)
    p = _dot(h, w_ref[...])
    q_ref[...] = p[:, 0:C_DK] * (C_DK_HEAD ** -0.5)
    k_ref[...] = p[:, C_DK:2 * C_DK]
    v_ref[...] = p[:, 2 * C_DK:2 * C_DK + C_DV]
    r = p[:, 2 * C_DK + C_DV:2 * C_DK + 2 * C_DV]
    r_ref[...] = r * _sigmoid(r)
    low = _dot(h, wl_ref[...]).astype(BF16)
    z = _dot(low, wu_ref[...]) + bu_ref[...]
    log_sig = jnp.minimum(z, 0.0) - jnp.log1p(jnp.exp(-jnp.abs(z)))
    la_ref[...] = log_sig * (1.0 / GATE_TAU)


def _gla_in(x, g, w, wl, wu, bu, tm):
    n = x.shape[0]
    row = lambda width: pl.BlockSpec((tm, width), lambda i: (i, 0))
    sds = lambda width: jax.ShapeDtypeStruct((n, width), F32)
    return pl.pallas_call(
        _gla_in_kernel,
        grid=(n // tm,),
        in_specs=[row(D_MODEL), _full((1, D_MODEL)), _full(w.shape), _full(wl.shape),
                  _full(wu.shape), _full((1, C_DK))],
        out_specs=[row(C_DK), row(C_DK), row(C_DV), row(C_DV), row(C_DK)],
        out_shape=[sds(C_DK), sds(C_DK), sds(C_DV), sds(C_DV), sds(C_DK)],
        compiler_params=_params("parallel"),
        name="gla_in",
    )(x, g, w, wl, wu, bu)


def _gla_kernel(chunk, n_chunks, q_ref, k_ref, v_ref, r_ref, la_ref, s0_ref, gain_ref,
                o_ref, s_ref, qin_s, kin_s, x_s, qst_s, kst_s, dec_s):
    @pl.when(pl.program_id(1) == 0)
    def _():
        s_ref[...] = s0_ref[...]

    half = chunk // 2
    ri = lax.broadcasted_iota(jnp.int32, (chunk, chunk), 0)
    ci = lax.broadcasted_iota(jnp.int32, (chunk, chunk), 1)
    causal = ci <= ri
    tri = jnp.where(causal, 1.0, 0.0).astype(BF16)
    cross = jnp.logical_and(ri >= half, ci < half)
    same_half = jnp.logical_and(causal, jnp.logical_not(cross))
    in_first = lax.broadcasted_iota(jnp.int32, (chunk, C_DK), 0) < half
    gain = gain_ref[...]

    def prepare(c):
        rows = slice(c * chunk, (c + 1) * chunk)
        la = la_ref[0, rows, :]
        la_hi = la.astype(BF16)
        la_lo = (la - la_hi.astype(F32)).astype(BF16)
        cum = _dot(tri, la_hi) + _dot(tri, la_lo)
        row = lambda i: cum[i:i + 1, :]
        last = row(chunk - 1)
        edge = row(half - 1)
        mid = jnp.where(in_first, row(half // 2 - 1), row(half + half // 2 - 1))
        q = q_ref[0, rows, :]
        k = k_ref[0, rows, :]
        grow = jnp.exp(cum - mid)
        qin_s[rows, :] = (q * grow).astype(BF16)
        kin_s[rows, :] = (k / grow).astype(BF16)
        k_x = k[:half] * jnp.exp(edge - cum[:half])
        q_x = q[half:] * jnp.exp(cum[half:] - edge)
        x_s[rows, :] = jnp.concatenate([k_x, q_x], axis=0).astype(BF16)
        qst_s[rows, :] = (q * jnp.exp(cum)).astype(BF16)
        kst_s[rows, :] = (k * jnp.exp(last - cum)).astype(BF16)
        dec_s[c] = jnp.broadcast_to(jnp.exp(last), (LANES, C_DK)).T

    def advance(c):
        rows = slice(c * chunk, (c + 1) * chunk)
        for h in range(C_HEADS):
            ks = slice(h * C_DK_HEAD, (h + 1) * C_DK_HEAD)
            vs = slice(h * C_DV_HEAD, (h + 1) * C_DV_HEAD)
            vh = v_ref[0, rows, vs].astype(BF16)
            xh = x_s[rows, ks]
            att = jnp.where(same_half, _dot_nt(qin_s[rows, ks], kin_s[rows, ks]),
                            jnp.where(cross, _dot_nt(xh, xh), 0.0))
            s = s_ref[0, h]
            o = _dot(jnp.concatenate([att.astype(BF16), qst_s[rows, ks]], axis=1),
                     jnp.concatenate([vh, s.astype(BF16)], axis=0))
            decay = dec_s[c, ks, :]
            s_ref[0, h] = (s * jnp.concatenate([decay] * (C_DV_HEAD // LANES), axis=1)
                           + _dot_tn(kst_s[rows, ks], vh))
            y = o * lax.rsqrt(jnp.mean(o * o, axis=-1, keepdims=True) + EPS) * gain
            o_ref[0, rows, vs] = (y * r_ref[0, rows, vs]).astype(o_ref.dtype)

    prepare(0)
    for c in range(n_chunks):
        if c + 1 < n_chunks:
            prepare(c + 1)
        advance(c)


def _gla(q, k, v, r, la, s0, gain, chunk, tb):
    b, t, _ = q.shape
    seq = lambda width: pl.BlockSpec((1, tb, width), lambda i, j: (i, j, 0))
    state = pl.BlockSpec((1, C_HEADS, C_DK_HEAD, C_DV_HEAD), lambda i, j: (i, 0, 0, 0))
    return pl.pallas_call(
        functools.partial(_gla_kernel, chunk, tb // chunk),
        grid=(b, t // tb),
        in_specs=[seq(C_DK), seq(C_DK), seq(C_DV), seq(C_DV), seq(C_DK), state,
                  _full((1, C_DV_HEAD))],
        out_specs=[seq(C_DV), state],
        out_shape=[jax.ShapeDtypeStruct((b, t, C_DV), BF16),
                   jax.ShapeDtypeStruct(s0.shape, F32)],
        scratch_shapes=[pltpu.VMEM((tb, C_DK), BF16)] * 5
                       + [pltpu.VMEM((tb // chunk, C_DK, LANES), F32)],
        compiler_params=_params("parallel", "arbitrary"),
        name="gla",
    )(q, k, v, r, la, s0, gain)


GLA_CHUNK = 128
SAMPLE_PAD = 16


def _trunk(x, past, P, tm, conv_tc, gla_tb):
    bsz, t, _ = x.shape
    n = bsz * t
    x = x.reshape(n, D_MODEL)
    outs = {}
    for layer in range(DEPTH):
        i = layer // 2
        x = _ffn(x, P['norm_ffn1'][layer], P['ffn1_w_in'][layer], P['ffn1_w_out'][layer], tm)
        if layer % 2 == 0:
            q, k, v, u = _even_in(x, P['norm_mix'][layer], P['ev_w_in'][i], P['ev_q_gain'][i],
                                  P['ev_k_gain'][i], P['head_ones'], tm)
            u3 = u.reshape(bsz, t, CONV_CH)
            if past is None:
                a = _attn_prompt(q, k, v, P['band_vectors'], bsz, t)
                keep = min(MAX_WINDOW, t)
                tail = lambda z: z.reshape(bsz, t, A_WIDTH)[:, t - keep:].reshape(
                    bsz, keep, A_HEADS, A_HEAD_DIM)
                new_k, new_v = tail(k), tail(v)
                hist = jnp.zeros((bsz, CONV_PAD, CONV_CH), F32)
                new_u = u3[:, t - (CONV_WIDTH - 1):]
            else:
                n_buf = past[0].shape[2]
                flat = lambda z: z.reshape(z.shape[0], bsz, n_buf, A_WIDTH)
                a, new_k, new_v = _attn_sample(
                    q.reshape(bsz, t, A_WIDTH), k.reshape(bsz, t, A_WIDTH), v.reshape(bsz, t, A_WIDTH),
                    flat(past[0]), flat(past[1]), i, P['decode_vectors'])
                a = a.reshape(n, A_WIDTH)
                new_k = new_k.reshape(bsz, n_buf, A_HEADS, A_HEAD_DIM)
                new_v = new_v.reshape(bsz, n_buf, A_HEADS, A_HEAD_DIM)
                hist = jnp.pad(past[2][i], ((0, 0), (CONV_PAD - (CONV_WIDTH - 1), 0), (0, 0)))
                new_u = jnp.concatenate([past[2][i], u3], axis=1)[:, -(CONV_WIDTH - 1):]
            c = _conv(u3, hist, P['ev_conv_w'][i], P['ev_conv_b'][i], P['ev_conv_ln_g'][i],
                      P['ev_conv_ln_b'][i], conv_tc).reshape(n, CONV_CH)
            mixed, w_mix = [a, c], [P['ev_w_out_a'][i], P['ev_w_out_c'][i]]
            outs.setdefault('k', []).append(new_k)
            outs.setdefault('v', []).append(new_v)
            outs.setdefault('u', []).append(new_u)
        else:
            q, k, v, r, la = _gla_in(x, P['norm_mix'][layer], P['od_w_main'][i], P['od_w_low'][i],
                                     P['od_gate_w_up'][i], P['od_gate_b'][i], tm)
            if past is None:
                s0 = jnp.zeros((bsz, C_HEADS, C_DK_HEAD, C_DV_HEAD), F32)
                tp, chunk = t, GLA_CHUNK
            else:
                s0 = past[3][i]
                tp, chunk = SAMPLE_PAD, SAMPLE_PAD
            seq = lambda z: jnp.pad(z.reshape(bsz, t, -1), ((0, 0), (0, tp - t), (0, 0)))
            o, s = _gla(seq(q), seq(k), seq(v), seq(r), seq(la), s0, P['od_o_gain'][i], chunk,
                        min(gla_tb, tp))
            mixed, w_mix = [o[:, :t].reshape(n, C_DV)], [P['od_w_out'][i]]
            outs.setdefault('s', []).append(s)
        x = _ffn(x, P['norm_ffn2'][layer], P['ffn2_w_in'][layer], P['ffn2_w_out'][layer], tm,
                 mixed, w_mix)
    return (x.reshape(bsz, t, D_MODEL), jnp.stack(outs['k']), jnp.stack(outs['v']),
            jnp.stack(outs['u']), jnp.stack(outs['s']))


def kernel(x_prompt, x_sample, cache_k, cache_v, cache_conv, state_gla, rel_bias, norm_ffn1, ffn1_w_in, ffn1_w_out, norm_mix, norm_ffn2, ffn2_w_in, ffn2_w_out, ev_w_in, ev_q_gain, ev_k_gain, ev_conv_w, ev_conv_b, ev_conv_ln_g, ev_conv_ln_b, ev_w_out, od_w_in, od_gate_w_up, od_gate_b, od_o_gain, od_w_out):
    n_even = ev_w_in.shape[0]
    n_odd = od_w_in.shape[0]
    main = 2 * C_DK + 2 * C_DV
    head_ids = np.arange(MXU_DIM) // A_HEAD_DIM
    per = lambda n, f: [f(j) for j in range(n)]
    row = lambda z: z[None, :]
    P = {
        'norm_ffn1': per(DEPTH, lambda j: row(norm_ffn1[j])),
        'norm_mix': per(DEPTH, lambda j: row(norm_mix[j])),
        'norm_ffn2': per(DEPTH, lambda j: row(norm_ffn2[j])),
        'ffn1_w_in': per(DEPTH, lambda j: _weight_bf16(ffn1_w_in, j)),
        'ffn1_w_out': per(DEPTH, lambda j: _weight_bf16(ffn1_w_out, j)),
        'ffn2_w_in': per(DEPTH, lambda j: _weight_bf16(ffn2_w_in, j)),
        'ffn2_w_out': per(DEPTH, lambda j: _weight_bf16(ffn2_w_out, j)),
        'ev_w_in': per(n_even, lambda j: _weight_bf16(ev_w_in, j)),
        'ev_q_gain': per(n_even, lambda j: row(jnp.tile(ev_q_gain[j], A_HEADS))),
        'ev_k_gain': per(n_even, lambda j: row(jnp.tile(ev_k_gain[j], A_HEADS))),
        'head_ones': jnp.asarray(head_ids[:, None] == head_ids[None, :], BF16),
        'ev_conv_w': per(n_even, lambda j: jnp.broadcast_to(
            ev_conv_w[j][:, None, :], (CONV_WIDTH, SUBLANES, CONV_CH))),
        'ev_conv_b': per(n_even, lambda j: row(ev_conv_b[j])),
        'ev_conv_ln_g': per(n_even, lambda j: row(ev_conv_ln_g[j])),
        'ev_conv_ln_b': per(n_even, lambda j: row(ev_conv_ln_b[j])),
        'ev_w_out_a': per(n_even, lambda j: _weight_bf16(ev_w_out, j, 0, A_WIDTH)),
        'ev_w_out_c': per(n_even, lambda j: _weight_bf16(ev_w_out, j, A_WIDTH, CONV_CH)),
        'od_w_main': per(n_odd, lambda j: _weight_bf16(od_w_in, j, n_cols=main)),
        'od_w_low': per(n_odd, lambda j: jnp.pad(od_w_in[j, :, main:],
                                                 ((0, 0), (0, LANES - GATE_RANK))).astype(BF16)),
        'od_gate_w_up': per(n_odd, lambda j: jnp.pad(od_gate_w_up[j],
                                                     ((0, LANES - GATE_RANK), (0, 0))).astype(BF16)),
        'od_gate_b': per(n_odd, lambda j: row(od_gate_b[j])),
        'od_o_gain': per(n_odd, lambda j: row(od_o_gain[j])),
        'od_w_out': per(n_odd, lambda j: _weight_bf16(od_w_out, j)),
        'band_vectors': _band_vectors(rel_bias),
        'decode_vectors': _decode_vectors(rel_bias),
    }
    P, x_prompt, x_sample = lax.optimization_barrier((P, x_prompt, x_sample))
    y_p, k_p, v_p, u_p, s_p = _trunk(x_prompt, None, P, tm=512, conv_tc=512, gla_tb=512)
    y_s, k_s, v_s, u_s, s_s = _trunk(x_sample, (cache_k, cache_v, cache_conv, state_gla), P,
                                     tm=256, conv_tc=x_sample.shape[1], gla_tb=SAMPLE_PAD)
    return (y_p, y_s, k_p, v_p, u_p, s_p, k_s, v_s, u_s, s_s)
```

```python
import functools

import numpy as np
import jax
import jax.numpy as jnp
from jax import lax
from jax.experimental import pallas as pl
from jax.experimental.pallas import tpu as pltpu

F32 = jnp.float32
BF16 = jnp.bfloat16

D_MODEL = 1024
DEPTH = 2
PAST_LEN = 16384
A_HEADS = 8
A_HEAD_DIM = 64
A_WIDTH = A_HEADS * A_HEAD_DIM
DILATED_GROUPS = ((128, 1), (512, 4), (2048, 16))
MAX_WINDOW = 2048
N_BUCKETS = 32
CONV_WIDTH = 31
CONV_CH = 512
C_HEADS = 4
C_DK = 512
C_DV = 1024
C_DK_HEAD = 128
C_DV_HEAD = 256
GATE_RANK = 16
GATE_TAU = 16.0
D_FF = 2816
EPS = 1e-6
NEG_INF = -1e30

LANES = 128
SUBLANES = 8
MXU_DIM = 256
WIN_KEYS = 128
ATT_BLOCK = 2048
ATT_UNROLL = 16
ATT_PHASES = 4
VMEM_LIMIT = 56 * 1024 * 1024


def _params(*sem):
    return pltpu.CompilerParams(dimension_semantics=sem, vmem_limit_bytes=VMEM_LIMIT)


def _dot(a, b):
    return jnp.dot(a, b, preferred_element_type=F32)


def _dot_nt(a, b):
    return lax.dot_general(a, b, (((1,), (1,)), ((), ())), preferred_element_type=F32)


def _dot_tn(a, b):
    return lax.dot_general(a, b, (((0,), (0,)), ((), ())), preferred_element_type=F32)


def _rms_rows(x, g):
    y = x * lax.rsqrt(jnp.mean(x * x, axis=-1, keepdims=True) + EPS)
    return y * g


def _sigmoid(x):
    return 1.0 / (1.0 + jnp.exp(-x))


def _full(shape):
    return pl.BlockSpec(shape, lambda *_: (0,) * len(shape), pipeline_mode=pl.Buffered(1))


FF_CHUNK = 256


def _ffn_kernel(n_pre, x_ref, *refs):
    m_refs, w_refs = refs[:n_pre], refs[n_pre:2 * n_pre]
    g_ref, wi_ref, wo_ref, o_ref, act_ref = refs[2 * n_pre:]
    x = x_ref[...]
    for m_ref, w_ref in zip(m_refs, w_refs):
        x = x + _dot(m_ref[...], w_ref[...])
    h = _rms_rows(x, g_ref[...]).astype(BF16)
    for c in range(D_FF // FF_CHUNK):
        lo = c * FF_CHUNK
        a = _dot(h, wi_ref[:, lo:lo + FF_CHUNK])
        b = _dot(h, wi_ref[:, D_FF + lo:D_FF + lo + FF_CHUNK])
        act_ref[:, lo:lo + FF_CHUNK] = (a * _sigmoid(a) * b).astype(BF16)
    o_ref[...] = x + 0.5 * _dot(act_ref[...], wo_ref[...])


def _ffn(x, g, wi, wo, tm, ms=(), ws=()):
    n = x.shape[0]
    row = lambda width: pl.BlockSpec((tm, width), lambda i: (i, 0))
    return pl.pallas_call(
        functools.partial(_ffn_kernel, len(ms)),
        grid=(n // tm,),
        in_specs=[row(D_MODEL)] + [row(m.shape[1]) for m in ms] + [_full(w.shape) for w in ws]
                 + [_full((1, D_MODEL)), _full((D_MODEL, 2 * D_FF)), _full((D_FF, D_MODEL))],
        out_specs=row(D_MODEL),
        out_shape=jax.ShapeDtypeStruct((n, D_MODEL), F32),
        scratch_shapes=[pltpu.VMEM((tm, D_FF), BF16)],
        compiler_params=_params("parallel"),
        name="ffn",
    )(x, *ms, *ws, g, wi, wo)


CAST_BLOCK_BYTES = 2 * 1024 * 1024
BF16_ROWS = 16


def _cast_kernel(w_ref, o_ref):
    o_ref[...] = w_ref[...].astype(o_ref.dtype)


def _weight_bf16(w, layer, row0=0, n_rows=None, n_cols=None):
    n_rows = w.shape[1] - row0 if n_rows is None else n_rows
    n_cols = w.shape[2] if n_cols is None else n_cols
    fits = [r for r in range(BF16_ROWS, n_rows + 1, BF16_ROWS)
            if n_rows % r == 0 and row0 % r == 0 and r * n_cols * 4 <= CAST_BLOCK_BYTES]
    rows = max(fits)
    first = row0 // rows
    return pl.pallas_call(
        _cast_kernel,
        grid=(n_rows // rows,),
        in_specs=[pl.BlockSpec((None, rows, n_cols), lambda i: (layer, first + i, 0))],
        out_specs=pl.BlockSpec((rows, n_cols), lambda i: (i, 0)),
        out_shape=jax.ShapeDtypeStruct((n_rows, n_cols), BF16),
        compiler_params=_params("parallel"),
        name="weight_bf16",
    )(w)


def _head_norm(z, gain, bd):
    zz = z * z
    hi = zz.astype(BF16)
    lo = (zz - hi.astype(F32)).astype(BF16)
    width = bd.shape[0]
    ss = jnp.concatenate(
        [_dot(hi[:, c:c + width], bd) + _dot(lo[:, c:c + width], bd)
         for c in range(0, z.shape[1], width)], axis=1)
    return z * lax.rsqrt(ss * (1.0 / A_HEAD_DIM) + EPS) * gain


def _even_in_kernel(x_ref, g_ref, w_ref, qg_ref, kg_ref, bd_ref, q_ref, k_ref, v_ref, u_ref):
    h = _rms_rows(x_ref[...], g_ref[...]).astype(BF16)
    p = _dot(h, w_ref[...])
    bd = bd_ref[...]
    q_ref[...] = _head_norm(p[:, 0:A_WIDTH], qg_ref[...], bd) * (A_HEAD_DIM ** -0.5)
    k_ref[...] = _head_norm(p[:, A_WIDTH:2 * A_WIDTH], kg_ref[...], bd)
    v_ref[...] = p[:, 2 * A_WIDTH:3 * A_WIDTH]
    gv = p[:, 3 * A_WIDTH:3 * A_WIDTH + CONV_CH]
    gg = p[:, 3 * A_WIDTH + CONV_CH:3 * A_WIDTH + 2 * CONV_CH]
    u_ref[...] = gv * _sigmoid(gg)


def _even_in(x, g, w, qg, kg, bd, tm):
    n = x.shape[0]
    row = lambda width: pl.BlockSpec((tm, width), lambda i: (i, 0))
    out = jax.ShapeDtypeStruct((n, A_WIDTH), F32)
    return pl.pallas_call(
        _even_in_kernel,
        grid=(n // tm,),
        in_specs=[row(D_MODEL), _full((1, D_MODEL)), _full(w.shape), _full((1, A_WIDTH)),
                  _full((1, A_WIDTH)), _full(bd.shape)],
        out_specs=[row(A_WIDTH)] * 4,
        out_shape=[out] * 4,
        compiler_params=_params("parallel"),
        name="even_in",
    )(x, g, w, qg, kg, bd)


def _t5_bucket(dist):
    max_exact = N_BUCKETS // 2
    d = np.asarray(dist, dtype=np.int32)
    df = np.maximum(d, 1).astype(np.float32)
    large = max_exact + (np.log(df / max_exact) / np.log(MAX_WINDOW / max_exact)
                         * (N_BUCKETS - max_exact)).astype(np.int32)
    large = np.minimum(large, N_BUCKETS - 1)
    return np.where(d < max_exact, d, large).astype(np.int32)


def _select_bias(rel_bias, dist, valid):
    onehot = (_t5_bucket(dist)[None, :] == np.arange(N_BUCKETS)[:, None]) & valid[None, :]
    picked = jnp.einsum('bh,bc->hc', rel_bias, jnp.asarray(onehot, F32),
                        precision=lax.Precision.HIGHEST)
    return picked + jnp.asarray(np.where(valid, 0.0, NEG_INF), F32)[None, :]


def _band_vectors(rel_bias):
    c = np.arange(2 * WIN_KEYS)
    valid = c <= WIN_KEYS
    vecs = [_select_bias(rel_bias, np.where(valid, (WIN_KEYS - c) * dil, 0), valid)
            for _, dil in DILATED_GROUPS]
    return jnp.stack(vecs).reshape(len(DILATED_GROUPS), A_HEADS // 2, 2, 2 * WIN_KEYS)


def _attn_prompt_kernel(q_ref, kp_ref, kc_ref, vp_ref, vc_ref, vec_ref, o_ref,
                        knat, vnat, k4, v4, q4, og1, lg1, og4, lg4, tab_ref):
    blk = pl.program_id(2)
    n_groups = len(DILATED_GROUPS)
    nph = ATT_PHASES
    per = ATT_BLOCK // nph

    @pl.when(blk == 0)
    def _():
        col = lax.broadcasted_iota(jnp.int32, (WIN_KEYS, 2 * WIN_KEYS), 1)
        for g in range(n_groups):
            for hh in range(2):
                vec = jnp.broadcast_to(vec_ref[g, 0, hh:hh + 1, :], (WIN_KEYS, 2 * WIN_KEYS))
                band = pltpu.roll(vec, 0, 1, stride=1, stride_axis=0)
                rows = slice(hh * WIN_KEYS, (hh + 1) * WIN_KEYS)
                tab_ref[g, rows, :] = band
                tab_ref[n_groups + g, rows, :] = jnp.where(col >= WIN_KEYS, band, NEG_INF)

    knat[0:WIN_KEYS, :] = kp_ref[ATT_BLOCK - WIN_KEYS:, :]
    knat[WIN_KEYS:, :] = kc_ref[...]
    vnat[0:WIN_KEYS, :] = vp_ref[ATT_BLOCK - WIN_KEYS:, :]
    vnat[WIN_KEYS:, :] = vc_ref[...]
    for r in range(nph):
        phase = pl.ds(r, per, stride=nph)
        k4[r, 0:per, :] = kp_ref[phase, :]
        k4[r, per:, :] = kc_ref[phase, :]
        v4[r, 0:per, :] = vp_ref[phase, :]
        v4[r, per:, :] = vc_ref[phase, :]
        q4[r] = q_ref[phase, :]
    first = blk == 0
    lane = lax.broadcasted_iota(jnp.int32, (WIN_KEYS, LANES), 1)
    low = lane < A_HEAD_DIM
    ones = jnp.ones((2 * WIN_KEYS, LANES), BF16)

    def block(qs, kk, vv, tab):
        qs = qs.astype(BF16)
        zero = jnp.zeros_like(qs)
        qst = jnp.concatenate([jnp.where(low, qs, zero), jnp.where(low, zero, qs)], axis=0)
        s = _dot_nt(qst, kk.astype(BF16)) + tab
        mx = jnp.max(s, axis=-1, keepdims=True)
        p = jnp.exp(s - mx).astype(BF16)
        r = _dot(p, jnp.concatenate([vv.astype(BF16), ones], axis=1))
        o2 = jnp.where(low, r[0:WIN_KEYS, 0:LANES], r[WIN_KEYS:, 0:LANES])
        l2 = jnp.where(low, r[0:WIN_KEYS, LANES:], r[WIN_KEYS:, LANES:])
        m2 = jnp.where(low, jnp.broadcast_to(mx[0:WIN_KEYS], (WIN_KEYS, LANES)),
                       jnp.broadcast_to(mx[WIN_KEYS:], (WIN_KEYS, LANES)))
        return o2 / l2, m2 + jnp.log(l2)

    def table(g, at_start):
        return tab_ref[jnp.where(jnp.logical_and(at_start, first), n_groups + g, g)]

    def body1(sub, carry):
        i0 = pl.multiple_of(sub * WIN_KEYS, WIN_KEYS)
        o, l = block(q_ref[pl.ds(i0, WIN_KEYS), :], knat[pl.ds(i0, 2 * WIN_KEYS), :],
                     vnat[pl.ds(i0, 2 * WIN_KEYS), :], table(0, sub == 0))
        og1[pl.ds(i0, WIN_KEYS), :] = o
        lg1[pl.ds(i0, WIN_KEYS), :] = l
        return carry

    def body2(pb, carry):
        sub = pb // nph
        r = pb - sub * nph
        i0 = pl.multiple_of(sub * WIN_KEYS, WIN_KEYS)
        keys = pl.ds(i0 + (per - WIN_KEYS), 2 * WIN_KEYS)
        o, l = block(q4[r, pl.ds(i0, WIN_KEYS), :], k4[r, keys, :], v4[r, keys, :],
                     table(1, sub == 0))
        og4[0, r, pl.ds(i0, WIN_KEYS), :] = o
        lg4[0, r, pl.ds(i0, WIN_KEYS), :] = l
        return carry

    def body3(pb, carry):
        a = pb // nph
        r = pb - a * nph
        rows = pl.ds(a, WIN_KEYS, stride=nph)
        keys = pl.ds(a, 2 * WIN_KEYS, stride=nph)
        o, l = block(q4[r, rows, :], k4[r, keys, :], v4[r, keys, :], table(2, True))
        og4[1, r, rows, :] = o
        lg4[1, r, rows, :] = l
        return carry

    n_blocks = ATT_BLOCK // WIN_KEYS
    for body in (body1, body2, body3):
        lax.fori_loop(0, n_blocks, body, 0, unroll=ATT_UNROLL)

    for r in range(nph):
        phase = pl.ds(r, per, stride=nph)
        la, lb, lc = lg1[phase, :], lg4[0, r], lg4[1, r]
        mx = jnp.maximum(jnp.maximum(la, lb), lc)
        wa, wb, wc = jnp.exp(la - mx), jnp.exp(lb - mx), jnp.exp(lc - mx)
        og1[phase, :] = (wa * og1[phase, :] + wb * og4[0, r] + wc * og4[1, r]) / (wa + wb + wc)
    o_ref[...] = og1[...].astype(o_ref.dtype)


def _attn_prompt(q, k, v, vecs, batch, seq):
    assert [d for _, d in DILATED_GROUPS] == [1, ATT_PHASES, ATT_PHASES ** 2]
    nb = seq // ATT_BLOCK
    n_groups = len(DILATED_GROUPS)
    per = ATT_BLOCK // ATT_PHASES
    cur = lambda b, p, t: (b * nb + t, p)
    prev = lambda b, p, t: (b * nb + jnp.maximum(t - 1, 0), p)
    blk = lambda imap: pl.BlockSpec((ATT_BLOCK, LANES), imap)
    vmem = lambda *shape: pltpu.VMEM(shape, F32)
    return pl.pallas_call(
        _attn_prompt_kernel,
        grid=(batch, A_HEADS // 2, nb),
        in_specs=[blk(cur), blk(prev), blk(cur), blk(prev), blk(cur),
                  pl.BlockSpec((n_groups, 1, 2, 2 * WIN_KEYS), lambda b, p, t: (0, p, 0, 0))],
        out_specs=blk(cur),
        out_shape=jax.ShapeDtypeStruct((batch * seq, A_WIDTH), BF16),
        scratch_shapes=[vmem(WIN_KEYS + ATT_BLOCK, LANES), vmem(WIN_KEYS + ATT_BLOCK, LANES),
                        vmem(ATT_PHASES, 2 * per, LANES), vmem(ATT_PHASES, 2 * per, LANES),
                        vmem(ATT_PHASES, per, LANES),
                        vmem(ATT_BLOCK, LANES), vmem(ATT_BLOCK, LANES),
                        vmem(2, ATT_PHASES, per, LANES), vmem(2, ATT_PHASES, per, LANES),
                        vmem(2 * n_groups, 2 * WIN_KEYS, 2 * WIN_KEYS)],
        compiler_params=_params("arbitrary", "arbitrary", "arbitrary"),
        name="attn_prompt",
    )(q, k, k, v, v, vecs)


DECODE_CHUNK = 4096


def _decode_tables(rel_bias, n_new):
    cols = MAX_WINDOW + n_new
    c = np.arange(cols)
    dist = MAX_WINDOW - c
    cnt = np.zeros(c.shape, np.float32)
    for window, dil in DILATED_GROUPS:
        cnt += ((dist >= 0) & (dist <= window) & (dist % dil == 0)).astype(np.float32)
    vec = _select_bias(rel_bias, np.clip(dist, 0, MAX_WINDOW), cnt > 0)
    vec = vec + jnp.asarray(np.log(np.maximum(cnt, 1.0)), F32)[None, :]
    rows = jnp.stack([jnp.pad(vec[:, :cols - i], ((0, 0), (i, 0)), constant_values=NEG_INF)
                      for i in range(n_new)], axis=1)
    same_head = jnp.asarray(np.eye(A_HEADS, dtype=bool))[:, None, None, :]
    full = jnp.where(same_head, rows[..., None], NEG_INF)
    full = full.reshape(A_HEADS * n_new, cols * A_HEADS)
    return full[:, :MAX_WINDOW * A_HEADS], full[:, MAX_WINDOW * A_HEADS:]


def _attn_sample_kernel(n_buf_rows, q_ref, kn_ref, vn_ref, kc_ref, vc_ref, tc_ref, tn_ref,
                        o_ref, ko_ref, vo_ref, sems):
    b = pl.program_id(0)
    n_new_rows = kn_ref.shape[1]
    kept = n_buf_rows - n_new_rows

    def shifts():
        return [pltpu.make_async_copy(kc_ref.at[0, pl.ds(n_new_rows, kept)],
                                      ko_ref.at[b, pl.ds(0, kept)], sems.at[0]),
                pltpu.make_async_copy(kn_ref.at[0], ko_ref.at[b, pl.ds(kept, n_new_rows)], sems.at[1]),
                pltpu.make_async_copy(vc_ref.at[0, pl.ds(n_new_rows, kept)],
                                      vo_ref.at[b, pl.ds(0, kept)], sems.at[2]),
                pltpu.make_async_copy(vn_ref.at[0], vo_ref.at[b, pl.ds(kept, n_new_rows)], sems.at[3])]

    for copy in shifts():
        copy.start()

    qb = q_ref[0].astype(BF16)
    rows = qb.shape[0]
    mx = jnp.full((rows, 1), NEG_INF, F32)
    den = jnp.zeros((rows, 1), F32)
    acc = jnp.zeros((rows, A_HEAD_DIM), F32)
    parts = [(kc_ref[0, c:c + DECODE_CHUNK, :], vc_ref[0, c:c + DECODE_CHUNK, :],
              tc_ref[:, c:c + DECODE_CHUNK]) for c in range(0, n_buf_rows, DECODE_CHUNK)]
    parts.append((kn_ref[0], vn_ref[0], tn_ref[...]))
    for kk, vv, tab in parts:
        s = _dot_nt(qb, kk.astype(BF16)) + tab
        mx_new = jnp.maximum(mx, jnp.max(s, axis=-1, keepdims=True))
        scale = jnp.exp(mx - mx_new)
        p = jnp.exp(s - mx_new)
        den = scale * den + jnp.sum(p, axis=-1, keepdims=True)
        acc = scale * acc + _dot(p.astype(BF16), vv.astype(BF16))
        mx = mx_new
    o_ref[0] = (acc / den).astype(o_ref.dtype)

    for copy in shifts():
        copy.wait()


def _attn_sample(q, k_new, v_new, cache_k, cache_v, layer, tables):
    b, rows, _ = q.shape
    n_buf_rows = cache_k.shape[1]
    new = pl.BlockSpec((1, rows, A_HEAD_DIM), lambda i: (i, 0, 0))
    past = pl.BlockSpec((1, n_buf_rows, A_HEAD_DIM), lambda i: (layer * b + i, 0, 0))
    hbm = pl.BlockSpec(memory_space=pl.ANY)
    buf = jax.ShapeDtypeStruct((b, n_buf_rows, A_HEAD_DIM), F32)
    return pl.pallas_call(
        functools.partial(_attn_sample_kernel, n_buf_rows),
        grid=(b,),
        in_specs=[new, new, new, past, past, _full(tables[0].shape), _full(tables[1].shape)],
        out_specs=[new, hbm, hbm],
        out_shape=[jax.ShapeDtypeStruct((b, rows, A_HEAD_DIM), BF16), buf, buf],
        scratch_shapes=[pltpu.SemaphoreType.DMA((4,))],
        compiler_params=_params("arbitrary"),
        name="attn_sample",
    )(q, k_new, v_new, cache_k, cache_v, *tables)


CONV_PAD = 32
CONV_ROWS = 32
CONV_UNROLL = 2


def _conv_kernel(tc, u_ref, up_ref, hist_ref, w_ref, b_ref, g_ref, beta_ref, o_ref, win, stage):
    t = pl.program_id(1)
    n_slab = CONV_CH // LANES
    slab = lambda c: slice(c * LANES, (c + 1) * LANES)
    for c in range(n_slab):
        win[c, CONV_PAD:CONV_PAD + tc, :] = u_ref[0, :, slab(c)]

    @pl.when(t == 0)
    def _():
        for c in range(n_slab):
            win[c, 0:CONV_PAD, :] = hist_ref[0, :, slab(c)]

    @pl.when(t > 0)
    def _():
        for c in range(n_slab):
            win[c, 0:CONV_PAD, :] = up_ref[0, :, slab(c)]

    off = CONV_PAD - (CONV_WIDTH - 1)
    rc = min(CONV_ROWS, tc)
    half = rc // 2

    def tap(k, c):
        w = w_ref[k, :, slab(c)]
        if half < SUBLANES:
            return w[0:half]
        return jnp.concatenate([w] * (half // SUBLANES), axis=0)

    def body(j, carry):
        r0 = j * rc
        for c in range(n_slab):
            for par in range(2):
                acc = jnp.zeros((half, LANES), F32) + b_ref[:, slab(c)]
                for k in range(CONV_WIDTH):
                    rows = pl.ds(r0 + off + k + par, half, stride=2)
                    acc = acc + win[c, rows, :] * tap(k, c)
                stage[c, pl.ds(r0 + par, half, stride=2), :] = acc
        return carry

    if tc == rc:
        body(0, 0)
    else:
        lax.fori_loop(0, tc // rc, body, 0, unroll=CONV_UNROLL)
    y = jnp.concatenate([stage[c] for c in range(n_slab)], axis=1)
    xc = y - jnp.mean(y, axis=-1, keepdims=True)
    y = xc * lax.rsqrt(jnp.mean(xc * xc, axis=-1, keepdims=True) + EPS)
    y = y * g_ref[...] + beta_ref[...]
    o_ref[0] = (y * _sigmoid(y)).astype(o_ref.dtype)


def _conv(u, hist, w, b, g, beta, tc):
    bsz, t, _ = u.shape
    per = tc // CONV_PAD
    if t >= CONV_PAD:
        prev = pl.BlockSpec((1, CONV_PAD, CONV_CH), lambda i, j: (i, jnp.maximum(j * per - 1, 0), 0))
        u_prev = u
    else:
        prev = pl.BlockSpec((1, CONV_PAD, CONV_CH), lambda i, j: (i, 0, 0))
        u_prev = hist
    return pl.pallas_call(
        functools.partial(_conv_kernel, tc),
        grid=(bsz, t // tc),
        in_specs=[pl.BlockSpec((1, tc, CONV_CH), lambda i, j: (i, j, 0)),
                  prev,
                  pl.BlockSpec((1, CONV_PAD, CONV_CH), lambda i, j: (i, 0, 0)),
                  _full(w.shape), _full((1, CONV_CH)), _full((1, CONV_CH)),
                  _full((1, CONV_CH))],
        out_specs=pl.BlockSpec((1, tc, CONV_CH), lambda i, j: (i, j, 0)),
        out_shape=jax.ShapeDtypeStruct((bsz, t, CONV_CH), BF16),
        scratch_shapes=[pltpu.VMEM((CONV_CH // LANES, CONV_PAD + tc, LANES), F32),
                        pltpu.VMEM((CONV_CH // LANES, tc, LANES), F32)],
        compiler_params=_params("parallel", "arbitrary"),
        name="conv",
    )(u, u_prev, hist, w, b, g, beta)


def _gla_in_kernel(x_ref, g_ref, w_ref, wl_ref, wu_ref, bu_ref, q_ref, k_ref, v_ref, r_ref, la_ref):
    h = _rms_rows(x_ref[...], g_ref[...]).astype(BF16)
    p = _dot(h, w_ref[...])
    q_ref[...] = p[:, 0:C_DK] * (C_DK_HEAD ** -0.5)
    k_ref[...] = p[:, C_DK:2 * C_DK]
    v_ref[...] = p[:, 2 * C_DK:2 * C_DK + C_DV]
    r = p[:, 2 * C_DK + C_DV:2 * C_DK + 2 * C_DV]
    r_ref[...] = r * _sigmoid(r)
    low = _dot(h, wl_ref[...]).astype(BF16)
    z = _dot(low, wu_ref[...]) + bu_ref[...]
    log_sig = jnp.minimum(z, 0.0) - jnp.log1p(jnp.exp(-jnp.abs(z)))
    la_ref[...] = log_sig * (1.0 / GATE_TAU)


def _gla_in(x, g, w, wl, wu, bu, tm):
    n = x.shape[0]
    row = lambda width: pl.BlockSpec((tm, width), lambda i: (i, 0))
    sds = lambda width: jax.ShapeDtypeStruct((n, width), F32)
    return pl.pallas_call(
        _gla_in_kernel,
        grid=(n // tm,),
        in_specs=[row(D_MODEL), _full((1, D_MODEL)), _full(w.shape), _full(wl.shape),
                  _full(wu.shape), _full((1, C_DK))],
        out_specs=[row(C_DK), row(C_DK), row(C_DV), row(C_DV), row(C_DK)],
        out_shape=[sds(C_DK), sds(C_DK), sds(C_DV), sds(C_DV), sds(C_DK)],
        compiler_params=_params("parallel"),
        name="gla_in",
    )(x, g, w, wl, wu, bu)


def _gla_kernel(chunk, n_chunks, q_ref, k_ref, v_ref, r_ref, la_ref, s0_ref, gain_ref,
                o_ref, s_ref, qin_s, kin_s, x_s, qst_s, kst_s, dec_s):
    @pl.when(pl.program_id(1) == 0)
    def _():
        s_ref[...] = s0_ref[...]

    half = chunk // 2
    ri = lax.broadcasted_iota(jnp.int32, (chunk, chunk), 0)
    ci = lax.broadcasted_iota(jnp.int32, (chunk, chunk), 1)
    causal = ci <= ri
    tri = jnp.where(causal, 1.0, 0.0).astype(BF16)
    cross = jnp.logical_and(ri >= half, ci < half)
    same_half = jnp.logical_and(causal, jnp.logical_not(cross))
    in_first = lax.broadcasted_iota(jnp.int32, (chunk, C_DK), 0) < half
    gain = gain_ref[...]

    def prepare(c):
        rows = slice(c * chunk, (c + 1) * chunk)
        la = la_ref[0, rows, :]
        la_hi = la.astype(BF16)
        la_lo = (la - la_hi.astype(F32)).astype(BF16)
        cum = _dot(tri, la_hi) + _dot(tri, la_lo)
        row = lambda i: cum[i:i + 1, :]
        last = row(chunk - 1)
        edge = row(half - 1)
        mid = jnp.where(in_first, row(half // 2 - 1), row(half + half // 2 - 1))
        q = q_ref[0, rows, :]
        k = k_ref[0, rows, :]
        grow = jnp.exp(cum - mid)
        qin_s[rows, :] = (q * grow).astype(BF16)
        kin_s[rows, :] = (k / grow).astype(BF16)
        k_x = k[:half] * jnp.exp(edge - cum[:half])
        q_x = q[half:] * jnp.exp(cum[half:] - edge)
        x_s[rows, :] = jnp.concatenate([k_x, q_x], axis=0).astype(BF16)
        qst_s[rows, :] = (q * jnp.exp(cum)).astype(BF16)
        kst_s[rows, :] = (k * jnp.exp(last - cum)).astype(BF16)
        dec_s[c] = jnp.broadcast_to(jnp.exp(last), (LANES, C_DK)).T

    def advance(c):
        rows = slice(c * chunk, (c + 1) * chunk)
        for h in range(C_HEADS):
            ks = slice(h * C_DK_HEAD, (h + 1) * C_DK_HEAD)
            vs = slice(h * C_DV_HEAD, (h + 1) * C_DV_HEAD)
            vh = v_ref[0, rows, vs].astype(BF16)
            xh = x_s[rows, ks]
            att = jnp.where(same_half, _dot_nt(qin_s[rows, ks], kin_s[rows, ks]),
                            jnp.where(cross, _dot_nt(xh, xh), 0.0))
            s = s_ref[0, h]
            o = _dot(jnp.concatenate([att.astype(BF16), qst_s[rows, ks]], axis=1),
                     jnp.concatenate([vh, s.astype(BF16)], axis=0))
            decay = dec_s[c, ks, :]
            s_ref[0, h] = (s * jnp.concatenate([decay] * (C_DV_HEAD // LANES), axis=1)
                           + _dot_tn(kst_s[rows, ks], vh))
            y = o * lax.rsqrt(jnp.mean(o * o, axis=-1, keepdims=True) + EPS) * gain
            o_ref[0, rows, vs] = (y * r_ref[0, rows, vs]).astype(o_ref.dtype)

    prepare(0)
    for c in range(n_chunks):
        if c + 1 < n_chunks:
            prepare(c + 1)
        advance(c)


def _gla(q, k, v, r, la, s0, gain, chunk, tb):
    b, t, _ = q.shape
    seq = lambda width: pl.BlockSpec((1, tb, width), lambda i, j: (i, j, 0))
    state = pl.BlockSpec((1, C_HEADS, C_DK_HEAD, C_DV_HEAD), lambda i, j: (i, 0, 0, 0))
    return pl.pallas_call(
        functools.partial(_gla_kernel, chunk, tb // chunk),
        grid=(b, t // tb),
        in_specs=[seq(C_DK), seq(C_DK), seq(C_DV), seq(C_DV), seq(C_DK), state,
                  _full((1, C_DV_HEAD))],
        out_specs=[seq(C_DV), state],
        out_shape=[jax.ShapeDtypeStruct((b, t, C_DV), BF16),
                   jax.ShapeDtypeStruct(s0.shape, F32)],
        scratch_shapes=[pltpu.VMEM((tb, C_DK), BF16)] * 5
                       + [pltpu.VMEM((tb // chunk, C_DK, LANES), F32)],
        compiler_params=_params("parallel", "arbitrary"),
        name="gla",
    )(q, k, v, r, la, s0, gain)


GLA_CHUNK = 128
SAMPLE_PAD = 16


def _trunk(x, past, P, tm, conv_tc, gla_tb):
    bsz, t, _ = x.shape
    n = bsz * t
    x = x.reshape(n, D_MODEL)
    outs = {}
    for layer in range(DEPTH):
        i = layer // 2
        x = _ffn(x, P['norm_ffn1'][layer], P['ffn1_w_in'][layer], P['ffn1_w_out'][layer], tm)
        if layer % 2 == 0:
            q, k, v, u = _even_in(x, P['norm_mix'][layer], P['ev_w_in'][i], P['ev_q_gain'][i],
                                  P['ev_k_gain'][i], P['head_ones'], tm)
            u3 = u.reshape(bsz, t, CONV_CH)
            if past is None:
                a = _attn_prompt(q, k, v, P['band_vectors'], bsz, t)
                keep = min(MAX_WINDOW, t)
                tail = lambda z: z.reshape(bsz, t, A_WIDTH)[:, t - keep:].reshape(
                    bsz, keep, A_HEADS, A_HEAD_DIM)
                new_k, new_v = tail(k), tail(v)
                hist = jnp.zeros((bsz, CONV_PAD, CONV_CH), F32)
                new_u = u3[:, t - (CONV_WIDTH - 1):]
            else:
                n_buf = past[0].shape[2]
                heads = lambda z: z.reshape(bsz, t, A_HEADS, A_HEAD_DIM)
                rows = lambda z: z.reshape(bsz, t * A_HEADS, A_HEAD_DIM)
                flat = lambda z: z.reshape(z.shape[0] * bsz, n_buf * A_HEADS, A_HEAD_DIM)
                a, new_k, new_v = _attn_sample(
                    rows(heads(q).transpose(0, 2, 1, 3)), rows(k), rows(v),
                    flat(past[0]), flat(past[1]), i, P['decode_tables'])
                a = a.reshape(bsz, A_HEADS, t, A_HEAD_DIM).transpose(0, 2, 1, 3).reshape(n, A_WIDTH)
                new_k = new_k.reshape(bsz, n_buf, A_HEADS, A_HEAD_DIM)
                new_v = new_v.reshape(bsz, n_buf, A_HEADS, A_HEAD_DIM)
                hist = jnp.pad(past[2][i], ((0, 0), (CONV_PAD - (CONV_WIDTH - 1), 0), (0, 0)))
                new_u = jnp.concatenate([past[2][i], u3], axis=1)[:, -(CONV_WIDTH - 1):]
            c = _conv(u3, hist, P['ev_conv_w'][i], P['ev_conv_b'][i], P['ev_conv_ln_g'][i],
                      P['ev_conv_ln_b'][i], conv_tc).reshape(n, CONV_CH)
            mixed, w_mix = [a, c], [P['ev_w_out_a'][i], P['ev_w_out_c'][i]]
            outs.setdefault('k', []).append(new_k)
            outs.setdefault('v', []).append(new_v)
            outs.setdefault('u', []).append(new_u)
        else:
            q, k, v, r, la = _gla_in(x, P['norm_mix'][layer], P['od_w_main'][i], P['od_w_low'][i],
                                     P['od_gate_w_up'][i], P['od_gate_b'][i], tm)
            if past is None:
                s0 = jnp.zeros((bsz, C_HEADS, C_DK_HEAD, C_DV_HEAD), F32)
                tp, chunk = t, GLA_CHUNK
            else:
                s0 = past[3][i]
                tp, chunk = SAMPLE_PAD, SAMPLE_PAD
            seq = lambda z: jnp.pad(z.reshape(bsz, t, -1), ((0, 0), (0, tp - t), (0, 0)))
            o, s = _gla(seq(q), seq(k), seq(v), seq(r), seq(la), s0, P['od_o_gain'][i], chunk,
                        min(gla_tb, tp))
            mixed, w_mix = [o[:, :t].reshape(n, C_DV)], [P['od_w_out'][i]]
            outs.setdefault('s', []).append(s)
        x = _ffn(x, P['norm_ffn2'][layer], P['ffn2_w_in'][layer], P['ffn2_w_out'][layer], tm,
                 mixed, w_mix)
    return (x.reshape(bsz, t, D_MODEL), jnp.stack(outs['k']), jnp.stack(outs['v']),
            jnp.stack(outs['u']), jnp.stack(outs['s']))


def kernel(x_prompt, x_sample, cache_k, cache_v, cache_conv, state_gla, rel_bias, norm_ffn1, ffn1_w_in, ffn1_w_out, norm_mix, norm_ffn2, ffn2_w_in, ffn2_w_out, ev_w_in, ev_q_gain, ev_k_gain, ev_conv_w, ev_conv_b, ev_conv_ln_g, ev_conv_ln_b, ev_w_out, od_w_in, od_gate_w_up, od_gate_b, od_o_gain, od_w_out):
    n_even = ev_w_in.shape[0]
    n_odd = od_w_in.shape[0]
    main = 2 * C_DK + 2 * C_DV
    head_ids = np.arange(MXU_DIM) // A_HEAD_DIM
    per = lambda n, f: [f(j) for j in range(n)]
    row = lambda z: z[None, :]
    P = {
        'norm_ffn1': per(DEPTH, lambda j: row(norm_ffn1[j])),
        'norm_mix': per(DEPTH, lambda j: row(norm_mix[j])),
        'norm_ffn2': per(DEPTH, lambda j: row(norm_ffn2[j])),
        'ffn1_w_in': per(DEPTH, lambda j: _weight_bf16(ffn1_w_in, j)),
        'ffn1_w_out': per(DEPTH, lambda j: _weight_bf16(ffn1_w_out, j)),
        'ffn2_w_in': per(DEPTH, lambda j: _weight_bf16(ffn2_w_in, j)),
        'ffn2_w_out': per(DEPTH, lambda j: _weight_bf16(ffn2_w_out, j)),
        'ev_w_in': per(n_even, lambda j: _weight_bf16(ev_w_in, j)),
        'ev_q_gain': per(n_even, lambda j: row(jnp.tile(ev_q_gain[j], A_HEADS))),
        'ev_k_gain': per(n_even, lambda j: row(jnp.tile(ev_k_gain[j], A_HEADS))),
        'head_ones': jnp.asarray(head_ids[:, None] == head_ids[None, :], BF16),
        'ev_conv_w': per(n_even, lambda j: jnp.broadcast_to(
            ev_conv_w[j][:, None, :], (CONV_WIDTH, SUBLANES, CONV_CH))),
        'ev_conv_b': per(n_even, lambda j: row(ev_conv_b[j])),
        'ev_conv_ln_g': per(n_even, lambda j: row(ev_conv_ln_g[j])),
        'ev_conv_ln_b': per(n_even, lambda j: row(ev_conv_ln_b[j])),
        'ev_w_out_a': per(n_even, lambda j: _weight_bf16(ev_w_out, j, 0, A_WIDTH)),
        'ev_w_out_c': per(n_even, lambda j: _weight_bf16(ev_w_out, j, A_WIDTH, CONV_CH)),
        'od_w_main': per(n_odd, lambda j: _weight_bf16(od_w_in, j, n_cols=main)),
        'od_w_low': per(n_odd, lambda j: jnp.pad(od_w_in[j, :, main:],
                                                 ((0, 0), (0, LANES - GATE_RANK))).astype(BF16)),
        'od_gate_w_up': per(n_odd, lambda j: jnp.pad(od_gate_w_up[j],
                                                     ((0, LANES - GATE_RANK), (0, 0))).astype(BF16)),
        'od_gate_b': per(n_odd, lambda j: row(od_gate_b[j])),
        'od_o_gain': per(n_odd, lambda j: row(od_o_gain[j])),
        'od_w_out': per(n_odd, lambda j: _weight_bf16(od_w_out, j)),
        'band_vectors': _band_vectors(rel_bias),
        'decode_tables': _decode_tables(rel_bias, x_sample.shape[1]),
    }
    P, x_prompt, x_sample = lax.optimization_barrier((P, x_prompt, x_sample))
    y_p, k_p, v_p, u_p, s_p = _trunk(x_prompt, None, P, tm=512, conv_tc=512, gla_tb=512)
    y_s, k_s, v_s, u_s, s_s = _trunk(x_sample, (cache_k, cache_v, cache_conv, state_gla), P,
                                     tm=256, conv_tc=x_sample.shape[1], gla_tb=SAMPLE_PAD)
    return (y_p, y_s, k_p, v_p, u_p, s_p, k_s, v_s, u_s, s_s)
```

```python
import functools

import numpy as np
import jax
import jax.numpy as jnp
from jax import lax
from jax.experimental import pallas as pl
from jax.experimental.pallas import tpu as pltpu

F32 = jnp.float32
BF16 = jnp.bfloat16

D_MODEL = 1024
DEPTH = 2
PAST_LEN = 16384
A_HEADS = 8
A_HEAD_DIM = 64
A_WIDTH = A_HEADS * A_HEAD_DIM
DILATED_GROUPS = ((128, 1), (512, 4), (2048, 16))
MAX_WINDOW = 2048
N_BUCKETS = 32
CONV_WIDTH = 31
CONV_CH = 512
C_HEADS = 4
C_DK = 512
C_DV = 1024
C_DK_HEAD = 128
C_DV_HEAD = 256
GATE_RANK = 16
GATE_TAU = 16.0
D_FF = 2816
EPS = 1e-6
NEG_INF = -1e30

LANES = 128
SUBLANES = 8
MXU_DIM = 256
WIN_KEYS = 128
ATT_BLOCK = 2048
ATT_UNROLL = 16
ATT_PHASES = 4
VMEM_LIMIT = 56 * 1024 * 1024


def _params(*sem):
    return pltpu.CompilerParams(dimension_semantics=sem, vmem_limit_bytes=VMEM_LIMIT)


def _dot(a, b):
    return jnp.dot(a, b, preferred_element_type=F32)


def _dot_nt(a, b):
    return lax.dot_general(a, b, (((1,), (1,)), ((), ())), preferred_element_type=F32)


def _dot_tn(a, b):
    return lax.dot_general(a, b, (((0,), (0,)), ((), ())), preferred_element_type=F32)


def _rms_rows(x, g):
    y = x * lax.rsqrt(jnp.mean(x * x, axis=-1, keepdims=True) + EPS)
    return y * g


def _sigmoid(x):
    return 1.0 / (1.0 + jnp.exp(-x))


def _full(shape):
    return pl.BlockSpec(shape, lambda *_: (0,) * len(shape), pipeline_mode=pl.Buffered(1))


FF_CHUNK = 256


def _ffn_kernel(n_pre, x_ref, *refs):
    m_refs, w_refs = refs[:n_pre], refs[n_pre:2 * n_pre]
    g_ref, wi_ref, wo_ref, o_ref, act_ref = refs[2 * n_pre:]
    x = x_ref[...]
    for m_ref, w_ref in zip(m_refs, w_refs):
        x = x + _dot(m_ref[...], w_ref[...])
    h = _rms_rows(x, g_ref[...]).astype(BF16)
    for c in range(D_FF // FF_CHUNK):
        lo = c * FF_CHUNK
        a = _dot(h, wi_ref[:, lo:lo + FF_CHUNK])
        b = _dot(h, wi_ref[:, D_FF + lo:D_FF + lo + FF_CHUNK])
        act_ref[:, lo:lo + FF_CHUNK] = (a * _sigmoid(a) * b).astype(BF16)
    o_ref[...] = x + 0.5 * _dot(act_ref[...], wo_ref[...])


def _ffn(x, g, wi, wo, tm, ms=(), ws=()):
    n = x.shape[0]
    row = lambda width: pl.BlockSpec((tm, width), lambda i: (i, 0))
    return pl.pallas_call(
        functools.partial(_ffn_kernel, len(ms)),
        grid=(n // tm,),
        in_specs=[row(D_MODEL)] + [row(m.shape[1]) for m in ms] + [_full(w.shape) for w in ws]
                 + [_full((1, D_MODEL)), _full((D_MODEL, 2 * D_FF)), _full((D_FF, D_MODEL))],
        out_specs=row(D_MODEL),
        out_shape=jax.ShapeDtypeStruct((n, D_MODEL), F32),
        scratch_shapes=[pltpu.VMEM((tm, D_FF), BF16)],
        compiler_params=_params("parallel"),
        name="ffn",
    )(x, *ms, *ws, g, wi, wo)


CAST_BLOCK_BYTES = 2 * 1024 * 1024
BF16_ROWS = 16


def _cast_kernel(w_ref, o_ref):
    o_ref[...] = w_ref[...].astype(o_ref.dtype)


def _weight_bf16(w, layer, row0=0, n_rows=None, n_cols=None):
    n_rows = w.shape[1] - row0 if n_rows is None else n_rows
    n_cols = w.shape[2] if n_cols is None else n_cols
    fits = [r for r in range(BF16_ROWS, n_rows + 1, BF16_ROWS)
            if n_rows % r == 0 and row0 % r == 0 and r * n_cols * 4 <= CAST_BLOCK_BYTES]
    rows = max(fits)
    first = row0 // rows
    return pl.pallas_call(
        _cast_kernel,
        grid=(n_rows // rows,),
        in_specs=[pl.BlockSpec((None, rows, n_cols), lambda i: (layer, first + i, 0))],
        out_specs=pl.BlockSpec((rows, n_cols), lambda i: (i, 0)),
        out_shape=jax.ShapeDtypeStruct((n_rows, n_cols), BF16),
        compiler_params=_params("parallel"),
        name="weight_bf16",
    )(w)


def _head_norm(z, gain, bd):
    zz = z * z
    hi = zz.astype(BF16)
    lo = (zz - hi.astype(F32)).astype(BF16)
    width = bd.shape[0]
    ss = jnp.concatenate(
        [_dot(hi[:, c:c + width], bd) + _dot(lo[:, c:c + width], bd)
         for c in range(0, z.shape[1], width)], axis=1)
    return z * lax.rsqrt(ss * (1.0 / A_HEAD_DIM) + EPS) * gain


def _even_in_kernel(x_ref, g_ref, w_ref, qg_ref, kg_ref, bd_ref, q_ref, k_ref, v_ref, u_ref):
    h = _rms_rows(x_ref[...], g_ref[...]).astype(BF16)
    p = _dot(h, w_ref[...])
    bd = bd_ref[...]
    q_ref[...] = _head_norm(p[:, 0:A_WIDTH], qg_ref[...], bd) * (A_HEAD_DIM ** -0.5)
    k_ref[...] = _head_norm(p[:, A_WIDTH:2 * A_WIDTH], kg_ref[...], bd)
    v_ref[...] = p[:, 2 * A_WIDTH:3 * A_WIDTH]
    gv = p[:, 3 * A_WIDTH:3 * A_WIDTH + CONV_CH]
    gg = p[:, 3 * A_WIDTH + CONV_CH:3 * A_WIDTH + 2 * CONV_CH]
    u_ref[...] = gv * _sigmoid(gg)


def _even_in(x, g, w, qg, kg, bd, tm):
    n = x.shape[0]
    row = lambda width: pl.BlockSpec((tm, width), lambda i: (i, 0))
    out = jax.ShapeDtypeStruct((n, A_WIDTH), F32)
    return pl.pallas_call(
        _even_in_kernel,
        grid=(n // tm,),
        in_specs=[row(D_MODEL), _full((1, D_MODEL)), _full(w.shape), _full((1, A_WIDTH)),
                  _full((1, A_WIDTH)), _full(bd.shape)],
        out_specs=[row(A_WIDTH)] * 4,
        out_shape=[out] * 4,
        compiler_params=_params("parallel"),
        name="even_in",
    )(x, g, w, qg, kg, bd)


def _t5_bucket(dist):
    max_exact = N_BUCKETS // 2
    d = np.asarray(dist, dtype=np.int32)
    df = np.maximum(d, 1).astype(np.float32)
    large = max_exact + (np.log(df / max_exact) / np.log(MAX_WINDOW / max_exact)
                         * (N_BUCKETS - max_exact)).astype(np.int32)
    large = np.minimum(large, N_BUCKETS - 1)
    return np.where(d < max_exact, d, large).astype(np.int32)


def _select_bias(rel_bias, dist, valid):
    onehot = (_t5_bucket(dist)[None, :] == np.arange(N_BUCKETS)[:, None]) & valid[None, :]
    picked = jnp.einsum('bh,bc->hc', rel_bias, jnp.asarray(onehot, F32),
                        precision=lax.Precision.HIGHEST)
    return picked + jnp.asarray(np.where(valid, 0.0, NEG_INF), F32)[None, :]


def _band_vectors(rel_bias):
    c = np.arange(2 * WIN_KEYS)
    valid = c <= WIN_KEYS
    vecs = [_select_bias(rel_bias, np.where(valid, (WIN_KEYS - c) * dil, 0), valid)
            for _, dil in DILATED_GROUPS]
    return jnp.stack(vecs).reshape(len(DILATED_GROUPS), A_HEADS // 2, 2, 2 * WIN_KEYS)


def _attn_prompt_kernel(q_ref, kp_ref, kc_ref, vp_ref, vc_ref, vec_ref, o_ref,
                        knat, vnat, k4, v4, q4, og1, lg1, og4, lg4, tab_ref):
    blk = pl.program_id(2)
    n_groups = len(DILATED_GROUPS)
    nph = ATT_PHASES
    per = ATT_BLOCK // nph

    @pl.when(blk == 0)
    def _():
        col = lax.broadcasted_iota(jnp.int32, (WIN_KEYS, 2 * WIN_KEYS), 1)
        for g in range(n_groups):
            for hh in range(2):
                vec = jnp.broadcast_to(vec_ref[g, 0, hh:hh + 1, :], (WIN_KEYS, 2 * WIN_KEYS))
                band = pltpu.roll(vec, 0, 1, stride=1, stride_axis=0)
                rows = slice(hh * WIN_KEYS, (hh + 1) * WIN_KEYS)
                tab_ref[g, rows, :] = band
                tab_ref[n_groups + g, rows, :] = jnp.where(col >= WIN_KEYS, band, NEG_INF)

    knat[0:WIN_KEYS, :] = kp_ref[ATT_BLOCK - WIN_KEYS:, :]
    knat[WIN_KEYS:, :] = kc_ref[...]
    vnat[0:WIN_KEYS, :] = vp_ref[ATT_BLOCK - WIN_KEYS:, :]
    vnat[WIN_KEYS:, :] = vc_ref[...]
    for r in range(nph):
        phase = pl.ds(r, per, stride=nph)
        k4[r, 0:per, :] = kp_ref[phase, :]
        k4[r, per:, :] = kc_ref[phase, :]
        v4[r, 0:per, :] = vp_ref[phase, :]
        v4[r, per:, :] = vc_ref[phase, :]
        q4[r] = q_ref[phase, :]
    first = blk == 0
    lane = lax.broadcasted_iota(jnp.int32, (WIN_KEYS, LANES), 1)
    low = lane < A_HEAD_DIM
    ones = jnp.ones((2 * WIN_KEYS, LANES), BF16)

    def block(qs, kk, vv, tab):
        qs = qs.astype(BF16)
        zero = jnp.zeros_like(qs)
        qst = jnp.concatenate([jnp.where(low, qs, zero), jnp.where(low, zero, qs)], axis=0)
        s = _dot_nt(qst, kk.astype(BF16)) + tab
        mx = jnp.max(s, axis=-1, keepdims=True)
        p = jnp.exp(s - mx).astype(BF16)
        r = _dot(p, jnp.concatenate([vv.astype(BF16), ones], axis=1))
        o2 = jnp.where(low, r[0:WIN_KEYS, 0:LANES], r[WIN_KEYS:, 0:LANES])
        l2 = jnp.where(low, r[0:WIN_KEYS, LANES:], r[WIN_KEYS:, LANES:])
        m2 = jnp.where(low, jnp.broadcast_to(mx[0:WIN_KEYS], (WIN_KEYS, LANES)),
                       jnp.broadcast_to(mx[WIN_KEYS:], (WIN_KEYS, LANES)))
        return o2 / l2, m2 + jnp.log(l2)

    def table(g, at_start):
        return tab_ref[jnp.where(jnp.logical_and(at_start, first), n_groups + g, g)]

    def body1(sub, carry):
        i0 = pl.multiple_of(sub * WIN_KEYS, WIN_KEYS)
        o, l = block(q_ref[pl.ds(i0, WIN_KEYS), :], knat[pl.ds(i0, 2 * WIN_KEYS), :],
                     vnat[pl.ds(i0, 2 * WIN_KEYS), :], table(0, sub == 0))
        og1[pl.ds(i0, WIN_KEYS), :] = o
        lg1[pl.ds(i0, WIN_KEYS), :] = l
        return carry

    def body2(pb, carry):
        sub = pb // nph
        r = pb - sub * nph
        i0 = pl.multiple_of(sub * WIN_KEYS, WIN_KEYS)
        keys = pl.ds(i0 + (per - WIN_KEYS), 2 * WIN_KEYS)
        o, l = block(q4[r, pl.ds(i0, WIN_KEYS), :], k4[r, keys, :], v4[r, keys, :],
                     table(1, sub == 0))
        og4[0, r, pl.ds(i0, WIN_KEYS), :] = o
        lg4[0, r, pl.ds(i0, WIN_KEYS), :] = l
        return carry

    def body3(pb, carry):
        a = pb // nph
        r = pb - a * nph
        rows = pl.ds(a, WIN_KEYS, stride=nph)
        keys = pl.ds(a, 2 * WIN_KEYS, stride=nph)
        o, l = block(q4[r, rows, :], k4[r, keys, :], v4[r, keys, :], table(2, True))
        og4[1, r, rows, :] = o
        lg4[1, r, rows, :] = l
        return carry

    n_blocks = ATT_BLOCK // WIN_KEYS
    for body in (body1, body2, body3):
        lax.fori_loop(0, n_blocks, body, 0, unroll=ATT_UNROLL)

    for r in range(nph):
        phase = pl.ds(r, per, stride=nph)
        la, lb, lc = lg1[phase, :], lg4[0, r], lg4[1, r]
        mx = jnp.maximum(jnp.maximum(la, lb), lc)
        wa, wb, wc = jnp.exp(la - mx), jnp.exp(lb - mx), jnp.exp(lc - mx)
        og1[phase, :] = (wa * og1[phase, :] + wb * og4[0, r] + wc * og4[1, r]) / (wa + wb + wc)
    o_ref[...] = og1[...].astype(o_ref.dtype)


def _attn_prompt(q, k, v, vecs, batch, seq):
    assert [d for _, d in DILATED_GROUPS] == [1, ATT_PHASES, ATT_PHASES ** 2]
    nb = seq // ATT_BLOCK
    n_groups = len(DILATED_GROUPS)
    per = ATT_BLOCK // ATT_PHASES
    cur = lambda b, p, t: (b * nb + t, p)
    prev = lambda b, p, t: (b * nb + jnp.maximum(t - 1, 0), p)
    blk = lambda imap: pl.BlockSpec((ATT_BLOCK, LANES), imap)
    vmem = lambda *shape: pltpu.VMEM(shape, F32)
    return pl.pallas_call(
        _attn_prompt_kernel,
        grid=(batch, A_HEADS // 2, nb),
        in_specs=[blk(cur), blk(prev), blk(cur), blk(prev), blk(cur),
                  pl.BlockSpec((n_groups, 1, 2, 2 * WIN_KEYS), lambda b, p, t: (0, p, 0, 0))],
        out_specs=blk(cur),
        out_shape=jax.ShapeDtypeStruct((batch * seq, A_WIDTH), BF16),
        scratch_shapes=[vmem(WIN_KEYS + ATT_BLOCK, LANES), vmem(WIN_KEYS + ATT_BLOCK, LANES),
                        vmem(ATT_PHASES, 2 * per, LANES), vmem(ATT_PHASES, 2 * per, LANES),
                        vmem(ATT_PHASES, per, LANES),
                        vmem(ATT_BLOCK, LANES), vmem(ATT_BLOCK, LANES),
                        vmem(2, ATT_PHASES, per, LANES), vmem(2, ATT_PHASES, per, LANES),
                        vmem(2 * n_groups, 2 * WIN_KEYS, 2 * WIN_KEYS)],
        compiler_params=_params("arbitrary", "arbitrary", "arbitrary"),
        name="attn_prompt",
    )(q, k, k, v, v, vecs)


def _decode_tables(rel_bias, n_new):
    cols = MAX_WINDOW + n_new
    c = np.arange(cols)
    dist = MAX_WINDOW - c
    cnt = np.zeros(c.shape, np.float32)
    for window, dil in DILATED_GROUPS:
        cnt += ((dist >= 0) & (dist <= window) & (dist % dil == 0)).astype(np.float32)
    vec = _select_bias(rel_bias, np.clip(dist, 0, MAX_WINDOW), cnt > 0)
    vec = vec + jnp.asarray(np.log(np.maximum(cnt, 1.0)), F32)[None, :]
    rows = jnp.stack([jnp.pad(vec[:, :cols - i], ((0, 0), (i, 0)), constant_values=NEG_INF)
                      for i in range(n_new)], axis=1)
    rows = rows.reshape(A_HEADS * n_new, cols)
    return rows[:, :MAX_WINDOW], rows[:, MAX_WINDOW:]


def _attn_sample_kernel(q_ref, kn_ref, vn_ref, kc_ref, vc_ref, tc_ref, tn_ref,
                        o_ref, ko_ref, vo_ref):
    n_new = q_ref.shape[1]
    n_buf = kc_ref.shape[2]
    rows = A_HEADS * n_new
    kn = kn_ref[0]
    vn = vn_ref[0]
    kc = kc_ref[0]
    vc = vc_ref[0]

    lane = lax.broadcasted_iota(jnp.int32, (A_WIDTH, LANES), 1)
    for new, old, out_ref in ((kn, kc, ko_ref), (vn, vc, vo_ref)):
        shifted = pltpu.roll(old, n_buf - n_new, 1)
        tail = jnp.concatenate([jnp.zeros((LANES - n_new, A_WIDTH), F32), new], axis=0).T
        out_ref[0, :, 0:n_buf - LANES] = shifted[:, 0:n_buf - LANES]
        out_ref[0, :, n_buf - LANES:] = jnp.where(lane >= LANES - n_new, tail,
                                                  shifted[:, n_buf - LANES:])

    q = q_ref[0]
    row_head = lax.broadcasted_iota(jnp.int32, (A_HEADS, n_new, A_WIDTH), 0).reshape(rows, A_WIDTH)
    col = lax.broadcasted_iota(jnp.int32, (rows, A_WIDTH), 1)
    own = jnp.logical_and(col >= row_head * A_HEAD_DIM, col < (row_head + 1) * A_HEAD_DIM)
    qblk = jnp.where(own, jnp.concatenate([q] * A_HEADS, axis=0), 0.0).astype(BF16)
    s_c = _dot(qblk, kc.astype(BF16)) + tc_ref[...]
    s_n = _dot_nt(qblk, kn.astype(BF16)) + tn_ref[...]
    mx = jnp.maximum(jnp.max(s_c, axis=-1, keepdims=True), jnp.max(s_n, axis=-1, keepdims=True))
    p_c = jnp.exp(s_c - mx)
    p_n = jnp.exp(s_n - mx)
    den = jnp.sum(p_c, axis=-1, keepdims=True) + jnp.sum(p_n, axis=-1, keepdims=True)
    acc = _dot_nt(p_c.astype(BF16), vc.astype(BF16)) + _dot(p_n.astype(BF16), vn.astype(BF16))
    acc = jnp.where(own, acc / den, 0.0)
    out = acc[0:n_new]
    for h in range(1, A_HEADS):
        out = out + acc[h * n_new:(h + 1) * n_new]
    o_ref[0] = out.astype(o_ref.dtype)


def _attn_sample(q, k_new, v_new, cache_k, cache_v, layer, tables):
    b, n_new, _ = q.shape
    n_buf = cache_k.shape[2]
    new = pl.BlockSpec((1, n_new, A_WIDTH), lambda i: (i, 0, 0))
    buf = pl.BlockSpec((1, A_WIDTH, n_buf), lambda i: (i, 0, 0))
    past = pl.BlockSpec((1, A_WIDTH, n_buf), lambda i: (layer * b + i, 0, 0))
    return pl.pallas_call(
        _attn_sample_kernel,
        grid=(b,),
        in_specs=[new, new, new, past, past, _full(tables[0].shape), _full(tables[1].shape)],
        out_specs=[new, buf, buf],
        out_shape=[jax.ShapeDtypeStruct((b, n_new, A_WIDTH), BF16),
                   jax.ShapeDtypeStruct((b, A_WIDTH, n_buf), F32),
                   jax.ShapeDtypeStruct((b, A_WIDTH, n_buf), F32)],
        compiler_params=_params("parallel"),
        name="attn_sample",
    )(q, k_new, v_new, cache_k, cache_v, *tables)


CONV_PAD = 32
CONV_ROWS = 32
CONV_UNROLL = 2


def _conv_kernel(tc, u_ref, up_ref, hist_ref, w_ref, b_ref, g_ref, beta_ref, o_ref, win, stage):
    t = pl.program_id(1)
    n_slab = CONV_CH // LANES
    slab = lambda c: slice(c * LANES, (c + 1) * LANES)
    for c in range(n_slab):
        win[c, CONV_PAD:CONV_PAD + tc, :] = u_ref[0, :, slab(c)]

    @pl.when(t == 0)
    def _():
        for c in range(n_slab):
            win[c, 0:CONV_PAD, :] = hist_ref[0, :, slab(c)]

    @pl.when(t > 0)
    def _():
        for c in range(n_slab):
            win[c, 0:CONV_PAD, :] = up_ref[0, :, slab(c)]

    off = CONV_PAD - (CONV_WIDTH - 1)
    rc = min(CONV_ROWS, tc)
    half = rc // 2

    def tap(k, c):
        w = w_ref[k, :, slab(c)]
        if half < SUBLANES:
            return w[0:half]
        return jnp.concatenate([w] * (half // SUBLANES), axis=0)

    def body(j, carry):
        r0 = j * rc
        for c in range(n_slab):
            for par in range(2):
                acc = jnp.zeros((half, LANES), F32) + b_ref[:, slab(c)]
                for k in range(CONV_WIDTH):
                    rows = pl.ds(r0 + off + k + par, half, stride=2)
                    acc = acc + win[c, rows, :] * tap(k, c)
                stage[c, pl.ds(r0 + par, half, stride=2), :] = acc
        return carry

    if tc == rc:
        body(0, 0)
    else:
        lax.fori_loop(0, tc // rc, body, 0, unroll=CONV_UNROLL)
    y = jnp.concatenate([stage[c] for c in range(n_slab)], axis=1)
    xc = y - jnp.mean(y, axis=-1, keepdims=True)
    y = xc * lax.rsqrt(jnp.mean(xc * xc, axis=-1, keepdims=True) + EPS)
    y = y * g_ref[...] + beta_ref[...]
    o_ref[0] = (y * _sigmoid(y)).astype(o_ref.dtype)


def _conv(u, hist, w, b, g, beta, tc):
    bsz, t, _ = u.shape
    per = tc // CONV_PAD
    if t >= CONV_PAD:
        prev = pl.BlockSpec((1, CONV_PAD, CONV_CH), lambda i, j: (i, jnp.maximum(j * per - 1, 0), 0))
        u_prev = u
    else:
        prev = pl.BlockSpec((1, CONV_PAD, CONV_CH), lambda i, j: (i, 0, 0))
        u_prev = hist
    return pl.pallas_call(
        functools.partial(_conv_kernel, tc),
        grid=(bsz, t // tc),
        in_specs=[pl.BlockSpec((1, tc, CONV_CH), lambda i, j: (i, j, 0)),
                  prev,
                  pl.BlockSpec((1, CONV_PAD, CONV_CH), lambda i, j: (i, 0, 0)),
                  _full(w.shape), _full((1, CONV_CH)), _full((1, CONV_CH)),
                  _full((1, CONV_CH))],
        out_specs=pl.BlockSpec((1, tc, CONV_CH), lambda i, j: (i, j, 0)),
        out_shape=jax.ShapeDtypeStruct((bsz, t, CONV_CH), BF16),
        scratch_shapes=[pltpu.VMEM((CONV_CH // LANES, CONV_PAD + tc, LANES), F32),
                        pltpu.VMEM((CONV_CH // LANES, tc, LANES), F32)],
        compiler_params=_params("parallel", "arbitrary"),
        name="conv",
    )(u, u_prev, hist, w, b, g, beta)


def _gla_in_kernel(x_ref, g_ref, w_ref, wl_ref, wu_ref, bu_ref, q_ref, k_ref, v_ref, r_ref, la_ref):
    h = _rms_rows(x_ref[...], g_ref[...]).astype(BF16)
    p = _dot(h, w_ref[...])
    q_ref[...] = p[:, 0:C_DK] * (C_DK_HEAD ** -0.5)
    k_ref[...] = p[:, C_DK:2 * C_DK]
    v_ref[...] = p[:, 2 * C_DK:2 * C_DK + C_DV]
    r = p[:, 2 * C_DK + C_DV:2 * C_DK + 2 * C_DV]
    r_ref[...] = r * _sigmoid(r)
    low = _dot(h, wl_ref[...]).astype(BF16)
    z = _dot(low, wu_ref[...]) + bu_ref[...]
    log_sig = jnp.minimum(z, 0.0) - jnp.log1p(jnp.exp(-jnp.abs(z)))
    la_ref[...] = log_sig * (1.0 / GATE_TAU)


def _gla_in(x, g, w, wl, wu, bu, tm):
    n = x.shape[0]
    row = lambda width: pl.BlockSpec((tm, width), lambda i: (i, 0))
    sds = lambda width: jax.ShapeDtypeStruct((n, width), F32)
    return pl.pallas_call(
        _gla_in_kernel,
        grid=(n // tm,),
        in_specs=[row(D_MODEL), _full((1, D_MODEL)), _full(w.shape), _full(wl.shape),
                  _full(wu.shape), _full((1, C_DK))],
        out_specs=[row(C_DK), row(C_DK), row(C_DV), row(C_DV), row(C_DK)],
        out_shape=[sds(C_DK), sds(C_DK), sds(C_DV), sds(C_DV), sds(C_DK)],
        compiler_params=_params("parallel"),
        name="gla_in",
    )(x, g, w, wl, wu, bu)


def _gla_kernel(chunk, n_chunks, q_ref, k_ref, v_ref, r_ref, la_ref, s0_ref, gain_ref,
                o_ref, s_ref, qin_s, kin_s, x_s, qst_s, kst_s, dec_s):
    @pl.when(pl.program_id(1) == 0)
    def _():
        s_ref[...] = s0_ref[...]

    half = chunk // 2
    ri = lax.broadcasted_iota(jnp.int32, (chunk, chunk), 0)
    ci = lax.broadcasted_iota(jnp.int32, (chunk, chunk), 1)
    causal = ci <= ri
    tri = jnp.where(causal, 1.0, 0.0).astype(BF16)
    cross = jnp.logical_and(ri >= half, ci < half)
    same_half = jnp.logical_and(causal, jnp.logical_not(cross))
    in_first = lax.broadcasted_iota(jnp.int32, (chunk, C_DK), 0) < half
    gain = gain_ref[...]

    def prepare(c):
        rows = slice(c * chunk, (c + 1) * chunk)
        la = la_ref[0, rows, :]
        la_hi = la.astype(BF16)
        la_lo = (la - la_hi.astype(F32)).astype(BF16)
        cum = _dot(tri, la_hi) + _dot(tri, la_lo)
        row = lambda i: cum[i:i + 1, :]
        last = row(chunk - 1)
        edge = row(half - 1)
        mid = jnp.where(in_first, row(half // 2 - 1), row(half + half // 2 - 1))
        q = q_ref[0, rows, :]
        k = k_ref[0, rows, :]
        grow = jnp.exp(cum - mid)
        qin_s[rows, :] = (q * grow).astype(BF16)
        kin_s[rows, :] = (k / grow).astype(BF16)
        k_x = k[:half] * jnp.exp(edge - cum[:half])
        q_x = q[half:] * jnp.exp(cum[half:] - edge)
        x_s[rows, :] = jnp.concatenate([k_x, q_x], axis=0).astype(BF16)
        qst_s[rows, :] = (q * jnp.exp(cum)).astype(BF16)
        kst_s[rows, :] = (k * jnp.exp(last - cum)).astype(BF16)
        dec_s[c] = jnp.broadcast_to(jnp.exp(last), (LANES, C_DK)).T

    def advance(c):
        rows = slice(c * chunk, (c + 1) * chunk)
        for h in range(C_HEADS):
            ks = slice(h * C_DK_HEAD, (h + 1) * C_DK_HEAD)
            vs = slice(h * C_DV_HEAD, (h + 1) * C_DV_HEAD)
            vh = v_ref[0, rows, vs].astype(BF16)
            xh = x_s[rows, ks]
            att = jnp.where(same_half, _dot_nt(qin_s[rows, ks], kin_s[rows, ks]),
                            jnp.where(cross, _dot_nt(xh, xh), 0.0))
            s = s_ref[0, h]
            o = _dot(jnp.concatenate([att.astype(BF16), qst_s[rows, ks]], axis=1),
                     jnp.concatenate([vh, s.astype(BF16)], axis=0))
            decay = dec_s[c, ks, :]
            s_ref[0, h] = (s * jnp.concatenate([decay] * (C_DV_HEAD // LANES), axis=1)
                           + _dot_tn(kst_s[rows, ks], vh))
            y = o * lax.rsqrt(jnp.mean(o * o, axis=-1, keepdims=True) + EPS) * gain
            o_ref[0, rows, vs] = (y * r_ref[0, rows, vs]).astype(o_ref.dtype)

    prepare(0)
    for c in range(n_chunks):
        if c + 1 < n_chunks:
            prepare(c + 1)
        advance(c)


def _gla(q, k, v, r, la, s0, gain, chunk, tb):
    b, t, _ = q.shape
    seq = lambda width: pl.BlockSpec((1, tb, width), lambda i, j: (i, j, 0))
    state = pl.BlockSpec((1, C_HEADS, C_DK_HEAD, C_DV_HEAD), lambda i, j: (i, 0, 0, 0))
    return pl.pallas_call(
        functools.partial(_gla_kernel, chunk, tb // chunk),
        grid=(b, t // tb),
        in_specs=[seq(C_DK), seq(C_DK), seq(C_DV), seq(C_DV), seq(C_DK), state,
                  _full((1, C_DV_HEAD))],
        out_specs=[seq(C_DV), state],
        out_shape=[jax.ShapeDtypeStruct((b, t, C_DV), BF16),
                   jax.ShapeDtypeStruct(s0.shape, F32)],
        scratch_shapes=[pltpu.VMEM((tb, C_DK), BF16)] * 5
                       + [pltpu.VMEM((tb // chunk, C_DK, LANES), F32)],
        compiler_params=_params("parallel", "arbitrary"),
        name="gla",
    )(q, k, v, r, la, s0, gain)


GLA_CHUNK = 128
SAMPLE_PAD = 16


def _trunk(x, past, P, tm, conv_tc, gla_tb):
    bsz, t, _ = x.shape
    n = bsz * t
    x = x.reshape(n, D_MODEL)
    outs = {}
    for layer in range(DEPTH):
        i = layer // 2
        x = _ffn(x, P['norm_ffn1'][layer], P['ffn1_w_in'][layer], P['ffn1_w_out'][layer], tm)
        if layer % 2 == 0:
            q, k, v, u = _even_in(x, P['norm_mix'][layer], P['ev_w_in'][i], P['ev_q_gain'][i],
                                  P['ev_k_gain'][i], P['head_ones'], tm)
            u3 = u.reshape(bsz, t, CONV_CH)
            if past is None:
                a = _attn_prompt(q, k, v, P['band_vectors'], bsz, t)
                keep = min(MAX_WINDOW, t)
                tail = lambda z: z.reshape(bsz, t, A_WIDTH)[:, t - keep:].reshape(
                    bsz, keep, A_HEADS, A_HEAD_DIM)
                new_k, new_v = tail(k), tail(v)
                hist = jnp.zeros((bsz, CONV_PAD, CONV_CH), F32)
                new_u = u3[:, t - (CONV_WIDTH - 1):]
            else:
                n_buf = past[0].shape[2]
                major = lambda z: z.transpose(0, 1, 3, 4, 2).reshape(z.shape[0] * bsz, A_WIDTH, n_buf)
                minor = lambda z: z.reshape(bsz, A_HEADS, A_HEAD_DIM, n_buf).transpose(0, 3, 1, 2)
                a, new_k, new_v = _attn_sample(
                    q.reshape(bsz, t, A_WIDTH), k.reshape(bsz, t, A_WIDTH), v.reshape(bsz, t, A_WIDTH),
                    major(past[0]), major(past[1]), i, P['decode_tables'])
                a = a.reshape(n, A_WIDTH)
                new_k, new_v = minor(new_k), minor(new_v)
                hist = jnp.pad(past[2][i], ((0, 0), (CONV_PAD - (CONV_WIDTH - 1), 0), (0, 0)))
                new_u = jnp.concatenate([past[2][i], u3], axis=1)[:, -(CONV_WIDTH - 1):]
            c = _conv(u3, hist, P['ev_conv_w'][i], P['ev_conv_b'][i], P['ev_conv_ln_g'][i],
                      P['ev_conv_ln_b'][i], conv_tc).reshape(n, CONV_CH)
            mixed, w_mix = [a, c], [P['ev_w_out_a'][i], P['ev_w_out_c'][i]]
            outs.setdefault('k', []).append(new_k)
            outs.setdefault('v', []).append(new_v)
            outs.setdefault('u', []).append(new_u)
        else:
            q, k, v, r, la = _gla_in(x, P['norm_mix'][layer], P['od_w_main'][i], P['od_w_low'][i],
                                     P['od_gate_w_up'][i], P['od_gate_b'][i], tm)
            if past is None:
                s0 = jnp.zeros((bsz, C_HEADS, C_DK_HEAD, C_DV_HEAD), F32)
                tp, chunk = t, GLA_CHUNK
            else:
                s0 = past[3][i]
                tp, chunk = SAMPLE_PAD, SAMPLE_PAD
            seq = lambda z: jnp.pad(z.reshape(bsz, t, -1), ((0, 0), (0, tp - t), (0, 0)))
            o, s = _gla(seq(q), seq(k), seq(v), seq(r), seq(la), s0, P['od_o_gain'][i], chunk,
                        min(gla_tb, tp))
            mixed, w_mix = [o[:, :t].reshape(n, C_DV)], [P['od_w_out'][i]]
            outs.setdefault('s', []).append(s)
        x = _ffn(x, P['norm_ffn2'][layer], P['ffn2_w_in'][layer], P['ffn2_w_out'][layer], tm,
                 mixed, w_mix)
    return (x.reshape(bsz, t, D_MODEL), jnp.stack(outs['k']), jnp.stack(outs['v']),
            jnp.stack(outs['u']), jnp.stack(outs['s']))


def kernel(x_prompt, x_sample, cache_k, cache_v, cache_conv, state_gla, rel_bias, norm_ffn1, ffn1_w_in, ffn1_w_out, norm_mix, norm_ffn2, ffn2_w_in, ffn2_w_out, ev_w_in, ev_q_gain, ev_k_gain, ev_conv_w, ev_conv_b, ev_conv_ln_g, ev_conv_ln_b, ev_w_out, od_w_in, od_gate_w_up, od_gate_b, od_o_gain, od_w_out):
    n_even = ev_w_in.shape[0]
    n_odd = od_w_in.shape[0]
    main = 2 * C_DK + 2 * C_DV
    head_ids = np.arange(MXU_DIM) // A_HEAD_DIM
    per = lambda n, f: [f(j) for j in range(n)]
    row = lambda z: z[None, :]
    P = {
        'norm_ffn1': per(DEPTH, lambda j: row(norm_ffn1[j])),
        'norm_mix': per(DEPTH, lambda j: row(norm_mix[j])),
        'norm_ffn2': per(DEPTH, lambda j: row(norm_ffn2[j])),
        'ffn1_w_in': per(DEPTH, lambda j: _weight_bf16(ffn1_w_in, j)),
        'ffn1_w_out': per(DEPTH, lambda j: _weight_bf16(ffn1_w_out, j)),
        'ffn2_w_in': per(DEPTH, lambda j: _weight_bf16(ffn2_w_in, j)),
        'ffn2_w_out': per(DEPTH, lambda j: _weight_bf16(ffn2_w_out, j)),
        'ev_w_in': per(n_even, lambda j: _weight_bf16(ev_w_in, j)),
        'ev_q_gain': per(n_even, lambda j: row(jnp.tile(ev_q_gain[j], A_HEADS))),
        'ev_k_gain': per(n_even, lambda j: row(jnp.tile(ev_k_gain[j], A_HEADS))),
        'head_ones': jnp.asarray(head_ids[:, None] == head_ids[None, :], BF16),
        'ev_conv_w': per(n_even, lambda j: jnp.broadcast_to(
            ev_conv_w[j][:, None, :], (CONV_WIDTH, SUBLANES, CONV_CH))),
        'ev_conv_b': per(n_even, lambda j: row(ev_conv_b[j])),
        'ev_conv_ln_g': per(n_even, lambda j: row(ev_conv_ln_g[j])),
        'ev_conv_ln_b': per(n_even, lambda j: row(ev_conv_ln_b[j])),
        'ev_w_out_a': per(n_even, lambda j: _weight_bf16(ev_w_out, j, 0, A_WIDTH)),
        'ev_w_out_c': per(n_even, lambda j: _weight_bf16(ev_w_out, j, A_WIDTH, CONV_CH)),
        'od_w_main': per(n_odd, lambda j: _weight_bf16(od_w_in, j, n_cols=main)),
        'od_w_low': per(n_odd, lambda j: jnp.pad(od_w_in[j, :, main:],
                                                 ((0, 0), (0, LANES - GATE_RANK))).astype(BF16)),
        'od_gate_w_up': per(n_odd, lambda j: jnp.pad(od_gate_w_up[j],
                                                     ((0, LANES - GATE_RANK), (0, 0))).astype(BF16)),
        'od_gate_b': per(n_odd, lambda j: row(od_gate_b[j])),
        'od_o_gain': per(n_odd, lambda j: row(od_o_gain[j])),
        'od_w_out': per(n_odd, lambda j: _weight_bf16(od_w_out, j)),
        'band_vectors': _band_vectors(rel_bias),
        'decode_tables': _decode_tables(rel_bias, x_sample.shape[1]),
    }
    P, x_prompt, x_sample = lax.optimization_barrier((P, x_prompt, x_sample))
    y_p, k_p, v_p, u_p, s_p = _trunk(x_prompt, None, P, tm=512, conv_tc=512, gla_tb=512)
    y_s, k_s, v_s, u_s, s_s = _trunk(x_sample, (cache_k, cache_v, cache_conv, state_gla), P,
                                     tm=256, conv_tc=x_sample.shape[1], gla_tb=SAMPLE_PAD)
    return (y_p, y_s, k_p, v_p, u_p, s_p, k_s, v_s, u_s, s_s)
```

```python
import functools

import numpy as np
import jax
import jax.numpy as jnp
from jax import lax
from jax.experimental import pallas as pl
from jax.experimental.pallas import tpu as pltpu

F32 = jnp.float32
BF16 = jnp.bfloat16

D_MODEL = 1024
DEPTH = 2
PAST_LEN = 16384
A_HEADS = 8
A_HEAD_DIM = 64
A_WIDTH = A_HEADS * A_HEAD_DIM
DILATED_GROUPS = ((128, 1), (512, 4), (2048, 16))
MAX_WINDOW = 2048
N_BUCKETS = 32
CONV_WIDTH = 31
CONV_CH = 512
C_HEADS = 4
C_DK = 512
C_DV = 1024
C_DK_HEAD = 128
C_DV_HEAD = 256
GATE_RANK = 16
GATE_TAU = 16.0
D_FF = 2816
EPS = 1e-6
NEG_INF = -1e30

LANES = 128
SUBLANES = 8
MXU_DIM = 256
WIN_KEYS = 128
ATT_BLOCK = 2048
ATT_UNROLL = 16
ATT_PHASES = 4
VMEM_LIMIT = 56 * 1024 * 1024


def _params(*sem):
    return pltpu.CompilerParams(dimension_semantics=sem, vmem_limit_bytes=VMEM_LIMIT)


def _dot(a, b):
    return jnp.dot(a, b, preferred_element_type=F32)


def _dot_nt(a, b):
    return lax.dot_general(a, b, (((1,), (1,)), ((), ())), preferred_element_type=F32)


def _dot_tn(a, b):
    return lax.dot_general(a, b, (((0,), (0,)), ((), ())), preferred_element_type=F32)


def _rms_rows(x, g):
    y = x * lax.rsqrt(jnp.mean(x * x, axis=-1, keepdims=True) + EPS)
    return y * g


def _sigmoid(x):
    return 1.0 / (1.0 + jnp.exp(-x))


def _full(shape):
    return pl.BlockSpec(shape, lambda *_: (0,) * len(shape), pipeline_mode=pl.Buffered(1))


FF_CHUNK = 256


def _rows_call(body, prompt_rows, sample_rows, consts, out_widths, tm, name, scratch=()):
    n_p, n_s = prompt_rows[0].shape[0], sample_rows[0].shape[0]
    steps = n_p // tm
    n_in, n_c, n_out = len(prompt_rows), len(consts), len(out_widths)

    def kernel(*refs):
        p_in, s_in = refs[:n_in], refs[n_in:2 * n_in]
        c_refs = refs[2 * n_in:2 * n_in + n_c]
        outs = refs[2 * n_in + n_c:]
        p_out, s_out, scr = outs[:n_out], outs[n_out:2 * n_out], outs[2 * n_out:]
        step = pl.program_id(0)

        @pl.when(step < steps)
        def _():
            body(p_in, c_refs, p_out, scr, tm)

        @pl.when(step == steps)
        def _():
            body(s_in, c_refs, s_out, scr, n_s)

    p_spec = lambda width: pl.BlockSpec((tm, width), lambda i: (jnp.minimum(i, steps - 1), 0))
    s_spec = lambda width: pl.BlockSpec((n_s, width), lambda i: (0, 0))
    res = pl.pallas_call(
        kernel,
        grid=(steps + 1,),
        in_specs=[p_spec(a.shape[1]) for a in prompt_rows] + [s_spec(a.shape[1]) for a in sample_rows]
                 + [_full(c.shape) for c in consts],
        out_specs=[p_spec(w) for w in out_widths] + [s_spec(w) for w in out_widths],
        out_shape=[jax.ShapeDtypeStruct((n_p, w), F32) for w in out_widths]
                  + [jax.ShapeDtypeStruct((n_s, w), F32) for w in out_widths],
        scratch_shapes=list(scratch),
        compiler_params=_params("arbitrary"),
        name=name,
    )(*prompt_rows, *sample_rows, *consts)
    return res[:n_out], res[n_out:]


def _ffn_body(ins, consts, outs, scratch, rows):
    x_ref, m_refs = ins[0], ins[1:]
    w_refs, (g_ref, wi_ref, wo_ref) = consts[:len(m_refs)], consts[len(m_refs):]
    (o_ref,), (act_ref,) = outs, scratch
    x = x_ref[...]
    for m_ref, w_ref in zip(m_refs, w_refs):
        x = x + _dot(m_ref[...], w_ref[...])
    h = _rms_rows(x, g_ref[...]).astype(BF16)
    for c in range(D_FF // FF_CHUNK):
        lo = c * FF_CHUNK
        a = _dot(h, wi_ref[:, lo:lo + FF_CHUNK])
        b = _dot(h, wi_ref[:, D_FF + lo:D_FF + lo + FF_CHUNK])
        act_ref[0:rows, lo:lo + FF_CHUNK] = (a * _sigmoid(a) * b).astype(BF16)
    o_ref[...] = x + 0.5 * _dot(act_ref[0:rows, :], wo_ref[...])


def _ffn(x_p, x_s, g, wi, wo, tm, ms_p=(), ms_s=(), ws=()):
    (y_p,), (y_s,) = _rows_call(_ffn_body, [x_p, *ms_p], [x_s, *ms_s], [*ws, g, wi, wo],
                                [D_MODEL], tm, "ffn", [pltpu.VMEM((tm, D_FF), BF16)])
    return y_p, y_s


CAST_BLOCK_BYTES = 4 * 1024 * 1024
BF16_ROWS = 16


def _cast_kernel(w_ref, o_ref):
    o_ref[...] = w_ref[...].astype(o_ref.dtype)


def _weight_bf16(w, layer, row0=0, n_rows=None, n_cols=None):
    n_rows = w.shape[1] - row0 if n_rows is None else n_rows
    n_cols = w.shape[2] if n_cols is None else n_cols
    fits = [r for r in range(BF16_ROWS, n_rows + 1, BF16_ROWS)
            if n_rows % r == 0 and row0 % r == 0 and r * n_cols * 4 <= CAST_BLOCK_BYTES]
    rows = max(fits)
    first = row0 // rows
    return pl.pallas_call(
        _cast_kernel,
        grid=(n_rows // rows,),
        in_specs=[pl.BlockSpec((None, rows, n_cols), lambda i: (layer, first + i, 0))],
        out_specs=pl.BlockSpec((rows, n_cols), lambda i: (i, 0)),
        out_shape=jax.ShapeDtypeStruct((n_rows, n_cols), BF16),
        compiler_params=_params("parallel"),
        name="weight_bf16",
    )(w)


def _head_norm(z, gain, bd):
    zz = z * z
    hi = zz.astype(BF16)
    lo = (zz - hi.astype(F32)).astype(BF16)
    width = bd.shape[0]
    ss = jnp.concatenate(
        [_dot(hi[:, c:c + width], bd) + _dot(lo[:, c:c + width], bd)
         for c in range(0, z.shape[1], width)], axis=1)
    return z * lax.rsqrt(ss * (1.0 / A_HEAD_DIM) + EPS) * gain


def _even_in_body(ins, consts, outs, scratch, rows):
    (x_ref,), (g_ref, w_ref, qg_ref, kg_ref, bd_ref), (q_ref, k_ref, v_ref, u_ref) = ins, consts, outs
    h = _rms_rows(x_ref[...], g_ref[...]).astype(BF16)
    p = _dot(h, w_ref[...])
    bd = bd_ref[...]
    q_ref[...] = _head_norm(p[:, 0:A_WIDTH], qg_ref[...], bd) * (A_HEAD_DIM ** -0.5)
    k_ref[...] = _head_norm(p[:, A_WIDTH:2 * A_WIDTH], kg_ref[...], bd)
    v_ref[...] = p[:, 2 * A_WIDTH:3 * A_WIDTH]
    gv = p[:, 3 * A_WIDTH:3 * A_WIDTH + CONV_CH]
    gg = p[:, 3 * A_WIDTH + CONV_CH:3 * A_WIDTH + 2 * CONV_CH]
    u_ref[...] = gv * _sigmoid(gg)


def _even_in(x_p, x_s, g, w, qg, kg, bd, tm):
    return _rows_call(_even_in_body, [x_p], [x_s], [g, w, qg, kg, bd], [A_WIDTH] * 4, tm, "even_in")


def _t5_bucket(dist):
    max_exact = N_BUCKETS // 2
    d = np.asarray(dist, dtype=np.int32)
    df = np.maximum(d, 1).astype(np.float32)
    large = max_exact + (np.log(df / max_exact) / np.log(MAX_WINDOW / max_exact)
                         * (N_BUCKETS - max_exact)).astype(np.int32)
    large = np.minimum(large, N_BUCKETS - 1)
    return np.where(d < max_exact, d, large).astype(np.int32)


def _select_bias(rel_bias, dist, valid):
    onehot = (_t5_bucket(dist)[None, :] == np.arange(N_BUCKETS)[:, None]) & valid[None, :]
    picked = jnp.einsum('bh,bc->hc', rel_bias, jnp.asarray(onehot, F32),
                        precision=lax.Precision.HIGHEST)
    return picked + jnp.asarray(np.where(valid, 0.0, NEG_INF), F32)[None, :]


def _band_vectors(rel_bias):
    c = np.arange(2 * WIN_KEYS)
    valid = c <= WIN_KEYS
    vecs = [_select_bias(rel_bias, np.where(valid, (WIN_KEYS - c) * dil, 0), valid)
            for _, dil in DILATED_GROUPS]
    return jnp.stack(vecs).reshape(len(DILATED_GROUPS), A_HEADS // 2, 2, 2 * WIN_KEYS)


def _attn_prompt_kernel(q_ref, kp_ref, kc_ref, vp_ref, vc_ref, vec_ref, o_ref,
                        knat, vnat, k4, v4, q4, og1, lg1, og4, lg4, tab_ref):
    blk = pl.program_id(2)
    n_groups = len(DILATED_GROUPS)
    nph = ATT_PHASES
    per = ATT_BLOCK // nph

    @pl.when(blk == 0)
    def _():
        col = lax.broadcasted_iota(jnp.int32, (WIN_KEYS, 2 * WIN_KEYS), 1)
        for g in range(n_groups):
            for hh in range(2):
                vec = jnp.broadcast_to(vec_ref[g, 0, hh:hh + 1, :], (WIN_KEYS, 2 * WIN_KEYS))
                band = pltpu.roll(vec, 0, 1, stride=1, stride_axis=0)
                rows = slice(hh * WIN_KEYS, (hh + 1) * WIN_KEYS)
                tab_ref[g, rows, :] = band
                tab_ref[n_groups + g, rows, :] = jnp.where(col >= WIN_KEYS, band, NEG_INF)

    knat[0:WIN_KEYS, :] = kp_ref[ATT_BLOCK - WIN_KEYS:, :]
    knat[WIN_KEYS:, :] = kc_ref[...]
    vnat[0:WIN_KEYS, :] = vp_ref[ATT_BLOCK - WIN_KEYS:, :]
    vnat[WIN_KEYS:, :] = vc_ref[...]
    for r in range(nph):
        phase = pl.ds(r, per, stride=nph)
        k4[r, 0:per, :] = kp_ref[phase, :]
        k4[r, per:, :] = kc_ref[phase, :]
        v4[r, 0:per, :] = vp_ref[phase, :]
        v4[r, per:, :] = vc_ref[phase, :]
        q4[r] = q_ref[phase, :]
    first = blk == 0
    lane = lax.broadcasted_iota(jnp.int32, (WIN_KEYS, LANES), 1)
    low = lane < A_HEAD_DIM
    ones = jnp.ones((2 * WIN_KEYS, LANES), BF16)

    def block(qs, kk, vv, tab):
        qs = qs.astype(BF16)
        zero = jnp.zeros_like(qs)
        qst = jnp.concatenate([jnp.where(low, qs, zero), jnp.where(low, zero, qs)], axis=0)
        s = _dot_nt(qst, kk.astype(BF16)) + tab
        mx = jnp.max(s, axis=-1, keepdims=True)
        p = jnp.exp(s - mx).astype(BF16)
        r = _dot(p, jnp.concatenate([vv.astype(BF16), ones], axis=1))
        o2 = jnp.where(low, r[0:WIN_KEYS, 0:LANES], r[WIN_KEYS:, 0:LANES])
        l2 = jnp.where(low, r[0:WIN_KEYS, LANES:], r[WIN_KEYS:, LANES:])
        m2 = jnp.where(low, jnp.broadcast_to(mx[0:WIN_KEYS], (WIN_KEYS, LANES)),
                       jnp.broadcast_to(mx[WIN_KEYS:], (WIN_KEYS, LANES)))
        return o2 / l2, m2 + jnp.log(l2)

    def table(g, at_start):
        return tab_ref[jnp.where(jnp.logical_and(at_start, first), n_groups + g, g)]

    def body1(sub, carry):
        i0 = pl.multiple_of(sub * WIN_KEYS, WIN_KEYS)
        o, l = block(q_ref[pl.ds(i0, WIN_KEYS), :], knat[pl.ds(i0, 2 * WIN_KEYS), :],
                     vnat[pl.ds(i0, 2 * WIN_KEYS), :], table(0, sub == 0))
        og1[pl.ds(i0, WIN_KEYS), :] = o
        lg1[pl.ds(i0, WIN_KEYS), :] = l
        return carry

    def body2(pb, carry):
        sub = pb // nph
        r = pb - sub * nph
        i0 = pl.multiple_of(sub * WIN_KEYS, WIN_KEYS)
        keys = pl.ds(i0 + (per - WIN_KEYS), 2 * WIN_KEYS)
        o, l = block(q4[r, pl.ds(i0, WIN_KEYS), :], k4[r, keys, :], v4[r, keys, :],
                     table(1, sub == 0))
        og4[0, r, pl.ds(i0, WIN_KEYS), :] = o
        lg4[0, r, pl.ds(i0, WIN_KEYS), :] = l
        return carry

    def body3(pb, carry):
        a = pb // nph
        r = pb - a * nph
        rows = pl.ds(a, WIN_KEYS, stride=nph)
        keys = pl.ds(a, 2 * WIN_KEYS, stride=nph)
        o, l = block(q4[r, rows, :], k4[r, keys, :], v4[r, keys, :], table(2, True))
        og4[1, r, rows, :] = o
        lg4[1, r, rows, :] = l
        return carry

    n_blocks = ATT_BLOCK // WIN_KEYS
    for body in (body1, body2, body3):
        lax.fori_loop(0, n_blocks, body, 0, unroll=ATT_UNROLL)

    for r in range(nph):
        phase = pl.ds(r, per, stride=nph)
        la, lb, lc = lg1[phase, :], lg4[0, r], lg4[1, r]
        mx = jnp.maximum(jnp.maximum(la, lb), lc)
        wa, wb, wc = jnp.exp(la - mx), jnp.exp(lb - mx), jnp.exp(lc - mx)
        og1[phase, :] = (wa * og1[phase, :] + wb * og4[0, r] + wc * og4[1, r]) / (wa + wb + wc)
    o_ref[...] = og1[...].astype(o_ref.dtype)


def _attn_prompt(q, k, v, vecs, batch, seq):
    assert [d for _, d in DILATED_GROUPS] == [1, ATT_PHASES, ATT_PHASES ** 2]
    nb = seq // ATT_BLOCK
    n_groups = len(DILATED_GROUPS)
    per = ATT_BLOCK // ATT_PHASES
    cur = lambda b, p, t: (b * nb + t, p)
    prev = lambda b, p, t: (b * nb + jnp.maximum(t - 1, 0), p)
    blk = lambda imap: pl.BlockSpec((ATT_BLOCK, LANES), imap)
    vmem = lambda *shape: pltpu.VMEM(shape, F32)
    return pl.pallas_call(
        _attn_prompt_kernel,
        grid=(batch, A_HEADS // 2, nb),
        in_specs=[blk(cur), blk(prev), blk(cur), blk(prev), blk(cur),
                  pl.BlockSpec((n_groups, 1, 2, 2 * WIN_KEYS), lambda b, p, t: (0, p, 0, 0))],
        out_specs=blk(cur),
        out_shape=jax.ShapeDtypeStruct((batch * seq, A_WIDTH), BF16),
        scratch_shapes=[vmem(WIN_KEYS + ATT_BLOCK, LANES), vmem(WIN_KEYS + ATT_BLOCK, LANES),
                        vmem(ATT_PHASES, 2 * per, LANES), vmem(ATT_PHASES, 2 * per, LANES),
                        vmem(ATT_PHASES, per, LANES),
                        vmem(ATT_BLOCK, LANES), vmem(ATT_BLOCK, LANES),
                        vmem(2, ATT_PHASES, per, LANES), vmem(2, ATT_PHASES, per, LANES),
                        vmem(2 * n_groups, 2 * WIN_KEYS, 2 * WIN_KEYS)],
        compiler_params=_params("arbitrary", "arbitrary", "arbitrary"),
        name="attn_prompt",
    )(q, k, k, v, v, vecs)


def _decode_tables(rel_bias, n_new):
    cols = MAX_WINDOW + n_new
    c = np.arange(cols)
    dist = MAX_WINDOW - c
    cnt = np.zeros(c.shape, np.float32)
    for window, dil in DILATED_GROUPS:
        cnt += ((dist >= 0) & (dist <= window) & (dist % dil == 0)).astype(np.float32)
    vec = _select_bias(rel_bias, np.clip(dist, 0, MAX_WINDOW), cnt > 0)
    vec = vec + jnp.asarray(np.log(np.maximum(cnt, 1.0)), F32)[None, :]
    rows = jnp.stack([jnp.pad(vec[:, :cols - i], ((0, 0), (i, 0)), constant_values=NEG_INF)
                      for i in range(n_new)], axis=1)
    rows = rows.reshape(A_HEADS * n_new, cols)
    return rows[:, :MAX_WINDOW], rows[:, MAX_WINDOW:]


def _attn_sample_kernel(q_ref, kn_ref, vn_ref, kc_ref, vc_ref, tc_ref, tn_ref,
                        o_ref, ko_ref, vo_ref):
    n_new = q_ref.shape[1]
    n_buf = kc_ref.shape[2]
    rows = A_HEADS * n_new
    kn = kn_ref[0]
    vn = vn_ref[0]
    kc = kc_ref[0]
    vc = vc_ref[0]

    lane = lax.broadcasted_iota(jnp.int32, (A_WIDTH, LANES), 1)
    for new, old, out_ref in ((kn, kc, ko_ref), (vn, vc, vo_ref)):
        shifted = pltpu.roll(old, n_buf - n_new, 1)
        tail = jnp.concatenate([jnp.zeros((LANES - n_new, A_WIDTH), F32), new], axis=0).T
        out_ref[0, :, 0:n_buf - LANES] = shifted[:, 0:n_buf - LANES]
        out_ref[0, :, n_buf - LANES:] = jnp.where(lane >= LANES - n_new, tail,
                                                  shifted[:, n_buf - LANES:])

    q = q_ref[0]
    row_head = lax.broadcasted_iota(jnp.int32, (A_HEADS, n_new, A_WIDTH), 0).reshape(rows, A_WIDTH)
    col = lax.broadcasted_iota(jnp.int32, (rows, A_WIDTH), 1)
    own = jnp.logical_and(col >= row_head * A_HEAD_DIM, col < (row_head + 1) * A_HEAD_DIM)
    qblk = jnp.where(own, jnp.concatenate([q] * A_HEADS, axis=0), 0.0).astype(BF16)
    s_c = _dot(qblk, kc.astype(BF16)) + tc_ref[...]
    s_n = _dot_nt(qblk, kn.astype(BF16)) + tn_ref[...]
    mx = jnp.maximum(jnp.max(s_c, axis=-1, keepdims=True), jnp.max(s_n, axis=-1, keepdims=True))
    p_c = jnp.exp(s_c - mx)
    p_n = jnp.exp(s_n - mx)
    den = jnp.sum(p_c, axis=-1, keepdims=True) + jnp.sum(p_n, axis=-1, keepdims=True)
    acc = _dot_nt(p_c.astype(BF16), vc.astype(BF16)) + _dot(p_n.astype(BF16), vn.astype(BF16))
    acc = jnp.where(own, acc / den, 0.0)
    out = acc[0:n_new]
    for h in range(1, A_HEADS):
        out = out + acc[h * n_new:(h + 1) * n_new]
    o_ref[0] = out.astype(o_ref.dtype)


def _attn_sample(q, k_new, v_new, cache_k, cache_v, layer, tables):
    b, n_new, _ = q.shape
    n_buf = cache_k.shape[2]
    new = pl.BlockSpec((1, n_new, A_WIDTH), lambda i: (i, 0, 0))
    buf = pl.BlockSpec((1, A_WIDTH, n_buf), lambda i: (i, 0, 0))
    past = pl.BlockSpec((1, A_WIDTH, n_buf), lambda i: (layer * b + i, 0, 0))
    return pl.pallas_call(
        _attn_sample_kernel,
        grid=(b,),
        in_specs=[new, new, new, past, past, _full(tables[0].shape), _full(tables[1].shape)],
        out_specs=[new, buf, buf],
        out_shape=[jax.ShapeDtypeStruct((b, n_new, A_WIDTH), BF16),
                   jax.ShapeDtypeStruct((b, A_WIDTH, n_buf), F32),
                   jax.ShapeDtypeStruct((b, A_WIDTH, n_buf), F32)],
        compiler_params=_params("parallel"),
        name="attn_sample",
    )(q, k_new, v_new, cache_k, cache_v, *tables)


CONV_PAD = 32
CONV_ROWS = 32
CONV_UNROLL = 2


def _conv_kernel(tc, u_ref, up_ref, hist_ref, w_ref, b_ref, g_ref, beta_ref, o_ref, win, stage):
    t = pl.program_id(1)
    n_slab = CONV_CH // LANES
    slab = lambda c: slice(c * LANES, (c + 1) * LANES)
    for c in range(n_slab):
        win[c, CONV_PAD:CONV_PAD + tc, :] = u_ref[0, :, slab(c)]

    @pl.when(t == 0)
    def _():
        for c in range(n_slab):
            win[c, 0:CONV_PAD, :] = hist_ref[0, :, slab(c)]

    @pl.when(t > 0)
    def _():
        for c in range(n_slab):
            win[c, 0:CONV_PAD, :] = up_ref[0, :, slab(c)]

    off = CONV_PAD - (CONV_WIDTH - 1)
    rc = min(CONV_ROWS, tc)
    half = rc // 2

    def tap(k, c):
        w = w_ref[k, :, slab(c)]
        if half < SUBLANES:
            return w[0:half]
        return jnp.concatenate([w] * (half // SUBLANES), axis=0)

    def body(j, carry):
        r0 = j * rc
        for c in range(n_slab):
            for par in range(2):
                acc = jnp.zeros((half, LANES), F32) + b_ref[:, slab(c)]
                for k in range(CONV_WIDTH):
                    rows = pl.ds(r0 + off + k + par, half, stride=2)
                    acc = acc + win[c, rows, :] * tap(k, c)
                stage[c, pl.ds(r0 + par, half, stride=2), :] = acc
        return carry

    if tc == rc:
        body(0, 0)
    else:
        lax.fori_loop(0, tc // rc, body, 0, unroll=CONV_UNROLL)
    y = jnp.concatenate([stage[c] for c in range(n_slab)], axis=1)
    xc = y - jnp.mean(y, axis=-1, keepdims=True)
    y = xc * lax.rsqrt(jnp.mean(xc * xc, axis=-1, keepdims=True) + EPS)
    y = y * g_ref[...] + beta_ref[...]
    o_ref[0] = (y * _sigmoid(y)).astype(o_ref.dtype)


def _conv(u, hist, w, b, g, beta, tc):
    bsz, t, _ = u.shape
    per = tc // CONV_PAD
    if t >= CONV_PAD:
        prev = pl.BlockSpec((1, CONV_PAD, CONV_CH), lambda i, j: (i, jnp.maximum(j * per - 1, 0), 0))
        u_prev = u
    else:
        prev = pl.BlockSpec((1, CONV_PAD, CONV_CH), lambda i, j: (i, 0, 0))
        u_prev = hist
    return pl.pallas_call(
        functools.partial(_conv_kernel, tc),
        grid=(bsz, t // tc),
        in_specs=[pl.BlockSpec((1, tc, CONV_CH), lambda i, j: (i, j, 0)),
                  prev,
                  pl.BlockSpec((1, CONV_PAD, CONV_CH), lambda i, j: (i, 0, 0)),
                  _full(w.shape), _full((1, CONV_CH)), _full((1, CONV_CH)),
                  _full((1, CONV_CH))],
        out_specs=pl.BlockSpec((1, tc, CONV_CH), lambda i, j: (i, j, 0)),
        out_shape=jax.ShapeDtypeStruct((bsz, t, CONV_CH), BF16),
        scratch_shapes=[pltpu.VMEM((CONV_CH // LANES, CONV_PAD + tc, LANES), F32),
                        pltpu.VMEM((CONV_CH // LANES, tc, LANES), F32)],
        compiler_params=_params("parallel", "arbitrary"),
        name="conv",
    )(u, u_prev, hist, w, b, g, beta)


def _gla_in_body(ins, consts, outs, scratch, rows):
    (x_ref,), (g_ref, w_ref, wl_ref, wu_ref, bu_ref) = ins, consts
    q_ref, k_ref, v_ref, r_ref, la_ref = outs
    h = _rms_rows(x_ref[...], g_ref[...]).astype(BF16)
    p = _dot(h, w_ref[...])
    q_ref[...] = p[:, 0:C_DK] * (C_DK_HEAD ** -0.5)
    k_ref[...] = p[:, C_DK:2 * C_DK]
    v_ref[...] = p[:, 2 * C_DK:2 * C_DK + C_DV]
    r = p[:, 2 * C_DK + C_DV:2 * C_DK + 2 * C_DV]
    r_ref[...] = r * _sigmoid(r)
    low = _dot(h, wl_ref[...]).astype(BF16)
    z = _dot(low, wu_ref[...]) + bu_ref[...]
    log_sig = jnp.minimum(z, 0.0) - jnp.log1p(jnp.exp(-jnp.abs(z)))
    la_ref[...] = log_sig * (1.0 / GATE_TAU)


def _gla_in(x_p, x_s, g, w, wl, wu, bu, tm):
    return _rows_call(_gla_in_body, [x_p], [x_s], [g, w, wl, wu, bu],
                      [C_DK, C_DK, C_DV, C_DV, C_DK], tm, "gla_in")


def _gla_kernel(chunk, n_chunks, q_ref, k_ref, v_ref, r_ref, la_ref, s0_ref, gain_ref,
                o_ref, s_ref, qin_s, kin_s, x1_s, x2_s, qst_s, kst_s, dec_s):
    @pl.when(pl.program_id(1) == 0)
    def _():
        s_ref[...] = s0_ref[...]

    half, quarter = chunk // 2, chunk // 4
    quarter_of = lambda i: sum((i >= j * quarter).astype(jnp.int32) for j in range(1, 4))
    ri = lax.broadcasted_iota(jnp.int32, (chunk, chunk), 0)
    ci = lax.broadcasted_iota(jnp.int32, (chunk, chunk), 1)
    causal = ci <= ri
    tri = jnp.where(causal, 1.0, 0.0).astype(BF16)
    rq, cq = quarter_of(ri), quarter_of(ci)
    same_quarter = jnp.logical_and(causal, rq == cq)
    cross_half = jnp.logical_and(ri >= half, ci < half)
    cross_quarter = jnp.logical_and(rq == cq + 1, (ri >= half) == (ci >= half))
    rr = lax.broadcasted_iota(jnp.int32, (chunk, C_DK), 0)
    rrq = quarter_of(rr)
    in_first = rr < half
    key_side_2 = jnp.logical_or(rrq == 0, rrq == 2)
    gain = gain_ref[...]

    def prepare(c):
        rows = slice(c * chunk, (c + 1) * chunk)
        la = la_ref[0, rows, :]
        la_hi = la.astype(BF16)
        la_lo = (la - la_hi.astype(F32)).astype(BF16)
        cum = _dot(tri, la_hi) + _dot(tri, la_lo)
        row = lambda i: cum[i:i + 1, :]
        last = row(chunk - 1)
        mids = [row(j * quarter + quarter // 2 - 1) for j in range(4)]
        mid = jnp.where(rrq == 0, mids[0], jnp.where(rrq == 1, mids[1],
                        jnp.where(rrq == 2, mids[2], mids[3])))
        q = q_ref[0, rows, :]
        k = k_ref[0, rows, :]
        grow = jnp.exp(cum - mid)
        qin_s[rows, :] = (q * grow).astype(BF16)
        kin_s[rows, :] = (k / grow).astype(BF16)

        def across(edge, key_side):
            gap = cum - edge
            return (jnp.where(key_side, k, q) * jnp.exp(jnp.where(key_side, -gap, gap))).astype(BF16)

        x1_s[rows, :] = across(row(half - 1), in_first)
        x2_s[rows, :] = across(jnp.where(in_first, row(quarter - 1), row(half + quarter - 1)),
                               key_side_2)
        qst_s[rows, :] = (q * jnp.exp(cum)).astype(BF16)
        kst_s[rows, :] = (k * jnp.exp(last - cum)).astype(BF16)
        dec_s[c] = jnp.broadcast_to(jnp.exp(last), (LANES, C_DK)).T

    def advance(c):
        rows = slice(c * chunk, (c + 1) * chunk)
        for h in range(C_HEADS):
            ks = slice(h * C_DK_HEAD, (h + 1) * C_DK_HEAD)
            vs = slice(h * C_DV_HEAD, (h + 1) * C_DV_HEAD)
            vh = v_ref[0, rows, vs].astype(BF16)
            x1, x2 = x1_s[rows, ks], x2_s[rows, ks]
            att = jnp.where(same_quarter, _dot_nt(qin_s[rows, ks], kin_s[rows, ks]),
                            jnp.where(cross_quarter, _dot_nt(x2, x2),
                                      jnp.where(cross_half, _dot_nt(x1, x1), 0.0)))
            s = s_ref[0, h]
            o = _dot(jnp.concatenate([att.astype(BF16), qst_s[rows, ks]], axis=1),
                     jnp.concatenate([vh, s.astype(BF16)], axis=0))
            decay = dec_s[c, ks, :]
            s_ref[0, h] = (s * jnp.concatenate([decay] * (C_DV_HEAD // LANES), axis=1)
                           + _dot_tn(kst_s[rows, ks], vh))
            y = o * lax.rsqrt(jnp.mean(o * o, axis=-1, keepdims=True) + EPS) * gain
            o_ref[0, rows, vs] = (y * r_ref[0, rows, vs]).astype(o_ref.dtype)

    prepare(0)
    for c in range(n_chunks):
        if c + 1 < n_chunks:
            prepare(c + 1)
        advance(c)


def _gla(q, k, v, r, la, s0, gain, chunk, tb):
    b, t, _ = q.shape
    seq = lambda width: pl.BlockSpec((1, tb, width), lambda i, j: (i, j, 0))
    state = pl.BlockSpec((1, C_HEADS, C_DK_HEAD, C_DV_HEAD), lambda i, j: (i, 0, 0, 0))
    return pl.pallas_call(
        functools.partial(_gla_kernel, chunk, tb // chunk),
        grid=(b, t // tb),
        in_specs=[seq(C_DK), seq(C_DK), seq(C_DV), seq(C_DV), seq(C_DK), state,
                  _full((1, C_DV_HEAD))],
        out_specs=[seq(C_DV), state],
        out_shape=[jax.ShapeDtypeStruct((b, t, C_DV), BF16),
                   jax.ShapeDtypeStruct(s0.shape, F32)],
        scratch_shapes=[pltpu.VMEM((tb, C_DK), BF16)] * 6
                       + [pltpu.VMEM((tb // chunk, C_DK, LANES), F32)],
        compiler_params=_params("parallel", "arbitrary"),
        name="gla",
    )(q, k, v, r, la, s0, gain)


GLA_CHUNK = 128
SAMPLE_PAD = 16


ROW_TILE = 512
CONV_TILE = 512
GLA_TILE = 512


def _even_mixer_prompt(q, k, v, u, P, i, bsz, t):
    n = bsz * t
    a = _attn_prompt(q, k, v, P['band_vectors'], bsz, t)
    keep = min(MAX_WINDOW, t)
    tail = lambda z: z.reshape(bsz, t, A_WIDTH)[:, t - keep:].reshape(bsz, keep, A_HEADS, A_HEAD_DIM)
    u3 = u.reshape(bsz, t, CONV_CH)
    hist = jnp.zeros((bsz, CONV_PAD, CONV_CH), F32)
    c = _conv(u3, hist, P['ev_conv_w'][i], P['ev_conv_b'][i], P['ev_conv_ln_g'][i],
              P['ev_conv_ln_b'][i], CONV_TILE).reshape(n, CONV_CH)
    return [a, c], (tail(k), tail(v), u3[:, t - (CONV_WIDTH - 1):])


def _even_mixer_sample(q, k, v, u, past, P, i, bsz, t):
    n = bsz * t
    n_buf = past[0].shape[2]
    major = lambda z: z.transpose(0, 1, 3, 4, 2).reshape(z.shape[0] * bsz, A_WIDTH, n_buf)
    minor = lambda z: z.reshape(bsz, A_HEADS, A_HEAD_DIM, n_buf).transpose(0, 3, 1, 2)
    a, new_k, new_v = _attn_sample(
        q.reshape(bsz, t, A_WIDTH), k.reshape(bsz, t, A_WIDTH), v.reshape(bsz, t, A_WIDTH),
        major(past[0]), major(past[1]), i, P['decode_tables'])
    u3 = u.reshape(bsz, t, CONV_CH)
    hist = jnp.pad(past[2][i], ((0, 0), (CONV_PAD - (CONV_WIDTH - 1), 0), (0, 0)))
    c = _conv(u3, hist, P['ev_conv_w'][i], P['ev_conv_b'][i], P['ev_conv_ln_g'][i],
              P['ev_conv_ln_b'][i], t).reshape(n, CONV_CH)
    new_u = jnp.concatenate([past[2][i], u3], axis=1)[:, -(CONV_WIDTH - 1):]
    return [a.reshape(n, A_WIDTH), c], (minor(new_k), minor(new_v), new_u)


def _gla_mixer(q, k, v, r, la, s0, gain, bsz, t, t_pad, chunk, tile):
    seq = lambda z: jnp.pad(z.reshape(bsz, t, -1), ((0, 0), (0, t_pad - t), (0, 0)))
    o, s = _gla(seq(q), seq(k), seq(v), seq(r), seq(la), s0, gain, chunk, tile)
    return [o[:, :t].reshape(bsz * t, C_DV)], s


def _trunks(x_p, x_s, past, P):
    (b_p, t_p, _), (b_s, t_s, _) = x_p.shape, x_s.shape
    x_p = x_p.reshape(b_p * t_p, D_MODEL)
    x_s = x_s.reshape(b_s * t_s, D_MODEL)
    new_p = {'k': [], 'v': [], 'u': [], 's': []}
    new_s = {'k': [], 'v': [], 'u': [], 's': []}
    for layer in range(DEPTH):
        i = layer // 2
        x_p, x_s = _ffn(x_p, x_s, P['norm_ffn1'][layer], P['ffn1_w_in'][layer],
                        P['ffn1_w_out'][layer], ROW_TILE)
        if layer % 2 == 0:
            proj_p, proj_s = _even_in(x_p, x_s, P['norm_mix'][layer], P['ev_w_in'][i],
                                      P['ev_q_gain'][i], P['ev_k_gain'][i], P['head_ones'], ROW_TILE)
            mix_p, kvu_p = _even_mixer_prompt(*proj_p, P, i, b_p, t_p)
            mix_s, kvu_s = _even_mixer_sample(*proj_s, past, P, i, b_s, t_s)
            w_mix = [P['ev_w_out_a'][i], P['ev_w_out_c'][i]]
            for new, kvu in ((new_p, kvu_p), (new_s, kvu_s)):
                for name, z in zip('kvu', kvu):
                    new[name].append(z)
        else:
            proj_p, proj_s = _gla_in(x_p, x_s, P['norm_mix'][layer], P['od_w_main'][i],
                                     P['od_w_low'][i], P['od_gate_w_up'][i], P['od_gate_b'][i],
                                     ROW_TILE)
            zeros = jnp.zeros((b_p, C_HEADS, C_DK_HEAD, C_DV_HEAD), F32)
            gain = P['od_o_gain'][i]
            mix_p, s_p = _gla_mixer(*proj_p, zeros, gain, b_p, t_p, t_p, GLA_CHUNK, GLA_TILE)
            mix_s, s_s = _gla_mixer(*proj_s, past[3][i], gain, b_s, t_s, SAMPLE_PAD, SAMPLE_PAD,
                                    SAMPLE_PAD)
            w_mix = [P['od_w_out'][i]]
            new_p['s'].append(s_p)
            new_s['s'].append(s_s)
        x_p, x_s = _ffn(x_p, x_s, P['norm_ffn2'][layer], P['ffn2_w_in'][layer],
                        P['ffn2_w_out'][layer], ROW_TILE, mix_p, mix_s, w_mix)
    stacked = lambda new: tuple(jnp.stack(new[name]) for name in 'kvus')
    return (x_p.reshape(b_p, t_p, D_MODEL), x_s.reshape(b_s, t_s, D_MODEL),
            *stacked(new_p), *stacked(new_s))


def kernel(x_prompt, x_sample, cache_k, cache_v, cache_conv, state_gla, rel_bias, norm_ffn1, ffn1_w_in, ffn1_w_out, norm_mix, norm_ffn2, ffn2_w_in, ffn2_w_out, ev_w_in, ev_q_gain, ev_k_gain, ev_conv_w, ev_conv_b, ev_conv_ln_g, ev_conv_ln_b, ev_w_out, od_w_in, od_gate_w_up, od_gate_b, od_o_gain, od_w_out):
    n_even = ev_w_in.shape[0]
    n_odd = od_w_in.shape[0]
    main = 2 * C_DK + 2 * C_DV
    head_ids = np.arange(MXU_DIM) // A_HEAD_DIM
    per = lambda n, f: [f(j) for j in range(n)]
    row = lambda z: z[None, :]
    P = {
        'norm_ffn1': per(DEPTH, lambda j: row(norm_ffn1[j])),
        'norm_mix': per(DEPTH, lambda j: row(norm_mix[j])),
        'norm_ffn2': per(DEPTH, lambda j: row(norm_ffn2[j])),
        'ffn1_w_in': per(DEPTH, lambda j: _weight_bf16(ffn1_w_in, j)),
        'ffn1_w_out': per(DEPTH, lambda j: _weight_bf16(ffn1_w_out, j)),
        'ffn2_w_in': per(DEPTH, lambda j: _weight_bf16(ffn2_w_in, j)),
        'ffn2_w_out': per(DEPTH, lambda j: _weight_bf16(ffn2_w_out, j)),
        'ev_w_in': per(n_even, lambda j: _weight_bf16(ev_w_in, j)),
        'ev_q_gain': per(n_even, lambda j: row(jnp.tile(ev_q_gain[j], A_HEADS))),
        'ev_k_gain': per(n_even, lambda j: row(jnp.tile(ev_k_gain[j], A_HEADS))),
        'head_ones': jnp.asarray(head_ids[:, None] == head_ids[None, :], BF16),
        'ev_conv_w': per(n_even, lambda j: jnp.broadcast_to(
            ev_conv_w[j][:, None, :], (CONV_WIDTH, SUBLANES, CONV_CH))),
        'ev_conv_b': per(n_even, lambda j: row(ev_conv_b[j])),
        'ev_conv_ln_g': per(n_even, lambda j: row(ev_conv_ln_g[j])),
        'ev_conv_ln_b': per(n_even, lambda j: row(ev_conv_ln_b[j])),
        'ev_w_out_a': per(n_even, lambda j: _weight_bf16(ev_w_out, j, 0, A_WIDTH)),
        'ev_w_out_c': per(n_even, lambda j: _weight_bf16(ev_w_out, j, A_WIDTH, CONV_CH)),
        'od_w_main': per(n_odd, lambda j: _weight_bf16(od_w_in, j, n_cols=main)),
        'od_w_low': per(n_odd, lambda j: jnp.pad(od_w_in[j, :, main:],
                                                 ((0, 0), (0, LANES - GATE_RANK))).astype(BF16)),
        'od_gate_w_up': per(n_odd, lambda j: jnp.pad(od_gate_w_up[j],
                                                     ((0, LANES - GATE_RANK), (0, 0))).astype(BF16)),
        'od_gate_b': per(n_odd, lambda j: row(od_gate_b[j])),
        'od_o_gain': per(n_odd, lambda j: row(od_o_gain[j])),
        'od_w_out': per(n_odd, lambda j: _weight_bf16(od_w_out, j)),
        'band_vectors': _band_vectors(rel_bias),
        'decode_tables': _decode_tables(rel_bias, x_sample.shape[1]),
    }
    P, x_prompt, x_sample = lax.optimization_barrier((P, x_prompt, x_sample))
    return _trunks(x_prompt, x_sample, (cache_k, cache_v, cache_conv, state_gla), P)
```

```python
import functools

import numpy as np
import jax
import jax.numpy as jnp
from jax import lax
from jax.experimental import pallas as pl
from jax.experimental.pallas import tpu as pltpu

F32 = jnp.float32
BF16 = jnp.bfloat16

D_MODEL = 1024
DEPTH = 2
PAST_LEN = 16384
A_HEADS = 8
A_HEAD_DIM = 64
A_WIDTH = A_HEADS * A_HEAD_DIM
DILATED_GROUPS = ((128, 1), (512, 4), (2048, 16))
MAX_WINDOW = 2048
N_BUCKETS = 32
CONV_WIDTH = 31
CONV_CH = 512
C_HEADS = 4
C_DK = 512
C_DV = 1024
C_DK_HEAD = 128
C_DV_HEAD = 256
GATE_RANK = 16
GATE_TAU = 16.0
D_FF = 2816
EPS = 1e-6
NEG_INF = -1e30

LANES = 128
SUBLANES = 8
MXU_DIM = 256
WIN_KEYS = 128
ATT_BLOCK = 2048
ATT_UNROLL = 16
ATT_PHASES = 4
VMEM_LIMIT = 56 * 1024 * 1024


def _params(*sem):
    return pltpu.CompilerParams(dimension_semantics=sem, vmem_limit_bytes=VMEM_LIMIT)


def _dot(a, b):
    return jnp.dot(a, b, preferred_element_type=F32)


def _dot_nt(a, b):
    return lax.dot_general(a, b, (((1,), (1,)), ((), ())), preferred_element_type=F32)


def _dot_tn(a, b):
    return lax.dot_general(a, b, (((0,), (0,)), ((), ())), preferred_element_type=F32)


def _rms_rows(x, g):
    y = x * lax.rsqrt(jnp.mean(x * x, axis=-1, keepdims=True) + EPS)
    return y * g


def _sigmoid(x):
    return 1.0 / (1.0 + jnp.exp(-x))


def _full(shape):
    return pl.BlockSpec(shape, lambda *_: (0,) * len(shape), pipeline_mode=pl.Buffered(1))


FF_CHUNK = 256


def _rows_call(body, prompt_rows, sample_rows, consts, out_widths, tm, name, scratch=()):
    n_p, n_s = prompt_rows[0].shape[0], sample_rows[0].shape[0]
    steps = n_p // tm
    n_in, n_c, n_out = len(prompt_rows), len(consts), len(out_widths)

    def kernel(*refs):
        p_in, s_in = refs[:n_in], refs[n_in:2 * n_in]
        c_refs = refs[2 * n_in:2 * n_in + n_c]
        outs = refs[2 * n_in + n_c:]
        p_out, s_out, scr = outs[:n_out], outs[n_out:2 * n_out], outs[2 * n_out:]
        step = pl.program_id(0)

        @pl.when(step < steps)
        def _():
            body(p_in, c_refs, p_out, scr, tm)

        @pl.when(step == steps)
        def _():
            body(s_in, c_refs, s_out, scr, n_s)

    p_spec = lambda width: pl.BlockSpec((tm, width), lambda i: (jnp.minimum(i, steps - 1), 0))
    s_spec = lambda width: pl.BlockSpec((n_s, width), lambda i: (0, 0))
    res = pl.pallas_call(
        kernel,
        grid=(steps + 1,),
        in_specs=[p_spec(a.shape[1]) for a in prompt_rows] + [s_spec(a.shape[1]) for a in sample_rows]
                 + [_full(c.shape) for c in consts],
        out_specs=[p_spec(w) for w in out_widths] + [s_spec(w) for w in out_widths],
        out_shape=[jax.ShapeDtypeStruct((n_p, w), F32) for w in out_widths]
                  + [jax.ShapeDtypeStruct((n_s, w), F32) for w in out_widths],
        scratch_shapes=list(scratch),
        compiler_params=_params("arbitrary"),
        name=name,
    )(*prompt_rows, *sample_rows, *consts)
    return res[:n_out], res[n_out:]


def _ffn_body(ins, consts, outs, scratch, rows):
    x_ref, m_refs = ins[0], ins[1:]
    w_refs, (g_ref, wi_ref, wo_ref) = consts[:len(m_refs)], consts[len(m_refs):]
    (o_ref,), (act_ref,) = outs, scratch
    x = x_ref[...]
    for m_ref, w_ref in zip(m_refs, w_refs):
        x = x + _dot(m_ref[...], w_ref[...])
    h = _rms_rows(x, g_ref[...]).astype(BF16)
    for c in range(D_FF // FF_CHUNK):
        lo = c * FF_CHUNK
        a = _dot(h, wi_ref[:, lo:lo + FF_CHUNK])
        b = _dot(h, wi_ref[:, D_FF + lo:D_FF + lo + FF_CHUNK])
        act_ref[0:rows, lo:lo + FF_CHUNK] = (a * _sigmoid(a) * b).astype(BF16)
    o_ref[...] = x + 0.5 * _dot(act_ref[0:rows, :], wo_ref[...])


def _ffn(x_p, x_s, g, wi, wo, tm, ms_p=(), ms_s=(), ws=()):
    (y_p,), (y_s,) = _rows_call(_ffn_body, [x_p, *ms_p], [x_s, *ms_s], [*ws, g, wi, wo],
                                [D_MODEL], tm, "ffn", [pltpu.VMEM((tm, D_FF), BF16)])
    return y_p, y_s


CAST_BLOCK_BYTES = 4 * 1024 * 1024
BF16_ROWS = 16


def _cast_kernel(w_ref, o_ref):
    o_ref[...] = w_ref[...].astype(o_ref.dtype)


def _weight_bf16(w, layer, row0=0, n_rows=None, n_cols=None):
    n_rows = w.shape[1] - row0 if n_rows is None else n_rows
    n_cols = w.shape[2] if n_cols is None else n_cols
    fits = [r for r in range(BF16_ROWS, n_rows + 1, BF16_ROWS)
            if n_rows % r == 0 and row0 % r == 0 and r * n_cols * 4 <= CAST_BLOCK_BYTES]
    rows = max(fits)
    first = row0 // rows
    return pl.pallas_call(
        _cast_kernel,
        grid=(n_rows // rows,),
        in_specs=[pl.BlockSpec((None, rows, n_cols), lambda i: (layer, first + i, 0))],
        out_specs=pl.BlockSpec((rows, n_cols), lambda i: (i, 0)),
        out_shape=jax.ShapeDtypeStruct((n_rows, n_cols), BF16),
        compiler_params=_params("parallel"),
        name="weight_bf16",
    )(w)


def _head_norm(z, gain, bd):
    zz = z * z
    hi = zz.astype(BF16)
    lo = (zz - hi.astype(F32)).astype(BF16)
    width = bd.shape[0]
    ss = jnp.concatenate(
        [_dot(hi[:, c:c + width], bd) + _dot(lo[:, c:c + width], bd)
         for c in range(0, z.shape[1], width)], axis=1)
    return z * lax.rsqrt(ss * (1.0 / A_HEAD_DIM) + EPS) * gain


def _even_in_body(ins, consts, outs, scratch, rows):
    (x_ref,), (g_ref, w_ref, qg_ref, kg_ref, bd_ref), (q_ref, k_ref, v_ref, u_ref) = ins, consts, outs
    h = _rms_rows(x_ref[...], g_ref[...]).astype(BF16)
    p = _dot(h, w_ref[...])
    bd = bd_ref[...]
    q_ref[...] = _head_norm(p[:, 0:A_WIDTH], qg_ref[...], bd) * (A_HEAD_DIM ** -0.5)
    k_ref[...] = _head_norm(p[:, A_WIDTH:2 * A_WIDTH], kg_ref[...], bd)
    v_ref[...] = p[:, 2 * A_WIDTH:3 * A_WIDTH]
    gv = p[:, 3 * A_WIDTH:3 * A_WIDTH + CONV_CH]
    gg = p[:, 3 * A_WIDTH + CONV_CH:3 * A_WIDTH + 2 * CONV_CH]
    u_ref[...] = gv * _sigmoid(gg)


def _even_in(x_p, x_s, g, w, qg, kg, bd, tm):
    return _rows_call(_even_in_body, [x_p], [x_s], [g, w, qg, kg, bd], [A_WIDTH] * 4, tm, "even_in")


def _t5_bucket(dist):
    max_exact = N_BUCKETS // 2
    d = np.asarray(dist, dtype=np.int32)
    df = np.maximum(d, 1).astype(np.float32)
    large = max_exact + (np.log(df / max_exact) / np.log(MAX_WINDOW / max_exact)
                         * (N_BUCKETS - max_exact)).astype(np.int32)
    large = np.minimum(large, N_BUCKETS - 1)
    return np.where(d < max_exact, d, large).astype(np.int32)


def _select_bias(rel_bias, dist, valid):
    onehot = (_t5_bucket(dist)[None, :] == np.arange(N_BUCKETS)[:, None]) & valid[None, :]
    picked = jnp.einsum('bh,bc->hc', rel_bias, jnp.asarray(onehot, F32),
                        precision=lax.Precision.HIGHEST)
    return picked + jnp.asarray(np.where(valid, 0.0, NEG_INF), F32)[None, :]


def _band_vectors(rel_bias):
    c = np.arange(2 * WIN_KEYS)
    valid = c <= WIN_KEYS
    vecs = [_select_bias(rel_bias, np.where(valid, (WIN_KEYS - c) * dil, 0), valid)
            for _, dil in DILATED_GROUPS]
    return jnp.stack(vecs).reshape(len(DILATED_GROUPS), A_HEADS // 2, 2, 2 * WIN_KEYS)


def _attn_prompt_kernel(q_ref, kp_ref, kc_ref, vp_ref, vc_ref, vec_ref, o_ref,
                        knat, vnat, k4, v4, q4, og1, lg1, og4, lg4, tab_ref):
    blk = pl.program_id(2)
    n_groups = len(DILATED_GROUPS)
    nph = ATT_PHASES
    per = ATT_BLOCK // nph

    @pl.when(blk == 0)
    def _():
        col = lax.broadcasted_iota(jnp.int32, (WIN_KEYS, 2 * WIN_KEYS), 1)
        for g in range(n_groups):
            for hh in range(2):
                vec = jnp.broadcast_to(vec_ref[g, 0, hh:hh + 1, :], (WIN_KEYS, 2 * WIN_KEYS))
                band = pltpu.roll(vec, 0, 1, stride=1, stride_axis=0)
                rows = slice(hh * WIN_KEYS, (hh + 1) * WIN_KEYS)
                tab_ref[g, rows, :] = band
                tab_ref[n_groups + g, rows, :] = jnp.where(col >= WIN_KEYS, band, NEG_INF)

    knat[0:WIN_KEYS, :] = kp_ref[ATT_BLOCK - WIN_KEYS:, :]
    knat[WIN_KEYS:, :] = kc_ref[...]
    vnat[0:WIN_KEYS, :] = vp_ref[ATT_BLOCK - WIN_KEYS:, :]
    vnat[WIN_KEYS:, :] = vc_ref[...]
    for r in range(nph):
        phase = pl.ds(r, per, stride=nph)
        k4[r, 0:per, :] = kp_ref[phase, :]
        k4[r, per:, :] = kc_ref[phase, :]
        v4[r, 0:per, :] = vp_ref[phase, :]
        v4[r, per:, :] = vc_ref[phase, :]
        q4[r] = q_ref[phase, :]
    first = blk == 0
    lane = lax.broadcasted_iota(jnp.int32, (WIN_KEYS, LANES), 1)
    low = lane < A_HEAD_DIM
    ones = jnp.ones((2 * WIN_KEYS, LANES), BF16)

    def block(qs, kk, vv, tab):
        qs = qs.astype(BF16)
        zero = jnp.zeros_like(qs)
        qst = jnp.concatenate([jnp.where(low, qs, zero), jnp.where(low, zero, qs)], axis=0)
        s = _dot_nt(qst, kk.astype(BF16)) + tab
        mx = jnp.max(s, axis=-1, keepdims=True)
        p = jnp.exp(s - mx).astype(BF16)
        r = _dot(p, jnp.concatenate([vv.astype(BF16), ones], axis=1))
        o2 = jnp.where(low, r[0:WIN_KEYS, 0:LANES], r[WIN_KEYS:, 0:LANES])
        l2 = jnp.where(low, r[0:WIN_KEYS, LANES:], r[WIN_KEYS:, LANES:])
        m2 = jnp.where(low, jnp.broadcast_to(mx[0:WIN_KEYS], (WIN_KEYS, LANES)),
                       jnp.broadcast_to(mx[WIN_KEYS:], (WIN_KEYS, LANES)))
        return o2 / l2, m2 + jnp.log(l2)

    def table(g, at_start):
        return tab_ref[jnp.where(jnp.logical_and(at_start, first), n_groups + g, g)]

    def body1(sub, carry):
        i0 = pl.multiple_of(sub * WIN_KEYS, WIN_KEYS)
        o, l = block(q_ref[pl.ds(i0, WIN_KEYS), :], knat[pl.ds(i0, 2 * WIN_KEYS), :],
                     vnat[pl.ds(i0, 2 * WIN_KEYS), :], table(0, sub == 0))
        og1[pl.ds(i0, WIN_KEYS), :] = o
        lg1[pl.ds(i0, WIN_KEYS), :] = l
        return carry

    def body2(pb, carry):
        sub = pb // nph
        r = pb - sub * nph
        i0 = pl.multiple_of(sub * WIN_KEYS, WIN_KEYS)
        keys = pl.ds(i0 + (per - WIN_KEYS), 2 * WIN_KEYS)
        o, l = block(q4[r, pl.ds(i0, WIN_KEYS), :], k4[r, keys, :], v4[r, keys, :],
                     table(1, sub == 0))
        og4[0, r, pl.ds(i0, WIN_KEYS), :] = o
        lg4[0, r, pl.ds(i0, WIN_KEYS), :] = l
        return carry

    def body3(pb, carry):
        a = pb // nph
        r = pb - a * nph
        rows = pl.ds(a, WIN_KEYS, stride=nph)
        keys = pl.ds(a, 2 * WIN_KEYS, stride=nph)
        o, l = block(q4[r, rows, :], k4[r, keys, :], v4[r, keys, :], table(2, True))
        og4[1, r, rows, :] = o
        lg4[1, r, rows, :] = l
        return carry

    n_blocks = ATT_BLOCK // WIN_KEYS
    for body in (body1, body2, body3):
        lax.fori_loop(0, n_blocks, body, 0, unroll=ATT_UNROLL)

    for r in range(nph):
        phase = pl.ds(r, per, stride=nph)
        la, lb, lc = lg1[phase, :], lg4[0, r], lg4[1, r]
        mx = jnp.maximum(jnp.maximum(la, lb), lc)
        wa, wb, wc = jnp.exp(la - mx), jnp.exp(lb - mx), jnp.exp(lc - mx)
        og1[phase, :] = (wa * og1[phase, :] + wb * og4[0, r] + wc * og4[1, r]) / (wa + wb + wc)
    o_ref[...] = og1[...].astype(o_ref.dtype)


def _attn_prompt(q, k, v, vecs, batch, seq):
    assert [d for _, d in DILATED_GROUPS] == [1, ATT_PHASES, ATT_PHASES ** 2]
    nb = seq // ATT_BLOCK
    n_groups = len(DILATED_GROUPS)
    per = ATT_BLOCK // ATT_PHASES
    cur = lambda b, p, t: (b * nb + t, p)
    prev = lambda b, p, t: (b * nb + jnp.maximum(t - 1, 0), p)
    blk = lambda imap: pl.BlockSpec((ATT_BLOCK, LANES), imap)
    vmem = lambda *shape: pltpu.VMEM(shape, F32)
    return pl.pallas_call(
        _attn_prompt_kernel,
        grid=(batch, A_HEADS // 2, nb),
        in_specs=[blk(cur), blk(prev), blk(cur), blk(prev), blk(cur),
                  pl.BlockSpec((n_groups, 1, 2, 2 * WIN_KEYS), lambda b, p, t: (0, p, 0, 0))],
        out_specs=blk(cur),
        out_shape=jax.ShapeDtypeStruct((batch * seq, A_WIDTH), BF16),
        scratch_shapes=[vmem(WIN_KEYS + ATT_BLOCK, LANES), vmem(WIN_KEYS + ATT_BLOCK, LANES),
                        vmem(ATT_PHASES, 2 * per, LANES), vmem(ATT_PHASES, 2 * per, LANES),
                        vmem(ATT_PHASES, per, LANES),
                        vmem(ATT_BLOCK, LANES), vmem(ATT_BLOCK, LANES),
                        vmem(2, ATT_PHASES, per, LANES), vmem(2, ATT_PHASES, per, LANES),
                        vmem(2 * n_groups, 2 * WIN_KEYS, 2 * WIN_KEYS)],
        compiler_params=_params("arbitrary", "arbitrary", "arbitrary"),
        name="attn_prompt",
    )(q, k, k, v, v, vecs)


def _decode_tables(rel_bias, n_new):
    cols = MAX_WINDOW + n_new
    c = np.arange(cols)
    dist = MAX_WINDOW - c
    cnt = np.zeros(c.shape, np.float32)
    for window, dil in DILATED_GROUPS:
        cnt += ((dist >= 0) & (dist <= window) & (dist % dil == 0)).astype(np.float32)
    vec = _select_bias(rel_bias, np.clip(dist, 0, MAX_WINDOW), cnt > 0)
    vec = vec + jnp.asarray(np.log(np.maximum(cnt, 1.0)), F32)[None, :]
    rows = jnp.stack([jnp.pad(vec[:, :cols - i], ((0, 0), (i, 0)), constant_values=NEG_INF)
                      for i in range(n_new)], axis=1)
    rows = rows.reshape(A_HEADS * n_new, cols)
    return rows[:, :MAX_WINDOW], rows[:, MAX_WINDOW:]


def _attn_sample_kernel(q_ref, kn_ref, vn_ref, kc_ref, vc_ref, tc_ref, tn_ref,
                        o_ref, ko_ref, vo_ref):
    n_new = q_ref.shape[1]
    n_buf = kc_ref.shape[2]
    rows = A_HEADS * n_new
    kn = kn_ref[0]
    vn = vn_ref[0]
    kc = kc_ref[0]
    vc = vc_ref[0]

    lane = lax.broadcasted_iota(jnp.int32, (A_WIDTH, LANES), 1)
    for new, old, out_ref in ((kn, kc, ko_ref), (vn, vc, vo_ref)):
        shifted = pltpu.roll(old, n_buf - n_new, 1)
        tail = jnp.concatenate([jnp.zeros((LANES - n_new, A_WIDTH), F32), new], axis=0).T
        out_ref[0, :, 0:n_buf - LANES] = shifted[:, 0:n_buf - LANES]
        out_ref[0, :, n_buf - LANES:] = jnp.where(lane >= LANES - n_new, tail,
                                                  shifted[:, n_buf - LANES:])

    q = q_ref[0]
    row_head = lax.broadcasted_iota(jnp.int32, (A_HEADS, n_new, A_WIDTH), 0).reshape(rows, A_WIDTH)
    col = lax.broadcasted_iota(jnp.int32, (rows, A_WIDTH), 1)
    own = jnp.logical_and(col >= row_head * A_HEAD_DIM, col < (row_head + 1) * A_HEAD_DIM)
    qblk = jnp.where(own, jnp.concatenate([q] * A_HEADS, axis=0), 0.0).astype(BF16)
    s_c = _dot(qblk, kc.astype(BF16)) + tc_ref[...]
    s_n = _dot_nt(qblk, kn.astype(BF16)) + tn_ref[...]
    mx = jnp.maximum(jnp.max(s_c, axis=-1, keepdims=True), jnp.max(s_n, axis=-1, keepdims=True))
    p_c = jnp.exp(s_c - mx)
    p_n = jnp.exp(s_n - mx)
    den = jnp.sum(p_c, axis=-1, keepdims=True) + jnp.sum(p_n, axis=-1, keepdims=True)
    acc = _dot_nt(p_c.astype(BF16), vc.astype(BF16)) + _dot(p_n.astype(BF16), vn.astype(BF16))
    acc = jnp.where(own, acc / den, 0.0)
    out = acc[0:n_new]
    for h in range(1, A_HEADS):
        out = out + acc[h * n_new:(h + 1) * n_new]
    o_ref[0] = out.astype(o_ref.dtype)


def _attn_sample(q, k_new, v_new, cache_k, cache_v, layer, tables):
    b, n_new, _ = q.shape
    n_buf = cache_k.shape[2]
    new = pl.BlockSpec((1, n_new, A_WIDTH), lambda i: (i, 0, 0))
    buf = pl.BlockSpec((1, A_WIDTH, n_buf), lambda i: (i, 0, 0))
    past = pl.BlockSpec((1, A_WIDTH, n_buf), lambda i: (layer * b + i, 0, 0))
    return pl.pallas_call(
        _attn_sample_kernel,
        grid=(b,),
        in_specs=[new, new, new, past, past, _full(tables[0].shape), _full(tables[1].shape)],
        out_specs=[new, buf, buf],
        out_shape=[jax.ShapeDtypeStruct((b, n_new, A_WIDTH), BF16),
                   jax.ShapeDtypeStruct((b, A_WIDTH, n_buf), F32),
                   jax.ShapeDtypeStruct((b, A_WIDTH, n_buf), F32)],
        compiler_params=_params("parallel"),
        name="attn_sample",
    )(q, k_new, v_new, cache_k, cache_v, *tables)


CONV_PAD = 32
CONV_ROWS = 32
CONV_UNROLL = 2


def _conv_kernel(tc, u_ref, up_ref, hist_ref, w_ref, b_ref, g_ref, beta_ref, o_ref, win, stage):
    t = pl.program_id(1)
    n_seq = u_ref.shape[0]
    n_slab = CONV_CH // LANES
    slab = lambda c: slice(c * LANES, (c + 1) * LANES)
    off = CONV_PAD - (CONV_WIDTH - 1)
    rc = min(CONV_ROWS, tc)
    half = rc // 2

    def tap(k, c):
        w = w_ref[k, :, slab(c)]
        if half < SUBLANES:
            return w[0:half]
        return jnp.concatenate([w] * (half // SUBLANES), axis=0)

    for s in range(n_seq):
        first = s * n_slab
        for c in range(n_slab):
            win[first + c, CONV_PAD:CONV_PAD + tc, :] = u_ref[s, :, slab(c)]

        @pl.when(t == 0)
        def _():
            for c in range(n_slab):
                win[first + c, 0:CONV_PAD, :] = hist_ref[s, :, slab(c)]

        @pl.when(t > 0)
        def _():
            for c in range(n_slab):
                win[first + c, 0:CONV_PAD, :] = up_ref[s, :, slab(c)]

        def body(j, carry):
            r0 = j * rc
            for c in range(n_slab):
                for par in range(2):
                    acc = jnp.zeros((half, LANES), F32) + b_ref[:, slab(c)]
                    for k in range(CONV_WIDTH):
                        rows = pl.ds(r0 + off + k + par, half, stride=2)
                        acc = acc + win[first + c, rows, :] * tap(k, c)
                    stage[first + c, pl.ds(r0 + par, half, stride=2), :] = acc
            return carry

        if tc == rc:
            body(0, 0)
        else:
            lax.fori_loop(0, tc // rc, body, 0, unroll=CONV_UNROLL)
        y = jnp.concatenate([stage[first + c] for c in range(n_slab)], axis=1)
        xc = y - jnp.mean(y, axis=-1, keepdims=True)
        y = xc * lax.rsqrt(jnp.mean(xc * xc, axis=-1, keepdims=True) + EPS)
        y = y * g_ref[...] + beta_ref[...]
        o_ref[s] = (y * _sigmoid(y)).astype(o_ref.dtype)


def _conv(u, hist, w, b, g, beta, tc, seqs):
    bsz, t, _ = u.shape
    per = tc // CONV_PAD
    if t >= CONV_PAD:
        prev = pl.BlockSpec((seqs, CONV_PAD, CONV_CH),
                            lambda i, j: (i, jnp.maximum(j * per - 1, 0), 0))
        u_prev = u
    else:
        prev = pl.BlockSpec((seqs, CONV_PAD, CONV_CH), lambda i, j: (i, 0, 0))
        u_prev = hist
    n_slab = seqs * CONV_CH // LANES
    return pl.pallas_call(
        functools.partial(_conv_kernel, tc),
        grid=(bsz // seqs, t // tc),
        in_specs=[pl.BlockSpec((seqs, tc, CONV_CH), lambda i, j: (i, j, 0)),
                  prev,
                  pl.BlockSpec((seqs, CONV_PAD, CONV_CH), lambda i, j: (i, 0, 0)),
                  _full(w.shape), _full((1, CONV_CH)), _full((1, CONV_CH)),
                  _full((1, CONV_CH))],
        out_specs=pl.BlockSpec((seqs, tc, CONV_CH), lambda i, j: (i, j, 0)),
        out_shape=jax.ShapeDtypeStruct((bsz, t, CONV_CH), BF16),
        scratch_shapes=[pltpu.VMEM((n_slab, CONV_PAD + tc, LANES), F32),
                        pltpu.VMEM((n_slab, tc, LANES), F32)],
        compiler_params=_params("parallel", "arbitrary"),
        name="conv",
    )(u, u_prev, hist, w, b, g, beta)


def _gla_in_body(ins, consts, outs, scratch, rows):
    (x_ref,), (g_ref, w_ref, wl_ref, wu_ref, bu_ref) = ins, consts
    q_ref, k_ref, v_ref, r_ref, la_ref = outs
    h = _rms_rows(x_ref[...], g_ref[...]).astype(BF16)
    p = _dot(h, w_ref[...])
    q_ref[...] = p[:, 0:C_DK] * (C_DK_HEAD ** -0.5)
    k_ref[...] = p[:, C_DK:2 * C_DK]
    v_ref[...] = p[:, 2 * C_DK:2 * C_DK + C_DV]
    r = p[:, 2 * C_DK + C_DV:2 * C_DK + 2 * C_DV]
    r_ref[...] = r * _sigmoid(r)
    low = _dot(h, wl_ref[...]).astype(BF16)
    z = _dot(low, wu_ref[...]) + bu_ref[...]
    log_sig = jnp.minimum(z, 0.0) - jnp.log1p(jnp.exp(-jnp.abs(z)))
    la_ref[...] = log_sig * (1.0 / GATE_TAU)


def _gla_in(x_p, x_s, g, w, wl, wu, bu, tm):
    return _rows_call(_gla_in_body, [x_p], [x_s], [g, w, wl, wu, bu],
                      [C_DK, C_DK, C_DV, C_DV, C_DK], tm, "gla_in")


def _gla_kernel(chunk, n_chunks, q_ref, k_ref, v_ref, r_ref, la_ref, s0_ref, gain_ref,
                o_ref, s_ref, qin_s, kin_s, x1_s, x2_s, qst_s, kst_s, dec_s):
    n_seq = q_ref.shape[0]
    @pl.when(pl.program_id(1) == 0)
    def _():
        s_ref[...] = s0_ref[...]

    half, quarter = chunk // 2, chunk // 4
    quarter_of = lambda i: sum((i >= j * quarter).astype(jnp.int32) for j in range(1, 4))
    ri = lax.broadcasted_iota(jnp.int32, (chunk, chunk), 0)
    ci = lax.broadcasted_iota(jnp.int32, (chunk, chunk), 1)
    causal = ci <= ri
    tri = jnp.where(causal, 1.0, 0.0).astype(BF16)
    rq, cq = quarter_of(ri), quarter_of(ci)
    same_quarter = jnp.logical_and(causal, rq == cq)
    cross_half = jnp.logical_and(ri >= half, ci < half)
    cross_quarter = jnp.logical_and(rq == cq + 1, (ri >= half) == (ci >= half))
    rr = lax.broadcasted_iota(jnp.int32, (chunk, C_DK), 0)
    rrq = quarter_of(rr)
    in_first = rr < half
    key_side_2 = jnp.logical_or(rrq == 0, rrq == 2)
    gain = gain_ref[...]

    def prepare(g, c):
        rows = slice(c * chunk, (c + 1) * chunk)
        la = la_ref[g, rows, :]
        la_hi = la.astype(BF16)
        la_lo = (la - la_hi.astype(F32)).astype(BF16)
        cum = _dot(tri, la_hi) + _dot(tri, la_lo)
        row = lambda i: cum[i:i + 1, :]
        last = row(chunk - 1)
        mids = [row(j * quarter + quarter // 2 - 1) for j in range(4)]
        mid = jnp.where(rrq == 0, mids[0], jnp.where(rrq == 1, mids[1],
                        jnp.where(rrq == 2, mids[2], mids[3])))
        q = q_ref[g, rows, :]
        k = k_ref[g, rows, :]
        grow = jnp.exp(cum - mid)
        qin_s[g, rows, :] = (q * grow).astype(BF16)
        kin_s[g, rows, :] = (k / grow).astype(BF16)

        def across(edge, key_side):
            gap = cum - edge
            return (jnp.where(key_side, k, q) * jnp.exp(jnp.where(key_side, -gap, gap))).astype(BF16)

        x1_s[g, rows, :] = across(row(half - 1), in_first)
        x2_s[g, rows, :] = across(jnp.where(in_first, row(quarter - 1), row(half + quarter - 1)),
                               key_side_2)
        qst_s[g, rows, :] = (q * jnp.exp(cum)).astype(BF16)
        kst_s[g, rows, :] = (k * jnp.exp(last - cum)).astype(BF16)
        dec_s[g, c] = jnp.broadcast_to(jnp.exp(last), (LANES, C_DK)).T

    def advance(g, c):
        rows = slice(c * chunk, (c + 1) * chunk)
        for h in range(C_HEADS):
            ks = slice(h * C_DK_HEAD, (h + 1) * C_DK_HEAD)
            vs = slice(h * C_DV_HEAD, (h + 1) * C_DV_HEAD)
            vh = v_ref[g, rows, vs].astype(BF16)
            x1, x2 = x1_s[g, rows, ks], x2_s[g, rows, ks]
            att = jnp.where(same_quarter, _dot_nt(qin_s[g, rows, ks], kin_s[g, rows, ks]),
                            jnp.where(cross_quarter, _dot_nt(x2, x2),
                                      jnp.where(cross_half, _dot_nt(x1, x1), 0.0)))
            s = s_ref[g, h]
            o = _dot(jnp.concatenate([att.astype(BF16), qst_s[g, rows, ks]], axis=1),
                     jnp.concatenate([vh, s.astype(BF16)], axis=0))
            decay = dec_s[g, c, ks, :]
            s_ref[g, h] = (s * jnp.concatenate([decay] * (C_DV_HEAD // LANES), axis=1)
                           + _dot_tn(kst_s[g, rows, ks], vh))
            y = o * lax.rsqrt(jnp.mean(o * o, axis=-1, keepdims=True) + EPS) * gain
            o_ref[g, rows, vs] = (y * r_ref[g, rows, vs]).astype(o_ref.dtype)

    for g in range(n_seq):
        prepare(g, 0)
    for c in range(n_chunks):
        for g in range(n_seq):
            if c + 1 < n_chunks:
                prepare(g, c + 1)
            advance(g, c)


def _gla(q, k, v, r, la, s0, gain, chunk, tb, seqs):
    b, t, _ = q.shape
    seq = lambda width: pl.BlockSpec((seqs, tb, width), lambda i, j: (i, j, 0))
    state = pl.BlockSpec((seqs, C_HEADS, C_DK_HEAD, C_DV_HEAD), lambda i, j: (i, 0, 0, 0))
    return pl.pallas_call(
        functools.partial(_gla_kernel, chunk, tb // chunk),
        grid=(b // seqs, t // tb),
        in_specs=[seq(C_DK), seq(C_DK), seq(C_DV), seq(C_DV), seq(C_DK), state,
                  _full((1, C_DV_HEAD))],
        out_specs=[seq(C_DV), state],
        out_shape=[jax.ShapeDtypeStruct((b, t, C_DV), BF16),
                   jax.ShapeDtypeStruct(s0.shape, F32)],
        scratch_shapes=[pltpu.VMEM((seqs, tb, C_DK), BF16)] * 6
                       + [pltpu.VMEM((seqs, tb // chunk, C_DK, LANES), F32)],
        compiler_params=_params("parallel", "arbitrary"),
        name="gla",
    )(q, k, v, r, la, s0, gain)


GLA_CHUNK = 128
SAMPLE_PAD = 16
SAMPLE_SEQS = 4


ROW_TILE = 1024
CONV_TILE = 512
GLA_TILE = 512


def _even_mixer_prompt(q, k, v, u, P, i, bsz, t):
    n = bsz * t
    a = _attn_prompt(q, k, v, P['band_vectors'], bsz, t)
    keep = min(MAX_WINDOW, t)
    tail = lambda z: z.reshape(bsz, t, A_WIDTH)[:, t - keep:].reshape(bsz, keep, A_HEADS, A_HEAD_DIM)
    u3 = u.reshape(bsz, t, CONV_CH)
    hist = jnp.zeros((bsz, CONV_PAD, CONV_CH), F32)
    c = _conv(u3, hist, P['ev_conv_w'][i], P['ev_conv_b'][i], P['ev_conv_ln_g'][i],
              P['ev_conv_ln_b'][i], CONV_TILE, 1).reshape(n, CONV_CH)
    return [a, c], (tail(k), tail(v), u3[:, t - (CONV_WIDTH - 1):])


def _even_mixer_sample(q, k, v, u, past, P, i, bsz, t):
    n = bsz * t
    n_buf = past[0].shape[2]
    major = lambda z: z.transpose(0, 1, 3, 4, 2).reshape(z.shape[0] * bsz, A_WIDTH, n_buf)
    minor = lambda z: z.reshape(bsz, A_HEADS, A_HEAD_DIM, n_buf).transpose(0, 3, 1, 2)
    a, new_k, new_v = _attn_sample(
        q.reshape(bsz, t, A_WIDTH), k.reshape(bsz, t, A_WIDTH), v.reshape(bsz, t, A_WIDTH),
        major(past[0]), major(past[1]), i, P['decode_tables'])
    u3 = u.reshape(bsz, t, CONV_CH)
    hist = jnp.pad(past[2][i], ((0, 0), (CONV_PAD - (CONV_WIDTH - 1), 0), (0, 0)))
    c = _conv(u3, hist, P['ev_conv_w'][i], P['ev_conv_b'][i], P['ev_conv_ln_g'][i],
              P['ev_conv_ln_b'][i], t, SAMPLE_SEQS).reshape(n, CONV_CH)
    new_u = jnp.concatenate([past[2][i], u3], axis=1)[:, -(CONV_WIDTH - 1):]
    return [a.reshape(n, A_WIDTH), c], (minor(new_k), minor(new_v), new_u)


def _gla_mixer(q, k, v, r, la, s0, gain, bsz, t, t_pad, chunk, tile, seqs):
    seq = lambda z: jnp.pad(z.reshape(bsz, t, -1), ((0, 0), (0, t_pad - t), (0, 0)))
    o, s = _gla(seq(q), seq(k), seq(v), seq(r), seq(la), s0, gain, chunk, tile, seqs)
    return [o[:, :t].reshape(bsz * t, C_DV)], s


def _trunks(x_p, x_s, past, P):
    (b_p, t_p, _), (b_s, t_s, _) = x_p.shape, x_s.shape
    x_p = x_p.reshape(b_p * t_p, D_MODEL)
    x_s = x_s.reshape(b_s * t_s, D_MODEL)
    new_p = {'k': [], 'v': [], 'u': [], 's': []}
    new_s = {'k': [], 'v': [], 'u': [], 's': []}
    for layer in range(DEPTH):
        i = layer // 2
        x_p, x_s = _ffn(x_p, x_s, P['norm_ffn1'][layer], P['ffn1_w_in'][layer],
                        P['ffn1_w_out'][layer], ROW_TILE)
        if layer % 2 == 0:
            proj_p, proj_s = _even_in(x_p, x_s, P['norm_mix'][layer], P['ev_w_in'][i],
                                      P['ev_q_gain'][i], P['ev_k_gain'][i], P['head_ones'], ROW_TILE)
            mix_p, kvu_p = _even_mixer_prompt(*proj_p, P, i, b_p, t_p)
            mix_s, kvu_s = _even_mixer_sample(*proj_s, past, P, i, b_s, t_s)
            w_mix = [P['ev_w_out_a'][i], P['ev_w_out_c'][i]]
            for new, kvu in ((new_p, kvu_p), (new_s, kvu_s)):
                for name, z in zip('kvu', kvu):
                    new[name].append(z)
        else:
            proj_p, proj_s = _gla_in(x_p, x_s, P['norm_mix'][layer], P['od_w_main'][i],
                                     P['od_w_low'][i], P['od_gate_w_up'][i], P['od_gate_b'][i],
                                     ROW_TILE)
            zeros = jnp.zeros((b_p, C_HEADS, C_DK_HEAD, C_DV_HEAD), F32)
            gain = P['od_o_gain'][i]
            mix_p, s_p = _gla_mixer(*proj_p, zeros, gain, b_p, t_p, t_p, GLA_CHUNK, GLA_TILE, 1)
            mix_s, s_s = _gla_mixer(*proj_s, past[3][i], gain, b_s, t_s, SAMPLE_PAD, SAMPLE_PAD,
                                    SAMPLE_PAD, SAMPLE_SEQS)
            w_mix = [P['od_w_out'][i]]
            new_p['s'].append(s_p)
            new_s['s'].append(s_s)
        x_p, x_s = _ffn(x_p, x_s, P['norm_ffn2'][layer], P['ffn2_w_in'][layer],
                        P['ffn2_w_out'][layer], ROW_TILE, mix_p, mix_s, w_mix)
    stacked = lambda new: tuple(jnp.stack(new[name]) for name in 'kvus')
    return (x_p.reshape(b_p, t_p, D_MODEL), x_s.reshape(b_s, t_s, D_MODEL),
            *stacked(new_p), *stacked(new_s))


def kernel(x_prompt, x_sample, cache_k, cache_v, cache_conv, state_gla, rel_bias, norm_ffn1, ffn1_w_in, ffn1_w_out, norm_mix, norm_ffn2, ffn2_w_in, ffn2_w_out, ev_w_in, ev_q_gain, ev_k_gain, ev_conv_w, ev_conv_b, ev_conv_ln_g, ev_conv_ln_b, ev_w_out, od_w_in, od_gate_w_up, od_gate_b, od_o_gain, od_w_out):
    n_even = ev_w_in.shape[0]
    n_odd = od_w_in.shape[0]
    main = 2 * C_DK + 2 * C_DV
    head_ids = np.arange(MXU_DIM) // A_HEAD_DIM
    per = lambda n, f: [f(j) for j in range(n)]
    row = lambda z: z[None, :]
    P = {
        'norm_ffn1': per(DEPTH, lambda j: row(norm_ffn1[j])),
        'norm_mix': per(DEPTH, lambda j: row(norm_mix[j])),
        'norm_ffn2': per(DEPTH, lambda j: row(norm_ffn2[j])),
        'ffn1_w_in': per(DEPTH, lambda j: _weight_bf16(ffn1_w_in, j)),
        'ffn1_w_out': per(DEPTH, lambda j: _weight_bf16(ffn1_w_out, j)),
        'ffn2_w_in': per(DEPTH, lambda j: _weight_bf16(ffn2_w_in, j)),
        'ffn2_w_out': per(DEPTH, lambda j: _weight_bf16(ffn2_w_out, j)),
        'ev_w_in': per(n_even, lambda j: _weight_bf16(ev_w_in, j)),
        'ev_q_gain': per(n_even, lambda j: row(jnp.tile(ev_q_gain[j], A_HEADS))),
        'ev_k_gain': per(n_even, lambda j: row(jnp.tile(ev_k_gain[j], A_HEADS))),
        'head_ones': jnp.asarray(head_ids[:, None] == head_ids[None, :], BF16),
        'ev_conv_w': per(n_even, lambda j: jnp.broadcast_to(
            ev_conv_w[j][:, None, :], (CONV_WIDTH, SUBLANES, CONV_CH))),
        'ev_conv_b': per(n_even, lambda j: row(ev_conv_b[j])),
        'ev_conv_ln_g': per(n_even, lambda j: row(ev_conv_ln_g[j])),
        'ev_conv_ln_b': per(n_even, lambda j: row(ev_conv_ln_b[j])),
        'ev_w_out_a': per(n_even, lambda j: _weight_bf16(ev_w_out, j, 0, A_WIDTH)),
        'ev_w_out_c': per(n_even, lambda j: _weight_bf16(ev_w_out, j, A_WIDTH, CONV_CH)),
        'od_w_main': per(n_odd, lambda j: _weight_bf16(od_w_in, j, n_cols=main)),
        'od_w_low': per(n_odd, lambda j: jnp.pad(od_w_in[j, :, main:],
                                                 ((0, 0), (0, LANES - GATE_RANK))).astype(BF16)),
        'od_gate_w_up': per(n_odd, lambda j: jnp.pad(od_gate_w_up[j],
                                                     ((0, LANES - GATE_RANK), (0, 0))).astype(BF16)),
        'od_gate_b': per(n_odd, lambda j: row(od_gate_b[j])),
        'od_o_gain': per(n_odd, lambda j: row(od_o_gain[j])),
        'od_w_out': per(n_odd, lambda j: _weight_bf16(od_w_out, j)),
        'band_vectors': _band_vectors(rel_bias),
        'decode_tables': _decode_tables(rel_bias, x_sample.shape[1]),
    }
    P, x_prompt, x_sample = lax.optimization_barrier((P, x_prompt, x_sample))
    return _trunks(x_prompt, x_sample, (cache_k, cache_v, cache_conv, state_gla), P)
```

```python
import functools

import numpy as np
import jax
import jax.numpy as jnp
from jax import lax
from jax.experimental import pallas as pl
from jax.experimental.pallas import tpu as pltpu

F32 = jnp.float32
BF16 = jnp.bfloat16

D_MODEL = 1024
DEPTH = 2
PAST_LEN = 16384
A_HEADS = 8
A_HEAD_DIM = 64
A_WIDTH = A_HEADS * A_HEAD_DIM
DILATED_GROUPS = ((128, 1), (512, 4), (2048, 16))
MAX_WINDOW = 2048
N_BUCKETS = 32
CONV_WIDTH = 31
CONV_CH = 512
C_HEADS = 4
C_DK = 512
C_DV = 1024
C_DK_HEAD = 128
C_DV_HEAD = 256
GATE_RANK = 16
GATE_TAU = 16.0
D_FF = 2816
EPS = 1e-6
NEG_INF = -1e30

LANES = 128
SUBLANES = 8
MXU_DIM = 256
WIN_KEYS = 128
ATT_BLOCK = 2048
ATT_UNROLL = 16
ATT_PHASES = 4
VMEM_LIMIT = 56 * 1024 * 1024


def _params(*sem):
    return pltpu.CompilerParams(dimension_semantics=sem, vmem_limit_bytes=VMEM_LIMIT)


def _dot(a, b):
    return jnp.dot(a, b, preferred_element_type=F32)


def _dot_nt(a, b):
    return lax.dot_general(a, b, (((1,), (1,)), ((), ())), preferred_element_type=F32)


def _dot_tn(a, b):
    return lax.dot_general(a, b, (((0,), (0,)), ((), ())), preferred_element_type=F32)


def _rms_rows(x, g):
    y = x * lax.rsqrt(jnp.mean(x * x, axis=-1, keepdims=True) + EPS)
    return y * g


def _sigmoid(x):
    return 1.0 / (1.0 + jnp.exp(-x))


def _full(shape):
    return pl.BlockSpec(shape, lambda *_: (0,) * len(shape), pipeline_mode=pl.Buffered(1))


FF_CHUNK = 256
FF_DOWN_CHUNK = 1024


def _layer(w, layer, row_block=0, n_blocks=1):
    rows = w.shape[1] // n_blocks
    return pl.BlockSpec((None, rows, w.shape[2]), lambda *_: (layer, row_block, 0),
                        pipeline_mode=pl.Buffered(1))


def _rows_call(body, prompt_rows, sample_rows, consts, out_widths, tm, name, scratch=()):
    n_p, n_s = prompt_rows[0].shape[0], sample_rows[0].shape[0]
    steps = n_p // tm
    n_in, n_c, n_out = len(prompt_rows), len(consts), len(out_widths)

    def kernel(*refs):
        p_in, s_in = refs[:n_in], refs[n_in:2 * n_in]
        c_refs = refs[2 * n_in:2 * n_in + n_c]
        outs = refs[2 * n_in + n_c:]
        p_out, s_out, scr = outs[:n_out], outs[n_out:2 * n_out], outs[2 * n_out:]
        step = pl.program_id(0)

        @pl.when(step < steps)
        def _():
            body(p_in, c_refs, p_out, scr, tm)

        @pl.when(step == steps)
        def _():
            body(s_in, c_refs, s_out, scr, n_s)

    p_spec = lambda width: pl.BlockSpec((tm, width), lambda i: (jnp.minimum(i, steps - 1), 0))
    s_spec = lambda width: pl.BlockSpec((n_s, width), lambda i: (0, 0))
    res = pl.pallas_call(
        kernel,
        grid=(steps + 1,),
        in_specs=[p_spec(a.shape[1]) for a in prompt_rows] + [s_spec(a.shape[1]) for a in sample_rows]
                 + [_layer(*c) if isinstance(c, tuple) else _full(c.shape) for c in consts],
        out_specs=[p_spec(w) for w in out_widths] + [s_spec(w) for w in out_widths],
        out_shape=[jax.ShapeDtypeStruct((n_p, w), F32) for w in out_widths]
                  + [jax.ShapeDtypeStruct((n_s, w), F32) for w in out_widths],
        scratch_shapes=list(scratch),
        compiler_params=_params("arbitrary"),
        name=name,
    )(*prompt_rows, *sample_rows, *[c[0] if isinstance(c, tuple) else c for c in consts])
    return res[:n_out], res[n_out:]


def _ffn_body(ins, consts, outs, scratch, rows):
    x_ref, m_refs = ins[0], ins[1:]
    w_refs, (g_ref, wi_ref, wo_ref) = consts[:len(m_refs)], consts[len(m_refs):]
    (o_ref,), (act_ref,) = outs, scratch
    x = x_ref[...]
    for m_ref, w_ref in zip(m_refs, w_refs):
        x = x + _dot(m_ref[...], w_ref[...].astype(BF16))
    h = _rms_rows(x, g_ref[...]).astype(BF16)
    for c in range(D_FF // FF_CHUNK):
        lo = c * FF_CHUNK
        a = _dot(h, wi_ref[:, lo:lo + FF_CHUNK].astype(BF16))
        b = _dot(h, wi_ref[:, D_FF + lo:D_FF + lo + FF_CHUNK].astype(BF16))
        act_ref[0:rows, lo:lo + FF_CHUNK] = (a * _sigmoid(a) * b).astype(BF16)
    y = jnp.zeros_like(x)
    for lo in range(0, D_FF, FF_DOWN_CHUNK):
        hi = min(lo + FF_DOWN_CHUNK, D_FF)
        y = y + _dot(act_ref[0:rows, lo:hi], wo_ref[lo:hi, :].astype(BF16))
    o_ref[...] = x + 0.5 * y


def _ffn(x_p, x_s, g, wi, wo, tm, ms_p=(), ms_s=(), ws=()):
    (y_p,), (y_s,) = _rows_call(_ffn_body, [x_p, *ms_p], [x_s, *ms_s], [*ws, g, wi, wo],
                                [D_MODEL], tm, "ffn", [pltpu.VMEM((tm, D_FF), BF16)])
    return y_p, y_s


def _head_norm(z, gain, bd):
    zz = z * z
    hi = zz.astype(BF16)
    lo = (zz - hi.astype(F32)).astype(BF16)
    width = bd.shape[0]
    ss = jnp.concatenate(
        [_dot(hi[:, c:c + width], bd) + _dot(lo[:, c:c + width], bd)
         for c in range(0, z.shape[1], width)], axis=1)
    return z * lax.rsqrt(ss * (1.0 / A_HEAD_DIM) + EPS) * gain


def _even_in_body(ins, consts, outs, scratch, rows):
    (x_ref,), (g_ref, w_ref, qg_ref, kg_ref, bd_ref), (q_ref, k_ref, v_ref, u_ref) = ins, consts, outs
    h = _rms_rows(x_ref[...], g_ref[...]).astype(BF16)
    proj = lambda lo, width: _dot(h, w_ref[:, lo:lo + width].astype(BF16))
    bd = bd_ref[...]
    q_ref[...] = _head_norm(proj(0, A_WIDTH), qg_ref[...], bd) * (A_HEAD_DIM ** -0.5)
    k_ref[...] = _head_norm(proj(A_WIDTH, A_WIDTH), kg_ref[...], bd)
    v_ref[...] = proj(2 * A_WIDTH, A_WIDTH)
    gv = proj(3 * A_WIDTH, CONV_CH)
    gg = proj(3 * A_WIDTH + CONV_CH, CONV_CH)
    u_ref[...] = gv * _sigmoid(gg)


def _even_in(x_p, x_s, g, w, qg, kg, bd, tm):
    return _rows_call(_even_in_body, [x_p], [x_s], [g, w, qg, kg, bd], [A_WIDTH] * 4, tm, "even_in")


def _t5_bucket(dist):
    max_exact = N_BUCKETS // 2
    d = np.asarray(dist, dtype=np.int32)
    df = np.maximum(d, 1).astype(np.float32)
    large = max_exact + (np.log(df / max_exact) / np.log(MAX_WINDOW / max_exact)
                         * (N_BUCKETS - max_exact)).astype(np.int32)
    large = np.minimum(large, N_BUCKETS - 1)
    return np.where(d < max_exact, d, large).astype(np.int32)


def _select_bias(rel_bias, dist, valid):
    onehot = (_t5_bucket(dist)[None, :] == np.arange(N_BUCKETS)[:, None]) & valid[None, :]
    picked = jnp.einsum('bh,bc->hc', rel_bias, jnp.asarray(onehot, F32),
                        precision=lax.Precision.HIGHEST)
    return picked + jnp.asarray(np.where(valid, 0.0, NEG_INF), F32)[None, :]


def _band_vectors(rel_bias):
    c = np.arange(2 * WIN_KEYS)
    valid = c <= WIN_KEYS
    vecs = [_select_bias(rel_bias, np.where(valid, (WIN_KEYS - c) * dil, 0), valid)
            for _, dil in DILATED_GROUPS]
    return jnp.stack(vecs).reshape(len(DILATED_GROUPS), A_HEADS // 2, 2, 2 * WIN_KEYS)


def _attn_prompt_kernel(q_ref, kp_ref, kc_ref, vp_ref, vc_ref, vec_ref, o_ref,
                        knat, vnat, k4, v4, q4, og1, lg1, og4, lg4, tab_ref):
    blk = pl.program_id(2)
    n_groups = len(DILATED_GROUPS)
    nph = ATT_PHASES
    per = ATT_BLOCK // nph

    @pl.when(blk == 0)
    def _():
        col = lax.broadcasted_iota(jnp.int32, (WIN_KEYS, 2 * WIN_KEYS), 1)
        for g in range(n_groups):
            for hh in range(2):
                vec = jnp.broadcast_to(vec_ref[g, 0, hh:hh + 1, :], (WIN_KEYS, 2 * WIN_KEYS))
                band = pltpu.roll(vec, 0, 1, stride=1, stride_axis=0)
                rows = slice(hh * WIN_KEYS, (hh + 1) * WIN_KEYS)
                tab_ref[g, rows, :] = band
                tab_ref[n_groups + g, rows, :] = jnp.where(col >= WIN_KEYS, band, NEG_INF)

    knat[0:WIN_KEYS, :] = kp_ref[ATT_BLOCK - WIN_KEYS:, :]
    knat[WIN_KEYS:, :] = kc_ref[...]
    vnat[0:WIN_KEYS, :] = vp_ref[ATT_BLOCK - WIN_KEYS:, :]
    vnat[WIN_KEYS:, :] = vc_ref[...]
    for r in range(nph):
        phase = pl.ds(r, per, stride=nph)
        k4[r, 0:per, :] = kp_ref[phase, :]
        k4[r, per:, :] = kc_ref[phase, :]
        v4[r, 0:per, :] = vp_ref[phase, :]
        v4[r, per:, :] = vc_ref[phase, :]
        q4[r] = q_ref[phase, :]
    first = blk == 0
    lane = lax.broadcasted_iota(jnp.int32, (WIN_KEYS, LANES), 1)
    low = lane < A_HEAD_DIM
    ones = jnp.ones((2 * WIN_KEYS, LANES), BF16)

    def block(qs, kk, vv, tab):
        qs = qs.astype(BF16)
        zero = jnp.zeros_like(qs)
        qst = jnp.concatenate([jnp.where(low, qs, zero), jnp.where(low, zero, qs)], axis=0)
        s = _dot_nt(qst, kk.astype(BF16)) + tab
        mx = jnp.max(s, axis=-1, keepdims=True)
        p = jnp.exp(s - mx).astype(BF16)
        r = _dot(p, jnp.concatenate([vv.astype(BF16), ones], axis=1))
        o2 = jnp.where(low, r[0:WIN_KEYS, 0:LANES], r[WIN_KEYS:, 0:LANES])
        l2 = jnp.where(low, r[0:WIN_KEYS, LANES:], r[WIN_KEYS:, LANES:])
        m2 = jnp.where(low, jnp.broadcast_to(mx[0:WIN_KEYS], (WIN_KEYS, LANES)),
                       jnp.broadcast_to(mx[WIN_KEYS:], (WIN_KEYS, LANES)))
        return o2 / l2, m2 + jnp.log(l2)

    def table(g, at_start):
        return tab_ref[jnp.where(jnp.logical_and(at_start, first), n_groups + g, g)]

    def body1(sub, carry):
        i0 = pl.multiple_of(sub * WIN_KEYS, WIN_KEYS)
        o, l = block(q_ref[pl.ds(i0, WIN_KEYS), :], knat[pl.ds(i0, 2 * WIN_KEYS), :],
                     vnat[pl.ds(i0, 2 * WIN_KEYS), :], table(0, sub == 0))
        og1[pl.ds(i0, WIN_KEYS), :] = o
        lg1[pl.ds(i0, WIN_KEYS), :] = l
        return carry

    def body2(pb, carry):
        sub = pb // nph
        r = pb - sub * nph
        i0 = pl.multiple_of(sub * WIN_KEYS, WIN_KEYS)
        keys = pl.ds(i0 + (per - WIN_KEYS), 2 * WIN_KEYS)
        o, l = block(q4[r, pl.ds(i0, WIN_KEYS), :], k4[r, keys, :], v4[r, keys, :],
                     table(1, sub == 0))
        og4[0, r, pl.ds(i0, WIN_KEYS), :] = o
        lg4[0, r, pl.ds(i0, WIN_KEYS), :] = l
        return carry

    def body3(pb, carry):
        a = pb // nph
        r = pb - a * nph
        rows = pl.ds(a, WIN_KEYS, stride=nph)
        keys = pl.ds(a, 2 * WIN_KEYS, stride=nph)
        o, l = block(q4[r, rows, :], k4[r, keys, :], v4[r, keys, :], table(2, True))
        og4[1, r, rows, :] = o
        lg4[1, r, rows, :] = l
        return carry

    n_blocks = ATT_BLOCK // WIN_KEYS
    for body in (body1, body2, body3):
        lax.fori_loop(0, n_blocks, body, 0, unroll=ATT_UNROLL)

    for r in range(nph):
        phase = pl.ds(r, per, stride=nph)
        la, lb, lc = lg1[phase, :], lg4[0, r], lg4[1, r]
        mx = jnp.maximum(jnp.maximum(la, lb), lc)
        wa, wb, wc = jnp.exp(la - mx), jnp.exp(lb - mx), jnp.exp(lc - mx)
        og1[phase, :] = (wa * og1[phase, :] + wb * og4[0, r] + wc * og4[1, r]) / (wa + wb + wc)
    o_ref[...] = og1[...].astype(o_ref.dtype)


def _attn_prompt(q, k, v, vecs, batch, seq):
    assert [d for _, d in DILATED_GROUPS] == [1, ATT_PHASES, ATT_PHASES ** 2]
    nb = seq // ATT_BLOCK
    n_groups = len(DILATED_GROUPS)
    per = ATT_BLOCK // ATT_PHASES
    cur = lambda b, p, t: (b * nb + t, p)
    prev = lambda b, p, t: (b * nb + jnp.maximum(t - 1, 0), p)
    blk = lambda imap: pl.BlockSpec((ATT_BLOCK, LANES), imap)
    vmem = lambda *shape: pltpu.VMEM(shape, F32)
    return pl.pallas_call(
        _attn_prompt_kernel,
        grid=(batch, A_HEADS // 2, nb),
        in_specs=[blk(cur), blk(prev), blk(cur), blk(prev), blk(cur),
                  pl.BlockSpec((n_groups, 1, 2, 2 * WIN_KEYS), lambda b, p, t: (0, p, 0, 0))],
        out_specs=blk(cur),
        out_shape=jax.ShapeDtypeStruct((batch * seq, A_WIDTH), BF16),
        scratch_shapes=[vmem(WIN_KEYS + ATT_BLOCK, LANES), vmem(WIN_KEYS + ATT_BLOCK, LANES),
                        vmem(ATT_PHASES, 2 * per, LANES), vmem(ATT_PHASES, 2 * per, LANES),
                        vmem(ATT_PHASES, per, LANES),
                        vmem(ATT_BLOCK, LANES), vmem(ATT_BLOCK, LANES),
                        vmem(2, ATT_PHASES, per, LANES), vmem(2, ATT_PHASES, per, LANES),
                        vmem(2 * n_groups, 2 * WIN_KEYS, 2 * WIN_KEYS)],
        compiler_params=_params("arbitrary", "arbitrary", "arbitrary"),
        name="attn_prompt",
    )(q, k, k, v, v, vecs)


def _decode_tables(rel_bias, n_new):
    cols = MAX_WINDOW + n_new
    c = np.arange(cols)
    dist = MAX_WINDOW - c
    cnt = np.zeros(c.shape, np.float32)
    for window, dil in DILATED_GROUPS:
        cnt += ((dist >= 0) & (dist <= window) & (dist % dil == 0)).astype(np.float32)
    vec = _select_bias(rel_bias, np.clip(dist, 0, MAX_WINDOW), cnt > 0)
    vec = vec + jnp.asarray(np.log(np.maximum(cnt, 1.0)), F32)[None, :]
    rows = jnp.stack([jnp.pad(vec[:, :cols - i], ((0, 0), (i, 0)), constant_values=NEG_INF)
                      for i in range(n_new)], axis=1)
    rows = rows.reshape(A_HEADS * n_new, cols)
    return rows[:, :MAX_WINDOW], rows[:, MAX_WINDOW:]


def _attn_sample_kernel(q_ref, kn_ref, vn_ref, kc_ref, vc_ref, tc_ref, tn_ref,
                        o_ref, ko_ref, vo_ref):
    n_new = q_ref.shape[1]
    n_buf = kc_ref.shape[2]
    rows = A_HEADS * n_new
    kn = kn_ref[0]
    vn = vn_ref[0]
    kc = kc_ref[0]
    vc = vc_ref[0]

    lane = lax.broadcasted_iota(jnp.int32, (A_WIDTH, LANES), 1)
    for new, old, out_ref in ((kn, kc, ko_ref), (vn, vc, vo_ref)):
        shifted = pltpu.roll(old, n_buf - n_new, 1)
        tail = jnp.concatenate([jnp.zeros((LANES - n_new, A_WIDTH), F32), new], axis=0).T
        out_ref[0, :, 0:n_buf - LANES] = shifted[:, 0:n_buf - LANES]
        out_ref[0, :, n_buf - LANES:] = jnp.where(lane >= LANES - n_new, tail,
                                                  shifted[:, n_buf - LANES:])

    q = q_ref[0]
    row_head = lax.broadcasted_iota(jnp.int32, (A_HEADS, n_new, A_WIDTH), 0).reshape(rows, A_WIDTH)
    col = lax.broadcasted_iota(jnp.int32, (rows, A_WIDTH), 1)
    own = jnp.logical_and(col >= row_head * A_HEAD_DIM, col < (row_head + 1) * A_HEAD_DIM)
    qblk = jnp.where(own, jnp.concatenate([q] * A_HEADS, axis=0), 0.0).astype(BF16)
    s_c = _dot(qblk, kc.astype(BF16)) + tc_ref[...]
    s_n = _dot_nt(qblk, kn.astype(BF16)) + tn_ref[...]
    mx = jnp.maximum(jnp.max(s_c, axis=-1, keepdims=True), jnp.max(s_n, axis=-1, keepdims=True))
    p_c = jnp.exp(s_c - mx)
    p_n = jnp.exp(s_n - mx)
    den = jnp.sum(p_c, axis=-1, keepdims=True) + jnp.sum(p_n, axis=-1, keepdims=True)
    acc = _dot_nt(p_c.astype(BF16), vc.astype(BF16)) + _dot(p_n.astype(BF16), vn.astype(BF16))
    acc = jnp.where(own, acc / den, 0.0)
    out = acc[0:n_new]
    for h in range(1, A_HEADS):
        out = out + acc[h * n_new:(h + 1) * n_new]
    o_ref[0] = out.astype(o_ref.dtype)


def _attn_sample(q, k_new, v_new, cache_k, cache_v, layer, tables):
    b, n_new, _ = q.shape
    n_buf = cache_k.shape[2]
    new = pl.BlockSpec((1, n_new, A_WIDTH), lambda i: (i, 0, 0))
    buf = pl.BlockSpec((1, A_WIDTH, n_buf), lambda i: (i, 0, 0))
    past = pl.BlockSpec((1, A_WIDTH, n_buf), lambda i: (layer * b + i, 0, 0))
    return pl.pallas_call(
        _attn_sample_kernel,
        grid=(b,),
        in_specs=[new, new, new, past, past, _full(tables[0].shape), _full(tables[1].shape)],
        out_specs=[new, buf, buf],
        out_shape=[jax.ShapeDtypeStruct((b, n_new, A_WIDTH), BF16),
                   jax.ShapeDtypeStruct((b, A_WIDTH, n_buf), F32),
                   jax.ShapeDtypeStruct((b, A_WIDTH, n_buf), F32)],
        compiler_params=_params("parallel"),
        name="attn_sample",
    )(q, k_new, v_new, cache_k, cache_v, *tables)


CONV_PAD = 32
CONV_ROWS = 32
CONV_UNROLL = 2


def _conv_kernel(tc, u_ref, up_ref, hist_ref, w_ref, b_ref, g_ref, beta_ref, o_ref, win, stage):
    t = pl.program_id(1)
    n_seq = u_ref.shape[0]
    n_slab = CONV_CH // LANES
    slab = lambda c: slice(c * LANES, (c + 1) * LANES)
    off = CONV_PAD - (CONV_WIDTH - 1)
    rc = min(CONV_ROWS, tc)
    half = rc // 2

    def tap(k, c):
        w = w_ref[k, :, slab(c)]
        if half < SUBLANES:
            return w[0:half]
        return jnp.concatenate([w] * (half // SUBLANES), axis=0)

    for s in range(n_seq):
        first = s * n_slab
        for c in range(n_slab):
            win[first + c, CONV_PAD:CONV_PAD + tc, :] = u_ref[s, :, slab(c)]

        @pl.when(t == 0)
        def _():
            for c in range(n_slab):
                win[first + c, 0:CONV_PAD, :] = hist_ref[s, :, slab(c)]

        @pl.when(t > 0)
        def _():
            for c in range(n_slab):
                win[first + c, 0:CONV_PAD, :] = up_ref[s, :, slab(c)]

        def body(j, carry):
            r0 = j * rc
            for c in range(n_slab):
                for par in range(2):
                    acc = jnp.zeros((half, LANES), F32) + b_ref[:, slab(c)]
                    for k in range(CONV_WIDTH):
                        rows = pl.ds(r0 + off + k + par, half, stride=2)
                        acc = acc + win[first + c, rows, :] * tap(k, c)
                    stage[first + c, pl.ds(r0 + par, half, stride=2), :] = acc
            return carry

        if tc == rc:
            body(0, 0)
        else:
            lax.fori_loop(0, tc // rc, body, 0, unroll=CONV_UNROLL)
        y = jnp.concatenate([stage[first + c] for c in range(n_slab)], axis=1)
        xc = y - jnp.mean(y, axis=-1, keepdims=True)
        y = xc * lax.rsqrt(jnp.mean(xc * xc, axis=-1, keepdims=True) + EPS)
        y = y * g_ref[...] + beta_ref[...]
        o_ref[s] = (y * _sigmoid(y)).astype(o_ref.dtype)


def _conv(u, hist, w, b, g, beta, tc, seqs):
    bsz, t, _ = u.shape
    per = tc // CONV_PAD
    if t >= CONV_PAD:
        prev = pl.BlockSpec((seqs, CONV_PAD, CONV_CH),
                            lambda i, j: (i, jnp.maximum(j * per - 1, 0), 0))
        u_prev = u
    else:
        prev = pl.BlockSpec((seqs, CONV_PAD, CONV_CH), lambda i, j: (i, 0, 0))
        u_prev = hist
    n_slab = seqs * CONV_CH // LANES
    return pl.pallas_call(
        functools.partial(_conv_kernel, tc),
        grid=(bsz // seqs, t // tc),
        in_specs=[pl.BlockSpec((seqs, tc, CONV_CH), lambda i, j: (i, j, 0)),
                  prev,
                  pl.BlockSpec((seqs, CONV_PAD, CONV_CH), lambda i, j: (i, 0, 0)),
                  _full(w.shape), _full((1, CONV_CH)), _full((1, CONV_CH)),
                  _full((1, CONV_CH))],
        out_specs=pl.BlockSpec((seqs, tc, CONV_CH), lambda i, j: (i, j, 0)),
        out_shape=jax.ShapeDtypeStruct((bsz, t, CONV_CH), BF16),
        scratch_shapes=[pltpu.VMEM((n_slab, CONV_PAD + tc, LANES), F32),
                        pltpu.VMEM((n_slab, tc, LANES), F32)],
        compiler_params=_params("parallel", "arbitrary"),
        name="conv",
    )(u, u_prev, hist, w, b, g, beta)


def _gla_in_body(ins, consts, outs, scratch, rows):
    (x_ref,), (g_ref, w_ref, wl_ref, wu_ref, bu_ref) = ins, consts
    q_ref, k_ref, v_ref, r_ref, la_ref = outs
    h = _rms_rows(x_ref[...], g_ref[...]).astype(BF16)
    proj = lambda lo, width: _dot(h, w_ref[:, lo:lo + width].astype(BF16))
    q_ref[...] = proj(0, C_DK) * (C_DK_HEAD ** -0.5)
    k_ref[...] = proj(C_DK, C_DK)
    v_ref[...] = proj(2 * C_DK, C_DV)
    r = proj(2 * C_DK + C_DV, C_DV)
    r_ref[...] = r * _sigmoid(r)
    low = _dot(h, wl_ref[...]).astype(BF16)
    z = _dot(low, wu_ref[...]) + bu_ref[...]
    log_sig = jnp.minimum(z, 0.0) - jnp.log1p(jnp.exp(-jnp.abs(z)))
    la_ref[...] = log_sig * (1.0 / GATE_TAU)


def _gla_in(x_p, x_s, g, w, wl, wu, bu, tm):
    return _rows_call(_gla_in_body, [x_p], [x_s], [g, w, wl, wu, bu],
                      [C_DK, C_DK, C_DV, C_DV, C_DK], tm, "gla_in")


def _gla_kernel(chunk, n_chunks, q_ref, k_ref, v_ref, r_ref, la_ref, s0_ref, gain_ref,
                o_ref, s_ref, qin_s, kin_s, x1_s, x2_s, qst_s, kst_s, dec_s):
    n_seq = q_ref.shape[0]
    @pl.when(pl.program_id(1) == 0)
    def _():
        s_ref[...] = s0_ref[...]

    half, quarter = chunk // 2, chunk // 4
    quarter_of = lambda i: sum((i >= j * quarter).astype(jnp.int32) for j in range(1, 4))
    ri = lax.broadcasted_iota(jnp.int32, (chunk, chunk), 0)
    ci = lax.broadcasted_iota(jnp.int32, (chunk, chunk), 1)
    causal = ci <= ri
    tri = jnp.where(causal, 1.0, 0.0).astype(BF16)
    rq, cq = quarter_of(ri), quarter_of(ci)
    same_quarter = jnp.logical_and(causal, rq == cq)
    cross_half = jnp.logical_and(ri >= half, ci < half)
    cross_quarter = jnp.logical_and(rq == cq + 1, (ri >= half) == (ci >= half))
    rr = lax.broadcasted_iota(jnp.int32, (chunk, C_DK), 0)
    rrq = quarter_of(rr)
    in_first = rr < half
    key_side_2 = jnp.logical_or(rrq == 0, rrq == 2)
    gain = gain_ref[...]

    def prepare(g, c):
        rows = slice(c * chunk, (c + 1) * chunk)
        la = la_ref[g, rows, :]
        la_hi = la.astype(BF16)
        la_lo = (la - la_hi.astype(F32)).astype(BF16)
        cum = _dot(tri, la_hi) + _dot(tri, la_lo)
        row = lambda i: cum[i:i + 1, :]
        last = row(chunk - 1)
        mids = [row(j * quarter + quarter // 2 - 1) for j in range(4)]
        mid = jnp.where(rrq == 0, mids[0], jnp.where(rrq == 1, mids[1],
                        jnp.where(rrq == 2, mids[2], mids[3])))
        q = q_ref[g, rows, :]
        k = k_ref[g, rows, :]
        grow = jnp.exp(cum - mid)
        qin_s[g, rows, :] = (q * grow).astype(BF16)
        kin_s[g, rows, :] = (k / grow).astype(BF16)

        def across(edge, key_side):
            gap = cum - edge
            return (jnp.where(key_side, k, q) * jnp.exp(jnp.where(key_side, -gap, gap))).astype(BF16)

        x1_s[g, rows, :] = across(row(half - 1), in_first)
        x2_s[g, rows, :] = across(jnp.where(in_first, row(quarter - 1), row(half + quarter - 1)),
                               key_side_2)
        qst_s[g, rows, :] = (q * jnp.exp(cum)).astype(BF16)
        kst_s[g, rows, :] = (k * jnp.exp(last - cum)).astype(BF16)
        dec_s[g, c] = jnp.broadcast_to(jnp.exp(last), (LANES, C_DK)).T

    def advance(g, c):
        rows = slice(c * chunk, (c + 1) * chunk)
        for h in range(C_HEADS):
            ks = slice(h * C_DK_HEAD, (h + 1) * C_DK_HEAD)
            vs = slice(h * C_DV_HEAD, (h + 1) * C_DV_HEAD)
            vh = v_ref[g, rows, vs].astype(BF16)
            x1, x2 = x1_s[g, rows, ks], x2_s[g, rows, ks]
            att = jnp.where(same_quarter, _dot_nt(qin_s[g, rows, ks], kin_s[g, rows, ks]),
                            jnp.where(cross_quarter, _dot_nt(x2, x2),
                                      jnp.where(cross_half, _dot_nt(x1, x1), 0.0)))
            s = s_ref[g, h]
            o = _dot(jnp.concatenate([att.astype(BF16), qst_s[g, rows, ks]], axis=1),
                     jnp.concatenate([vh, s.astype(BF16)], axis=0))
            decay = dec_s[g, c, ks, :]
            s_ref[g, h] = (s * jnp.concatenate([decay] * (C_DV_HEAD // LANES), axis=1)
                           + _dot_tn(kst_s[g, rows, ks], vh))
            y = o * lax.rsqrt(jnp.mean(o * o, axis=-1, keepdims=True) + EPS) * gain
            o_ref[g, rows, vs] = (y * r_ref[g, rows, vs]).astype(o_ref.dtype)

    for g in range(n_seq):
        prepare(g, 0)
    for c in range(n_chunks):
        for g in range(n_seq):
            if c + 1 < n_chunks:
                prepare(g, c + 1)
            advance(g, c)


def _gla(q, k, v, r, la, s0, gain, chunk, tb, seqs):
    b, t, _ = q.shape
    seq = lambda width: pl.BlockSpec((seqs, tb, width), lambda i, j: (i, j, 0))
    state = pl.BlockSpec((seqs, C_HEADS, C_DK_HEAD, C_DV_HEAD), lambda i, j: (i, 0, 0, 0))
    return pl.pallas_call(
        functools.partial(_gla_kernel, chunk, tb // chunk),
        grid=(b // seqs, t // tb),
        in_specs=[seq(C_DK), seq(C_DK), seq(C_DV), seq(C_DV), seq(C_DK), state,
                  _full((1, C_DV_HEAD))],
        out_specs=[seq(C_DV), state],
        out_shape=[jax.ShapeDtypeStruct((b, t, C_DV), BF16),
                   jax.ShapeDtypeStruct(s0.shape, F32)],
        scratch_shapes=[pltpu.VMEM((seqs, tb, C_DK), BF16)] * 6
                       + [pltpu.VMEM((seqs, tb // chunk, C_DK, LANES), F32)],
        compiler_params=_params("parallel", "arbitrary"),
        name="gla",
    )(q, k, v, r, la, s0, gain)


GLA_CHUNK = 128
SAMPLE_PAD = 16
SAMPLE_SEQS = 4


ROW_TILE = 512
FFN_TILE = ROW_TILE
CONV_TILE = 512
GLA_TILE = 512


def _even_mixer_prompt(q, k, v, u, P, i, bsz, t):
    n = bsz * t
    a = _attn_prompt(q, k, v, P['band_vectors'], bsz, t)
    keep = min(MAX_WINDOW, t)
    tail = lambda z: z.reshape(bsz, t, A_WIDTH)[:, t - keep:].reshape(bsz, keep, A_HEADS, A_HEAD_DIM)
    u3 = u.reshape(bsz, t, CONV_CH)
    hist = jnp.zeros((bsz, CONV_PAD, CONV_CH), F32)
    c = _conv(u3, hist, P['ev_conv_w'][i], P['ev_conv_b'][i], P['ev_conv_ln_g'][i],
              P['ev_conv_ln_b'][i], CONV_TILE, 1).reshape(n, CONV_CH)
    return [a, c], (tail(k), tail(v), u3[:, t - (CONV_WIDTH - 1):])


def _even_mixer_sample(q, k, v, u, past, P, i, bsz, t):
    n = bsz * t
    n_buf = past[0].shape[2]
    major = lambda z: z.transpose(0, 1, 3, 4, 2).reshape(z.shape[0] * bsz, A_WIDTH, n_buf)
    minor = lambda z: z.reshape(bsz, A_HEADS, A_HEAD_DIM, n_buf).transpose(0, 3, 1, 2)
    a, new_k, new_v = _attn_sample(
        q.reshape(bsz, t, A_WIDTH), k.reshape(bsz, t, A_WIDTH), v.reshape(bsz, t, A_WIDTH),
        major(past[0]), major(past[1]), i, P['decode_tables'])
    u3 = u.reshape(bsz, t, CONV_CH)
    hist = jnp.pad(past[2][i], ((0, 0), (CONV_PAD - (CONV_WIDTH - 1), 0), (0, 0)))
    c = _conv(u3, hist, P['ev_conv_w'][i], P['ev_conv_b'][i], P['ev_conv_ln_g'][i],
              P['ev_conv_ln_b'][i], t, SAMPLE_SEQS).reshape(n, CONV_CH)
    new_u = jnp.concatenate([past[2][i], u3], axis=1)[:, -(CONV_WIDTH - 1):]
    return [a.reshape(n, A_WIDTH), c], (minor(new_k), minor(new_v), new_u)


def _gla_mixer(q, k, v, r, la, s0, gain, bsz, t, t_pad, chunk, tile, seqs):
    seq = lambda z: jnp.pad(z.reshape(bsz, t, -1), ((0, 0), (0, t_pad - t), (0, 0)))
    o, s = _gla(seq(q), seq(k), seq(v), seq(r), seq(la), s0, gain, chunk, tile, seqs)
    return [o[:, :t].reshape(bsz * t, C_DV)], s


def _trunks(x_p, x_s, past, P):
    (b_p, t_p, _), (b_s, t_s, _) = x_p.shape, x_s.shape
    x_p = x_p.reshape(b_p * t_p, D_MODEL)
    x_s = x_s.reshape(b_s * t_s, D_MODEL)
    new_p = {'k': [], 'v': [], 'u': [], 's': []}
    new_s = {'k': [], 'v': [], 'u': [], 's': []}
    for layer in range(DEPTH):
        i = layer // 2
        x_p, x_s = _ffn(x_p, x_s, P['norm_ffn1'][layer], (P['ffn1_w_in'], layer),
                        (P['ffn1_w_out'], layer), FFN_TILE)
        if layer % 2 == 0:
            proj_p, proj_s = _even_in(x_p, x_s, P['norm_mix'][layer], (P['ev_w_in'], i),
                                      P['ev_q_gain'][i], P['ev_k_gain'][i], P['head_ones'], ROW_TILE)
            mix_p, kvu_p = _even_mixer_prompt(*proj_p, P, i, b_p, t_p)
            mix_s, kvu_s = _even_mixer_sample(*proj_s, past, P, i, b_s, t_s)
            w_mix = [(P['ev_w_out'], i, 0, 2), (P['ev_w_out'], i, 1, 2)]
            for new, kvu in ((new_p, kvu_p), (new_s, kvu_s)):
                for name, z in zip('kvu', kvu):
                    new[name].append(z)
        else:
            proj_p, proj_s = _gla_in(x_p, x_s, P['norm_mix'][layer], (P['od_w_in'], i),
                                     P['od_w_low'][i], P['od_gate_w_up'][i], P['od_gate_b'][i],
                                     ROW_TILE)
            zeros = jnp.zeros((b_p, C_HEADS, C_DK_HEAD, C_DV_HEAD), F32)
            gain = P['od_o_gain'][i]
            mix_p, s_p = _gla_mixer(*proj_p, zeros, gain, b_p, t_p, t_p, GLA_CHUNK, GLA_TILE, 1)
            mix_s, s_s = _gla_mixer(*proj_s, past[3][i], gain, b_s, t_s, SAMPLE_PAD, SAMPLE_PAD,
                                    SAMPLE_PAD, SAMPLE_SEQS)
            w_mix = [(P['od_w_out'], i)]
            new_p['s'].append(s_p)
            new_s['s'].append(s_s)
        x_p, x_s = _ffn(x_p, x_s, P['norm_ffn2'][layer], (P['ffn2_w_in'], layer),
                        (P['ffn2_w_out'], layer), FFN_TILE, mix_p, mix_s, w_mix)
    stacked = lambda new: tuple(jnp.stack(new[name]) for name in 'kvus')
    return (x_p.reshape(b_p, t_p, D_MODEL), x_s.reshape(b_s, t_s, D_MODEL),
            *stacked(new_p), *stacked(new_s))


def kernel(x_prompt, x_sample, cache_k, cache_v, cache_conv, state_gla, rel_bias, norm_ffn1, ffn1_w_in, ffn1_w_out, norm_mix, norm_ffn2, ffn2_w_in, ffn2_w_out, ev_w_in, ev_q_gain, ev_k_gain, ev_conv_w, ev_conv_b, ev_conv_ln_g, ev_conv_ln_b, ev_w_out, od_w_in, od_gate_w_up, od_gate_b, od_o_gain, od_w_out):
    n_even = ev_w_in.shape[0]
    n_odd = od_w_in.shape[0]
    main = 2 * C_DK + 2 * C_DV
    head_ids = np.arange(MXU_DIM) // A_HEAD_DIM
    per = lambda n, f: [f(j) for j in range(n)]
    row = lambda z: z[None, :]
    P = {
        'norm_ffn1': per(DEPTH, lambda j: row(norm_ffn1[j])),
        'norm_mix': per(DEPTH, lambda j: row(norm_mix[j])),
        'norm_ffn2': per(DEPTH, lambda j: row(norm_ffn2[j])),
        'ffn1_w_in': ffn1_w_in, 'ffn1_w_out': ffn1_w_out,
        'ffn2_w_in': ffn2_w_in, 'ffn2_w_out': ffn2_w_out,
        'ev_w_in': ev_w_in, 'ev_w_out': ev_w_out, 'od_w_in': od_w_in, 'od_w_out': od_w_out,
        'ev_q_gain': per(n_even, lambda j: row(jnp.tile(ev_q_gain[j], A_HEADS))),
        'ev_k_gain': per(n_even, lambda j: row(jnp.tile(ev_k_gain[j], A_HEADS))),
        'head_ones': jnp.asarray(head_ids[:, None] == head_ids[None, :], BF16),
        'ev_conv_w': per(n_even, lambda j: jnp.broadcast_to(
            ev_conv_w[j][:, None, :], (CONV_WIDTH, SUBLANES, CONV_CH))),
        'ev_conv_b': per(n_even, lambda j: row(ev_conv_b[j])),
        'ev_conv_ln_g': per(n_even, lambda j: row(ev_conv_ln_g[j])),
        'ev_conv_ln_b': per(n_even, lambda j: row(ev_conv_ln_b[j])),
        'od_w_low': per(n_odd, lambda j: jnp.pad(od_w_in[j, :, main:],
                                                 ((0, 0), (0, LANES - GATE_RANK))).astype(BF16)),
        'od_gate_w_up': per(n_odd, lambda j: jnp.pad(od_gate_w_up[j],
                                                     ((0, LANES - GATE_RANK), (0, 0))).astype(BF16)),
        'od_gate_b': per(n_odd, lambda j: row(od_gate_b[j])),
        'od_o_gain': per(n_odd, lambda j: row(od_o_gain[j])),
        'band_vectors': _band_vectors(rel_bias),
        'decode_tables': _decode_tables(rel_bias, x_sample.shape[1]),
    }
    P, x_prompt, x_sample = lax.optimization_barrier((P, x_prompt, x_sample))
    return _trunks(x_prompt, x_sample, (cache_k, cache_v, cache_conv, state_gla), P)
```

```python
import functools

import numpy as np
import jax
import jax.numpy as jnp
from jax import lax
from jax.experimental import pallas as pl
from jax.experimental.pallas import tpu as pltpu

F32 = jnp.float32
BF16 = jnp.bfloat16

D_MODEL = 1024
DEPTH = 2
PAST_LEN = 16384
A_HEADS = 8
A_HEAD_DIM = 64
A_WIDTH = A_HEADS * A_HEAD_DIM
DILATED_GROUPS = ((128, 1), (512, 4), (2048, 16))
MAX_WINDOW = 2048
N_BUCKETS = 32
CONV_WIDTH = 31
CONV_CH = 512
C_HEADS = 4
C_DK = 512
C_DV = 1024
C_DK_HEAD = 128
C_DV_HEAD = 256
GATE_RANK = 16
GATE_TAU = 16.0
D_FF = 2816
EPS = 1e-6
NEG_INF = -1e30
LOG2E = 1.4426950408889634

LANES = 128
SUBLANES = 8
MXU_DIM = 256
WIN_KEYS = 128
ATT_BLOCK = 2048
ATT_UNROLL = 16
ATT_PHASES = 4
VMEM_LIMIT = 56 * 1024 * 1024


def _params(*sem):
    return pltpu.CompilerParams(dimension_semantics=sem, vmem_limit_bytes=VMEM_LIMIT)


def _dot(a, b):
    return jnp.dot(a, b, preferred_element_type=F32)


def _dot_nt(a, b):
    return lax.dot_general(a, b, (((1,), (1,)), ((), ())), preferred_element_type=F32)


def _dot_tn(a, b):
    return lax.dot_general(a, b, (((0,), (0,)), ((), ())), preferred_element_type=F32)


def _rms_rows(x, g):
    y = x * lax.rsqrt(jnp.mean(x * x, axis=-1, keepdims=True) + EPS)
    return y * g


def _sigmoid(x):
    return 1.0 / (1.0 + jnp.exp(-x))


def _full(shape):
    return pl.BlockSpec(shape, lambda *_: (0,) * len(shape), pipeline_mode=pl.Buffered(1))


FF_CHUNK = 256
FF_DOWN_CHUNK = 1024


def _layer(w, layer, row_block=0, n_blocks=1):
    rows = w.shape[1] // n_blocks
    return pl.BlockSpec((None, rows, w.shape[2]), lambda *_: (layer, row_block, 0),
                        pipeline_mode=pl.Buffered(1))


def _rows_call(body, prompt_rows, sample_rows, consts, out_widths, tm, name, scratch=()):
    n_p, n_s = prompt_rows[0].shape[0], sample_rows[0].shape[0]
    steps = n_p // tm
    n_in, n_c, n_out = len(prompt_rows), len(consts), len(out_widths)

    def kernel(*refs):
        p_in, s_in = refs[:n_in], refs[n_in:2 * n_in]
        c_refs = refs[2 * n_in:2 * n_in + n_c]
        outs = refs[2 * n_in + n_c:]
        p_out, s_out, scr = outs[:n_out], outs[n_out:2 * n_out], outs[2 * n_out:]
        step = pl.program_id(0)

        @pl.when(step < steps)
        def _():
            body(p_in, c_refs, p_out, scr, tm)

        @pl.when(step == steps)
        def _():
            body(s_in, c_refs, s_out, scr, n_s)

    p_spec = lambda width: pl.BlockSpec((tm, width), lambda i: (jnp.minimum(i, steps - 1), 0))
    s_spec = lambda width: pl.BlockSpec((n_s, width), lambda i: (0, 0))
    res = pl.pallas_call(
        kernel,
        grid=(steps + 1,),
        in_specs=[p_spec(a.shape[1]) for a in prompt_rows] + [s_spec(a.shape[1]) for a in sample_rows]
                 + [_layer(*c) if isinstance(c, tuple) else _full(c.shape) for c in consts],
        out_specs=[p_spec(w) for w in out_widths] + [s_spec(w) for w in out_widths],
        out_shape=[jax.ShapeDtypeStruct((n_p, w), F32) for w in out_widths]
                  + [jax.ShapeDtypeStruct((n_s, w), F32) for w in out_widths],
        scratch_shapes=list(scratch),
        compiler_params=_params("arbitrary"),
        name=name,
    )(*prompt_rows, *sample_rows, *[c[0] if isinstance(c, tuple) else c for c in consts])
    return res[:n_out], res[n_out:]


def _ffn_body(ins, consts, outs, scratch, rows):
    x_ref, m_refs = ins[0], ins[1:]
    w_refs, (g_ref, wi_ref, wo_ref) = consts[:len(m_refs)], consts[len(m_refs):]
    (o_ref,), (act_ref,) = outs, scratch
    x = x_ref[...]
    for m_ref, w_ref in zip(m_refs, w_refs):
        x = x + _dot(m_ref[...], w_ref[...].astype(BF16))
    h = _rms_rows(x, g_ref[...]).astype(BF16)
    for c in range(D_FF // FF_CHUNK):
        lo = c * FF_CHUNK
        a = _dot(h, wi_ref[:, lo:lo + FF_CHUNK].astype(BF16))
        b = _dot(h, wi_ref[:, D_FF + lo:D_FF + lo + FF_CHUNK].astype(BF16))
        act_ref[0:rows, lo:lo + FF_CHUNK] = (a * _sigmoid(a) * b).astype(BF16)
    y = jnp.zeros_like(x)
    for lo in range(0, D_FF, FF_DOWN_CHUNK):
        hi = min(lo + FF_DOWN_CHUNK, D_FF)
        y = y + _dot(act_ref[0:rows, lo:hi], wo_ref[lo:hi, :].astype(BF16))
    o_ref[...] = x + 0.5 * y


def _ffn(x_p, x_s, g, wi, wo, tm, ms_p=(), ms_s=(), ws=()):
    (y_p,), (y_s,) = _rows_call(_ffn_body, [x_p, *ms_p], [x_s, *ms_s], [*ws, g, wi, wo],
                                [D_MODEL], tm, "ffn", [pltpu.VMEM((tm, D_FF), BF16)])
    return y_p, y_s


def _head_norm(z, gain, bd):
    zz = z * z
    hi = zz.astype(BF16)
    lo = (zz - hi.astype(F32)).astype(BF16)
    width = bd.shape[0]
    ss = jnp.concatenate(
        [_dot(hi[:, c:c + width], bd) + _dot(lo[:, c:c + width], bd)
         for c in range(0, z.shape[1], width)], axis=1)
    return z * lax.rsqrt(ss * (1.0 / A_HEAD_DIM) + EPS) * gain


def _even_in_body(ins, consts, outs, scratch, rows):
    (x_ref,), (g_ref, w_ref, qg_ref, kg_ref, bd_ref), (q_ref, k_ref, v_ref, u_ref) = ins, consts, outs
    h = _rms_rows(x_ref[...], g_ref[...]).astype(BF16)
    proj = lambda lo, width: _dot(h, w_ref[:, lo:lo + width].astype(BF16))
    bd = bd_ref[...]
    q_ref[...] = _head_norm(proj(0, A_WIDTH), qg_ref[...], bd) * (A_HEAD_DIM ** -0.5 * LOG2E)
    k_ref[...] = _head_norm(proj(A_WIDTH, A_WIDTH), kg_ref[...], bd)
    v_ref[...] = proj(2 * A_WIDTH, A_WIDTH)
    gv = proj(3 * A_WIDTH, CONV_CH)
    gg = proj(3 * A_WIDTH + CONV_CH, CONV_CH)
    u_ref[...] = gv * _sigmoid(gg)


def _even_in(x_p, x_s, g, w, qg, kg, bd, tm):
    return _rows_call(_even_in_body, [x_p], [x_s], [g, w, qg, kg, bd], [A_WIDTH] * 4, tm, "even_in")


def _t5_bucket(dist):
    max_exact = N_BUCKETS // 2
    d = np.asarray(dist, dtype=np.int32)
    df = np.maximum(d, 1).astype(np.float32)
    large = max_exact + (np.log(df / max_exact) / np.log(MAX_WINDOW / max_exact)
                         * (N_BUCKETS - max_exact)).astype(np.int32)
    large = np.minimum(large, N_BUCKETS - 1)
    return np.where(d < max_exact, d, large).astype(np.int32)


def _select_bias(rel_bias, dist, valid):
    onehot = (_t5_bucket(dist)[None, :] == np.arange(N_BUCKETS)[:, None]) & valid[None, :]
    picked = jnp.einsum('bh,bc->hc', rel_bias, jnp.asarray(onehot, F32),
                        precision=lax.Precision.HIGHEST)
    return picked * LOG2E + jnp.asarray(np.where(valid, 0.0, NEG_INF), F32)[None, :]


def _band_vectors(rel_bias):
    c = np.arange(2 * WIN_KEYS)
    valid = c <= WIN_KEYS
    vecs = [_select_bias(rel_bias, np.where(valid, (WIN_KEYS - c) * dil, 0), valid)
            for _, dil in DILATED_GROUPS]
    return jnp.stack(vecs).reshape(len(DILATED_GROUPS), A_HEADS // 2, 2, 2 * WIN_KEYS)


def _attn_prompt_kernel(q_ref, kp_ref, kc_ref, vp_ref, vc_ref, vec_ref, o_ref,
                        knat, vnat, k4, v4, q4, og1, lg1, og4, lg4, tab_ref):
    blk = pl.program_id(2)
    n_groups = len(DILATED_GROUPS)
    nph = ATT_PHASES
    per = ATT_BLOCK // nph

    @pl.when(blk == 0)
    def _():
        col = lax.broadcasted_iota(jnp.int32, (WIN_KEYS, 2 * WIN_KEYS), 1)
        for g in range(n_groups):
            for hh in range(2):
                vec = jnp.broadcast_to(vec_ref[g, 0, hh:hh + 1, :], (WIN_KEYS, 2 * WIN_KEYS))
                band = pltpu.roll(vec, 0, 1, stride=1, stride_axis=0)
                rows = slice(hh * WIN_KEYS, (hh + 1) * WIN_KEYS)
                tab_ref[g, rows, :] = band
                tab_ref[n_groups + g, rows, :] = jnp.where(col >= WIN_KEYS, band, NEG_INF)

    knat[0:WIN_KEYS, :] = kp_ref[ATT_BLOCK - WIN_KEYS:, :]
    knat[WIN_KEYS:, :] = kc_ref[...]
    vnat[0:WIN_KEYS, :] = vp_ref[ATT_BLOCK - WIN_KEYS:, :]
    vnat[WIN_KEYS:, :] = vc_ref[...]
    for r in range(nph):
        phase = pl.ds(r, per, stride=nph)
        k4[r, 0:per, :] = kp_ref[phase, :]
        k4[r, per:, :] = kc_ref[phase, :]
        v4[r, 0:per, :] = vp_ref[phase, :]
        v4[r, per:, :] = vc_ref[phase, :]
        q4[r] = q_ref[phase, :]
    first = blk == 0
    lane = lax.broadcasted_iota(jnp.int32, (WIN_KEYS, LANES), 1)
    low = lane < A_HEAD_DIM
    ones = jnp.ones((2 * WIN_KEYS, LANES), BF16)

    def block(qs, kk, vv, tab):
        qs = qs.astype(BF16)
        zero = jnp.zeros_like(qs)
        qst = jnp.concatenate([jnp.where(low, qs, zero), jnp.where(low, zero, qs)], axis=0)
        s = _dot_nt(qst, kk.astype(BF16)) + tab
        mx = jnp.max(s, axis=-1, keepdims=True)
        p = jnp.exp2(s - mx).astype(BF16)
        r = _dot(p, jnp.concatenate([vv.astype(BF16), ones], axis=1))
        o2 = jnp.where(low, r[0:WIN_KEYS, 0:LANES], r[WIN_KEYS:, 0:LANES])
        l2 = jnp.where(low, r[0:WIN_KEYS, LANES:], r[WIN_KEYS:, LANES:])
        m2 = jnp.where(low, jnp.broadcast_to(mx[0:WIN_KEYS], (WIN_KEYS, LANES)),
                       jnp.broadcast_to(mx[WIN_KEYS:], (WIN_KEYS, LANES)))
        return o2 / l2, m2 + jnp.log(l2) * LOG2E

    def table(g, at_start):
        return tab_ref[jnp.where(jnp.logical_and(at_start, first), n_groups + g, g)]

    def body1(sub, carry):
        i0 = pl.multiple_of(sub * WIN_KEYS, WIN_KEYS)
        o, l = block(q_ref[pl.ds(i0, WIN_KEYS), :], knat[pl.ds(i0, 2 * WIN_KEYS), :],
                     vnat[pl.ds(i0, 2 * WIN_KEYS), :], table(0, sub == 0))
        og1[pl.ds(i0, WIN_KEYS), :] = o
        lg1[pl.ds(i0, WIN_KEYS), :] = l
        return carry

    def body2(pb, carry):
        sub = pb // nph
        r = pb - sub * nph
        i0 = pl.multiple_of(sub * WIN_KEYS, WIN_KEYS)
        keys = pl.ds(i0 + (per - WIN_KEYS), 2 * WIN_KEYS)
        o, l = block(q4[r, pl.ds(i0, WIN_KEYS), :], k4[r, keys, :], v4[r, keys, :],
                     table(1, sub == 0))
        og4[0, r, pl.ds(i0, WIN_KEYS), :] = o
        lg4[0, r, pl.ds(i0, WIN_KEYS), :] = l
        return carry

    def body3(pb, carry):
        a = pb // nph
        r = pb - a * nph
        rows = pl.ds(a, WIN_KEYS, stride=nph)
        keys = pl.ds(a, 2 * WIN_KEYS, stride=nph)
        o, l = block(q4[r, rows, :], k4[r, keys, :], v4[r, keys, :], table(2, True))
        og4[1, r, rows, :] = o
        lg4[1, r, rows, :] = l
        return carry

    n_blocks = ATT_BLOCK // WIN_KEYS
    for body in (body1, body2, body3):
        lax.fori_loop(0, n_blocks, body, 0, unroll=ATT_UNROLL)

    for r in range(nph):
        phase = pl.ds(r, per, stride=nph)
        la, lb, lc = lg1[phase, :], lg4[0, r], lg4[1, r]
        mx = jnp.maximum(jnp.maximum(la, lb), lc)
        wa, wb, wc = jnp.exp2(la - mx), jnp.exp2(lb - mx), jnp.exp2(lc - mx)
        og1[phase, :] = (wa * og1[phase, :] + wb * og4[0, r] + wc * og4[1, r]) / (wa + wb + wc)
    o_ref[...] = og1[...].astype(o_ref.dtype)


def _attn_prompt(q, k, v, vecs, batch, seq):
    assert [d for _, d in DILATED_GROUPS] == [1, ATT_PHASES, ATT_PHASES ** 2]
    nb = seq // ATT_BLOCK
    n_groups = len(DILATED_GROUPS)
    per = ATT_BLOCK // ATT_PHASES
    cur = lambda b, p, t: (b * nb + t, p)
    prev = lambda b, p, t: (b * nb + jnp.maximum(t - 1, 0), p)
    blk = lambda imap: pl.BlockSpec((ATT_BLOCK, LANES), imap)
    vmem = lambda *shape: pltpu.VMEM(shape, F32)
    return pl.pallas_call(
        _attn_prompt_kernel,
        grid=(batch, A_HEADS // 2, nb),
        in_specs=[blk(cur), blk(prev), blk(cur), blk(prev), blk(cur),
                  pl.BlockSpec((n_groups, 1, 2, 2 * WIN_KEYS), lambda b, p, t: (0, p, 0, 0))],
        out_specs=blk(cur),
        out_shape=jax.ShapeDtypeStruct((batch * seq, A_WIDTH), BF16),
        scratch_shapes=[vmem(WIN_KEYS + ATT_BLOCK, LANES), vmem(WIN_KEYS + ATT_BLOCK, LANES),
                        vmem(ATT_PHASES, 2 * per, LANES), vmem(ATT_PHASES, 2 * per, LANES),
                        vmem(ATT_PHASES, per, LANES),
                        vmem(ATT_BLOCK, LANES), vmem(ATT_BLOCK, LANES),
                        vmem(2, ATT_PHASES, per, LANES), vmem(2, ATT_PHASES, per, LANES),
                        vmem(2 * n_groups, 2 * WIN_KEYS, 2 * WIN_KEYS)],
        compiler_params=_params("arbitrary", "arbitrary", "arbitrary"),
        name="attn_prompt",
    )(q, k, k, v, v, vecs)


def _decode_tables(rel_bias, n_new):
    cols = MAX_WINDOW + n_new
    c = np.arange(cols)
    dist = MAX_WINDOW - c
    cnt = np.zeros(c.shape, np.float32)
    for window, dil in DILATED_GROUPS:
        cnt += ((dist >= 0) & (dist <= window) & (dist % dil == 0)).astype(np.float32)
    vec = _select_bias(rel_bias, np.clip(dist, 0, MAX_WINDOW), cnt > 0)
    vec = vec + jnp.asarray(np.log2(np.maximum(cnt, 1.0)), F32)[None, :]
    rows = jnp.stack([jnp.pad(vec[:, :cols - i], ((0, 0), (i, 0)), constant_values=NEG_INF)
                      for i in range(n_new)], axis=1)
    rows = rows.reshape(A_HEADS * n_new, cols)
    return rows[:, :MAX_WINDOW], rows[:, MAX_WINDOW:]


def _attn_sample_kernel(q_ref, kn_ref, vn_ref, kc_ref, vc_ref, tc_ref, tn_ref,
                        o_ref, ko_ref, vo_ref):
    n_new = q_ref.shape[1]
    n_buf = kc_ref.shape[2]
    rows = A_HEADS * n_new
    kn = kn_ref[0]
    vn = vn_ref[0]
    kc = kc_ref[0]
    vc = vc_ref[0]

    lane = lax.broadcasted_iota(jnp.int32, (A_WIDTH, LANES), 1)
    for new, old, out_ref in ((kn, kc, ko_ref), (vn, vc, vo_ref)):
        shifted = pltpu.roll(old, n_buf - n_new, 1)
        tail = jnp.concatenate([jnp.zeros((LANES - n_new, A_WIDTH), F32), new], axis=0).T
        out_ref[0, :, 0:n_buf - LANES] = shifted[:, 0:n_buf - LANES]
        out_ref[0, :, n_buf - LANES:] = jnp.where(lane >= LANES - n_new, tail,
                                                  shifted[:, n_buf - LANES:])

    q = q_ref[0]
    row_head = lax.broadcasted_iota(jnp.int32, (A_HEADS, n_new, A_WIDTH), 0).reshape(rows, A_WIDTH)
    col = lax.broadcasted_iota(jnp.int32, (rows, A_WIDTH), 1)
    own = jnp.logical_and(col >= row_head * A_HEAD_DIM, col < (row_head + 1) * A_HEAD_DIM)
    qblk = jnp.where(own, jnp.concatenate([q] * A_HEADS, axis=0), 0.0).astype(BF16)
    s_c = _dot(qblk, kc.astype(BF16)) + tc_ref[...]
    s_n = _dot_nt(qblk, kn.astype(BF16)) + tn_ref[...]
    mx = jnp.maximum(jnp.max(s_c, axis=-1, keepdims=True), jnp.max(s_n, axis=-1, keepdims=True))
    p_c = jnp.exp2(s_c - mx)
    p_n = jnp.exp2(s_n - mx)
    den = jnp.sum(p_c, axis=-1, keepdims=True) + jnp.sum(p_n, axis=-1, keepdims=True)
    acc = _dot_nt(p_c.astype(BF16), vc.astype(BF16)) + _dot(p_n.astype(BF16), vn.astype(BF16))
    acc = jnp.where(own, acc / den, 0.0)
    out = acc[0:n_new]
    for h in range(1, A_HEADS):
        out = out + acc[h * n_new:(h + 1) * n_new]
    o_ref[0] = out.astype(o_ref.dtype)


def _attn_sample(q, k_new, v_new, cache_k, cache_v, layer, tables):
    b, n_new, _ = q.shape
    n_buf = cache_k.shape[2]
    new = pl.BlockSpec((1, n_new, A_WIDTH), lambda i: (i, 0, 0))
    buf = pl.BlockSpec((1, A_WIDTH, n_buf), lambda i: (i, 0, 0))
    past = pl.BlockSpec((1, A_WIDTH, n_buf), lambda i: (layer * b + i, 0, 0))
    return pl.pallas_call(
        _attn_sample_kernel,
        grid=(b,),
        in_specs=[new, new, new, past, past, _full(tables[0].shape), _full(tables[1].shape)],
        out_specs=[new, buf, buf],
        out_shape=[jax.ShapeDtypeStruct((b, n_new, A_WIDTH), BF16),
                   jax.ShapeDtypeStruct((b, A_WIDTH, n_buf), F32),
                   jax.ShapeDtypeStruct((b, A_WIDTH, n_buf), F32)],
        compiler_params=_params("parallel"),
        name="attn_sample",
    )(q, k_new, v_new, cache_k, cache_v, *tables)


CONV_PAD = 32
CONV_ROWS = 32
CONV_UNROLL = 2


def _conv_kernel(tc, u_ref, up_ref, hist_ref, w_ref, b_ref, g_ref, beta_ref, o_ref, win, stage):
    t = pl.program_id(1)
    n_seq = u_ref.shape[0]
    n_slab = CONV_CH // LANES
    slab = lambda c: slice(c * LANES, (c + 1) * LANES)
    off = CONV_PAD - (CONV_WIDTH - 1)
    rc = min(CONV_ROWS, tc)
    half = rc // 2

    def tap(k, c):
        w = w_ref[k, :, slab(c)]
        if half < SUBLANES:
            return w[0:half]
        return jnp.concatenate([w] * (half // SUBLANES), axis=0)

    for s in range(n_seq):
        first = s * n_slab
        for c in range(n_slab):
            win[first + c, CONV_PAD:CONV_PAD + tc, :] = u_ref[s, :, slab(c)]

        @pl.when(t == 0)
        def _():
            for c in range(n_slab):
                win[first + c, 0:CONV_PAD, :] = hist_ref[s, :, slab(c)]

        @pl.when(t > 0)
        def _():
            for c in range(n_slab):
                win[first + c, 0:CONV_PAD, :] = up_ref[s, :, slab(c)]

        def body(j, carry):
            r0 = j * rc
            for c in range(n_slab):
                for par in range(2):
                    acc = jnp.zeros((half, LANES), F32) + b_ref[:, slab(c)]
                    for k in range(CONV_WIDTH):
                        rows = pl.ds(r0 + off + k + par, half, stride=2)
                        acc = acc + win[first + c, rows, :] * tap(k, c)
                    stage[first + c, pl.ds(r0 + par, half, stride=2), :] = acc
            return carry

        if tc == rc:
            body(0, 0)
        else:
            lax.fori_loop(0, tc // rc, body, 0, unroll=CONV_UNROLL)
        y = jnp.concatenate([stage[first + c] for c in range(n_slab)], axis=1)
        xc = y - jnp.mean(y, axis=-1, keepdims=True)
        y = xc * lax.rsqrt(jnp.mean(xc * xc, axis=-1, keepdims=True) + EPS)
        y = y * g_ref[...] + beta_ref[...]
        o_ref[s] = (y * _sigmoid(y)).astype(o_ref.dtype)


def _conv(u, hist, w, b, g, beta, tc, seqs):
    bsz, t, _ = u.shape
    per = tc // CONV_PAD
    if t >= CONV_PAD:
        prev = pl.BlockSpec((seqs, CONV_PAD, CONV_CH),
                            lambda i, j: (i, jnp.maximum(j * per - 1, 0), 0))
        u_prev = u
    else:
        prev = pl.BlockSpec((seqs, CONV_PAD, CONV_CH), lambda i, j: (i, 0, 0))
        u_prev = hist
    n_slab = seqs * CONV_CH // LANES
    return pl.pallas_call(
        functools.partial(_conv_kernel, tc),
        grid=(bsz // seqs, t // tc),
        in_specs=[pl.BlockSpec((seqs, tc, CONV_CH), lambda i, j: (i, j, 0)),
                  prev,
                  pl.BlockSpec((seqs, CONV_PAD, CONV_CH), lambda i, j: (i, 0, 0)),
                  _full(w.shape), _full((1, CONV_CH)), _full((1, CONV_CH)),
                  _full((1, CONV_CH))],
        out_specs=pl.BlockSpec((seqs, tc, CONV_CH), lambda i, j: (i, j, 0)),
        out_shape=jax.ShapeDtypeStruct((bsz, t, CONV_CH), BF16),
        scratch_shapes=[pltpu.VMEM((n_slab, CONV_PAD + tc, LANES), F32),
                        pltpu.VMEM((n_slab, tc, LANES), F32)],
        compiler_params=_params("parallel", "arbitrary"),
        name="conv",
    )(u, u_prev, hist, w, b, g, beta)


def _gla_in_body(ins, consts, outs, scratch, rows):
    (x_ref,), (g_ref, w_ref, wl_ref, wu_ref, bu_ref) = ins, consts
    q_ref, k_ref, v_ref, r_ref, la_ref = outs
    h = _rms_rows(x_ref[...], g_ref[...]).astype(BF16)
    proj = lambda lo, width: _dot(h, w_ref[:, lo:lo + width].astype(BF16))
    q_ref[...] = proj(0, C_DK) * (C_DK_HEAD ** -0.5)
    k_ref[...] = proj(C_DK, C_DK)
    v_ref[...] = proj(2 * C_DK, C_DV)
    r = proj(2 * C_DK + C_DV, C_DV)
    r_ref[...] = r * _sigmoid(r)
    low = _dot(h, wl_ref[...]).astype(BF16)
    z = _dot(low, wu_ref[...]) + bu_ref[...]
    log_sig = jnp.minimum(z, 0.0) - jnp.log1p(jnp.exp(-jnp.abs(z)))
    la_ref[...] = log_sig * (1.0 / GATE_TAU)


def _gla_in(x_p, x_s, g, w, wl, wu, bu, tm):
    return _rows_call(_gla_in_body, [x_p], [x_s], [g, w, wl, wu, bu],
                      [C_DK, C_DK, C_DV, C_DV, C_DK], tm, "gla_in")


def _gla_kernel(chunk, n_chunks, q_ref, k_ref, v_ref, r_ref, la_ref, s0_ref, gain_ref,
                o_ref, s_ref, qin_s, kin_s, x1_s, x2_s, qst_s, kst_s, dec_s):
    n_seq = q_ref.shape[0]
    @pl.when(pl.program_id(1) == 0)
    def _():
        s_ref[...] = s0_ref[...]

    half, quarter = chunk // 2, chunk // 4
    quarter_of = lambda i: sum((i >= j * quarter).astype(jnp.int32) for j in range(1, 4))
    ri = lax.broadcasted_iota(jnp.int32, (chunk, chunk), 0)
    ci = lax.broadcasted_iota(jnp.int32, (chunk, chunk), 1)
    causal = ci <= ri
    tri = jnp.where(causal, 1.0, 0.0).astype(BF16)
    rq, cq = quarter_of(ri), quarter_of(ci)
    same_quarter = jnp.logical_and(causal, rq == cq)
    cross_half = jnp.logical_and(ri >= half, ci < half)
    cross_quarter = jnp.logical_and(rq == cq + 1, (ri >= half) == (ci >= half))
    rr = lax.broadcasted_iota(jnp.int32, (chunk, C_DK), 0)
    rrq = quarter_of(rr)
    in_first = rr < half
    key_side_2 = jnp.logical_or(rrq == 0, rrq == 2)
    gain = gain_ref[...]

    def prepare(g, c):
        rows = slice(c * chunk, (c + 1) * chunk)
        la = la_ref[g, rows, :]
        la_hi = la.astype(BF16)
        la_lo = (la - la_hi.astype(F32)).astype(BF16)
        cum = _dot(tri, la_hi) + _dot(tri, la_lo)
        row = lambda i: cum[i:i + 1, :]
        last = row(chunk - 1)
        mids = [row(j * quarter + quarter // 2 - 1) for j in range(4)]
        mid = jnp.where(rrq == 0, mids[0], jnp.where(rrq == 1, mids[1],
                        jnp.where(rrq == 2, mids[2], mids[3])))
        q = q_ref[g, rows, :]
        k = k_ref[g, rows, :]
        grow = jnp.exp(cum - mid)
        qin_s[g, rows, :] = (q * grow).astype(BF16)
        kin_s[g, rows, :] = (k / grow).astype(BF16)

        def across(edge, key_side):
            gap = cum - edge
            return (jnp.where(key_side, k, q) * jnp.exp(jnp.where(key_side, -gap, gap))).astype(BF16)

        x1_s[g, rows, :] = across(row(half - 1), in_first)
        x2_s[g, rows, :] = across(jnp.where(in_first, row(quarter - 1), row(half + quarter - 1)),
                               key_side_2)
        qst_s[g, rows, :] = (q * jnp.exp(cum)).astype(BF16)
        kst_s[g, rows, :] = (k * jnp.exp(last - cum)).astype(BF16)
        dec_s[g, c] = jnp.broadcast_to(jnp.exp(last), (LANES, C_DK)).T

    def advance(g, c):
        rows = slice(c * chunk, (c + 1) * chunk)
        for h in range(C_HEADS):
            ks = slice(h * C_DK_HEAD, (h + 1) * C_DK_HEAD)
            vs = slice(h * C_DV_HEAD, (h + 1) * C_DV_HEAD)
            vh = v_ref[g, rows, vs].astype(BF16)
            x1, x2 = x1_s[g, rows, ks], x2_s[g, rows, ks]
            att = jnp.where(same_quarter, _dot_nt(qin_s[g, rows, ks], kin_s[g, rows, ks]),
                            jnp.where(cross_quarter, _dot_nt(x2, x2),
                                      jnp.where(cross_half, _dot_nt(x1, x1), 0.0)))
            s = s_ref[g, h]
            o = _dot(jnp.concatenate([att.astype(BF16), qst_s[g, rows, ks]], axis=1),
                     jnp.concatenate([vh, s.astype(BF16)], axis=0))
            decay = dec_s[g, c, ks, :]
            s_ref[g, h] = (s * jnp.concatenate([decay] * (C_DV_HEAD // LANES), axis=1)
                           + _dot_tn(kst_s[g, rows, ks], vh))
            y = o * lax.rsqrt(jnp.mean(o * o, axis=-1, keepdims=True) + EPS) * gain
            o_ref[g, rows, vs] = (y * r_ref[g, rows, vs]).astype(o_ref.dtype)

    for g in range(n_seq):
        prepare(g, 0)
    for c in range(n_chunks):
        for g in range(n_seq):
            if c + 1 < n_chunks:
                prepare(g, c + 1)
            advance(g, c)


def _gla(q, k, v, r, la, s0, gain, chunk, tb, seqs):
    b, t, _ = q.shape
    seq = lambda width: pl.BlockSpec((seqs, tb, width), lambda i, j: (i, j, 0))
    state = pl.BlockSpec((seqs, C_HEADS, C_DK_HEAD, C_DV_HEAD), lambda i, j: (i, 0, 0, 0))
    return pl.pallas_call(
        functools.partial(_gla_kernel, chunk, tb // chunk),
        grid=(b // seqs, t // tb),
        in_specs=[seq(C_DK), seq(C_DK), seq(C_DV), seq(C_DV), seq(C_DK), state,
                  _full((1, C_DV_HEAD))],
        out_specs=[seq(C_DV), state],
        out_shape=[jax.ShapeDtypeStruct((b, t, C_DV), BF16),
                   jax.ShapeDtypeStruct(s0.shape, F32)],
        scratch_shapes=[pltpu.VMEM((seqs, tb, C_DK), BF16)] * 6
                       + [pltpu.VMEM((seqs, tb // chunk, C_DK, LANES), F32)],
        compiler_params=_params("parallel", "arbitrary"),
        name="gla",
    )(q, k, v, r, la, s0, gain)


GLA_CHUNK = 128
SAMPLE_PAD = 16
SAMPLE_SEQS = 4


ROW_TILE = 512
FFN_TILE = ROW_TILE
EVEN_TILE = 1024
CONV_TILE = 512
GLA_TILE = 512


def _even_mixer_prompt(q, k, v, u, P, i, bsz, t):
    n = bsz * t
    a = _attn_prompt(q, k, v, P['band_vectors'], bsz, t)
    keep = min(MAX_WINDOW, t)
    tail = lambda z: z.reshape(bsz, t, A_WIDTH)[:, t - keep:].reshape(bsz, keep, A_HEADS, A_HEAD_DIM)
    u3 = u.reshape(bsz, t, CONV_CH)
    hist = jnp.zeros((bsz, CONV_PAD, CONV_CH), F32)
    c = _conv(u3, hist, P['ev_conv_w'][i], P['ev_conv_b'][i], P['ev_conv_ln_g'][i],
              P['ev_conv_ln_b'][i], CONV_TILE, 1).reshape(n, CONV_CH)
    return [a, c], (tail(k), tail(v), u3[:, t - (CONV_WIDTH - 1):])


def _even_mixer_sample(q, k, v, u, past, P, i, bsz, t):
    n = bsz * t
    n_buf = past[0].shape[2]
    major = lambda z: z.transpose(0, 1, 3, 4, 2).reshape(z.shape[0] * bsz, A_WIDTH, n_buf)
    minor = lambda z: z.reshape(bsz, A_HEADS, A_HEAD_DIM, n_buf).transpose(0, 3, 1, 2)
    a, new_k, new_v = _attn_sample(
        q.reshape(bsz, t, A_WIDTH), k.reshape(bsz, t, A_WIDTH), v.reshape(bsz, t, A_WIDTH),
        major(past[0]), major(past[1]), i, P['decode_tables'])
    u3 = u.reshape(bsz, t, CONV_CH)
    hist = jnp.pad(past[2][i], ((0, 0), (CONV_PAD - (CONV_WIDTH - 1), 0), (0, 0)))
    c = _conv(u3, hist, P['ev_conv_w'][i], P['ev_conv_b'][i], P['ev_conv_ln_g'][i],
              P['ev_conv_ln_b'][i], t, SAMPLE_SEQS).reshape(n, CONV_CH)
    new_u = jnp.concatenate([past[2][i], u3], axis=1)[:, -(CONV_WIDTH - 1):]
    return [a.reshape(n, A_WIDTH), c], (minor(new_k), minor(new_v), new_u)


def _gla_mixer(q, k, v, r, la, s0, gain, bsz, t, t_pad, chunk, tile, seqs):
    seq = lambda z: jnp.pad(z.reshape(bsz, t, -1), ((0, 0), (0, t_pad - t), (0, 0)))
    o, s = _gla(seq(q), seq(k), seq(v), seq(r), seq(la), s0, gain, chunk, tile, seqs)
    return [o[:, :t].reshape(bsz * t, C_DV)], s


def _trunks(x_p, x_s, past, P):
    (b_p, t_p, _), (b_s, t_s, _) = x_p.shape, x_s.shape
    x_p = x_p.reshape(b_p * t_p, D_MODEL)
    x_s = x_s.reshape(b_s * t_s, D_MODEL)
    new_p = {'k': [], 'v': [], 'u': [], 's': []}
    new_s = {'k': [], 'v': [], 'u': [], 's': []}
    for layer in range(DEPTH):
        i = layer // 2
        x_p, x_s = _ffn(x_p, x_s, P['norm_ffn1'][layer], (P['ffn1_w_in'], layer),
                        (P['ffn1_w_out'], layer), FFN_TILE)
        if layer % 2 == 0:
            proj_p, proj_s = _even_in(x_p, x_s, P['norm_mix'][layer], (P['ev_w_in'], i),
                                      P['ev_q_gain'][i], P['ev_k_gain'][i], P['head_ones'], EVEN_TILE)
            mix_p, kvu_p = _even_mixer_prompt(*proj_p, P, i, b_p, t_p)
            mix_s, kvu_s = _even_mixer_sample(*proj_s, past, P, i, b_s, t_s)
            w_mix = [(P['ev_w_out'], i, 0, 2), (P['ev_w_out'], i, 1, 2)]
            for new, kvu in ((new_p, kvu_p), (new_s, kvu_s)):
                for name, z in zip('kvu', kvu):
                    new[name].append(z)
        else:
            proj_p, proj_s = _gla_in(x_p, x_s, P['norm_mix'][layer], (P['od_w_in'], i),
                                     P['od_w_low'][i], P['od_gate_w_up'][i], P['od_gate_b'][i],
                                     ROW_TILE)
            zeros = jnp.zeros((b_p, C_HEADS, C_DK_HEAD, C_DV_HEAD), F32)
            gain = P['od_o_gain'][i]
            mix_p, s_p = _gla_mixer(*proj_p, zeros, gain, b_p, t_p, t_p, GLA_CHUNK, GLA_TILE, 1)
            mix_s, s_s = _gla_mixer(*proj_s, past[3][i], gain, b_s, t_s, SAMPLE_PAD, SAMPLE_PAD,
                                    SAMPLE_PAD, SAMPLE_SEQS)
            w_mix = [(P['od_w_out'], i)]
            new_p['s'].append(s_p)
            new_s['s'].append(s_s)
        x_p, x_s = _ffn(x_p, x_s, P['norm_ffn2'][layer], (P['ffn2_w_in'], layer),
                        (P['ffn2_w_out'], layer), FFN_TILE, mix_p, mix_s, w_mix)
    stacked = lambda new: tuple(jnp.stack(new[name]) for name in 'kvus')
    return (x_p.reshape(b_p, t_p, D_MODEL), x_s.reshape(b_s, t_s, D_MODEL),
            *stacked(new_p), *stacked(new_s))


def kernel(x_prompt, x_sample, cache_k, cache_v, cache_conv, state_gla, rel_bias, norm_ffn1, ffn1_w_in, ffn1_w_out, norm_mix, norm_ffn2, ffn2_w_in, ffn2_w_out, ev_w_in, ev_q_gain, ev_k_gain, ev_conv_w, ev_conv_b, ev_conv_ln_g, ev_conv_ln_b, ev_w_out, od_w_in, od_gate_w_up, od_gate_b, od_o_gain, od_w_out):
    n_even = ev_w_in.shape[0]
    n_odd = od_w_in.shape[0]
    main = 2 * C_DK + 2 * C_DV
    head_ids = np.arange(MXU_DIM) // A_HEAD_DIM
    per = lambda n, f: [f(j) for j in range(n)]
    row = lambda z: z[None, :]
    P = {
        'norm_ffn1': per(DEPTH, lambda j: row(norm_ffn1[j])),
        'norm_mix': per(DEPTH, lambda j: row(norm_mix[j])),
        'norm_ffn2': per(DEPTH, lambda j: row(norm_ffn2[j])),
        'ffn1_w_in': ffn1_w_in, 'ffn1_w_out': ffn1_w_out,
        'ffn2_w_in': ffn2_w_in, 'ffn2_w_out': ffn2_w_out,
        'ev_w_in': ev_w_in, 'ev_w_out': ev_w_out, 'od_w_in': od_w_in, 'od_w_out': od_w_out,
        'ev_q_gain': per(n_even, lambda j: row(jnp.tile(ev_q_gain[j], A_HEADS))),
        'ev_k_gain': per(n_even, lambda j: row(jnp.tile(ev_k_gain[j], A_HEADS))),
        'head_ones': jnp.asarray(head_ids[:, None] == head_ids[None, :], BF16),
        'ev_conv_w': per(n_even, lambda j: jnp.broadcast_to(
            ev_conv_w[j][:, None, :], (CONV_WIDTH, SUBLANES, CONV_CH))),
        'ev_conv_b': per(n_even, lambda j: row(ev_conv_b[j])),
        'ev_conv_ln_g': per(n_even, lambda j: row(ev_conv_ln_g[j])),
        'ev_conv_ln_b': per(n_even, lambda j: row(ev_conv_ln_b[j])),
        'od_w_low': per(n_odd, lambda j: jnp.pad(od_w_in[j, :, main:],
                                                 ((0, 0), (0, LANES - GATE_RANK))).astype(BF16)),
        'od_gate_w_up': per(n_odd, lambda j: jnp.pad(od_gate_w_up[j],
                                                     ((0, LANES - GATE_RANK), (0, 0))).astype(BF16)),
        'od_gate_b': per(n_odd, lambda j: row(od_gate_b[j])),
        'od_o_gain': per(n_odd, lambda j: row(od_o_gain[j])),
        'band_vectors': _band_vectors(rel_bias),
        'decode_tables': _decode_tables(rel_bias, x_sample.shape[1]),
    }
    P, x_prompt, x_sample = lax.optimization_barrier((P, x_prompt, x_sample))
    return _trunks(x_prompt, x_sample, (cache_k, cache_v, cache_conv, state_gla), P)
```

```python
import functools

import numpy as np
import jax
import jax.numpy as jnp
from jax import lax
from jax.experimental import pallas as pl
from jax.experimental.pallas import tpu as pltpu

F32 = jnp.float32
BF16 = jnp.bfloat16

D_MODEL = 1024
DEPTH = 2
A_HEADS = 8
A_HEAD_DIM = 64
A_WIDTH = A_HEADS * A_HEAD_DIM
DILATED_GROUPS = ((128, 1), (512, 4), (2048, 16))
MAX_WINDOW = 2048
N_BUCKETS = 32
CONV_WIDTH = 31
CONV_CH = 512
C_HEADS = 4
C_DK = 512
C_DV = 1024
C_DK_HEAD = 128
C_DV_HEAD = 256
GATE_RANK = 16
GATE_TAU = 16.0
D_FF = 2816
EPS = 1e-6
NEG_INF = -1e30
LOG2E = 1.4426950408889634

LANES = 128
SUBLANES = 8
MXU_DIM = 256
WIN_KEYS = 128
ATT_BLOCK = 2048
ATT_UNROLL = 16
ATT_PHASES = 4
VMEM_LIMIT = 56 * 1024 * 1024


def _params(*sem):
    return pltpu.CompilerParams(dimension_semantics=sem, vmem_limit_bytes=VMEM_LIMIT)


def _dot(a, b):
    return jnp.dot(a, b, preferred_element_type=F32)


def _dot_nt(a, b):
    return lax.dot_general(a, b, (((1,), (1,)), ((), ())), preferred_element_type=F32)


def _dot_tn(a, b):
    return lax.dot_general(a, b, (((0,), (0,)), ((), ())), preferred_element_type=F32)


def _rms_rows(x, g):
    y = x * lax.rsqrt(jnp.mean(x * x, axis=-1, keepdims=True) + EPS)
    return y * g


def _sigmoid(x):
    return 1.0 / (1.0 + jnp.exp(-x))


def _full(shape):
    return pl.BlockSpec(shape, lambda *_: (0,) * len(shape), pipeline_mode=pl.Buffered(1))


FF_CHUNK = 256
FF_DOWN_CHUNK = 1024


def _layer(w, layer, row_block=0, n_blocks=1):
    rows = w.shape[1] // n_blocks
    return pl.BlockSpec((None, rows, w.shape[2]), lambda *_: (layer, row_block, 0),
                        pipeline_mode=pl.Buffered(1))


def _rows_call(body, prompt_rows, sample_rows, consts, out_widths, tm, name, scratch=()):
    n_p, n_s = prompt_rows[0].shape[0], sample_rows[0].shape[0]
    steps = n_p // tm
    n_in, n_c, n_out = len(prompt_rows), len(consts), len(out_widths)

    def kernel(*refs):
        p_in, s_in = refs[:n_in], refs[n_in:2 * n_in]
        c_refs = refs[2 * n_in:2 * n_in + n_c]
        outs = refs[2 * n_in + n_c:]
        p_out, s_out, scr = outs[:n_out], outs[n_out:2 * n_out], outs[2 * n_out:]
        step = pl.program_id(0)

        @pl.when(step < steps)
        def _():
            body(p_in, c_refs, p_out, scr, tm)

        @pl.when(step == steps)
        def _():
            body(s_in, c_refs, s_out, scr, n_s)

    p_spec = lambda width: pl.BlockSpec((tm, width), lambda i: (jnp.minimum(i, steps - 1), 0))
    s_spec = lambda width: pl.BlockSpec((n_s, width), lambda i: (0, 0))
    res = pl.pallas_call(
        kernel,
        grid=(steps + 1,),
        in_specs=[p_spec(a.shape[1]) for a in prompt_rows] + [s_spec(a.shape[1]) for a in sample_rows]
                 + [_layer(*c) if isinstance(c, tuple) else _full(c.shape) for c in consts],
        out_specs=[p_spec(w) for w in out_widths] + [s_spec(w) for w in out_widths],
        out_shape=[jax.ShapeDtypeStruct((n_p, w), F32) for w in out_widths]
                  + [jax.ShapeDtypeStruct((n_s, w), F32) for w in out_widths],
        scratch_shapes=list(scratch),
        compiler_params=_params("arbitrary"),
        name=name,
    )(*prompt_rows, *sample_rows, *[c[0] if isinstance(c, tuple) else c for c in consts])
    return res[:n_out], res[n_out:]


def _ffn_body(ins, consts, outs, scratch, rows):
    x_ref, m_refs = ins[0], ins[1:]
    w_refs, (g_ref, wi_ref, wo_ref) = consts[:len(m_refs)], consts[len(m_refs):]
    (o_ref,), (act_ref,) = outs, scratch
    x = x_ref[...]
    for m_ref, w_ref in zip(m_refs, w_refs):
        x = x + _dot(m_ref[...], w_ref[...].astype(BF16))
    h = _rms_rows(x, g_ref[...]).astype(BF16)
    for c in range(D_FF // FF_CHUNK):
        lo = c * FF_CHUNK
        a = _dot(h, wi_ref[:, lo:lo + FF_CHUNK].astype(BF16))
        b = _dot(h, wi_ref[:, D_FF + lo:D_FF + lo + FF_CHUNK].astype(BF16))
        act_ref[0:rows, lo:lo + FF_CHUNK] = (a * _sigmoid(a) * b).astype(BF16)
    y = jnp.zeros_like(x)
    for lo in range(0, D_FF, FF_DOWN_CHUNK):
        hi = min(lo + FF_DOWN_CHUNK, D_FF)
        y = y + _dot(act_ref[0:rows, lo:hi], wo_ref[lo:hi, :].astype(BF16))
    o_ref[...] = x + 0.5 * y


def _ffn(x_p, x_s, g, wi, wo, tm, ms_p=(), ms_s=(), ws=()):
    (y_p,), (y_s,) = _rows_call(_ffn_body, [x_p, *ms_p], [x_s, *ms_s], [*ws, g, wi, wo],
                                [D_MODEL], tm, "ffn", [pltpu.VMEM((tm, D_FF), BF16)])
    return y_p, y_s


def _head_norm(z, gain, bd):
    zz = z * z
    hi = zz.astype(BF16)
    lo = (zz - hi.astype(F32)).astype(BF16)
    width = bd.shape[0]
    ss = jnp.concatenate(
        [_dot(hi[:, c:c + width], bd) + _dot(lo[:, c:c + width], bd)
         for c in range(0, z.shape[1], width)], axis=1)
    return z * lax.rsqrt(ss * (1.0 / A_HEAD_DIM) + EPS) * gain


def _even_in_body(ins, consts, outs, scratch, rows):
    (x_ref,), (g_ref, w_ref, qg_ref, kg_ref, bd_ref), (q_ref, k_ref, v_ref, u_ref) = ins, consts, outs
    h = _rms_rows(x_ref[...], g_ref[...]).astype(BF16)
    proj = lambda lo, width: _dot(h, w_ref[:, lo:lo + width].astype(BF16))
    bd = bd_ref[...]
    q_ref[...] = _head_norm(proj(0, A_WIDTH), qg_ref[...], bd) * (A_HEAD_DIM ** -0.5 * LOG2E)
    k_ref[...] = _head_norm(proj(A_WIDTH, A_WIDTH), kg_ref[...], bd)
    v_ref[...] = proj(2 * A_WIDTH, A_WIDTH)
    gv = proj(3 * A_WIDTH, CONV_CH)
    gg = proj(3 * A_WIDTH + CONV_CH, CONV_CH)
    u_ref[...] = gv * _sigmoid(gg)


def _even_in(x_p, x_s, g, w, qg, kg, bd, tm):
    return _rows_call(_even_in_body, [x_p], [x_s], [g, w, qg, kg, bd], [A_WIDTH] * 4, tm, "even_in")


def _t5_bucket(dist):
    max_exact = N_BUCKETS // 2
    d = np.asarray(dist, dtype=np.int32)
    df = np.maximum(d, 1).astype(np.float32)
    large = max_exact + (np.log(df / max_exact) / np.log(MAX_WINDOW / max_exact)
                         * (N_BUCKETS - max_exact)).astype(np.int32)
    large = np.minimum(large, N_BUCKETS - 1)
    return np.where(d < max_exact, d, large).astype(np.int32)


def _select_bias(rel_bias, dist, valid):
    onehot = (_t5_bucket(dist)[None, :] == np.arange(N_BUCKETS)[:, None]) & valid[None, :]
    picked = jnp.einsum('bh,bc->hc', rel_bias, jnp.asarray(onehot, F32),
                        precision=lax.Precision.HIGHEST)
    return picked * LOG2E + jnp.asarray(np.where(valid, 0.0, NEG_INF), F32)[None, :]


def _band_vectors(rel_bias):
    c = np.arange(2 * WIN_KEYS)
    valid = c <= WIN_KEYS
    vecs = [_select_bias(rel_bias, np.where(valid, (WIN_KEYS - c) * dil, 0), valid)
            for _, dil in DILATED_GROUPS]
    return jnp.stack(vecs).reshape(len(DILATED_GROUPS), A_HEADS // 2, 2, 2 * WIN_KEYS)


def _attn_prompt_kernel(q_ref, kp_ref, kc_ref, vp_ref, vc_ref, vec_ref, o_ref,
                        knat, vnat, k4, v4, q4, og1, lg1, og4, lg4, tab_ref):
    blk = pl.program_id(2)
    n_groups = len(DILATED_GROUPS)
    nph = ATT_PHASES
    per = ATT_BLOCK // nph

    @pl.when(blk == 0)
    def _():
        col = lax.broadcasted_iota(jnp.int32, (WIN_KEYS, 2 * WIN_KEYS), 1)
        for g in range(n_groups):
            for hh in range(2):
                vec = jnp.broadcast_to(vec_ref[g, 0, hh:hh + 1, :], (WIN_KEYS, 2 * WIN_KEYS))
                band = pltpu.roll(vec, 0, 1, stride=1, stride_axis=0)
                rows = slice(hh * WIN_KEYS, (hh + 1) * WIN_KEYS)
                tab_ref[g, rows, :] = band
                tab_ref[n_groups + g, rows, :] = jnp.where(col >= WIN_KEYS, band, NEG_INF)

    knat[0:WIN_KEYS, :] = kp_ref[ATT_BLOCK - WIN_KEYS:, :]
    knat[WIN_KEYS:, :] = kc_ref[...]
    vnat[0:WIN_KEYS, :] = vp_ref[ATT_BLOCK - WIN_KEYS:, :]
    vnat[WIN_KEYS:, :] = vc_ref[...]
    for r in range(nph):
        phase = pl.ds(r, per, stride=nph)
        k4[r, 0:per, :] = kp_ref[phase, :]
        k4[r, per:, :] = kc_ref[phase, :]
        v4[r, 0:per, :] = vp_ref[phase, :]
        v4[r, per:, :] = vc_ref[phase, :]
        q4[r] = q_ref[phase, :]
    first = blk == 0
    lane = lax.broadcasted_iota(jnp.int32, (WIN_KEYS, LANES), 1)
    low = lane < A_HEAD_DIM
    ones = jnp.ones((2 * WIN_KEYS, LANES), BF16)

    def block(qs, kk, vv, tab):
        qs = qs.astype(BF16)
        zero = jnp.zeros_like(qs)
        qst = jnp.concatenate([jnp.where(low, qs, zero), jnp.where(low, zero, qs)], axis=0)
        s = _dot_nt(qst, kk.astype(BF16)) + tab
        mx = jnp.max(s, axis=-1, keepdims=True)
        p = jnp.exp2(s - mx).astype(BF16)
        r = _dot(p, jnp.concatenate([vv.astype(BF16), ones], axis=1))
        o2 = jnp.where(low, r[0:WIN_KEYS, 0:LANES], r[WIN_KEYS:, 0:LANES])
        l2 = jnp.where(low, r[0:WIN_KEYS, LANES:], r[WIN_KEYS:, LANES:])
        m2 = jnp.where(low, jnp.broadcast_to(mx[0:WIN_KEYS], (WIN_KEYS, LANES)),
                       jnp.broadcast_to(mx[WIN_KEYS:], (WIN_KEYS, LANES)))
        return o2 / l2, m2 + jnp.log(l2) * LOG2E

    def table(g, at_start):
        return tab_ref[jnp.where(jnp.logical_and(at_start, first), n_groups + g, g)]

    def body1(sub, carry):
        i0 = pl.multiple_of(sub * WIN_KEYS, WIN_KEYS)
        o, l = block(q_ref[pl.ds(i0, WIN_KEYS), :], knat[pl.ds(i0, 2 * WIN_KEYS), :],
                     vnat[pl.ds(i0, 2 * WIN_KEYS), :], table(0, sub == 0))
        og1[pl.ds(i0, WIN_KEYS), :] = o
        lg1[pl.ds(i0, WIN_KEYS), :] = l
        return carry

    def body2(pb, carry):
        sub = pb // nph
        r = pb - sub * nph
        i0 = pl.multiple_of(sub * WIN_KEYS, WIN_KEYS)
        keys = pl.ds(i0 + (per - WIN_KEYS), 2 * WIN_KEYS)
        o, l = block(q4[r, pl.ds(i0, WIN_KEYS), :], k4[r, keys, :], v4[r, keys, :],
                     table(1, sub == 0))
        og4[0, r, pl.ds(i0, WIN_KEYS), :] = o
        lg4[0, r, pl.ds(i0, WIN_KEYS), :] = l
        return carry

    def body3(pb, carry):
        a = pb // nph
        r = pb - a * nph
        rows = pl.ds(a, WIN_KEYS, stride=nph)
        keys = pl.ds(a, 2 * WIN_KEYS, stride=nph)
        o, l = block(q4[r, rows, :], k4[r, keys, :], v4[r, keys, :], table(2, True))
        og4[1, r, rows, :] = o
        lg4[1, r, rows, :] = l
        return carry

    n_blocks = ATT_BLOCK // WIN_KEYS
    for body in (body1, body2, body3):
        lax.fori_loop(0, n_blocks, body, 0, unroll=ATT_UNROLL)

    for r in range(nph):
        phase = pl.ds(r, per, stride=nph)
        la, lb, lc = lg1[phase, :], lg4[0, r], lg4[1, r]
        mx = jnp.maximum(jnp.maximum(la, lb), lc)
        wa, wb, wc = jnp.exp2(la - mx), jnp.exp2(lb - mx), jnp.exp2(lc - mx)
        og1[phase, :] = (wa * og1[phase, :] + wb * og4[0, r] + wc * og4[1, r]) / (wa + wb + wc)
    o_ref[...] = og1[...].astype(o_ref.dtype)


def _attn_prompt(q, k, v, vecs, batch, seq):
    assert [d for _, d in DILATED_GROUPS] == [1, ATT_PHASES, ATT_PHASES ** 2]
    nb = seq // ATT_BLOCK
    n_groups = len(DILATED_GROUPS)
    per = ATT_BLOCK // ATT_PHASES
    cur = lambda b, p, t: (b * nb + t, p)
    prev = lambda b, p, t: (b * nb + jnp.maximum(t - 1, 0), p)
    blk = lambda imap: pl.BlockSpec((ATT_BLOCK, LANES), imap)
    vmem = lambda *shape: pltpu.VMEM(shape, F32)
    return pl.pallas_call(
        _attn_prompt_kernel,
        grid=(batch, A_HEADS // 2, nb),
        in_specs=[blk(cur), blk(prev), blk(cur), blk(prev), blk(cur),
                  pl.BlockSpec((n_groups, 1, 2, 2 * WIN_KEYS), lambda b, p, t: (0, p, 0, 0))],
        out_specs=blk(cur),
        out_shape=jax.ShapeDtypeStruct((batch * seq, A_WIDTH), BF16),
        scratch_shapes=[vmem(WIN_KEYS + ATT_BLOCK, LANES), vmem(WIN_KEYS + ATT_BLOCK, LANES),
                        vmem(ATT_PHASES, 2 * per, LANES), vmem(ATT_PHASES, 2 * per, LANES),
                        vmem(ATT_PHASES, per, LANES),
                        vmem(ATT_BLOCK, LANES), vmem(ATT_BLOCK, LANES),
                        vmem(2, ATT_PHASES, per, LANES), vmem(2, ATT_PHASES, per, LANES),
                        vmem(2 * n_groups, 2 * WIN_KEYS, 2 * WIN_KEYS)],
        compiler_params=_params("arbitrary", "arbitrary", "arbitrary"),
        name="attn_prompt",
    )(q, k, k, v, v, vecs)


def _decode_tables(rel_bias, n_new):
    cols = MAX_WINDOW + n_new
    c = np.arange(cols)
    dist = MAX_WINDOW - c
    cnt = np.zeros(c.shape, np.float32)
    for window, dil in DILATED_GROUPS:
        cnt += ((dist >= 0) & (dist <= window) & (dist % dil == 0)).astype(np.float32)
    vec = _select_bias(rel_bias, np.clip(dist, 0, MAX_WINDOW), cnt > 0)
    vec = vec + jnp.asarray(np.log2(np.maximum(cnt, 1.0)), F32)[None, :]
    rows = jnp.stack([jnp.pad(vec[:, :cols - i], ((0, 0), (i, 0)), constant_values=NEG_INF)
                      for i in range(n_new)], axis=1)
    rows = rows.reshape(A_HEADS * n_new, cols)
    return rows[:, :MAX_WINDOW], rows[:, MAX_WINDOW:]


def _attn_sample_kernel(q_ref, kn_ref, vn_ref, kc_ref, vc_ref, tc_ref, tn_ref,
                        o_ref, ko_ref, vo_ref):
    n_new = q_ref.shape[1]
    n_buf = kc_ref.shape[2]
    rows = A_HEADS * n_new
    kn = kn_ref[0]
    vn = vn_ref[0]
    kc = kc_ref[0]
    vc = vc_ref[0]

    lane = lax.broadcasted_iota(jnp.int32, (A_WIDTH, LANES), 1)
    for new, old, out_ref in ((kn, kc, ko_ref), (vn, vc, vo_ref)):
        shifted = pltpu.roll(old, n_buf - n_new, 1)
        tail = jnp.concatenate([jnp.zeros((LANES - n_new, A_WIDTH), F32), new], axis=0).T
        out_ref[0, :, 0:n_buf - LANES] = shifted[:, 0:n_buf - LANES]
        out_ref[0, :, n_buf - LANES:] = jnp.where(lane >= LANES - n_new, tail,
                                                  shifted[:, n_buf - LANES:])

    q = q_ref[0]
    row_head = lax.broadcasted_iota(jnp.int32, (A_HEADS, n_new, A_WIDTH), 0).reshape(rows, A_WIDTH)
    col = lax.broadcasted_iota(jnp.int32, (rows, A_WIDTH), 1)
    own = jnp.logical_and(col >= row_head * A_HEAD_DIM, col < (row_head + 1) * A_HEAD_DIM)
    qblk = jnp.where(own, jnp.concatenate([q] * A_HEADS, axis=0), 0.0).astype(BF16)
    s_c = _dot(qblk, kc.astype(BF16)) + tc_ref[...]
    s_n = _dot_nt(qblk, kn.astype(BF16)) + tn_ref[...]
    mx = jnp.maximum(jnp.max(s_c, axis=-1, keepdims=True), jnp.max(s_n, axis=-1, keepdims=True))
    p_c = jnp.exp2(s_c - mx)
    p_n = jnp.exp2(s_n - mx)
    den = jnp.sum(p_c, axis=-1, keepdims=True) + jnp.sum(p_n, axis=-1, keepdims=True)
    acc = _dot_nt(p_c.astype(BF16), vc.astype(BF16)) + _dot(p_n.astype(BF16), vn.astype(BF16))
    acc = jnp.where(own, acc / den, 0.0)
    out = acc[0:n_new]
    for h in range(1, A_HEADS):
        out = out + acc[h * n_new:(h + 1) * n_new]
    o_ref[0] = out.astype(o_ref.dtype)


def _attn_sample(q, k_new, v_new, cache_k, cache_v, layer, tables):
    b, n_new, _ = q.shape
    n_buf = cache_k.shape[2]
    new = pl.BlockSpec((1, n_new, A_WIDTH), lambda i: (i, 0, 0))
    buf = pl.BlockSpec((1, A_WIDTH, n_buf), lambda i: (i, 0, 0))
    past = pl.BlockSpec((1, A_WIDTH, n_buf), lambda i: (layer * b + i, 0, 0))
    return pl.pallas_call(
        _attn_sample_kernel,
        grid=(b,),
        in_specs=[new, new, new, past, past, _full(tables[0].shape), _full(tables[1].shape)],
        out_specs=[new, buf, buf],
        out_shape=[jax.ShapeDtypeStruct((b, n_new, A_WIDTH), BF16),
                   jax.ShapeDtypeStruct((b, A_WIDTH, n_buf), F32),
                   jax.ShapeDtypeStruct((b, A_WIDTH, n_buf), F32)],
        compiler_params=_params("parallel"),
        name="attn_sample",
    )(q, k_new, v_new, cache_k, cache_v, *tables)


CONV_PAD = 32
CONV_ROWS = 32
CONV_UNROLL = 2


def _conv_kernel(tc, u_ref, up_ref, hist_ref, w_ref, b_ref, g_ref, beta_ref, o_ref, win, stage):
    t = pl.program_id(1)
    n_seq = u_ref.shape[0]
    n_slab = CONV_CH // LANES
    slab = lambda c: slice(c * LANES, (c + 1) * LANES)
    off = CONV_PAD - (CONV_WIDTH - 1)
    rc = min(CONV_ROWS, tc)
    half = rc // 2

    def tap(k, c):
        w = w_ref[k, :, slab(c)]
        if half < SUBLANES:
            return w[0:half]
        return jnp.concatenate([w] * (half // SUBLANES), axis=0)

    for s in range(n_seq):
        first = s * n_slab
        for c in range(n_slab):
            win[first + c, CONV_PAD:CONV_PAD + tc, :] = u_ref[s, :, slab(c)]

        @pl.when(t == 0)
        def _():
            for c in range(n_slab):
                win[first + c, 0:CONV_PAD, :] = hist_ref[s, :, slab(c)]

        @pl.when(t > 0)
        def _():
            for c in range(n_slab):
                win[first + c, 0:CONV_PAD, :] = up_ref[s, :, slab(c)]

        def body(j, carry):
            r0 = j * rc
            for c in range(n_slab):
                for par in range(2):
                    acc = jnp.zeros((half, LANES), F32) + b_ref[:, slab(c)]
                    for k in range(CONV_WIDTH):
                        rows = pl.ds(r0 + off + k + par, half, stride=2)
                        acc = acc + win[first + c, rows, :] * tap(k, c)
                    stage[first + c, pl.ds(r0 + par, half, stride=2), :] = acc
            return carry

        if tc == rc:
            body(0, 0)
        else:
            lax.fori_loop(0, tc // rc, body, 0, unroll=CONV_UNROLL)
        y = jnp.concatenate([stage[first + c] for c in range(n_slab)], axis=1)
        xc = y - jnp.mean(y, axis=-1, keepdims=True)
        y = xc * lax.rsqrt(jnp.mean(xc * xc, axis=-1, keepdims=True) + EPS)
        y = y * g_ref[...] + beta_ref[...]
        o_ref[s] = (y * _sigmoid(y)).astype(o_ref.dtype)


def _conv(u, hist, w, b, g, beta, tc, seqs):
    bsz, t, _ = u.shape
    per = tc // CONV_PAD
    if t >= CONV_PAD:
        prev = pl.BlockSpec((seqs, CONV_PAD, CONV_CH),
                            lambda i, j: (i, jnp.maximum(j * per - 1, 0), 0))
        u_prev = u
    else:
        prev = pl.BlockSpec((seqs, CONV_PAD, CONV_CH), lambda i, j: (i, 0, 0))
        u_prev = hist
    n_slab = seqs * CONV_CH // LANES
    return pl.pallas_call(
        functools.partial(_conv_kernel, tc),
        grid=(bsz // seqs, t // tc),
        in_specs=[pl.BlockSpec((seqs, tc, CONV_CH), lambda i, j: (i, j, 0)),
                  prev,
                  pl.BlockSpec((seqs, CONV_PAD, CONV_CH), lambda i, j: (i, 0, 0)),
                  _full(w.shape), _full((1, CONV_CH)), _full((1, CONV_CH)),
                  _full((1, CONV_CH))],
        out_specs=pl.BlockSpec((seqs, tc, CONV_CH), lambda i, j: (i, j, 0)),
        out_shape=jax.ShapeDtypeStruct((bsz, t, CONV_CH), BF16),
        scratch_shapes=[pltpu.VMEM((n_slab, CONV_PAD + tc, LANES), F32),
                        pltpu.VMEM((n_slab, tc, LANES), F32)],
        compiler_params=_params("parallel", "arbitrary"),
        name="conv",
    )(u, u_prev, hist, w, b, g, beta)


def _gla_in_body(ins, consts, outs, scratch, rows):
    (x_ref,), (g_ref, w_ref, wl_ref, wu_ref, bu_ref) = ins, consts
    q_ref, k_ref, v_ref, r_ref, la_ref = outs
    h = _rms_rows(x_ref[...], g_ref[...]).astype(BF16)
    proj = lambda lo, width: _dot_nt(h, w_ref[lo:lo + width, :].astype(BF16))
    q_ref[...] = proj(0, C_DK) * (C_DK_HEAD ** -0.5)
    k_ref[...] = proj(C_DK, C_DK)
    v_ref[...] = proj(2 * C_DK, C_DV)
    r = proj(2 * C_DK + C_DV, C_DV)
    r_ref[...] = r * _sigmoid(r)
    low = _dot(h, wl_ref[...]).astype(BF16)
    z = _dot(low, wu_ref[...]) + bu_ref[...]
    log_sig = jnp.minimum(z, 0.0) - jnp.log1p(jnp.exp(-jnp.abs(z)))
    la_ref[...] = log_sig * (1.0 / GATE_TAU)


def _gla_in(x_p, x_s, g, w, wl, wu, bu, tm):
    return _rows_call(_gla_in_body, [x_p], [x_s], [g, w, wl, wu, bu],
                      [C_DK, C_DK, C_DV, C_DV, C_DK], tm, "gla_in")


def _gla_kernel(chunk, n_chunks, q_ref, k_ref, v_ref, r_ref, la_ref, s0_ref, gain_ref,
                o_ref, s_ref, qin_s, kin_s, x1_s, x2_s, qst_s, kst_s, dec_s):
    n_seq = q_ref.shape[0]
    @pl.when(pl.program_id(1) == 0)
    def _():
        s_ref[...] = s0_ref[...]

    half, quarter = chunk // 2, chunk // 4
    quarter_of = lambda i: sum((i >= j * quarter).astype(jnp.int32) for j in range(1, 4))
    ri = lax.broadcasted_iota(jnp.int32, (chunk, chunk), 0)
    ci = lax.broadcasted_iota(jnp.int32, (chunk, chunk), 1)
    causal = ci <= ri
    tri = jnp.where(causal, 1.0, 0.0).astype(BF16)
    rq, cq = quarter_of(ri), quarter_of(ci)
    same_quarter = jnp.logical_and(causal, rq == cq)
    cross_half = jnp.logical_and(ri >= half, ci < half)
    cross_quarter = jnp.logical_and(rq == cq + 1, (ri >= half) == (ci >= half))
    rr = lax.broadcasted_iota(jnp.int32, (chunk, C_DK), 0)
    rrq = quarter_of(rr)
    in_first = rr < half
    key_side_2 = jnp.logical_or(rrq == 0, rrq == 2)
    gain = gain_ref[...]

    def prepare(g, c):
        rows = slice(c * chunk, (c + 1) * chunk)
        la = la_ref[g, rows, :]
        la_hi = la.astype(BF16)
        la_lo = (la - la_hi.astype(F32)).astype(BF16)
        cum = _dot(tri, la_hi) + _dot(tri, la_lo)
        row = lambda i: cum[i:i + 1, :]
        last = row(chunk - 1)
        mids = [row(j * quarter + quarter // 2 - 1) for j in range(4)]
        mid = jnp.where(rrq == 0, mids[0], jnp.where(rrq == 1, mids[1],
                        jnp.where(rrq == 2, mids[2], mids[3])))
        q = q_ref[g, rows, :]
        k = k_ref[g, rows, :]
        grow = jnp.exp(cum - mid)
        qin_s[g, rows, :] = (q * grow).astype(BF16)
        kin_s[g, rows, :] = (k / grow).astype(BF16)

        def across(edge, key_side):
            gap = cum - edge
            return (jnp.where(key_side, k, q) * jnp.exp(jnp.where(key_side, -gap, gap))).astype(BF16)

        x1_s[g, rows, :] = across(row(half - 1), in_first)
        x2_s[g, rows, :] = across(jnp.where(in_first, row(quarter - 1), row(half + quarter - 1)),
                               key_side_2)
        qst_s[g, rows, :] = (q * jnp.exp(cum)).astype(BF16)
        kst_s[g, rows, :] = (k * jnp.exp(last - cum)).astype(BF16)
        dec_s[g, c] = jnp.broadcast_to(jnp.exp(last), (LANES, C_DK)).T

    def advance(g, c):
        rows = slice(c * chunk, (c + 1) * chunk)
        for h in range(C_HEADS):
            ks = slice(h * C_DK_HEAD, (h + 1) * C_DK_HEAD)
            vs = slice(h * C_DV_HEAD, (h + 1) * C_DV_HEAD)
            vh = v_ref[g, rows, vs].astype(BF16)
            x1, x2 = x1_s[g, rows, ks], x2_s[g, rows, ks]
            att = jnp.where(same_quarter, _dot_nt(qin_s[g, rows, ks], kin_s[g, rows, ks]),
                            jnp.where(cross_quarter, _dot_nt(x2, x2),
                                      jnp.where(cross_half, _dot_nt(x1, x1), 0.0)))
            s = s_ref[g, h]
            o = _dot(jnp.concatenate([att.astype(BF16), qst_s[g, rows, ks]], axis=1),
                     jnp.concatenate([vh, s.astype(BF16)], axis=0))
            decay = dec_s[g, c, ks, :]
            s_ref[g, h] = (s * jnp.concatenate([decay] * (C_DV_HEAD // LANES), axis=1)
                           + _dot_tn(kst_s[g, rows, ks], vh))
            y = o * lax.rsqrt(jnp.mean(o * o, axis=-1, keepdims=True) + EPS) * gain
            o_ref[g, rows, vs] = (y * r_ref[g, rows, vs]).astype(o_ref.dtype)

    for g in range(n_seq):
        prepare(g, 0)
    for c in range(n_chunks):
        for g in range(n_seq):
            if c + 1 < n_chunks:
                prepare(g, c + 1)
            advance(g, c)


def _gla(q, k, v, r, la, s0, gain, chunk, tb, seqs):
    b, t, _ = q.shape
    seq = lambda width: pl.BlockSpec((seqs, tb, width), lambda i, j: (i, j, 0))
    state = pl.BlockSpec((seqs, C_HEADS, C_DK_HEAD, C_DV_HEAD), lambda i, j: (i, 0, 0, 0))
    return pl.pallas_call(
        functools.partial(_gla_kernel, chunk, tb // chunk),
        grid=(b // seqs, t // tb),
        in_specs=[seq(C_DK), seq(C_DK), seq(C_DV), seq(C_DV), seq(C_DK), state,
                  _full((1, C_DV_HEAD))],
        out_specs=[seq(C_DV), state],
        out_shape=[jax.ShapeDtypeStruct((b, t, C_DV), BF16),
                   jax.ShapeDtypeStruct(s0.shape, F32)],
        scratch_shapes=[pltpu.VMEM((seqs, tb, C_DK), BF16)] * 6
                       + [pltpu.VMEM((seqs, tb // chunk, C_DK, LANES), F32)],
        compiler_params=_params("parallel", "arbitrary"),
        name="gla",
    )(q, k, v, r, la, s0, gain)


GLA_CHUNK = 128
SAMPLE_PAD = 16
SAMPLE_SEQS = 4


ROW_TILE = 512
FFN_TILE = ROW_TILE
EVEN_TILE = 1024
CONV_TILE = 512
GLA_TILE = 512


def _even_mixer_prompt(q, k, v, u, P, i, bsz, t):
    n = bsz * t
    a = _attn_prompt(q, k, v, P['band_vectors'], bsz, t)
    keep = min(MAX_WINDOW, t)
    tail = lambda z: z.reshape(bsz, t, A_WIDTH)[:, t - keep:].reshape(bsz, keep, A_HEADS, A_HEAD_DIM)
    u3 = u.reshape(bsz, t, CONV_CH)
    hist = jnp.zeros((bsz, CONV_PAD, CONV_CH), F32)
    c = _conv(u3, hist, P['ev_conv_w'][i], P['ev_conv_b'][i], P['ev_conv_ln_g'][i],
              P['ev_conv_ln_b'][i], CONV_TILE, 1).reshape(n, CONV_CH)
    return [a, c], (tail(k), tail(v), u3[:, t - (CONV_WIDTH - 1):])


def _even_mixer_sample(q, k, v, u, past, P, i, bsz, t):
    n = bsz * t
    n_buf = past[0].shape[2]
    assert n_buf == MAX_WINDOW, "the sample kernel expects a full window buffer"
    major = lambda z: z.transpose(0, 1, 3, 4, 2).reshape(z.shape[0] * bsz, A_WIDTH, n_buf)
    minor = lambda z: z.reshape(bsz, A_HEADS, A_HEAD_DIM, n_buf).transpose(0, 3, 1, 2)
    a, new_k, new_v = _attn_sample(
        q.reshape(bsz, t, A_WIDTH), k.reshape(bsz, t, A_WIDTH), v.reshape(bsz, t, A_WIDTH),
        major(past[0]), major(past[1]), i, P['decode_tables'])
    u3 = u.reshape(bsz, t, CONV_CH)
    hist = jnp.pad(past[2][i], ((0, 0), (CONV_PAD - (CONV_WIDTH - 1), 0), (0, 0)))
    c = _conv(u3, hist, P['ev_conv_w'][i], P['ev_conv_b'][i], P['ev_conv_ln_g'][i],
              P['ev_conv_ln_b'][i], t, SAMPLE_SEQS).reshape(n, CONV_CH)
    new_u = jnp.concatenate([past[2][i], u3], axis=1)[:, -(CONV_WIDTH - 1):]
    return [a.reshape(n, A_WIDTH), c], (minor(new_k), minor(new_v), new_u)


def _gla_mixer(q, k, v, r, la, s0, gain, bsz, t, t_pad, chunk, tile, seqs):
    seq = lambda z: jnp.pad(z.reshape(bsz, t, -1), ((0, 0), (0, t_pad - t), (0, 0)))
    o, s = _gla(seq(q), seq(k), seq(v), seq(r), seq(la), s0, gain, chunk, tile, seqs)
    return [o[:, :t].reshape(bsz * t, C_DV)], s


def _trunks(x_p, x_s, past, P):
    (b_p, t_p, _), (b_s, t_s, _) = x_p.shape, x_s.shape
    x_p = x_p.reshape(b_p * t_p, D_MODEL)
    x_s = x_s.reshape(b_s * t_s, D_MODEL)
    new_p = {'k': [], 'v': [], 'u': [], 's': []}
    new_s = {'k': [], 'v': [], 'u': [], 's': []}
    for layer in range(DEPTH):
        i = layer // 2
        x_p, x_s = _ffn(x_p, x_s, P['norm_ffn1'][layer], (P['ffn1_w_in'], layer),
                        (P['ffn1_w_out'], layer), FFN_TILE)
        if layer % 2 == 0:
            proj_p, proj_s = _even_in(x_p, x_s, P['norm_mix'][layer], (P['ev_w_in'], i),
                                      P['ev_q_gain'][i], P['ev_k_gain'][i], P['head_ones'], EVEN_TILE)
            mix_p, kvu_p = _even_mixer_prompt(*proj_p, P, i, b_p, t_p)
            mix_s, kvu_s = _even_mixer_sample(*proj_s, past, P, i, b_s, t_s)
            w_mix = [(P['ev_w_out'], i, 0, 2), (P['ev_w_out'], i, 1, 2)]
            for new, kvu in ((new_p, kvu_p), (new_s, kvu_s)):
                for name, z in zip('kvu', kvu):
                    new[name].append(z)
        else:
            proj_p, proj_s = _gla_in(x_p, x_s, P['norm_mix'][layer], (P['od_w_in'], i),
                                     P['od_w_low'][i], P['od_gate_w_up'][i], P['od_gate_b'][i],
                                     ROW_TILE)
            zeros = jnp.zeros((b_p, C_HEADS, C_DK_HEAD, C_DV_HEAD), F32)
            gain = P['od_o_gain'][i]
            mix_p, s_p = _gla_mixer(*proj_p, zeros, gain, b_p, t_p, t_p, GLA_CHUNK, GLA_TILE, 1)
            mix_s, s_s = _gla_mixer(*proj_s, past[3][i], gain, b_s, t_s, SAMPLE_PAD, SAMPLE_PAD,
                                    SAMPLE_PAD, SAMPLE_SEQS)
            w_mix = [(P['od_w_out'], i)]
            new_p['s'].append(s_p)
            new_s['s'].append(s_s)
        x_p, x_s = _ffn(x_p, x_s, P['norm_ffn2'][layer], (P['ffn2_w_in'], layer),
                        (P['ffn2_w_out'], layer), FFN_TILE, mix_p, mix_s, w_mix)
    stacked = lambda new: tuple(jnp.stack(new[name]) for name in 'kvus')
    return (x_p.reshape(b_p, t_p, D_MODEL), x_s.reshape(b_s, t_s, D_MODEL),
            *stacked(new_p), *stacked(new_s))


def kernel(x_prompt, x_sample, cache_k, cache_v, cache_conv, state_gla, rel_bias, norm_ffn1, ffn1_w_in, ffn1_w_out, norm_mix, norm_ffn2, ffn2_w_in, ffn2_w_out, ev_w_in, ev_q_gain, ev_k_gain, ev_conv_w, ev_conv_b, ev_conv_ln_g, ev_conv_ln_b, ev_w_out, od_w_in, od_gate_w_up, od_gate_b, od_o_gain, od_w_out):
    n_even = ev_w_in.shape[0]
    n_odd = od_w_in.shape[0]
    main = 2 * C_DK + 2 * C_DV
    head_ids = np.arange(MXU_DIM) // A_HEAD_DIM
    per = lambda n, f: [f(j) for j in range(n)]
    row = lambda z: z[None, :]
    P = {
        'norm_ffn1': per(DEPTH, lambda j: row(norm_ffn1[j])),
        'norm_mix': per(DEPTH, lambda j: row(norm_mix[j])),
        'norm_ffn2': per(DEPTH, lambda j: row(norm_ffn2[j])),
        'ffn1_w_in': ffn1_w_in, 'ffn1_w_out': ffn1_w_out,
        'ffn2_w_in': ffn2_w_in, 'ffn2_w_out': ffn2_w_out,
        'ev_w_in': ev_w_in, 'ev_w_out': ev_w_out, 'od_w_out': od_w_out,
        'od_w_in': od_w_in.transpose(0, 2, 1),
        'ev_q_gain': per(n_even, lambda j: row(jnp.tile(ev_q_gain[j], A_HEADS))),
        'ev_k_gain': per(n_even, lambda j: row(jnp.tile(ev_k_gain[j], A_HEADS))),
        'head_ones': jnp.asarray(head_ids[:, None] == head_ids[None, :], BF16),
        'ev_conv_w': per(n_even, lambda j: jnp.broadcast_to(
            ev_conv_w[j][:, None, :], (CONV_WIDTH, SUBLANES, CONV_CH))),
        'ev_conv_b': per(n_even, lambda j: row(ev_conv_b[j])),
        'ev_conv_ln_g': per(n_even, lambda j: row(ev_conv_ln_g[j])),
        'ev_conv_ln_b': per(n_even, lambda j: row(ev_conv_ln_b[j])),
        'od_w_low': per(n_odd, lambda j: jnp.pad(od_w_in[j, :, main:],
                                                 ((0, 0), (0, LANES - GATE_RANK))).astype(BF16)),
        'od_gate_w_up': per(n_odd, lambda j: jnp.pad(od_gate_w_up[j],
                                                     ((0, LANES - GATE_RANK), (0, 0))).astype(BF16)),
        'od_gate_b': per(n_odd, lambda j: row(od_gate_b[j])),
        'od_o_gain': per(n_odd, lambda j: row(od_o_gain[j])),
        'band_vectors': _band_vectors(rel_bias),
        'decode_tables': _decode_tables(rel_bias, x_sample.shape[1]),
    }
    P, x_prompt, x_sample = lax.optimization_barrier((P, x_prompt, x_sample))
    return _trunks(x_prompt, x_sample, (cache_k, cache_v, cache_conv, state_gla), P)
```

```python
import functools

import numpy as np
import jax
import jax.numpy as jnp
from jax import lax
from jax.experimental import pallas as pl
from jax.experimental.pallas import tpu as pltpu

F32 = jnp.float32
BF16 = jnp.bfloat16

D_MODEL = 1024
DEPTH = 2
A_HEADS = 8
A_HEAD_DIM = 64
A_WIDTH = A_HEADS * A_HEAD_DIM
DILATED_GROUPS = ((128, 1), (512, 4), (2048, 16))
MAX_WINDOW = 2048
N_BUCKETS = 32
CONV_WIDTH = 31
CONV_CH = 512
C_HEADS = 4
C_DK = 512
C_DV = 1024
C_DK_HEAD = 128
C_DV_HEAD = 256
GATE_RANK = 16
GATE_TAU = 16.0
D_FF = 2816
EPS = 1e-6
NEG_INF = -1e30
LOG2E = 1.4426950408889634

LANES = 128
SUBLANES = 8
MXU_DIM = 256
WIN_KEYS = 128
ATT_BLOCK = 2048
ATT_UNROLL = 16
ATT_PHASES = 4
VMEM_LIMIT = 60 * 1024 * 1024


def _params(*sem):
    return pltpu.CompilerParams(dimension_semantics=sem, vmem_limit_bytes=VMEM_LIMIT)


def _dot(a, b):
    return jnp.dot(a, b, preferred_element_type=F32)


def _dot_nt(a, b):
    return lax.dot_general(a, b, (((1,), (1,)), ((), ())), preferred_element_type=F32)


def _dot_tn(a, b):
    return lax.dot_general(a, b, (((0,), (0,)), ((), ())), preferred_element_type=F32)


def _rms_rows(x, g):
    y = x * lax.rsqrt(jnp.mean(x * x, axis=-1, keepdims=True) + EPS)
    return y * g


def _sigmoid(x):
    return 1.0 / (1.0 + jnp.exp(-x))


def _full(shape):
    return pl.BlockSpec(shape, lambda *_: (0,) * len(shape), pipeline_mode=pl.Buffered(1))


FF_CHUNK = 256
FF_DOWN_CHUNK = 1024


def _layer(w, layer, row_block=0, n_blocks=1):
    rows = w.shape[1] // n_blocks
    return pl.BlockSpec((None, rows, w.shape[2]), lambda *_: (layer, row_block, 0),
                        pipeline_mode=pl.Buffered(1))


def _rows_call(body, prompt_rows, sample_rows, consts, out_widths, tm, name, scratch=(), hbm=()):
    n_p, n_s = prompt_rows[0].shape[0], sample_rows[0].shape[0]
    steps = n_p // tm
    n_in, n_c, n_out = len(prompt_rows), len(consts) + len(hbm), len(out_widths)

    def kernel(*refs):
        p_in, s_in = refs[:n_in], refs[n_in:2 * n_in]
        c_refs = refs[2 * n_in:2 * n_in + n_c]
        outs = refs[2 * n_in + n_c:]
        p_out, s_out, scr = outs[:n_out], outs[n_out:2 * n_out], outs[2 * n_out:]
        step = pl.program_id(0)

        if hbm:
            @pl.when(step == 0)
            def _():
                body(p_in, c_refs, p_out, scr, tm, True)

        @pl.when(jnp.logical_and(step >= (1 if hbm else 0), step < steps))
        def _():
            body(p_in, c_refs, p_out, scr, tm, False)

        @pl.when(step == steps)
        def _():
            body(s_in, c_refs, s_out, scr, n_s, False)

    p_spec = lambda width: pl.BlockSpec((tm, width), lambda i: (jnp.minimum(i, steps - 1), 0))
    s_spec = lambda width: pl.BlockSpec((n_s, width), lambda i: (0, 0))
    res = pl.pallas_call(
        kernel,
        grid=(steps + 1,),
        in_specs=[p_spec(a.shape[1]) for a in prompt_rows] + [s_spec(a.shape[1]) for a in sample_rows]
                 + [_layer(*c) if isinstance(c, tuple) else _full(c.shape) for c in consts]
                 + [pl.BlockSpec(memory_space=pl.ANY) for _ in hbm],
        out_specs=[p_spec(w) for w in out_widths] + [s_spec(w) for w in out_widths],
        out_shape=[jax.ShapeDtypeStruct((n_p, w), F32) for w in out_widths]
                  + [jax.ShapeDtypeStruct((n_s, w), F32) for w in out_widths],
        scratch_shapes=list(scratch),
        compiler_params=_params("arbitrary"),
        name=name,
    )(*prompt_rows, *sample_rows, *[c[0] if isinstance(c, tuple) else c for c in consts], *hbm)
    return res[:n_out], res[n_out:]


def _ffn_body(layer, ins, consts, outs, scratch, rows, first):
    x_ref, m_refs = ins[0], ins[1:]
    w_refs, (g_ref, wi_hbm, wo_hbm) = consts[:len(m_refs)], consts[len(m_refs):]
    (o_ref,), (act_ref, wi_ref, wo_ref, sems) = outs, scratch
    up_chunks = range(D_FF // FF_CHUNK)
    down_chunks = [(lo, min(lo + FF_DOWN_CHUNK, D_FF)) for lo in range(0, D_FF, FF_DOWN_CHUNK)]

    def up_copies(c):
        cols = [pl.ds(half * D_FF + c * FF_CHUNK, FF_CHUNK) for half in range(2)]
        return [pltpu.make_async_copy(wi_hbm.at[layer, :, cs], wi_ref.at[:, cs], sems.at[c])
                for cs in cols]

    def down_copy(j):
        rows_j = pl.ds(down_chunks[j][0], down_chunks[j][1] - down_chunks[j][0])
        return pltpu.make_async_copy(wo_hbm.at[layer, rows_j], wo_ref.at[rows_j],
                                     sems.at[len(up_chunks) + j])

    def on_first(copies, act):
        if first:
            for copy in copies():
                act(copy)

    every = lambda: ([cp for c in up_chunks for cp in up_copies(c)]
                     + [down_copy(j) for j in range(len(down_chunks))])
    on_first(every, lambda copy: copy.start())
    x = x_ref[...]
    for m_ref, w_ref in zip(m_refs, w_refs):
        x = x + _dot(m_ref[...], w_ref[...].astype(BF16))
    h = _rms_rows(x, g_ref[...]).astype(BF16)
    for c in up_chunks:
        on_first(lambda c=c: up_copies(c), lambda copy: copy.wait())
        lo = c * FF_CHUNK
        a = _dot(h, wi_ref[:, lo:lo + FF_CHUNK].astype(BF16))
        b = _dot(h, wi_ref[:, D_FF + lo:D_FF + lo + FF_CHUNK].astype(BF16))
        act_ref[0:rows, lo:lo + FF_CHUNK] = (a * _sigmoid(a) * b).astype(BF16)
    y = jnp.zeros_like(x)
    for j, (lo, hi) in enumerate(down_chunks):
        on_first(lambda j=j: [down_copy(j)], lambda copy: copy.wait())
        y = y + _dot(act_ref[0:rows, lo:hi], wo_ref[lo:hi, :].astype(BF16))
    o_ref[...] = x + 0.5 * y


def _ffn(x_p, x_s, g, wi, wo, layer, tm, ms_p=(), ms_s=(), ws=()):
    n_sems = D_FF // FF_CHUNK + -(-D_FF // FF_DOWN_CHUNK)
    scratch = [pltpu.VMEM((tm, D_FF), BF16), pltpu.VMEM(wi.shape[1:], F32),
               pltpu.VMEM(wo.shape[1:], F32), pltpu.SemaphoreType.DMA((n_sems,))]
    (y_p,), (y_s,) = _rows_call(functools.partial(_ffn_body, layer), [x_p, *ms_p], [x_s, *ms_s],
                                [*ws, g], [D_MODEL], tm, "ffn", scratch, hbm=[wi, wo])
    return y_p, y_s


def _head_norm(z, gain, bd):
    zz = z * z
    hi = zz.astype(BF16)
    lo = (zz - hi.astype(F32)).astype(BF16)
    width = bd.shape[0]
    ss = jnp.concatenate(
        [_dot(hi[:, c:c + width], bd) + _dot(lo[:, c:c + width], bd)
         for c in range(0, z.shape[1], width)], axis=1)
    return z * lax.rsqrt(ss * (1.0 / A_HEAD_DIM) + EPS) * gain


def _even_in_body(ins, consts, outs, scratch, rows, first):
    (x_ref,), (g_ref, w_ref, qg_ref, kg_ref, bd_ref), (q_ref, k_ref, v_ref, u_ref) = ins, consts, outs
    h = _rms_rows(x_ref[...], g_ref[...]).astype(BF16)
    proj = lambda lo, width: _dot(h, w_ref[:, lo:lo + width].astype(BF16))
    bd = bd_ref[...]
    q_ref[...] = _head_norm(proj(0, A_WIDTH), qg_ref[...], bd) * (A_HEAD_DIM ** -0.5 * LOG2E)
    k_ref[...] = _head_norm(proj(A_WIDTH, A_WIDTH), kg_ref[...], bd)
    v_ref[...] = proj(2 * A_WIDTH, A_WIDTH)
    gv = proj(3 * A_WIDTH, CONV_CH)
    gg = proj(3 * A_WIDTH + CONV_CH, CONV_CH)
    u_ref[...] = gv * _sigmoid(gg)


def _even_in(x_p, x_s, g, w, qg, kg, bd, tm):
    return _rows_call(_even_in_body, [x_p], [x_s], [g, w, qg, kg, bd], [A_WIDTH] * 4, tm, "even_in")


def _t5_bucket(dist):
    max_exact = N_BUCKETS // 2
    d = np.asarray(dist, dtype=np.int32)
    df = np.maximum(d, 1).astype(np.float32)
    large = max_exact + (np.log(df / max_exact) / np.log(MAX_WINDOW / max_exact)
                         * (N_BUCKETS - max_exact)).astype(np.int32)
    large = np.minimum(large, N_BUCKETS - 1)
    return np.where(d < max_exact, d, large).astype(np.int32)


def _select_bias(rel_bias, dist, valid):
    onehot = (_t5_bucket(dist)[None, :] == np.arange(N_BUCKETS)[:, None]) & valid[None, :]
    picked = jnp.einsum('bh,bc->hc', rel_bias, jnp.asarray(onehot, F32),
                        precision=lax.Precision.HIGHEST)
    return picked * LOG2E + jnp.asarray(np.where(valid, 0.0, NEG_INF), F32)[None, :]


def _band_vectors(rel_bias):
    c = np.arange(2 * WIN_KEYS)
    valid = c <= WIN_KEYS
    vecs = [_select_bias(rel_bias, np.where(valid, (WIN_KEYS - c) * dil, 0), valid)
            for _, dil in DILATED_GROUPS]
    return jnp.stack(vecs).reshape(len(DILATED_GROUPS), A_HEADS // 2, 2, 2 * WIN_KEYS)


def _attn_prompt_kernel(q_ref, kp_ref, kc_ref, vp_ref, vc_ref, vec_ref, o_ref,
                        knat, vnat, k4, v4, q4, og1, lg1, og4, lg4, tab_ref):
    blk = pl.program_id(2)
    n_groups = len(DILATED_GROUPS)
    nph = ATT_PHASES
    per = ATT_BLOCK // nph

    @pl.when(blk == 0)
    def _():
        col = lax.broadcasted_iota(jnp.int32, (WIN_KEYS, 2 * WIN_KEYS), 1)
        for g in range(n_groups):
            for hh in range(2):
                vec = jnp.broadcast_to(vec_ref[g, 0, hh:hh + 1, :], (WIN_KEYS, 2 * WIN_KEYS))
                band = pltpu.roll(vec, 0, 1, stride=1, stride_axis=0)
                rows = slice(hh * WIN_KEYS, (hh + 1) * WIN_KEYS)
                tab_ref[g, rows, :] = band
                tab_ref[n_groups + g, rows, :] = jnp.where(col >= WIN_KEYS, band, NEG_INF)

    knat[0:WIN_KEYS, :] = kp_ref[ATT_BLOCK - WIN_KEYS:, :]
    knat[WIN_KEYS:, :] = kc_ref[...]
    vnat[0:WIN_KEYS, :] = vp_ref[ATT_BLOCK - WIN_KEYS:, :]
    vnat[WIN_KEYS:, :] = vc_ref[...]
    for r in range(nph):
        phase = pl.ds(r, per, stride=nph)
        k4[r, 0:per, :] = kp_ref[phase, :]
        k4[r, per:, :] = kc_ref[phase, :]
        v4[r, 0:per, :] = vp_ref[phase, :]
        v4[r, per:, :] = vc_ref[phase, :]
        q4[r] = q_ref[phase, :]
    first = blk == 0
    lane = lax.broadcasted_iota(jnp.int32, (WIN_KEYS, LANES), 1)
    low = lane < A_HEAD_DIM
    ones = jnp.ones((2 * WIN_KEYS, LANES), BF16)

    def block(qs, kk, vv, tab):
        qs = qs.astype(BF16)
        zero = jnp.zeros_like(qs)
        qst = jnp.concatenate([jnp.where(low, qs, zero), jnp.where(low, zero, qs)], axis=0)
        s = _dot_nt(qst, kk.astype(BF16)) + tab
        mx = jnp.max(s, axis=-1, keepdims=True)
        p = jnp.exp2(s - mx).astype(BF16)
        r = _dot(p, jnp.concatenate([vv.astype(BF16), ones], axis=1))
        o2 = jnp.where(low, r[0:WIN_KEYS, 0:LANES], r[WIN_KEYS:, 0:LANES])
        l2 = jnp.where(low, r[0:WIN_KEYS, LANES:], r[WIN_KEYS:, LANES:])
        m2 = jnp.where(low, jnp.broadcast_to(mx[0:WIN_KEYS], (WIN_KEYS, LANES)),
                       jnp.broadcast_to(mx[WIN_KEYS:], (WIN_KEYS, LANES)))
        return o2 / l2, m2 + jnp.log(l2) * LOG2E

    def table(g, at_start):
        return tab_ref[jnp.where(jnp.logical_and(at_start, first), n_groups + g, g)]

    def body1(sub, carry):
        i0 = pl.multiple_of(sub * WIN_KEYS, WIN_KEYS)
        o, l = block(q_ref[pl.ds(i0, WIN_KEYS), :], knat[pl.ds(i0, 2 * WIN_KEYS), :],
                     vnat[pl.ds(i0, 2 * WIN_KEYS), :], table(0, sub == 0))
        og1[pl.ds(i0, WIN_KEYS), :] = o
        lg1[pl.ds(i0, WIN_KEYS), :] = l
        return carry

    def body2(pb, carry):
        sub = pb // nph
        r = pb - sub * nph
        i0 = pl.multiple_of(sub * WIN_KEYS, WIN_KEYS)
        keys = pl.ds(i0 + (per - WIN_KEYS), 2 * WIN_KEYS)
        o, l = block(q4[r, pl.ds(i0, WIN_KEYS), :], k4[r, keys, :], v4[r, keys, :],
                     table(1, sub == 0))
        og4[0, r, pl.ds(i0, WIN_KEYS), :] = o
        lg4[0, r, pl.ds(i0, WIN_KEYS), :] = l
        return carry

    def body3(pb, carry):
        a = pb // nph
        r = pb - a * nph
        rows = pl.ds(a, WIN_KEYS, stride=nph)
        keys = pl.ds(a, 2 * WIN_KEYS, stride=nph)
        o, l = block(q4[r, rows, :], k4[r, keys, :], v4[r, keys, :], table(2, True))
        og4[1, r, rows, :] = o
        lg4[1, r, rows, :] = l
        return carry

    n_blocks = ATT_BLOCK // WIN_KEYS
    for body in (body1, body2, body3):
        lax.fori_loop(0, n_blocks, body, 0, unroll=ATT_UNROLL)

    for r in range(nph):
        phase = pl.ds(r, per, stride=nph)
        la, lb, lc = lg1[phase, :], lg4[0, r], lg4[1, r]
        mx = jnp.maximum(jnp.maximum(la, lb), lc)
        wa, wb, wc = jnp.exp2(la - mx), jnp.exp2(lb - mx), jnp.exp2(lc - mx)
        og1[phase, :] = (wa * og1[phase, :] + wb * og4[0, r] + wc * og4[1, r]) / (wa + wb + wc)
    o_ref[...] = og1[...].astype(o_ref.dtype)


def _attn_prompt(q, k, v, vecs, batch, seq):
    assert [d for _, d in DILATED_GROUPS] == [1, ATT_PHASES, ATT_PHASES ** 2]
    nb = seq // ATT_BLOCK
    n_groups = len(DILATED_GROUPS)
    per = ATT_BLOCK // ATT_PHASES
    cur = lambda b, p, t: (b * nb + t, p)
    prev = lambda b, p, t: (b * nb + jnp.maximum(t - 1, 0), p)
    blk = lambda imap: pl.BlockSpec((ATT_BLOCK, LANES), imap)
    vmem = lambda *shape: pltpu.VMEM(shape, F32)
    return pl.pallas_call(
        _attn_prompt_kernel,
        grid=(batch, A_HEADS // 2, nb),
        in_specs=[blk(cur), blk(prev), blk(cur), blk(prev), blk(cur),
                  pl.BlockSpec((n_groups, 1, 2, 2 * WIN_KEYS), lambda b, p, t: (0, p, 0, 0))],
        out_specs=blk(cur),
        out_shape=jax.ShapeDtypeStruct((batch * seq, A_WIDTH), BF16),
        scratch_shapes=[vmem(WIN_KEYS + ATT_BLOCK, LANES), vmem(WIN_KEYS + ATT_BLOCK, LANES),
                        vmem(ATT_PHASES, 2 * per, LANES), vmem(ATT_PHASES, 2 * per, LANES),
                        vmem(ATT_PHASES, per, LANES),
                        vmem(ATT_BLOCK, LANES), vmem(ATT_BLOCK, LANES),
                        vmem(2, ATT_PHASES, per, LANES), vmem(2, ATT_PHASES, per, LANES),
                        vmem(2 * n_groups, 2 * WIN_KEYS, 2 * WIN_KEYS)],
        compiler_params=_params("arbitrary", "arbitrary", "arbitrary"),
        name="attn_prompt",
    )(q, k, k, v, v, vecs)


def _decode_tables(rel_bias, n_new):
    cols = MAX_WINDOW + n_new
    c = np.arange(cols)
    dist = MAX_WINDOW - c
    cnt = np.zeros(c.shape, np.float32)
    for window, dil in DILATED_GROUPS:
        cnt += ((dist >= 0) & (dist <= window) & (dist % dil == 0)).astype(np.float32)
    vec = _select_bias(rel_bias, np.clip(dist, 0, MAX_WINDOW), cnt > 0)
    vec = vec + jnp.asarray(np.log2(np.maximum(cnt, 1.0)), F32)[None, :]
    rows = jnp.stack([jnp.pad(vec[:, :cols - i], ((0, 0), (i, 0)), constant_values=NEG_INF)
                      for i in range(n_new)], axis=1)
    rows = rows.reshape(A_HEADS * n_new, cols)
    return rows[:, :MAX_WINDOW], rows[:, MAX_WINDOW:]


def _attn_sample_kernel(q_ref, kn_ref, vn_ref, kc_ref, vc_ref, tc_ref, tn_ref,
                        o_ref, ko_ref, vo_ref):
    n_new = q_ref.shape[1]
    n_buf = kc_ref.shape[2]
    rows = A_HEADS * n_new
    kn = kn_ref[0]
    vn = vn_ref[0]
    kc = kc_ref[0]
    vc = vc_ref[0]

    lane = lax.broadcasted_iota(jnp.int32, (A_WIDTH, LANES), 1)
    for new, old, out_ref in ((kn, kc, ko_ref), (vn, vc, vo_ref)):
        shifted = pltpu.roll(old, n_buf - n_new, 1)
        tail = jnp.concatenate([jnp.zeros((LANES - n_new, A_WIDTH), F32), new], axis=0).T
        out_ref[0, :, 0:n_buf - LANES] = shifted[:, 0:n_buf - LANES]
        out_ref[0, :, n_buf - LANES:] = jnp.where(lane >= LANES - n_new, tail,
                                                  shifted[:, n_buf - LANES:])

    q = q_ref[0]
    row_head = lax.broadcasted_iota(jnp.int32, (A_HEADS, n_new, A_WIDTH), 0).reshape(rows, A_WIDTH)
    col = lax.broadcasted_iota(jnp.int32, (rows, A_WIDTH), 1)
    own = jnp.logical_and(col >= row_head * A_HEAD_DIM, col < (row_head + 1) * A_HEAD_DIM)
    qblk = jnp.where(own, jnp.concatenate([q] * A_HEADS, axis=0), 0.0).astype(BF16)
    s_c = _dot(qblk, kc.astype(BF16)) + tc_ref[...]
    s_n = _dot_nt(qblk, kn.astype(BF16)) + tn_ref[...]
    mx = jnp.maximum(jnp.max(s_c, axis=-1, keepdims=True), jnp.max(s_n, axis=-1, keepdims=True))
    p_c = jnp.exp2(s_c - mx)
    p_n = jnp.exp2(s_n - mx)
    den = jnp.sum(p_c, axis=-1, keepdims=True) + jnp.sum(p_n, axis=-1, keepdims=True)
    acc = _dot_nt(p_c.astype(BF16), vc.astype(BF16)) + _dot(p_n.astype(BF16), vn.astype(BF16))
    acc = jnp.where(own, acc / den, 0.0)
    out = acc[0:n_new]
    for h in range(1, A_HEADS):
        out = out + acc[h * n_new:(h + 1) * n_new]
    o_ref[0] = out.astype(o_ref.dtype)


def _attn_sample(q, k_new, v_new, cache_k, cache_v, layer, tables):
    b, n_new, _ = q.shape
    n_buf = cache_k.shape[2]
    new = pl.BlockSpec((1, n_new, A_WIDTH), lambda i: (i, 0, 0))
    buf = pl.BlockSpec((1, A_WIDTH, n_buf), lambda i: (i, 0, 0))
    past = pl.BlockSpec((1, A_WIDTH, n_buf), lambda i: (layer * b + i, 0, 0))
    return pl.pallas_call(
        _attn_sample_kernel,
        grid=(b,),
        in_specs=[new, new, new, past, past, _full(tables[0].shape), _full(tables[1].shape)],
        out_specs=[new, buf, buf],
        out_shape=[jax.ShapeDtypeStruct((b, n_new, A_WIDTH), BF16),
                   jax.ShapeDtypeStruct((b, A_WIDTH, n_buf), F32),
                   jax.ShapeDtypeStruct((b, A_WIDTH, n_buf), F32)],
        compiler_params=_params("parallel"),
        name="attn_sample",
    )(q, k_new, v_new, cache_k, cache_v, *tables)


CONV_PAD = 32
CONV_ROWS = 32
CONV_UNROLL = 2


def _conv_kernel(tc, u_ref, up_ref, hist_ref, w_ref, b_ref, g_ref, beta_ref, o_ref, win, stage):
    t = pl.program_id(1)
    n_seq = u_ref.shape[0]
    n_slab = CONV_CH // LANES
    slab = lambda c: slice(c * LANES, (c + 1) * LANES)
    off = CONV_PAD - (CONV_WIDTH - 1)
    rc = min(CONV_ROWS, tc)
    half = rc // 2

    def tap(k, c):
        w = w_ref[k, :, slab(c)]
        if half < SUBLANES:
            return w[0:half]
        return jnp.concatenate([w] * (half // SUBLANES), axis=0)

    for s in range(n_seq):
        first = s * n_slab
        for c in range(n_slab):
            win[first + c, CONV_PAD:CONV_PAD + tc, :] = u_ref[s, :, slab(c)]

        @pl.when(t == 0)
        def _():
            for c in range(n_slab):
                win[first + c, 0:CONV_PAD, :] = hist_ref[s, :, slab(c)]

        @pl.when(t > 0)
        def _():
            for c in range(n_slab):
                win[first + c, 0:CONV_PAD, :] = up_ref[s, :, slab(c)]

        def body(j, carry):
            r0 = j * rc
            for c in range(n_slab):
                for par in range(2):
                    acc = jnp.zeros((half, LANES), F32) + b_ref[:, slab(c)]
                    for k in range(CONV_WIDTH):
                        rows = pl.ds(r0 + off + k + par, half, stride=2)
                        acc = acc + win[first + c, rows, :] * tap(k, c)
                    stage[first + c, pl.ds(r0 + par, half, stride=2), :] = acc
            return carry

        if tc == rc:
            body(0, 0)
        else:
            lax.fori_loop(0, tc // rc, body, 0, unroll=CONV_UNROLL)
        y = jnp.concatenate([stage[first + c] for c in range(n_slab)], axis=1)
        xc = y - jnp.mean(y, axis=-1, keepdims=True)
        y = xc * lax.rsqrt(jnp.mean(xc * xc, axis=-1, keepdims=True) + EPS)
        y = y * g_ref[...] + beta_ref[...]
        o_ref[s] = (y * _sigmoid(y)).astype(o_ref.dtype)


def _conv(u, hist, w, b, g, beta, tc, seqs):
    bsz, t, _ = u.shape
    per = tc // CONV_PAD
    if t >= CONV_PAD:
        prev = pl.BlockSpec((seqs, CONV_PAD, CONV_CH),
                            lambda i, j: (i, jnp.maximum(j * per - 1, 0), 0))
        u_prev = u
    else:
        prev = pl.BlockSpec((seqs, CONV_PAD, CONV_CH), lambda i, j: (i, 0, 0))
        u_prev = hist
    n_slab = seqs * CONV_CH // LANES
    return pl.pallas_call(
        functools.partial(_conv_kernel, tc),
        grid=(bsz // seqs, t // tc),
        in_specs=[pl.BlockSpec((seqs, tc, CONV_CH), lambda i, j: (i, j, 0)),
                  prev,
                  pl.BlockSpec((seqs, CONV_PAD, CONV_CH), lambda i, j: (i, 0, 0)),
                  _full(w.shape), _full((1, CONV_CH)), _full((1, CONV_CH)),
                  _full((1, CONV_CH))],
        out_specs=pl.BlockSpec((seqs, tc, CONV_CH), lambda i, j: (i, j, 0)),
        out_shape=jax.ShapeDtypeStruct((bsz, t, CONV_CH), BF16),
        scratch_shapes=[pltpu.VMEM((n_slab, CONV_PAD + tc, LANES), F32),
                        pltpu.VMEM((n_slab, tc, LANES), F32)],
        compiler_params=_params("parallel", "arbitrary"),
        name="conv",
    )(u, u_prev, hist, w, b, g, beta)


def _gla_in_body(ins, consts, outs, scratch, rows, first):
    (x_ref,), (g_ref, w_ref, wl_ref, wu_ref, bu_ref) = ins, consts
    q_ref, k_ref, v_ref, r_ref, la_ref = outs
    h = _rms_rows(x_ref[...], g_ref[...]).astype(BF16)
    proj = lambda lo, width: _dot_nt(h, w_ref[lo:lo + width, :].astype(BF16))
    q_ref[...] = proj(0, C_DK) * (C_DK_HEAD ** -0.5)
    k_ref[...] = proj(C_DK, C_DK)
    v_ref[...] = proj(2 * C_DK, C_DV)
    r = proj(2 * C_DK + C_DV, C_DV)
    r_ref[...] = r * _sigmoid(r)
    low = _dot(h, wl_ref[...]).astype(BF16)
    z = _dot(low, wu_ref[...]) + bu_ref[...]
    log_sig = jnp.minimum(z, 0.0) - jnp.log1p(jnp.exp(-jnp.abs(z)))
    la_ref[...] = log_sig * (1.0 / GATE_TAU)


def _gla_in(x_p, x_s, g, w, wl, wu, bu, tm):
    return _rows_call(_gla_in_body, [x_p], [x_s], [g, w, wl, wu, bu],
                      [C_DK, C_DK, C_DV, C_DV, C_DK], tm, "gla_in")


def _gla_kernel(chunk, n_chunks, q_ref, k_ref, v_ref, r_ref, la_ref, s0_ref, gain_ref,
                o_ref, s_ref, qin_s, kin_s, x1_s, x2_s, qst_s, kst_s, dec_s):
    n_seq = q_ref.shape[0]
    @pl.when(pl.program_id(1) == 0)
    def _():
        s_ref[...] = s0_ref[...]

    half, quarter = chunk // 2, chunk // 4
    quarter_of = lambda i: sum((i >= j * quarter).astype(jnp.int32) for j in range(1, 4))
    ri = lax.broadcasted_iota(jnp.int32, (chunk, chunk), 0)
    ci = lax.broadcasted_iota(jnp.int32, (chunk, chunk), 1)
    causal = ci <= ri
    tri = jnp.where(causal, 1.0, 0.0).astype(BF16)
    rq, cq = quarter_of(ri), quarter_of(ci)
    same_quarter = jnp.logical_and(causal, rq == cq)
    cross_half = jnp.logical_and(ri >= half, ci < half)
    cross_quarter = jnp.logical_and(rq == cq + 1, (ri >= half) == (ci >= half))
    rr = lax.broadcasted_iota(jnp.int32, (chunk, C_DK), 0)
    rrq = quarter_of(rr)
    in_first = rr < half
    key_side_2 = jnp.logical_or(rrq == 0, rrq == 2)
    gain = gain_ref[...]

    def prepare(g, c):
        rows = slice(c * chunk, (c + 1) * chunk)
        la = la_ref[g, rows, :]
        la_hi = la.astype(BF16)
        la_lo = (la - la_hi.astype(F32)).astype(BF16)
        cum = _dot(tri, la_hi) + _dot(tri, la_lo)
        row = lambda i: cum[i:i + 1, :]
        last = row(chunk - 1)
        mids = [row(j * quarter + quarter // 2 - 1) for j in range(4)]
        mid = jnp.where(rrq == 0, mids[0], jnp.where(rrq == 1, mids[1],
                        jnp.where(rrq == 2, mids[2], mids[3])))
        q = q_ref[g, rows, :]
        k = k_ref[g, rows, :]
        grow = jnp.exp(cum - mid)
        qin_s[g, rows, :] = (q * grow).astype(BF16)
        kin_s[g, rows, :] = (k / grow).astype(BF16)

        def across(edge, key_side):
            gap = cum - edge
            return (jnp.where(key_side, k, q) * jnp.exp(jnp.where(key_side, -gap, gap))).astype(BF16)

        x1_s[g, rows, :] = across(row(half - 1), in_first)
        x2_s[g, rows, :] = across(jnp.where(in_first, row(quarter - 1), row(half + quarter - 1)),
                               key_side_2)
        qst_s[g, rows, :] = (q * jnp.exp(cum)).astype(BF16)
        kst_s[g, rows, :] = (k * jnp.exp(last - cum)).astype(BF16)
        dec_s[g, c] = jnp.broadcast_to(jnp.exp(last), (LANES, C_DK)).T

    def advance(g, c):
        rows = slice(c * chunk, (c + 1) * chunk)
        for h in range(C_HEADS):
            ks = slice(h * C_DK_HEAD, (h + 1) * C_DK_HEAD)
            vs = slice(h * C_DV_HEAD, (h + 1) * C_DV_HEAD)
            vh = v_ref[g, rows, vs].astype(BF16)
            x1, x2 = x1_s[g, rows, ks], x2_s[g, rows, ks]
            att = jnp.where(same_quarter, _dot_nt(qin_s[g, rows, ks], kin_s[g, rows, ks]),
                            jnp.where(cross_quarter, _dot_nt(x2, x2),
                                      jnp.where(cross_half, _dot_nt(x1, x1), 0.0)))
            s = s_ref[g, h]
            o = _dot(jnp.concatenate([att.astype(BF16), qst_s[g, rows, ks]], axis=1),
                     jnp.concatenate([vh, s.astype(BF16)], axis=0))
            decay = dec_s[g, c, ks, :]
            s_ref[g, h] = (s * jnp.concatenate([decay] * (C_DV_HEAD // LANES), axis=1)
                           + _dot_tn(kst_s[g, rows, ks], vh))
            y = o * lax.rsqrt(jnp.mean(o * o, axis=-1, keepdims=True) + EPS) * gain
            o_ref[g, rows, vs] = (y * r_ref[g, rows, vs]).astype(o_ref.dtype)

    for g in range(n_seq):
        prepare(g, 0)
    for c in range(n_chunks):
        for g in range(n_seq):
            if c + 1 < n_chunks:
                prepare(g, c + 1)
            advance(g, c)


def _gla(q, k, v, r, la, s0, gain, chunk, tb, seqs):
    b, t, _ = q.shape
    seq = lambda width: pl.BlockSpec((seqs, tb, width), lambda i, j: (i, j, 0))
    state = pl.BlockSpec((seqs, C_HEADS, C_DK_HEAD, C_DV_HEAD), lambda i, j: (i, 0, 0, 0))
    return pl.pallas_call(
        functools.partial(_gla_kernel, chunk, tb // chunk),
        grid=(b // seqs, t // tb),
        in_specs=[seq(C_DK), seq(C_DK), seq(C_DV), seq(C_DV), seq(C_DK), state,
                  _full((1, C_DV_HEAD))],
        out_specs=[seq(C_DV), state],
        out_shape=[jax.ShapeDtypeStruct((b, t, C_DV), BF16),
                   jax.ShapeDtypeStruct(s0.shape, F32)],
        scratch_shapes=[pltpu.VMEM((seqs, tb, C_DK), BF16)] * 6
                       + [pltpu.VMEM((seqs, tb // chunk, C_DK, LANES), F32)],
        compiler_params=_params("parallel", "arbitrary"),
        name="gla",
    )(q, k, v, r, la, s0, gain)


GLA_CHUNK = 128
SAMPLE_PAD = 16
SAMPLE_SEQS = 4


ROW_TILE = 512
FFN_TILE = ROW_TILE
EVEN_TILE = 1024
CONV_TILE = 512
GLA_TILE = 512


def _even_mixer_prompt(q, k, v, u, P, i, bsz, t):
    n = bsz * t
    a = _attn_prompt(q, k, v, P['band_vectors'], bsz, t)
    keep = min(MAX_WINDOW, t)
    tail = lambda z: z.reshape(bsz, t, A_WIDTH)[:, t - keep:].reshape(bsz, keep, A_HEADS, A_HEAD_DIM)
    u3 = u.reshape(bsz, t, CONV_CH)
    hist = jnp.zeros((bsz, CONV_PAD, CONV_CH), F32)
    c = _conv(u3, hist, P['ev_conv_w'][i], P['ev_conv_b'][i], P['ev_conv_ln_g'][i],
              P['ev_conv_ln_b'][i], CONV_TILE, 1).reshape(n, CONV_CH)
    return [a, c], (tail(k), tail(v), u3[:, t - (CONV_WIDTH - 1):])


def _even_mixer_sample(q, k, v, u, past, P, i, bsz, t):
    n = bsz * t
    n_buf = past[0].shape[2]
    assert n_buf == MAX_WINDOW, "the sample kernel expects a full window buffer"
    major = lambda z: z.transpose(0, 1, 3, 4, 2).reshape(z.shape[0] * bsz, A_WIDTH, n_buf)
    minor = lambda z: z.reshape(bsz, A_HEADS, A_HEAD_DIM, n_buf).transpose(0, 3, 1, 2)
    a, new_k, new_v = _attn_sample(
        q.reshape(bsz, t, A_WIDTH), k.reshape(bsz, t, A_WIDTH), v.reshape(bsz, t, A_WIDTH),
        major(past[0]), major(past[1]), i, P['decode_tables'])
    u3 = u.reshape(bsz, t, CONV_CH)
    hist = jnp.pad(past[2][i], ((0, 0), (CONV_PAD - (CONV_WIDTH - 1), 0), (0, 0)))
    c = _conv(u3, hist, P['ev_conv_w'][i], P['ev_conv_b'][i], P['ev_conv_ln_g'][i],
              P['ev_conv_ln_b'][i], t, SAMPLE_SEQS).reshape(n, CONV_CH)
    new_u = jnp.concatenate([past[2][i], u3], axis=1)[:, -(CONV_WIDTH - 1):]
    return [a.reshape(n, A_WIDTH), c], (minor(new_k), minor(new_v), new_u)


def _gla_mixer(q, k, v, r, la, s0, gain, bsz, t, t_pad, chunk, tile, seqs):
    seq = lambda z: jnp.pad(z.reshape(bsz, t, -1), ((0, 0), (0, t_pad - t), (0, 0)))
    o, s = _gla(seq(q), seq(k), seq(v), seq(r), seq(la), s0, gain, chunk, tile, seqs)
    return [o[:, :t].reshape(bsz * t, C_DV)], s


def _trunks(x_p, x_s, past, P):
    (b_p, t_p, _), (b_s, t_s, _) = x_p.shape, x_s.shape
    x_p = x_p.reshape(b_p * t_p, D_MODEL)
    x_s = x_s.reshape(b_s * t_s, D_MODEL)
    new_p = {'k': [], 'v': [], 'u': [], 's': []}
    new_s = {'k': [], 'v': [], 'u': [], 's': []}
    for layer in range(DEPTH):
        i = layer // 2
        x_p, x_s = _ffn(x_p, x_s, P['norm_ffn1'][layer], P['ffn1_w_in'], P['ffn1_w_out'], layer,
                        FFN_TILE)
        if layer % 2 == 0:
            proj_p, proj_s = _even_in(x_p, x_s, P['norm_mix'][layer], (P['ev_w_in'], i),
                                      P['ev_q_gain'][i], P['ev_k_gain'][i], P['head_ones'], EVEN_TILE)
            mix_p, kvu_p = _even_mixer_prompt(*proj_p, P, i, b_p, t_p)
            mix_s, kvu_s = _even_mixer_sample(*proj_s, past, P, i, b_s, t_s)
            w_mix = [(P['ev_w_out'], i, 0, 2), (P['ev_w_out'], i, 1, 2)]
            for new, kvu in ((new_p, kvu_p), (new_s, kvu_s)):
                for name, z in zip('kvu', kvu):
                    new[name].append(z)
        else:
            proj_p, proj_s = _gla_in(x_p, x_s, P['norm_mix'][layer], (P['od_w_in'], i),
                                     P['od_w_low'][i], P['od_gate_w_up'][i], P['od_gate_b'][i],
                                     ROW_TILE)
            zeros = jnp.zeros((b_p, C_HEADS, C_DK_HEAD, C_DV_HEAD), F32)
            gain = P['od_o_gain'][i]
            mix_p, s_p = _gla_mixer(*proj_p, zeros, gain, b_p, t_p, t_p, GLA_CHUNK, GLA_TILE, 1)
            mix_s, s_s = _gla_mixer(*proj_s, past[3][i], gain, b_s, t_s, SAMPLE_PAD, SAMPLE_PAD,
                                    SAMPLE_PAD, SAMPLE_SEQS)
            w_mix = [(P['od_w_out'], i)]
            new_p['s'].append(s_p)
            new_s['s'].append(s_s)
        x_p, x_s = _ffn(x_p, x_s, P['norm_ffn2'][layer], P['ffn2_w_in'], P['ffn2_w_out'], layer,
                        FFN_TILE, mix_p, mix_s, w_mix)
    stacked = lambda new: tuple(jnp.stack(new[name]) for name in 'kvus')
    return (x_p.reshape(b_p, t_p, D_MODEL), x_s.reshape(b_s, t_s, D_MODEL),
            *stacked(new_p), *stacked(new_s))


def kernel(x_prompt, x_sample, cache_k, cache_v, cache_conv, state_gla, rel_bias, norm_ffn1, ffn1_w_in, ffn1_w_out, norm_mix, norm_ffn2, ffn2_w_in, ffn2_w_out, ev_w_in, ev_q_gain, ev_k_gain, ev_conv_w, ev_conv_b, ev_conv_ln_g, ev_conv_ln_b, ev_w_out, od_w_in, od_gate_w_up, od_gate_b, od_o_gain, od_w_out):
    n_even = ev_w_in.shape[0]
    n_odd = od_w_in.shape[0]
    main = 2 * C_DK + 2 * C_DV
    head_ids = np.arange(MXU_DIM) // A_HEAD_DIM
    per = lambda n, f: [f(j) for j in range(n)]
    row = lambda z: z[None, :]
    P = {
        'norm_ffn1': per(DEPTH, lambda j: row(norm_ffn1[j])),
        'norm_mix': per(DEPTH, lambda j: row(norm_mix[j])),
        'norm_ffn2': per(DEPTH, lambda j: row(norm_ffn2[j])),
        'ffn1_w_in': ffn1_w_in, 'ffn1_w_out': ffn1_w_out,
        'ffn2_w_in': ffn2_w_in, 'ffn2_w_out': ffn2_w_out,
        'ev_w_in': ev_w_in, 'ev_w_out': ev_w_out, 'od_w_out': od_w_out,
        'od_w_in': od_w_in.transpose(0, 2, 1),
        'ev_q_gain': per(n_even, lambda j: row(jnp.tile(ev_q_gain[j], A_HEADS))),
        'ev_k_gain': per(n_even, lambda j: row(jnp.tile(ev_k_gain[j], A_HEADS))),
        'head_ones': jnp.asarray(head_ids[:, None] == head_ids[None, :], BF16),
        'ev_conv_w': per(n_even, lambda j: jnp.broadcast_to(
            ev_conv_w[j][:, None, :], (CONV_WIDTH, SUBLANES, CONV_CH))),
        'ev_conv_b': per(n_even, lambda j: row(ev_conv_b[j])),
        'ev_conv_ln_g': per(n_even, lambda j: row(ev_conv_ln_g[j])),
        'ev_conv_ln_b': per(n_even, lambda j: row(ev_conv_ln_b[j])),
        'od_w_low': per(n_odd, lambda j: jnp.pad(od_w_in[j, :, main:],
                                                 ((0, 0), (0, LANES - GATE_RANK))).astype(BF16)),
        'od_gate_w_up': per(n_odd, lambda j: jnp.pad(od_gate_w_up[j],
                                                     ((0, LANES - GATE_RANK), (0, 0))).astype(BF16)),
        'od_gate_b': per(n_odd, lambda j: row(od_gate_b[j])),
        'od_o_gain': per(n_odd, lambda j: row(od_o_gain[j])),
        'band_vectors': _band_vectors(rel_bias),
        'decode_tables': _decode_tables(rel_bias, x_sample.shape[1]),
    }
    P, x_prompt, x_sample = lax.optimization_barrier((P, x_prompt, x_sample))
    return _trunks(x_prompt, x_sample, (cache_k, cache_v, cache_conv, state_gla), P)
```

```python
import functools

import numpy as np
import jax
import jax.numpy as jnp
from jax import lax
from jax.experimental import pallas as pl
from jax.experimental.pallas import tpu as pltpu

F32 = jnp.float32
BF16 = jnp.bfloat16

D_MODEL = 1024
DEPTH = 2
A_HEADS = 8
A_HEAD_DIM = 64
A_WIDTH = A_HEADS * A_HEAD_DIM
DILATED_GROUPS = ((128, 1), (512, 4), (2048, 16))
MAX_WINDOW = 2048
N_BUCKETS = 32
CONV_WIDTH = 31
CONV_CH = 512
C_HEADS = 4
C_DK = 512
C_DV = 1024
C_DK_HEAD = 128
C_DV_HEAD = 256
GATE_RANK = 16
GATE_TAU = 16.0
D_FF = 2816
EPS = 1e-6
NEG_INF = -1e30
LOG2E = 1.4426950408889634

LANES = 128
SUBLANES = 8
MXU_DIM = 256
WIN_KEYS = 128
ATT_BLOCK = 2048
ATT_UNROLL = 16
ATT_PHASES = 4
VMEM_LIMIT = 60 * 1024 * 1024


def _params(*sem):
    return pltpu.CompilerParams(dimension_semantics=sem, vmem_limit_bytes=VMEM_LIMIT)


def _dot(a, b):
    return jnp.dot(a, b, preferred_element_type=F32)


def _dot_nt(a, b):
    return lax.dot_general(a, b, (((1,), (1,)), ((), ())), preferred_element_type=F32)


def _dot_tn(a, b):
    return lax.dot_general(a, b, (((0,), (0,)), ((), ())), preferred_element_type=F32)


def _rms_rows(x, g):
    y = x * lax.rsqrt(jnp.mean(x * x, axis=-1, keepdims=True) + EPS)
    return y * g


def _sigmoid(x):
    return 1.0 / (1.0 + jnp.exp(-x))


def _full(shape):
    return pl.BlockSpec(shape, lambda *_: (0,) * len(shape), pipeline_mode=pl.Buffered(1))


FF_CHUNK = 256
FF_DOWN_CHUNK = 1024


def _layer(w, layer, row_block=0, n_blocks=1):
    rows = w.shape[1] // n_blocks
    return pl.BlockSpec((None, rows, w.shape[2]), lambda *_: (layer, row_block, 0),
                        pipeline_mode=pl.Buffered(1))


def _rows_call(body, prompt_rows, sample_rows, consts, out_widths, tm, name, scratch=(), hbm=()):
    n_p, n_s = prompt_rows[0].shape[0], sample_rows[0].shape[0]
    steps = n_p // tm
    n_in, n_c, n_out = len(prompt_rows), len(consts) + len(hbm), len(out_widths)

    def kernel(*refs):
        p_in, s_in = refs[:n_in], refs[n_in:2 * n_in]
        c_refs = refs[2 * n_in:2 * n_in + n_c]
        outs = refs[2 * n_in + n_c:]
        p_out, s_out, scr = outs[:n_out], outs[n_out:2 * n_out], outs[2 * n_out:]
        step = pl.program_id(0)

        if hbm:
            @pl.when(step == 0)
            def _():
                body(p_in, c_refs, p_out, scr, tm, True)

        @pl.when(jnp.logical_and(step >= (1 if hbm else 0), step < steps))
        def _():
            body(p_in, c_refs, p_out, scr, tm, False)

        @pl.when(step == steps)
        def _():
            body(s_in, c_refs, s_out, scr, n_s, False)

    p_spec = lambda width: pl.BlockSpec((tm, width), lambda i: (jnp.minimum(i, steps - 1), 0))
    s_spec = lambda width: pl.BlockSpec((n_s, width), lambda i: (0, 0))
    res = pl.pallas_call(
        kernel,
        grid=(steps + 1,),
        in_specs=[p_spec(a.shape[1]) for a in prompt_rows] + [s_spec(a.shape[1]) for a in sample_rows]
                 + [_layer(*c) if isinstance(c, tuple) else _full(c.shape) for c in consts]
                 + [pl.BlockSpec(memory_space=pl.ANY) for _ in hbm],
        out_specs=[p_spec(w) for w in out_widths] + [s_spec(w) for w in out_widths],
        out_shape=[jax.ShapeDtypeStruct((n_p, w), F32) for w in out_widths]
                  + [jax.ShapeDtypeStruct((n_s, w), F32) for w in out_widths],
        scratch_shapes=list(scratch),
        compiler_params=_params("arbitrary"),
        name=name,
    )(*prompt_rows, *sample_rows, *[c[0] if isinstance(c, tuple) else c for c in consts], *hbm)
    return res[:n_out], res[n_out:]


def _ffn_body(layer, ins, consts, outs, scratch, rows, first):
    x_ref, m_refs = ins[0], ins[1:]
    w_refs, (g_ref, wi_hbm, wo_hbm) = consts[:len(m_refs)], consts[len(m_refs):]
    (o_ref,), (act_ref, wi_ref, wo_ref, sems) = outs, scratch
    up_chunks = range(D_FF // FF_CHUNK)
    down_chunks = [(lo, min(lo + FF_DOWN_CHUNK, D_FF)) for lo in range(0, D_FF, FF_DOWN_CHUNK)]

    def up_copies(c):
        cols = [pl.ds(half * D_FF + c * FF_CHUNK, FF_CHUNK) for half in range(2)]
        return [pltpu.make_async_copy(wi_hbm.at[layer, :, cs], wi_ref.at[:, cs], sems.at[c])
                for cs in cols]

    def down_copy(j):
        rows_j = pl.ds(down_chunks[j][0], down_chunks[j][1] - down_chunks[j][0])
        return pltpu.make_async_copy(wo_hbm.at[layer, rows_j], wo_ref.at[rows_j],
                                     sems.at[len(up_chunks) + j])

    def on_first(copies, act):
        if first:
            for copy in copies():
                act(copy)

    every = lambda: ([cp for c in up_chunks for cp in up_copies(c)]
                     + [down_copy(j) for j in range(len(down_chunks))])
    on_first(every, lambda copy: copy.start())
    x = x_ref[...]
    for m_ref, w_ref in zip(m_refs, w_refs):
        x = x + _dot(m_ref[...], w_ref[...].astype(BF16))
    h = _rms_rows(x, g_ref[...]).astype(BF16)
    for c in up_chunks:
        on_first(lambda c=c: up_copies(c), lambda copy: copy.wait())
        lo = c * FF_CHUNK
        a = _dot(h, wi_ref[:, lo:lo + FF_CHUNK].astype(BF16))
        b = _dot(h, wi_ref[:, D_FF + lo:D_FF + lo + FF_CHUNK].astype(BF16))
        act_ref[0:rows, lo:lo + FF_CHUNK] = (a * _sigmoid(a) * b).astype(BF16)
    y = jnp.zeros_like(x)
    for j, (lo, hi) in enumerate(down_chunks):
        on_first(lambda j=j: [down_copy(j)], lambda copy: copy.wait())
        y = y + _dot(act_ref[0:rows, lo:hi], wo_ref[lo:hi, :].astype(BF16))
    o_ref[...] = x + 0.5 * y


def _ffn(x_p, x_s, g, wi, wo, layer, tm, ms_p=(), ms_s=(), ws=()):
    n_sems = D_FF // FF_CHUNK + -(-D_FF // FF_DOWN_CHUNK)
    scratch = [pltpu.VMEM((tm, D_FF), BF16), pltpu.VMEM(wi.shape[1:], F32),
               pltpu.VMEM(wo.shape[1:], F32), pltpu.SemaphoreType.DMA((n_sems,))]
    (y_p,), (y_s,) = _rows_call(functools.partial(_ffn_body, layer), [x_p, *ms_p], [x_s, *ms_s],
                                [*ws, g], [D_MODEL], tm, "ffn", scratch, hbm=[wi, wo])
    return y_p, y_s


def _head_norm(z, gain, bd):
    zz = z * z
    hi = zz.astype(BF16)
    lo = (zz - hi.astype(F32)).astype(BF16)
    width = bd.shape[0]
    ss = jnp.concatenate(
        [_dot(hi[:, c:c + width], bd) + _dot(lo[:, c:c + width], bd)
         for c in range(0, z.shape[1], width)], axis=1)
    return z * lax.rsqrt(ss * (1.0 / A_HEAD_DIM) + EPS) * gain


def _even_in_body(ins, consts, outs, scratch, rows, first):
    (x_ref,), (g_ref, w_ref, qg_ref, kg_ref, bd_ref), (q_ref, k_ref, v_ref, u_ref) = ins, consts, outs
    h = _rms_rows(x_ref[...], g_ref[...]).astype(BF16)
    proj = lambda lo, width: _dot(h, w_ref[:, lo:lo + width].astype(BF16))
    bd = bd_ref[...]
    q_ref[...] = _head_norm(proj(0, A_WIDTH), qg_ref[...], bd) * (A_HEAD_DIM ** -0.5 * LOG2E)
    k_ref[...] = _head_norm(proj(A_WIDTH, A_WIDTH), kg_ref[...], bd)
    v_ref[...] = proj(2 * A_WIDTH, A_WIDTH)
    gv = proj(3 * A_WIDTH, CONV_CH)
    gg = proj(3 * A_WIDTH + CONV_CH, CONV_CH)
    u_ref[...] = gv * _sigmoid(gg)


def _even_in(x_p, x_s, g, w, qg, kg, bd, tm):
    return _rows_call(_even_in_body, [x_p], [x_s], [g, w, qg, kg, bd], [A_WIDTH] * 4, tm, "even_in")


def _t5_bucket(dist):
    max_exact = N_BUCKETS // 2
    d = np.asarray(dist, dtype=np.int32)
    df = np.maximum(d, 1).astype(np.float32)
    large = max_exact + (np.log(df / max_exact) / np.log(MAX_WINDOW / max_exact)
                         * (N_BUCKETS - max_exact)).astype(np.int32)
    large = np.minimum(large, N_BUCKETS - 1)
    return np.where(d < max_exact, d, large).astype(np.int32)


def _select_bias(rel_bias, dist, valid):
    onehot = (_t5_bucket(dist)[None, :] == np.arange(N_BUCKETS)[:, None]) & valid[None, :]
    picked = jnp.einsum('bh,bc->hc', rel_bias, jnp.asarray(onehot, F32),
                        precision=lax.Precision.HIGHEST)
    return picked * LOG2E + jnp.asarray(np.where(valid, 0.0, NEG_INF), F32)[None, :]


def _band_vectors(rel_bias):
    c = np.arange(2 * WIN_KEYS)
    valid = c <= WIN_KEYS
    vecs = [_select_bias(rel_bias, np.where(valid, (WIN_KEYS - c) * dil, 0), valid)
            for _, dil in DILATED_GROUPS]
    return jnp.stack(vecs).reshape(len(DILATED_GROUPS), A_HEADS // 2, 2, 2 * WIN_KEYS)


def _attn_prompt_kernel(q_ref, kp_ref, kc_ref, vp_ref, vc_ref, vec_ref, o_ref,
                        knat, vnat, k4, v4, q4, og1, lg1, og4, lg4, tab_ref):
    blk = pl.program_id(2)
    n_groups = len(DILATED_GROUPS)
    nph = ATT_PHASES
    per = ATT_BLOCK // nph

    @pl.when(blk == 0)
    def _():
        col = lax.broadcasted_iota(jnp.int32, (WIN_KEYS, 2 * WIN_KEYS), 1)
        for g in range(n_groups):
            for hh in range(2):
                vec = jnp.broadcast_to(vec_ref[g, 0, hh:hh + 1, :], (WIN_KEYS, 2 * WIN_KEYS))
                band = pltpu.roll(vec, 0, 1, stride=1, stride_axis=0)
                rows = slice(hh * WIN_KEYS, (hh + 1) * WIN_KEYS)
                tab_ref[g, rows, :] = band
                tab_ref[n_groups + g, rows, :] = jnp.where(col >= WIN_KEYS, band, NEG_INF)

    knat[0:WIN_KEYS, :] = kp_ref[ATT_BLOCK - WIN_KEYS:, :]
    knat[WIN_KEYS:, :] = kc_ref[...]
    vnat[0:WIN_KEYS, :] = vp_ref[ATT_BLOCK - WIN_KEYS:, :]
    vnat[WIN_KEYS:, :] = vc_ref[...]
    for r in range(nph):
        phase = pl.ds(r, per, stride=nph)
        k4[r, 0:per, :] = kp_ref[phase, :]
        k4[r, per:, :] = kc_ref[phase, :]
        v4[r, 0:per, :] = vp_ref[phase, :]
        v4[r, per:, :] = vc_ref[phase, :]
        q4[r] = q_ref[phase, :]
    first = blk == 0
    lane = lax.broadcasted_iota(jnp.int32, (WIN_KEYS, LANES), 1)
    low = lane < A_HEAD_DIM
    ones = jnp.ones((2 * WIN_KEYS, LANES), BF16)

    def block(qs, kk, vv, tab):
        qs = qs.astype(BF16)
        zero = jnp.zeros_like(qs)
        qst = jnp.concatenate([jnp.where(low, qs, zero), jnp.where(low, zero, qs)], axis=0)
        s = _dot_nt(qst, kk.astype(BF16)) + tab
        mx = jnp.max(s, axis=-1, keepdims=True)
        p = jnp.exp2(s - mx).astype(BF16)
        r = _dot(p, jnp.concatenate([vv.astype(BF16), ones], axis=1))
        o2 = jnp.where(low, r[0:WIN_KEYS, 0:LANES], r[WIN_KEYS:, 0:LANES])
        l2 = jnp.where(low, r[0:WIN_KEYS, LANES:], r[WIN_KEYS:, LANES:])
        m2 = jnp.where(low, jnp.broadcast_to(mx[0:WIN_KEYS], (WIN_KEYS, LANES)),
                       jnp.broadcast_to(mx[WIN_KEYS:], (WIN_KEYS, LANES)))
        return o2 / l2, m2 + jnp.log(l2) * LOG2E

    def table(g, at_start):
        return tab_ref[jnp.where(jnp.logical_and(at_start, first), n_groups + g, g)]

    def body1(sub, carry):
        i0 = pl.multiple_of(sub * WIN_KEYS, WIN_KEYS)
        o, l = block(q_ref[pl.ds(i0, WIN_KEYS), :], knat[pl.ds(i0, 2 * WIN_KEYS), :],
                     vnat[pl.ds(i0, 2 * WIN_KEYS), :], table(0, sub == 0))
        og1[pl.ds(i0, WIN_KEYS), :] = o
        lg1[pl.ds(i0, WIN_KEYS), :] = l
        return carry

    def body2(pb, carry):
        sub = pb // nph
        r = pb - sub * nph
        i0 = pl.multiple_of(sub * WIN_KEYS, WIN_KEYS)
        keys = pl.ds(i0 + (per - WIN_KEYS), 2 * WIN_KEYS)
        o, l = block(q4[r, pl.ds(i0, WIN_KEYS), :], k4[r, keys, :], v4[r, keys, :],
                     table(1, sub == 0))
        og4[0, r, pl.ds(i0, WIN_KEYS), :] = o
        lg4[0, r, pl.ds(i0, WIN_KEYS), :] = l
        return carry

    def body3(pb, carry):
        a = pb // nph
        r = pb - a * nph
        rows = pl.ds(a, WIN_KEYS, stride=nph)
        keys = pl.ds(a, 2 * WIN_KEYS, stride=nph)
        o, l = block(q4[r, rows, :], k4[r, keys, :], v4[r, keys, :], table(2, True))
        og4[1, r, rows, :] = o
        lg4[1, r, rows, :] = l
        return carry

    n_blocks = ATT_BLOCK // WIN_KEYS
    for body in (body1, body2, body3):
        lax.fori_loop(0, n_blocks, body, 0, unroll=ATT_UNROLL)

    for r in range(nph):
        phase = pl.ds(r, per, stride=nph)
        la, lb, lc = lg1[phase, :], lg4[0, r], lg4[1, r]
        mx = jnp.maximum(jnp.maximum(la, lb), lc)
        wa, wb, wc = jnp.exp2(la - mx), jnp.exp2(lb - mx), jnp.exp2(lc - mx)
        og1[phase, :] = (wa * og1[phase, :] + wb * og4[0, r] + wc * og4[1, r]) / (wa + wb + wc)
    o_ref[...] = og1[...].astype(o_ref.dtype)


def _attn_prompt(q, k, v, vecs, batch, seq):
    assert [d for _, d in DILATED_GROUPS] == [1, ATT_PHASES, ATT_PHASES ** 2]
    nb = seq // ATT_BLOCK
    n_groups = len(DILATED_GROUPS)
    per = ATT_BLOCK // ATT_PHASES
    cur = lambda b, p, t: (b * nb + t, p)
    prev = lambda b, p, t: (b * nb + jnp.maximum(t - 1, 0), p)
    blk = lambda imap: pl.BlockSpec((ATT_BLOCK, LANES), imap)
    vmem = lambda *shape: pltpu.VMEM(shape, F32)
    return pl.pallas_call(
        _attn_prompt_kernel,
        grid=(batch, A_HEADS // 2, nb),
        in_specs=[blk(cur), blk(prev), blk(cur), blk(prev), blk(cur),
                  pl.BlockSpec((n_groups, 1, 2, 2 * WIN_KEYS), lambda b, p, t: (0, p, 0, 0))],
        out_specs=blk(cur),
        out_shape=jax.ShapeDtypeStruct((batch * seq, A_WIDTH), BF16),
        scratch_shapes=[vmem(WIN_KEYS + ATT_BLOCK, LANES), vmem(WIN_KEYS + ATT_BLOCK, LANES),
                        vmem(ATT_PHASES, 2 * per, LANES), vmem(ATT_PHASES, 2 * per, LANES),
                        vmem(ATT_PHASES, per, LANES),
                        vmem(ATT_BLOCK, LANES), vmem(ATT_BLOCK, LANES),
                        vmem(2, ATT_PHASES, per, LANES), vmem(2, ATT_PHASES, per, LANES),
                        vmem(2 * n_groups, 2 * WIN_KEYS, 2 * WIN_KEYS)],
        compiler_params=_params("arbitrary", "arbitrary", "arbitrary"),
        name="attn_prompt",
    )(q, k, k, v, v, vecs)


def _decode_tables(rel_bias, n_new):
    cols = MAX_WINDOW + n_new
    c = np.arange(cols)
    dist = MAX_WINDOW - c
    cnt = np.zeros(c.shape, np.float32)
    for window, dil in DILATED_GROUPS:
        cnt += ((dist >= 0) & (dist <= window) & (dist % dil == 0)).astype(np.float32)
    vec = _select_bias(rel_bias, np.clip(dist, 0, MAX_WINDOW), cnt > 0)
    vec = vec + jnp.asarray(np.log2(np.maximum(cnt, 1.0)), F32)[None, :]
    rows = jnp.stack([jnp.pad(vec[:, :cols - i], ((0, 0), (i, 0)), constant_values=NEG_INF)
                      for i in range(n_new)], axis=1)
    rows = rows.reshape(A_HEADS * n_new, cols)
    return rows[:, :MAX_WINDOW], rows[:, MAX_WINDOW:]


def _attn_sample_kernel(q_ref, kn_ref, vn_ref, kc_ref, vc_ref, tc_ref, tn_ref,
                        o_ref, ko_ref, vo_ref):
    n_new = q_ref.shape[1]
    n_buf = kc_ref.shape[2]
    rows = A_HEADS * n_new
    kn = kn_ref[0]
    vn = vn_ref[0]
    kc = kc_ref[0]
    vc = vc_ref[0]

    lane = lax.broadcasted_iota(jnp.int32, (A_WIDTH, LANES), 1)
    for new, old, out_ref in ((kn, kc, ko_ref), (vn, vc, vo_ref)):
        shifted = pltpu.roll(old, n_buf - n_new, 1)
        tail = jnp.concatenate([jnp.zeros((LANES - n_new, A_WIDTH), F32), new], axis=0).T
        out_ref[0, :, 0:n_buf - LANES] = shifted[:, 0:n_buf - LANES]
        out_ref[0, :, n_buf - LANES:] = jnp.where(lane >= LANES - n_new, tail,
                                                  shifted[:, n_buf - LANES:])

    q = q_ref[0]
    row_head = lax.broadcasted_iota(jnp.int32, (A_HEADS, n_new, A_WIDTH), 0).reshape(rows, A_WIDTH)
    col = lax.broadcasted_iota(jnp.int32, (rows, A_WIDTH), 1)
    own = jnp.logical_and(col >= row_head * A_HEAD_DIM, col < (row_head + 1) * A_HEAD_DIM)
    qblk = jnp.where(own, jnp.concatenate([q] * A_HEADS, axis=0), 0.0).astype(BF16)
    s_c = _dot(qblk, kc.astype(BF16)) + tc_ref[...]
    s_n = _dot_nt(qblk, kn.astype(BF16)) + tn_ref[...]
    mx = jnp.maximum(jnp.max(s_c, axis=-1, keepdims=True), jnp.max(s_n, axis=-1, keepdims=True))
    p_c = jnp.exp2(s_c - mx)
    p_n = jnp.exp2(s_n - mx)
    den = jnp.sum(p_c, axis=-1, keepdims=True) + jnp.sum(p_n, axis=-1, keepdims=True)
    acc = _dot_nt(p_c.astype(BF16), vc.astype(BF16)) + _dot(p_n.astype(BF16), vn.astype(BF16))
    acc = jnp.where(own, acc / den, 0.0)
    out = acc[0:n_new]
    for h in range(1, A_HEADS):
        out = out + acc[h * n_new:(h + 1) * n_new]
    o_ref[0] = out.astype(o_ref.dtype)


def _attn_sample(q, k_new, v_new, cache_k, cache_v, layer, tables):
    b, n_new, _ = q.shape
    n_buf = cache_k.shape[2]
    new = pl.BlockSpec((1, n_new, A_WIDTH), lambda i: (i, 0, 0))
    buf = pl.BlockSpec((1, A_WIDTH, n_buf), lambda i: (i, 0, 0))
    past = pl.BlockSpec((1, A_WIDTH, n_buf), lambda i: (layer * b + i, 0, 0))
    return pl.pallas_call(
        _attn_sample_kernel,
        grid=(b,),
        in_specs=[new, new, new, past, past, _full(tables[0].shape), _full(tables[1].shape)],
        out_specs=[new, buf, buf],
        out_shape=[jax.ShapeDtypeStruct((b, n_new, A_WIDTH), BF16),
                   jax.ShapeDtypeStruct((b, A_WIDTH, n_buf), F32),
                   jax.ShapeDtypeStruct((b, A_WIDTH, n_buf), F32)],
        compiler_params=_params("parallel"),
        name="attn_sample",
    )(q, k_new, v_new, cache_k, cache_v, *tables)


CONV_PAD = 32
CONV_ROWS = 32
CONV_UNROLL = 2


def _conv_kernel(tc, u_ref, up_ref, hist_ref, w_ref, b_ref, g_ref, beta_ref, o_ref, win, stage):
    t = pl.program_id(1)
    n_seq = u_ref.shape[0]
    n_slab = CONV_CH // LANES
    slab = lambda c: slice(c * LANES, (c + 1) * LANES)
    off = CONV_PAD - (CONV_WIDTH - 1)
    rc = min(CONV_ROWS, tc)
    half = rc // 2

    def tap(k, c):
        w = w_ref[k, :, slab(c)]
        if half < SUBLANES:
            return w[0:half]
        return jnp.concatenate([w] * (half // SUBLANES), axis=0)

    for s in range(n_seq):
        first = s * n_slab
        for c in range(n_slab):
            win[first + c, CONV_PAD:CONV_PAD + tc, :] = u_ref[s, :, slab(c)]

        @pl.when(t == 0)
        def _():
            for c in range(n_slab):
                win[first + c, 0:CONV_PAD, :] = hist_ref[s, :, slab(c)]

        @pl.when(t > 0)
        def _():
            for c in range(n_slab):
                win[first + c, 0:CONV_PAD, :] = up_ref[s, :, slab(c)]

        def body(j, carry):
            r0 = j * rc
            for c in range(n_slab):
                for par in range(2):
                    acc = jnp.zeros((half, LANES), F32) + b_ref[:, slab(c)]
                    for k in range(CONV_WIDTH):
                        rows = pl.ds(r0 + off + k + par, half, stride=2)
                        acc = acc + win[first + c, rows, :] * tap(k, c)
                    stage[first + c, pl.ds(r0 + par, half, stride=2), :] = acc
            return carry

        if tc == rc:
            body(0, 0)
        else:
            lax.fori_loop(0, tc // rc, body, 0, unroll=CONV_UNROLL)
        y = jnp.concatenate([stage[first + c] for c in range(n_slab)], axis=1)
        xc = y - jnp.mean(y, axis=-1, keepdims=True)
        y = xc * lax.rsqrt(jnp.mean(xc * xc, axis=-1, keepdims=True) + EPS)
        y = y * g_ref[...] + beta_ref[...]
        o_ref[s] = (y * _sigmoid(y)).astype(o_ref.dtype)


def _conv(u, hist, w, b, g, beta, tc, seqs):
    bsz, t, _ = u.shape
    per = tc // CONV_PAD
    if t >= CONV_PAD:
        prev = pl.BlockSpec((seqs, CONV_PAD, CONV_CH),
                            lambda i, j: (i, jnp.maximum(j * per - 1, 0), 0))
        u_prev = u
    else:
        prev = pl.BlockSpec((seqs, CONV_PAD, CONV_CH), lambda i, j: (i, 0, 0))
        u_prev = hist
    n_slab = seqs * CONV_CH // LANES
    return pl.pallas_call(
        functools.partial(_conv_kernel, tc),
        grid=(bsz // seqs, t // tc),
        in_specs=[pl.BlockSpec((seqs, tc, CONV_CH), lambda i, j: (i, j, 0)),
                  prev,
                  pl.BlockSpec((seqs, CONV_PAD, CONV_CH), lambda i, j: (i, 0, 0)),
                  _full(w.shape), _full((1, CONV_CH)), _full((1, CONV_CH)),
                  _full((1, CONV_CH))],
        out_specs=pl.BlockSpec((seqs, tc, CONV_CH), lambda i, j: (i, j, 0)),
        out_shape=jax.ShapeDtypeStruct((bsz, t, CONV_CH), BF16),
        scratch_shapes=[pltpu.VMEM((n_slab, CONV_PAD + tc, LANES), F32),
                        pltpu.VMEM((n_slab, tc, LANES), F32)],
        compiler_params=_params("parallel", "arbitrary"),
        name="conv",
    )(u, u_prev, hist, w, b, g, beta)


def _gla_in_body(ins, consts, outs, scratch, rows, first):
    (x_ref,), (g_ref, w_ref, wl_ref, wu_ref, bu_ref) = ins, consts
    q_ref, k_ref, v_ref, r_ref, la_ref = outs
    h = _rms_rows(x_ref[...], g_ref[...]).astype(BF16)
    proj = lambda lo, width: _dot_nt(h, w_ref[lo:lo + width, :].astype(BF16))
    q_ref[...] = proj(0, C_DK) * (C_DK_HEAD ** -0.5)
    k_ref[...] = proj(C_DK, C_DK)
    v_ref[...] = proj(2 * C_DK, C_DV)
    r = proj(2 * C_DK + C_DV, C_DV)
    r_ref[...] = r * _sigmoid(r)
    low = _dot(h, wl_ref[...]).astype(BF16)
    z = _dot(low, wu_ref[...]) + bu_ref[...]
    log_sig = jnp.minimum(z, 0.0) - jnp.log1p(jnp.exp(-jnp.abs(z)))
    la_ref[...] = log_sig * (1.0 / GATE_TAU)


def _gla_in(x_p, x_s, g, w, wl, wu, bu, tm):
    return _rows_call(_gla_in_body, [x_p], [x_s], [g, w, wl, wu, bu],
                      [C_DK, C_DK, C_DV, C_DV, C_DK], tm, "gla_in")


def _gla_kernel(chunk, n_chunks, q_ref, k_ref, v_ref, r_ref, la_ref, s0_ref, gain_ref,
                o_ref, s_ref, qin_s, kin_s, x1_s, x2_s, qst_s, kst_s, dec_s):
    n_seq = q_ref.shape[0]
    @pl.when(pl.program_id(1) == 0)
    def _():
        s_ref[...] = s0_ref[...]

    half, quarter = chunk // 2, chunk // 4
    quarter_of = lambda i: sum((i >= j * quarter).astype(jnp.int32) for j in range(1, 4))
    ri = lax.broadcasted_iota(jnp.int32, (chunk, chunk), 0)
    ci = lax.broadcasted_iota(jnp.int32, (chunk, chunk), 1)
    causal = ci <= ri
    tri = jnp.where(causal, 1.0, 0.0).astype(BF16)
    rq, cq = quarter_of(ri), quarter_of(ci)
    same_quarter = jnp.logical_and(causal, rq == cq)
    cross_half = jnp.logical_and(ri >= half, ci < half)
    cross_quarter = jnp.logical_and(rq == cq + 1, (ri >= half) == (ci >= half))
    rr = lax.broadcasted_iota(jnp.int32, (chunk, C_DK), 0)
    rrq = quarter_of(rr)
    in_first = rr < half
    key_side_2 = jnp.logical_or(rrq == 0, rrq == 2)
    gain = gain_ref[...]

    def prepare(g, c):
        rows = slice(c * chunk, (c + 1) * chunk)
        la = la_ref[g, rows, :]
        la_hi = la.astype(BF16)
        la_lo = (la - la_hi.astype(F32)).astype(BF16)
        cum = _dot(tri, la_hi) + _dot(tri, la_lo)
        row = lambda i: cum[i:i + 1, :]
        last = row(chunk - 1)
        mids = [row(j * quarter + quarter // 2 - 1) for j in range(4)]
        mid = jnp.where(rrq == 0, mids[0], jnp.where(rrq == 1, mids[1],
                        jnp.where(rrq == 2, mids[2], mids[3])))
        q = q_ref[g, rows, :]
        k = k_ref[g, rows, :]
        grow = jnp.exp(cum - mid)
        qin_s[g, rows, :] = (q * grow).astype(BF16)
        kin_s[g, rows, :] = (k / grow).astype(BF16)

        def across(edge, key_side):
            gap = cum - edge
            return (jnp.where(key_side, k, q) * jnp.exp(jnp.where(key_side, -gap, gap))).astype(BF16)

        x1_s[g, rows, :] = across(row(half - 1), in_first)
        x2_s[g, rows, :] = across(jnp.where(in_first, row(quarter - 1), row(half + quarter - 1)),
                               key_side_2)
        qst_s[g, rows, :] = (q * jnp.exp(cum)).astype(BF16)
        kst_s[g, rows, :] = (k * jnp.exp(last - cum)).astype(BF16)
        dec_s[g, c] = jnp.broadcast_to(jnp.exp(last), (LANES, C_DK)).T

    def advance(g, c):
        rows = slice(c * chunk, (c + 1) * chunk)
        for h in range(C_HEADS):
            ks = slice(h * C_DK_HEAD, (h + 1) * C_DK_HEAD)
            vs = slice(h * C_DV_HEAD, (h + 1) * C_DV_HEAD)
            vh = v_ref[g, rows, vs].astype(BF16)
            x1, x2 = x1_s[g, rows, ks], x2_s[g, rows, ks]
            att = jnp.where(same_quarter, _dot_nt(qin_s[g, rows, ks], kin_s[g, rows, ks]),
                            jnp.where(cross_quarter, _dot_nt(x2, x2),
                                      jnp.where(cross_half, _dot_nt(x1, x1), 0.0)))
            s = s_ref[g, h]
            o = _dot(jnp.concatenate([att.astype(BF16), qst_s[g, rows, ks]], axis=1),
                     jnp.concatenate([vh, s.astype(BF16)], axis=0))
            decay = dec_s[g, c, ks, :]
            s_ref[g, h] = (s * jnp.concatenate([decay] * (C_DV_HEAD // LANES), axis=1)
                           + _dot_tn(kst_s[g, rows, ks], vh))
            y = o * lax.rsqrt(jnp.mean(o * o, axis=-1, keepdims=True) + EPS) * gain
            o_ref[g, rows, vs] = (y * r_ref[g, rows, vs]).astype(o_ref.dtype)

    for g in range(n_seq):
        prepare(g, 0)
    for c in range(n_chunks):
        for g in range(n_seq):
            if c + 1 < n_chunks:
                prepare(g, c + 1)
            advance(g, c)


def _gla(q, k, v, r, la, s0, gain, chunk, tb, seqs):
    b, t, _ = q.shape
    seq = lambda width: pl.BlockSpec((seqs, tb, width), lambda i, j: (i, j, 0))
    state = pl.BlockSpec((seqs, C_HEADS, C_DK_HEAD, C_DV_HEAD), lambda i, j: (i, 0, 0, 0))
    return pl.pallas_call(
        functools.partial(_gla_kernel, chunk, tb // chunk),
        grid=(b // seqs, t // tb),
        in_specs=[seq(C_DK), seq(C_DK), seq(C_DV), seq(C_DV), seq(C_DK), state,
                  _full((1, C_DV_HEAD))],
        out_specs=[seq(C_DV), state],
        out_shape=[jax.ShapeDtypeStruct((b, t, C_DV), BF16),
                   jax.ShapeDtypeStruct(s0.shape, F32)],
        scratch_shapes=[pltpu.VMEM((seqs, tb, C_DK), BF16)] * 6
                       + [pltpu.VMEM((seqs, tb // chunk, C_DK, LANES), F32)],
        compiler_params=_params("parallel", "arbitrary"),
        name="gla",
    )(q, k, v, r, la, s0, gain)


GLA_CHUNK = 128
SAMPLE_PAD = 16
SAMPLE_SEQS = 8


ROW_TILE = 512
FFN_TILE = ROW_TILE
EVEN_TILE = 1024
CONV_TILE = 512
GLA_TILE = 1024


def _even_mixer_prompt(q, k, v, u, P, i, bsz, t):
    n = bsz * t
    a = _attn_prompt(q, k, v, P['band_vectors'], bsz, t)
    keep = min(MAX_WINDOW, t)
    tail = lambda z: z.reshape(bsz, t, A_WIDTH)[:, t - keep:].reshape(bsz, keep, A_HEADS, A_HEAD_DIM)
    u3 = u.reshape(bsz, t, CONV_CH)
    hist = jnp.zeros((bsz, CONV_PAD, CONV_CH), F32)
    c = _conv(u3, hist, P['ev_conv_w'][i], P['ev_conv_b'][i], P['ev_conv_ln_g'][i],
              P['ev_conv_ln_b'][i], CONV_TILE, 1).reshape(n, CONV_CH)
    return [a, c], (tail(k), tail(v), u3[:, t - (CONV_WIDTH - 1):])


def _even_mixer_sample(q, k, v, u, past, P, i, bsz, t):
    n = bsz * t
    n_buf = past[0].shape[2]
    assert n_buf == MAX_WINDOW, "the sample kernel expects a full window buffer"
    major = lambda z: z.transpose(0, 1, 3, 4, 2).reshape(z.shape[0] * bsz, A_WIDTH, n_buf)
    minor = lambda z: z.reshape(bsz, A_HEADS, A_HEAD_DIM, n_buf).transpose(0, 3, 1, 2)
    a, new_k, new_v = _attn_sample(
        q.reshape(bsz, t, A_WIDTH), k.reshape(bsz, t, A_WIDTH), v.reshape(bsz, t, A_WIDTH),
        major(past[0]), major(past[1]), i, P['decode_tables'])
    u3 = u.reshape(bsz, t, CONV_CH)
    hist = jnp.pad(past[2][i], ((0, 0), (CONV_PAD - (CONV_WIDTH - 1), 0), (0, 0)))
    c = _conv(u3, hist, P['ev_conv_w'][i], P['ev_conv_b'][i], P['ev_conv_ln_g'][i],
              P['ev_conv_ln_b'][i], t, SAMPLE_SEQS).reshape(n, CONV_CH)
    new_u = jnp.concatenate([past[2][i], u3], axis=1)[:, -(CONV_WIDTH - 1):]
    return [a.reshape(n, A_WIDTH), c], (minor(new_k), minor(new_v), new_u)


def _gla_mixer(q, k, v, r, la, s0, gain, bsz, t, t_pad, chunk, tile, seqs):
    seq = lambda z: jnp.pad(z.reshape(bsz, t, -1), ((0, 0), (0, t_pad - t), (0, 0)))
    o, s = _gla(seq(q), seq(k), seq(v), seq(r), seq(la), s0, gain, chunk, tile, seqs)
    return [o[:, :t].reshape(bsz * t, C_DV)], s


def _trunks(x_p, x_s, past, P):
    (b_p, t_p, _), (b_s, t_s, _) = x_p.shape, x_s.shape
    x_p = x_p.reshape(b_p * t_p, D_MODEL)
    x_s = x_s.reshape(b_s * t_s, D_MODEL)
    new_p = {'k': [], 'v': [], 'u': [], 's': []}
    new_s = {'k': [], 'v': [], 'u': [], 's': []}
    for layer in range(DEPTH):
        i = layer // 2
        x_p, x_s = _ffn(x_p, x_s, P['norm_ffn1'][layer], P['ffn1_w_in'], P['ffn1_w_out'], layer,
                        FFN_TILE)
        if layer % 2 == 0:
            proj_p, proj_s = _even_in(x_p, x_s, P['norm_mix'][layer], (P['ev_w_in'], i),
                                      P['ev_q_gain'][i], P['ev_k_gain'][i], P['head_ones'], EVEN_TILE)
            mix_p, kvu_p = _even_mixer_prompt(*proj_p, P, i, b_p, t_p)
            mix_s, kvu_s = _even_mixer_sample(*proj_s, past, P, i, b_s, t_s)
            w_mix = [(P['ev_w_out'], i, 0, 2), (P['ev_w_out'], i, 1, 2)]
            for new, kvu in ((new_p, kvu_p), (new_s, kvu_s)):
                for name, z in zip('kvu', kvu):
                    new[name].append(z)
        else:
            proj_p, proj_s = _gla_in(x_p, x_s, P['norm_mix'][layer], (P['od_w_in'], i),
                                     P['od_w_low'][i], P['od_gate_w_up'][i], P['od_gate_b'][i],
                                     ROW_TILE)
            zeros = jnp.zeros((b_p, C_HEADS, C_DK_HEAD, C_DV_HEAD), F32)
            gain = P['od_o_gain'][i]
            mix_p, s_p = _gla_mixer(*proj_p, zeros, gain, b_p, t_p, t_p, GLA_CHUNK, GLA_TILE, 1)
            mix_s, s_s = _gla_mixer(*proj_s, past[3][i], gain, b_s, t_s, SAMPLE_PAD, SAMPLE_PAD,
                                    SAMPLE_PAD, SAMPLE_SEQS)
            w_mix = [(P['od_w_out'], i)]
            new_p['s'].append(s_p)
            new_s['s'].append(s_s)
        x_p, x_s = _ffn(x_p, x_s, P['norm_ffn2'][layer], P['ffn2_w_in'], P['ffn2_w_out'], layer,
                        FFN_TILE, mix_p, mix_s, w_mix)
    stacked = lambda new: tuple(jnp.stack(new[name]) for name in 'kvus')
    return (x_p.reshape(b_p, t_p, D_MODEL), x_s.reshape(b_s, t_s, D_MODEL),
            *stacked(new_p), *stacked(new_s))


def kernel(x_prompt, x_sample, cache_k, cache_v, cache_conv, state_gla, rel_bias, norm_ffn1, ffn1_w_in, ffn1_w_out, norm_mix, norm_ffn2, ffn2_w_in, ffn2_w_out, ev_w_in, ev_q_gain, ev_k_gain, ev_conv_w, ev_conv_b, ev_conv_ln_g, ev_conv_ln_b, ev_w_out, od_w_in, od_gate_w_up, od_gate_b, od_o_gain, od_w_out):
    n_even = ev_w_in.shape[0]
    n_odd = od_w_in.shape[0]
    main = 2 * C_DK + 2 * C_DV
    head_ids = np.arange(MXU_DIM) // A_HEAD_DIM
    per = lambda n, f: [f(j) for j in range(n)]
    row = lambda z: z[None, :]
    P = {
        'norm_ffn1': per(DEPTH, lambda j: row(norm_ffn1[j])),
        'norm_mix': per(DEPTH, lambda j: row(norm_mix[j])),
        'norm_ffn2': per(DEPTH, lambda j: row(norm_ffn2[j])),
        'ffn1_w_in': ffn1_w_in, 'ffn1_w_out': ffn1_w_out,
        'ffn2_w_in': ffn2_w_in, 'ffn2_w_out': ffn2_w_out,
        'ev_w_in': ev_w_in, 'ev_w_out': ev_w_out, 'od_w_out': od_w_out,
        'od_w_in': od_w_in.transpose(0, 2, 1),
        'ev_q_gain': per(n_even, lambda j: row(jnp.tile(ev_q_gain[j], A_HEADS))),
        'ev_k_gain': per(n_even, lambda j: row(jnp.tile(ev_k_gain[j], A_HEADS))),
        'head_ones': jnp.asarray(head_ids[:, None] == head_ids[None, :], BF16),
        'ev_conv_w': per(n_even, lambda j: jnp.broadcast_to(
            ev_conv_w[j][:, None, :], (CONV_WIDTH, SUBLANES, CONV_CH))),
        'ev_conv_b': per(n_even, lambda j: row(ev_conv_b[j])),
        'ev_conv_ln_g': per(n_even, lambda j: row(ev_conv_ln_g[j])),
        'ev_conv_ln_b': per(n_even, lambda j: row(ev_conv_ln_b[j])),
        'od_w_low': per(n_odd, lambda j: jnp.pad(od_w_in[j, :, main:],
                                                 ((0, 0), (0, LANES - GATE_RANK))).astype(BF16)),
        'od_gate_w_up': per(n_odd, lambda j: jnp.pad(od_gate_w_up[j],
                                                     ((0, LANES - GATE_RANK), (0, 0))).astype(BF16)),
        'od_gate_b': per(n_odd, lambda j: row(od_gate_b[j])),
        'od_o_gain': per(n_odd, lambda j: row(od_o_gain[j])),
        'band_vectors': _band_vectors(rel_bias),
        'decode_tables': _decode_tables(rel_bias, x_sample.shape[1]),
    }
    P, x_prompt, x_sample = lax.optimization_barrier((P, x_prompt, x_sample))
    return _trunks(x_prompt, x_sample, (cache_k, cache_v, cache_conv, state_gla), P)
```

```python
import functools

import numpy as np
import jax
import jax.numpy as jnp
from jax import lax
from jax.experimental import pallas as pl
from jax.experimental.pallas import tpu as pltpu

F32 = jnp.float32
BF16 = jnp.bfloat16

D_MODEL = 1024
DEPTH = 2
A_HEADS = 8
A_HEAD_DIM = 64
A_WIDTH = A_HEADS * A_HEAD_DIM
DILATED_GROUPS = ((128, 1), (512, 4), (2048, 16))
MAX_WINDOW = 2048
N_BUCKETS = 32
CONV_WIDTH = 31
CONV_CH = 512
C_HEADS = 4
C_DK = 512
C_DV = 1024
C_DK_HEAD = 128
C_DV_HEAD = 256
GATE_RANK = 16
GATE_TAU = 16.0
D_FF = 2816
EPS = 1e-6
NEG_INF = -1e30
LOG2E = 1.4426950408889634

LANES = 128
SUBLANES = 8
MXU_DIM = 256
WIN_KEYS = 128
ATT_BLOCK = 2048
ATT_UNROLL = 16
ATT_PHASES = 4
VMEM_LIMIT = 60 * 1024 * 1024


def _params(*sem):
    return pltpu.CompilerParams(dimension_semantics=sem, vmem_limit_bytes=VMEM_LIMIT)


def _dot(a, b):
    return jnp.dot(a, b, preferred_element_type=F32)


def _dot_nt(a, b):
    return lax.dot_general(a, b, (((1,), (1,)), ((), ())), preferred_element_type=F32)


def _dot_tn(a, b):
    return lax.dot_general(a, b, (((0,), (0,)), ((), ())), preferred_element_type=F32)


def _rms_rows(x, g):
    y = x * lax.rsqrt(jnp.mean(x * x, axis=-1, keepdims=True) + EPS)
    return y * g


def _sigmoid(x):
    return 1.0 / (1.0 + jnp.exp(-x))


def _full(shape):
    return pl.BlockSpec(shape, lambda *_: (0,) * len(shape), pipeline_mode=pl.Buffered(1))


FF_CHUNK = 256
FF_DOWN_CHUNK = 1024


def _layer(w, layer):
    return pl.BlockSpec((None,) + w.shape[1:], lambda *_: (layer, 0, 0),
                        pipeline_mode=pl.Buffered(1))


def _rows_call(body, prompt_rows, sample_rows, consts, out_widths, tm, name, scratch=(), hbm=()):
    n_p, n_s = prompt_rows[0].shape[0], sample_rows[0].shape[0]
    steps = n_p // tm
    n_in, n_c, n_out = len(prompt_rows), len(consts) + len(hbm), len(out_widths)

    def kernel(*refs):
        p_in, s_in = refs[:n_in], refs[n_in:2 * n_in]
        c_refs = refs[2 * n_in:2 * n_in + n_c]
        outs = refs[2 * n_in + n_c:]
        p_out, s_out, scr = outs[:n_out], outs[n_out:2 * n_out], outs[2 * n_out:]
        step = pl.program_id(0)

        if hbm:
            @pl.when(step == 0)
            def _():
                body(p_in, c_refs, p_out, scr, tm, True)

        @pl.when(jnp.logical_and(step >= (1 if hbm else 0), step < steps))
        def _():
            body(p_in, c_refs, p_out, scr, tm, False)

        @pl.when(step == steps)
        def _():
            body(s_in, c_refs, s_out, scr, n_s, False)

    p_spec = lambda width: pl.BlockSpec((tm, width), lambda i: (jnp.minimum(i, steps - 1), 0))
    s_spec = lambda width: pl.BlockSpec((n_s, width), lambda i: (0, 0))
    res = pl.pallas_call(
        kernel,
        grid=(steps + 1,),
        in_specs=[p_spec(a.shape[1]) for a in prompt_rows] + [s_spec(a.shape[1]) for a in sample_rows]
                 + [_layer(*c) if isinstance(c, tuple) else _full(c.shape) for c in consts]
                 + [pl.BlockSpec(memory_space=pl.ANY) for _ in hbm],
        out_specs=[p_spec(w) for w in out_widths] + [s_spec(w) for w in out_widths],
        out_shape=[jax.ShapeDtypeStruct((n_p, w), F32) for w in out_widths]
                  + [jax.ShapeDtypeStruct((n_s, w), F32) for w in out_widths],
        scratch_shapes=list(scratch),
        compiler_params=_params("arbitrary"),
        name=name,
    )(*prompt_rows, *sample_rows, *[c[0] if isinstance(c, tuple) else c for c in consts], *hbm)
    return res[:n_out], res[n_out:]


def _ffn_body(layer, ins, consts, outs, scratch, rows, first):
    x_ref, m_refs = ins[0], ins[1:]
    w_refs, (g_ref, wi_hbm, wo_hbm) = consts[:-3], consts[-3:]
    (o_ref,), (act_ref, wi_ref, wo_ref, sems) = outs, scratch
    up_chunks = range(D_FF // FF_CHUNK)
    down_chunks = [(lo, min(lo + FF_DOWN_CHUNK, D_FF)) for lo in range(0, D_FF, FF_DOWN_CHUNK)]

    def up_copies(c):
        cols = [pl.ds(half * D_FF + c * FF_CHUNK, FF_CHUNK) for half in range(2)]
        return [pltpu.make_async_copy(wi_hbm.at[layer, :, cs], wi_ref.at[:, cs], sems.at[c])
                for cs in cols]

    def down_copy(j):
        rows_j = pl.ds(down_chunks[j][0], down_chunks[j][1] - down_chunks[j][0])
        return pltpu.make_async_copy(wo_hbm.at[layer, rows_j], wo_ref.at[rows_j],
                                     sems.at[len(up_chunks) + j])

    def on_first(copies, act):
        if first:
            for copy in copies():
                act(copy)

    every = lambda: ([cp for c in up_chunks for cp in up_copies(c)]
                     + [down_copy(j) for j in range(len(down_chunks))])
    on_first(every, lambda copy: copy.start())
    x = x_ref[...]
    for w_ref in w_refs:
        mixed = jnp.concatenate([m_ref[...] for m_ref in m_refs], axis=1)
        x = x + _dot(mixed, w_ref[...].astype(BF16))
    h = _rms_rows(x, g_ref[...]).astype(BF16)
    for c in up_chunks:
        on_first(lambda c=c: up_copies(c), lambda copy: copy.wait())
        lo = c * FF_CHUNK
        a = _dot(h, wi_ref[:, lo:lo + FF_CHUNK].astype(BF16))
        b = _dot(h, wi_ref[:, D_FF + lo:D_FF + lo + FF_CHUNK].astype(BF16))
        act_ref[0:rows, lo:lo + FF_CHUNK] = (a * _sigmoid(a) * b).astype(BF16)
    y = jnp.zeros_like(x)
    for j, (lo, hi) in enumerate(down_chunks):
        on_first(lambda j=j: [down_copy(j)], lambda copy: copy.wait())
        y = y + _dot(act_ref[0:rows, lo:hi], wo_ref[lo:hi, :].astype(BF16))
    o_ref[...] = x + 0.5 * y


def _ffn(x_p, x_s, g, wi, wo, layer, tm, ms_p=(), ms_s=(), ws=()):
    n_sems = D_FF // FF_CHUNK + -(-D_FF // FF_DOWN_CHUNK)
    scratch = [pltpu.VMEM((tm, D_FF), BF16), pltpu.VMEM(wi.shape[1:], F32),
               pltpu.VMEM(wo.shape[1:], F32), pltpu.SemaphoreType.DMA((n_sems,))]
    (y_p,), (y_s,) = _rows_call(functools.partial(_ffn_body, layer), [x_p, *ms_p], [x_s, *ms_s],
                                [*ws, g], [D_MODEL], tm, "ffn", scratch, hbm=[wi, wo])
    return y_p, y_s


def _head_norm(z, gain, bd):
    zz = z * z
    hi = zz.astype(BF16)
    lo = (zz - hi.astype(F32)).astype(BF16)
    width = bd.shape[0]
    ss = jnp.concatenate(
        [_dot(hi[:, c:c + width], bd) + _dot(lo[:, c:c + width], bd)
         for c in range(0, z.shape[1], width)], axis=1)
    return z * lax.rsqrt(ss * (1.0 / A_HEAD_DIM) + EPS) * gain


def _even_in_body(ins, consts, outs, scratch, rows, first):
    (x_ref,), (g_ref, w_ref, qg_ref, kg_ref, bd_ref), (q_ref, k_ref, v_ref, u_ref) = ins, consts, outs
    h = _rms_rows(x_ref[...], g_ref[...]).astype(BF16)
    proj = lambda lo, width: _dot(h, w_ref[:, lo:lo + width].astype(BF16))
    bd = bd_ref[...]
    q_ref[...] = _head_norm(proj(0, A_WIDTH), qg_ref[...], bd) * (A_HEAD_DIM ** -0.5 * LOG2E)
    k_ref[...] = _head_norm(proj(A_WIDTH, A_WIDTH), kg_ref[...], bd)
    v_ref[...] = proj(2 * A_WIDTH, A_WIDTH)
    gv = proj(3 * A_WIDTH, CONV_CH)
    gg = proj(3 * A_WIDTH + CONV_CH, CONV_CH)
    u_ref[...] = gv * _sigmoid(gg)


def _even_in(x_p, x_s, g, w, qg, kg, bd, tm):
    return _rows_call(_even_in_body, [x_p], [x_s], [g, w, qg, kg, bd], [A_WIDTH] * 4, tm, "even_in")


def _t5_bucket(dist):
    max_exact = N_BUCKETS // 2
    d = np.asarray(dist, dtype=np.int32)
    df = np.maximum(d, 1).astype(np.float32)
    large = max_exact + (np.log(df / max_exact) / np.log(MAX_WINDOW / max_exact)
                         * (N_BUCKETS - max_exact)).astype(np.int32)
    large = np.minimum(large, N_BUCKETS - 1)
    return np.where(d < max_exact, d, large).astype(np.int32)


def _select_bias(rel_bias, dist, valid):
    onehot = (_t5_bucket(dist)[None, :] == np.arange(N_BUCKETS)[:, None]) & valid[None, :]
    picked = jnp.einsum('bh,bc->hc', rel_bias, jnp.asarray(onehot, F32),
                        precision=lax.Precision.HIGHEST)
    return picked * LOG2E + jnp.asarray(np.where(valid, 0.0, NEG_INF), F32)[None, :]


def _band_vectors(rel_bias):
    c = np.arange(2 * WIN_KEYS)
    valid = c <= WIN_KEYS
    vecs = [_select_bias(rel_bias, np.where(valid, (WIN_KEYS - c) * dil, 0), valid)
            for _, dil in DILATED_GROUPS]
    return jnp.stack(vecs).reshape(len(DILATED_GROUPS), A_HEADS // 2, 2, 2 * WIN_KEYS)


def _attn_prompt_kernel(q_ref, kp_ref, kc_ref, vp_ref, vc_ref, vec_ref, o_ref,
                        knat, vnat, k4, v4, q4, og1, lg1, og4, lg4, tab_ref):
    blk = pl.program_id(2)
    n_groups = len(DILATED_GROUPS)
    nph = ATT_PHASES
    per = ATT_BLOCK // nph

    @pl.when(blk == 0)
    def _():
        col = lax.broadcasted_iota(jnp.int32, (WIN_KEYS, 2 * WIN_KEYS), 1)
        for g in range(n_groups):
            for hh in range(2):
                vec = jnp.broadcast_to(vec_ref[g, 0, hh:hh + 1, :], (WIN_KEYS, 2 * WIN_KEYS))
                band = pltpu.roll(vec, 0, 1, stride=1, stride_axis=0)
                rows = slice(hh * WIN_KEYS, (hh + 1) * WIN_KEYS)
                tab_ref[g, rows, :] = band
                tab_ref[n_groups + g, rows, :] = jnp.where(col >= WIN_KEYS, band, NEG_INF)

    knat[0:WIN_KEYS, :] = kp_ref[ATT_BLOCK - WIN_KEYS:, :]
    knat[WIN_KEYS:, :] = kc_ref[...]
    vnat[0:WIN_KEYS, :] = vp_ref[ATT_BLOCK - WIN_KEYS:, :]
    vnat[WIN_KEYS:, :] = vc_ref[...]
    for r in range(nph):
        phase = pl.ds(r, per, stride=nph)
        k4[r, 0:per, :] = kp_ref[phase, :]
        k4[r, per:, :] = kc_ref[phase, :]
        v4[r, 0:per, :] = vp_ref[phase, :]
        v4[r, per:, :] = vc_ref[phase, :]
        q4[r] = q_ref[phase, :]
    first = blk == 0
    lane = lax.broadcasted_iota(jnp.int32, (WIN_KEYS, LANES), 1)
    low = lane < A_HEAD_DIM
    ones = jnp.ones((2 * WIN_KEYS, LANES), BF16)

    def block(qs, kk, vv, tab):
        qs = qs.astype(BF16)
        zero = jnp.zeros_like(qs)
        qst = jnp.concatenate([jnp.where(low, qs, zero), jnp.where(low, zero, qs)], axis=0)
        s = _dot_nt(qst, kk.astype(BF16)) + tab
        mx = jnp.max(s, axis=-1, keepdims=True)
        p = jnp.exp2(s - mx).astype(BF16)
        r = _dot(p, jnp.concatenate([vv.astype(BF16), ones], axis=1))
        o2 = jnp.where(low, r[0:WIN_KEYS, 0:LANES], r[WIN_KEYS:, 0:LANES])
        l2 = jnp.where(low, r[0:WIN_KEYS, LANES:], r[WIN_KEYS:, LANES:])
        m2 = jnp.where(low, jnp.broadcast_to(mx[0:WIN_KEYS], (WIN_KEYS, LANES)),
                       jnp.broadcast_to(mx[WIN_KEYS:], (WIN_KEYS, LANES)))
        return o2 / l2, m2 + jnp.log(l2) * LOG2E

    def table(g, at_start):
        return tab_ref[jnp.where(jnp.logical_and(at_start, first), n_groups + g, g)]

    def body1(sub, carry):
        i0 = pl.multiple_of(sub * WIN_KEYS, WIN_KEYS)
        o, l = block(q_ref[pl.ds(i0, WIN_KEYS), :], knat[pl.ds(i0, 2 * WIN_KEYS), :],
                     vnat[pl.ds(i0, 2 * WIN_KEYS), :], table(0, sub == 0))
        og1[pl.ds(i0, WIN_KEYS), :] = o
        lg1[pl.ds(i0, WIN_KEYS), :] = l
        return carry

    def body2(pb, carry):
        sub = pb // nph
        r = pb - sub * nph
        i0 = pl.multiple_of(sub * WIN_KEYS, WIN_KEYS)
        keys = pl.ds(i0 + (per - WIN_KEYS), 2 * WIN_KEYS)
        o, l = block(q4[r, pl.ds(i0, WIN_KEYS), :], k4[r, keys, :], v4[r, keys, :],
                     table(1, sub == 0))
        og4[0, r, pl.ds(i0, WIN_KEYS), :] = o
        lg4[0, r, pl.ds(i0, WIN_KEYS), :] = l
        return carry

    def body3(pb, carry):
        a = pb // nph
        r = pb - a * nph
        rows = pl.ds(a, WIN_KEYS, stride=nph)
        keys = pl.ds(a, 2 * WIN_KEYS, stride=nph)
        o, l = block(q4[r, rows, :], k4[r, keys, :], v4[r, keys, :], table(2, True))
        og4[1, r, rows, :] = o
        lg4[1, r, rows, :] = l
        return carry

    n_blocks = ATT_BLOCK // WIN_KEYS
    for body in (body1, body2, body3):
        lax.fori_loop(0, n_blocks, body, 0, unroll=ATT_UNROLL)

    for r in range(nph):
        phase = pl.ds(r, per, stride=nph)
        la, lb, lc = lg1[phase, :], lg4[0, r], lg4[1, r]
        mx = jnp.maximum(jnp.maximum(la, lb), lc)
        wa, wb, wc = jnp.exp2(la - mx), jnp.exp2(lb - mx), jnp.exp2(lc - mx)
        og1[phase, :] = (wa * og1[phase, :] + wb * og4[0, r] + wc * og4[1, r]) / (wa + wb + wc)
    o_ref[...] = og1[...].astype(o_ref.dtype)


def _attn_prompt(q, k, v, vecs, batch, seq):
    assert [d for _, d in DILATED_GROUPS] == [1, ATT_PHASES, ATT_PHASES ** 2]
    nb = seq // ATT_BLOCK
    n_groups = len(DILATED_GROUPS)
    per = ATT_BLOCK // ATT_PHASES
    cur = lambda b, p, t: (b * nb + t, p)
    prev = lambda b, p, t: (b * nb + jnp.maximum(t - 1, 0), p)
    blk = lambda imap: pl.BlockSpec((ATT_BLOCK, LANES), imap)
    vmem = lambda *shape: pltpu.VMEM(shape, F32)
    return pl.pallas_call(
        _attn_prompt_kernel,
        grid=(batch, A_HEADS // 2, nb),
        in_specs=[blk(cur), blk(prev), blk(cur), blk(prev), blk(cur),
                  pl.BlockSpec((n_groups, 1, 2, 2 * WIN_KEYS), lambda b, p, t: (0, p, 0, 0))],
        out_specs=blk(cur),
        out_shape=jax.ShapeDtypeStruct((batch * seq, A_WIDTH), BF16),
        scratch_shapes=[vmem(WIN_KEYS + ATT_BLOCK, LANES), vmem(WIN_KEYS + ATT_BLOCK, LANES),
                        vmem(ATT_PHASES, 2 * per, LANES), vmem(ATT_PHASES, 2 * per, LANES),
                        vmem(ATT_PHASES, per, LANES),
                        vmem(ATT_BLOCK, LANES), vmem(ATT_BLOCK, LANES),
                        vmem(2, ATT_PHASES, per, LANES), vmem(2, ATT_PHASES, per, LANES),
                        vmem(2 * n_groups, 2 * WIN_KEYS, 2 * WIN_KEYS)],
        compiler_params=_params("arbitrary", "arbitrary", "arbitrary"),
        name="attn_prompt",
    )(q, k, k, v, v, vecs)


def _decode_tables(rel_bias, n_new):
    cols = MAX_WINDOW + n_new
    c = np.arange(cols)
    dist = MAX_WINDOW - c
    cnt = np.zeros(c.shape, np.float32)
    for window, dil in DILATED_GROUPS:
        cnt += ((dist >= 0) & (dist <= window) & (dist % dil == 0)).astype(np.float32)
    vec = _select_bias(rel_bias, np.clip(dist, 0, MAX_WINDOW), cnt > 0)
    vec = vec + jnp.asarray(np.log2(np.maximum(cnt, 1.0)), F32)[None, :]
    rows = jnp.stack([jnp.pad(vec[:, :cols - i], ((0, 0), (i, 0)), constant_values=NEG_INF)
                      for i in range(n_new)], axis=1)
    rows = rows.reshape(A_HEADS * n_new, cols)
    return rows[:, :MAX_WINDOW], rows[:, MAX_WINDOW:]


def _attn_sample_kernel(q_ref, kn_ref, vn_ref, kc_ref, vc_ref, tc_ref, tn_ref,
                        o_ref, ko_ref, vo_ref):
    n_new = q_ref.shape[1]
    n_buf = kc_ref.shape[2]
    rows = A_HEADS * n_new
    kn = kn_ref[0]
    vn = vn_ref[0]
    kc = kc_ref[0]
    vc = vc_ref[0]

    lane = lax.broadcasted_iota(jnp.int32, (A_WIDTH, LANES), 1)
    for new, old, out_ref in ((kn, kc, ko_ref), (vn, vc, vo_ref)):
        shifted = pltpu.roll(old, n_buf - n_new, 1)
        tail = jnp.concatenate([jnp.zeros((LANES - n_new, A_WIDTH), F32), new], axis=0).T
        out_ref[0, :, 0:n_buf - LANES] = shifted[:, 0:n_buf - LANES]
        out_ref[0, :, n_buf - LANES:] = jnp.where(lane >= LANES - n_new, tail,
                                                  shifted[:, n_buf - LANES:])

    q = q_ref[0]
    row_head = lax.broadcasted_iota(jnp.int32, (A_HEADS, n_new, A_WIDTH), 0).reshape(rows, A_WIDTH)
    col = lax.broadcasted_iota(jnp.int32, (rows, A_WIDTH), 1)
    own = jnp.logical_and(col >= row_head * A_HEAD_DIM, col < (row_head + 1) * A_HEAD_DIM)
    qblk = jnp.where(own, jnp.concatenate([q] * A_HEADS, axis=0), 0.0).astype(BF16)
    s_c = _dot(qblk, kc.astype(BF16)) + tc_ref[...]
    s_n = _dot_nt(qblk, kn.astype(BF16)) + tn_ref[...]
    mx = jnp.maximum(jnp.max(s_c, axis=-1, keepdims=True), jnp.max(s_n, axis=-1, keepdims=True))
    p_c = jnp.exp2(s_c - mx)
    p_n = jnp.exp2(s_n - mx)
    den = jnp.sum(p_c, axis=-1, keepdims=True) + jnp.sum(p_n, axis=-1, keepdims=True)
    acc = _dot_nt(p_c.astype(BF16), vc.astype(BF16)) + _dot(p_n.astype(BF16), vn.astype(BF16))
    acc = jnp.where(own, acc / den, 0.0)
    out = acc[0:n_new]
    for h in range(1, A_HEADS):
        out = out + acc[h * n_new:(h + 1) * n_new]
    o_ref[0] = out.astype(o_ref.dtype)


def _attn_sample(q, k_new, v_new, cache_k, cache_v, layer, tables):
    b, n_new, _ = q.shape
    n_buf = cache_k.shape[2]
    new = pl.BlockSpec((1, n_new, A_WIDTH), lambda i: (i, 0, 0))
    buf = pl.BlockSpec((1, A_WIDTH, n_buf), lambda i: (i, 0, 0))
    past = pl.BlockSpec((1, A_WIDTH, n_buf), lambda i: (layer * b + i, 0, 0))
    return pl.pallas_call(
        _attn_sample_kernel,
        grid=(b,),
        in_specs=[new, new, new, past, past, _full(tables[0].shape), _full(tables[1].shape)],
        out_specs=[new, buf, buf],
        out_shape=[jax.ShapeDtypeStruct((b, n_new, A_WIDTH), BF16),
                   jax.ShapeDtypeStruct((b, A_WIDTH, n_buf), F32),
                   jax.ShapeDtypeStruct((b, A_WIDTH, n_buf), F32)],
        compiler_params=_params("parallel"),
        name="attn_sample",
    )(q, k_new, v_new, cache_k, cache_v, *tables)


CONV_PAD = 32
CONV_ROWS = 32
CONV_UNROLL = 2


def _conv_kernel(tc, u_ref, up_ref, hist_ref, w_ref, b_ref, g_ref, beta_ref, o_ref, win, stage):
    t = pl.program_id(1)
    n_seq = u_ref.shape[0]
    n_slab = CONV_CH // LANES
    slab = lambda c: slice(c * LANES, (c + 1) * LANES)
    off = CONV_PAD - (CONV_WIDTH - 1)
    rc = min(CONV_ROWS, tc)
    half = rc // 2

    def tap(k, c):
        w = w_ref[k, :, slab(c)]
        if half < SUBLANES:
            return w[0:half]
        return jnp.concatenate([w] * (half // SUBLANES), axis=0)

    for s in range(n_seq):
        first = s * n_slab
        for c in range(n_slab):
            win[first + c, CONV_PAD:CONV_PAD + tc, :] = u_ref[s, :, slab(c)]

        @pl.when(t == 0)
        def _():
            for c in range(n_slab):
                win[first + c, 0:CONV_PAD, :] = hist_ref[s, :, slab(c)]

        @pl.when(t > 0)
        def _():
            for c in range(n_slab):
                win[first + c, 0:CONV_PAD, :] = up_ref[s, :, slab(c)]

        def body(j, carry):
            r0 = j * rc
            for c in range(n_slab):
                for par in range(2):
                    acc = jnp.zeros((half, LANES), F32) + b_ref[:, slab(c)]
                    for k in range(CONV_WIDTH):
                        rows = pl.ds(r0 + off + k + par, half, stride=2)
                        acc = acc + win[first + c, rows, :] * tap(k, c)
                    stage[first + c, pl.ds(r0 + par, half, stride=2), :] = acc
            return carry

        if tc == rc:
            body(0, 0)
        else:
            lax.fori_loop(0, tc // rc, body, 0, unroll=CONV_UNROLL)
        y = jnp.concatenate([stage[first + c] for c in range(n_slab)], axis=1)
        xc = y - jnp.mean(y, axis=-1, keepdims=True)
        y = xc * lax.rsqrt(jnp.mean(xc * xc, axis=-1, keepdims=True) + EPS)
        y = y * g_ref[...] + beta_ref[...]
        o_ref[s] = (y * _sigmoid(y)).astype(o_ref.dtype)


def _conv(u, hist, w, b, g, beta, tc, seqs):
    bsz, t, _ = u.shape
    per = tc // CONV_PAD
    if t >= CONV_PAD:
        prev = pl.BlockSpec((seqs, CONV_PAD, CONV_CH),
                            lambda i, j: (i, jnp.maximum(j * per - 1, 0), 0))
        u_prev = u
    else:
        prev = pl.BlockSpec((seqs, CONV_PAD, CONV_CH), lambda i, j: (i, 0, 0))
        u_prev = hist
    n_slab = seqs * CONV_CH // LANES
    return pl.pallas_call(
        functools.partial(_conv_kernel, tc),
        grid=(bsz // seqs, t // tc),
        in_specs=[pl.BlockSpec((seqs, tc, CONV_CH), lambda i, j: (i, j, 0)),
                  prev,
                  pl.BlockSpec((seqs, CONV_PAD, CONV_CH), lambda i, j: (i, 0, 0)),
                  _full(w.shape), _full((1, CONV_CH)), _full((1, CONV_CH)),
                  _full((1, CONV_CH))],
        out_specs=pl.BlockSpec((seqs, tc, CONV_CH), lambda i, j: (i, j, 0)),
        out_shape=jax.ShapeDtypeStruct((bsz, t, CONV_CH), BF16),
        scratch_shapes=[pltpu.VMEM((n_slab, CONV_PAD + tc, LANES), F32),
                        pltpu.VMEM((n_slab, tc, LANES), F32)],
        compiler_params=_params("parallel", "arbitrary"),
        name="conv",
    )(u, u_prev, hist, w, b, g, beta)


def _gla_in_body(ins, consts, outs, scratch, rows, first):
    (x_ref,), (g_ref, w_ref, wl_ref, wu_ref, bu_ref) = ins, consts
    q_ref, k_ref, v_ref, r_ref, la_ref = outs
    h = _rms_rows(x_ref[...], g_ref[...]).astype(BF16)
    proj = lambda lo, width: _dot_nt(h, w_ref[lo:lo + width, :].astype(BF16))
    q_ref[...] = proj(0, C_DK) * (C_DK_HEAD ** -0.5)
    k_ref[...] = proj(C_DK, C_DK)
    v_ref[...] = proj(2 * C_DK, C_DV)
    r = proj(2 * C_DK + C_DV, C_DV)
    r_ref[...] = r * _sigmoid(r)
    low = _dot(h, wl_ref[...]).astype(BF16)
    z = _dot(low, wu_ref[...]) + bu_ref[...]
    log_sig = jnp.minimum(z, 0.0) - jnp.log1p(jnp.exp(-jnp.abs(z)))
    la_ref[...] = log_sig * (1.0 / GATE_TAU)


def _gla_in(x_p, x_s, g, w, wl, wu, bu, tm):
    return _rows_call(_gla_in_body, [x_p], [x_s], [g, w, wl, wu, bu],
                      [C_DK, C_DK, C_DV, C_DV, C_DK], tm, "gla_in")


def _gla_kernel(chunk, n_chunks, q_ref, k_ref, v_ref, r_ref, la_ref, s0_ref, gain_ref,
                o_ref, s_ref, qin_s, kin_s, x1_s, x2_s, qst_s, kst_s, dec_s):
    n_seq = q_ref.shape[0]
    @pl.when(pl.program_id(1) == 0)
    def _():
        s_ref[...] = s0_ref[...]

    half, quarter = chunk // 2, chunk // 4
    quarter_of = lambda i: sum((i >= j * quarter).astype(jnp.int32) for j in range(1, 4))
    ri = lax.broadcasted_iota(jnp.int32, (chunk, chunk), 0)
    ci = lax.broadcasted_iota(jnp.int32, (chunk, chunk), 1)
    causal = ci <= ri
    tri = jnp.where(causal, 1.0, 0.0).astype(BF16)
    rq, cq = quarter_of(ri), quarter_of(ci)
    same_quarter = jnp.logical_and(causal, rq == cq)
    cross_half = jnp.logical_and(ri >= half, ci < half)
    cross_quarter = jnp.logical_and(rq == cq + 1, (ri >= half) == (ci >= half))
    rr = lax.broadcasted_iota(jnp.int32, (chunk, C_DK), 0)
    rrq = quarter_of(rr)
    in_first = rr < half
    key_side_2 = jnp.logical_or(rrq == 0, rrq == 2)
    gain = gain_ref[...]

    def prepare(g, c):
        rows = slice(c * chunk, (c + 1) * chunk)
        la = la_ref[g, rows, :]
        la_hi = la.astype(BF16)
        la_lo = (la - la_hi.astype(F32)).astype(BF16)
        cum = _dot(tri, la_hi) + _dot(tri, la_lo)
        row = lambda i: cum[i:i + 1, :]
        last = row(chunk - 1)
        mids = [row(j * quarter + quarter // 2 - 1) for j in range(4)]
        mid = jnp.where(rrq == 0, mids[0], jnp.where(rrq == 1, mids[1],
                        jnp.where(rrq == 2, mids[2], mids[3])))
        q = q_ref[g, rows, :]
        k = k_ref[g, rows, :]
        grow = jnp.exp(cum - mid)
        qin_s[g, rows, :] = (q * grow).astype(BF16)
        kin_s[g, rows, :] = (k / grow).astype(BF16)

        def across(edge, key_side):
            gap = cum - edge
            return (jnp.where(key_side, k, q) * jnp.exp(jnp.where(key_side, -gap, gap))).astype(BF16)

        x1_s[g, rows, :] = across(row(half - 1), in_first)
        x2_s[g, rows, :] = across(jnp.where(in_first, row(quarter - 1), row(half + quarter - 1)),
                               key_side_2)
        qst_s[g, rows, :] = (q * jnp.exp(cum)).astype(BF16)
        kst_s[g, rows, :] = (k * jnp.exp(last - cum)).astype(BF16)
        dec_s[g, c] = jnp.broadcast_to(jnp.exp(last), (LANES, C_DK)).T

    def advance(g, c):
        rows = slice(c * chunk, (c + 1) * chunk)
        for h in range(C_HEADS):
            ks = slice(h * C_DK_HEAD, (h + 1) * C_DK_HEAD)
            vs = slice(h * C_DV_HEAD, (h + 1) * C_DV_HEAD)
            vh = v_ref[g, rows, vs].astype(BF16)
            x1, x2 = x1_s[g, rows, ks], x2_s[g, rows, ks]
            att = jnp.where(same_quarter, _dot_nt(qin_s[g, rows, ks], kin_s[g, rows, ks]),
                            jnp.where(cross_quarter, _dot_nt(x2, x2),
                                      jnp.where(cross_half, _dot_nt(x1, x1), 0.0)))
            s = s_ref[g, h]
            o = _dot(jnp.concatenate([att.astype(BF16), qst_s[g, rows, ks]], axis=1),
                     jnp.concatenate([vh, s.astype(BF16)], axis=0))
            decay = dec_s[g, c, ks, :]
            s_ref[g, h] = (s * jnp.concatenate([decay] * (C_DV_HEAD // LANES), axis=1)
                           + _dot_tn(kst_s[g, rows, ks], vh))
            y = o * lax.rsqrt(jnp.mean(o * o, axis=-1, keepdims=True) + EPS) * gain
            o_ref[g, rows, vs] = (y * r_ref[g, rows, vs]).astype(o_ref.dtype)

    for g in range(n_seq):
        prepare(g, 0)
    for c in range(n_chunks):
        for g in range(n_seq):
            if c + 1 < n_chunks:
                prepare(g, c + 1)
            advance(g, c)


def _gla(q, k, v, r, la, s0, gain, chunk, tb, seqs):
    b, t, _ = q.shape
    seq = lambda width: pl.BlockSpec((seqs, tb, width), lambda i, j: (i, j, 0))
    state = pl.BlockSpec((seqs, C_HEADS, C_DK_HEAD, C_DV_HEAD), lambda i, j: (i, 0, 0, 0))
    return pl.pallas_call(
        functools.partial(_gla_kernel, chunk, tb // chunk),
        grid=(b // seqs, t // tb),
        in_specs=[seq(C_DK), seq(C_DK), seq(C_DV), seq(C_DV), seq(C_DK), state,
                  _full((1, C_DV_HEAD))],
        out_specs=[seq(C_DV), state],
        out_shape=[jax.ShapeDtypeStruct((b, t, C_DV), BF16),
                   jax.ShapeDtypeStruct(s0.shape, F32)],
        scratch_shapes=[pltpu.VMEM((seqs, tb, C_DK), BF16)] * 6
                       + [pltpu.VMEM((seqs, tb // chunk, C_DK, LANES), F32)],
        compiler_params=_params("parallel", "arbitrary"),
        name="gla",
    )(q, k, v, r, la, s0, gain)


GLA_CHUNK = 128
SAMPLE_PAD = 16
SAMPLE_SEQS = 8


ROW_TILE = 512
FFN_TILE = ROW_TILE
EVEN_TILE = 1024
CONV_TILE = 512
GLA_TILE = 1024


def _even_mixer_prompt(q, k, v, u, P, i, bsz, t):
    n = bsz * t
    a = _attn_prompt(q, k, v, P['band_vectors'], bsz, t)
    keep = min(MAX_WINDOW, t)
    tail = lambda z: z.reshape(bsz, t, A_WIDTH)[:, t - keep:].reshape(bsz, keep, A_HEADS, A_HEAD_DIM)
    u3 = u.reshape(bsz, t, CONV_CH)
    hist = jnp.zeros((bsz, CONV_PAD, CONV_CH), F32)
    c = _conv(u3, hist, P['ev_conv_w'][i], P['ev_conv_b'][i], P['ev_conv_ln_g'][i],
              P['ev_conv_ln_b'][i], CONV_TILE, 1).reshape(n, CONV_CH)
    return [a, c], (tail(k), tail(v), u3[:, t - (CONV_WIDTH - 1):])


def _even_mixer_sample(q, k, v, u, past, P, i, bsz, t):
    n = bsz * t
    n_buf = past[0].shape[2]
    assert n_buf == MAX_WINDOW, "the sample kernel expects a full window buffer"
    major = lambda z: z.transpose(0, 1, 3, 4, 2).reshape(z.shape[0] * bsz, A_WIDTH, n_buf)
    minor = lambda z: z.reshape(bsz, A_HEADS, A_HEAD_DIM, n_buf).transpose(0, 3, 1, 2)
    a, new_k, new_v = _attn_sample(
        q.reshape(bsz, t, A_WIDTH), k.reshape(bsz, t, A_WIDTH), v.reshape(bsz, t, A_WIDTH),
        major(past[0]), major(past[1]), i, P['decode_tables'])
    u3 = u.reshape(bsz, t, CONV_CH)
    hist = jnp.pad(past[2][i], ((0, 0), (CONV_PAD - (CONV_WIDTH - 1), 0), (0, 0)))
    c = _conv(u3, hist, P['ev_conv_w'][i], P['ev_conv_b'][i], P['ev_conv_ln_g'][i],
              P['ev_conv_ln_b'][i], t, SAMPLE_SEQS).reshape(n, CONV_CH)
    new_u = jnp.concatenate([past[2][i], u3], axis=1)[:, -(CONV_WIDTH - 1):]
    return [a.reshape(n, A_WIDTH), c], (minor(new_k), minor(new_v), new_u)


def _gla_mixer(q, k, v, r, la, s0, gain, bsz, t, t_pad, chunk, tile, seqs):
    seq = lambda z: jnp.pad(z.reshape(bsz, t, -1), ((0, 0), (0, t_pad - t), (0, 0)))
    o, s = _gla(seq(q), seq(k), seq(v), seq(r), seq(la), s0, gain, chunk, tile, seqs)
    return [o[:, :t].reshape(bsz * t, C_DV)], s


def _trunks(x_p, x_s, past, P):
    (b_p, t_p, _), (b_s, t_s, _) = x_p.shape, x_s.shape
    x_p = x_p.reshape(b_p * t_p, D_MODEL)
    x_s = x_s.reshape(b_s * t_s, D_MODEL)
    new_p = {'k': [], 'v': [], 'u': [], 's': []}
    new_s = {'k': [], 'v': [], 'u': [], 's': []}
    for layer in range(DEPTH):
        i = layer // 2
        x_p, x_s = _ffn(x_p, x_s, P['norm_ffn1'][layer], P['ffn1_w_in'], P['ffn1_w_out'], layer,
                        FFN_TILE)
        if layer % 2 == 0:
            proj_p, proj_s = _even_in(x_p, x_s, P['norm_mix'][layer], (P['ev_w_in'], i),
                                      P['ev_q_gain'][i], P['ev_k_gain'][i], P['head_ones'], EVEN_TILE)
            mix_p, kvu_p = _even_mixer_prompt(*proj_p, P, i, b_p, t_p)
            mix_s, kvu_s = _even_mixer_sample(*proj_s, past, P, i, b_s, t_s)
            w_mix = [(P['ev_w_out'], i)]
            for new, kvu in ((new_p, kvu_p), (new_s, kvu_s)):
                for name, z in zip('kvu', kvu):
                    new[name].append(z)
        else:
            proj_p, proj_s = _gla_in(x_p, x_s, P['norm_mix'][layer], (P['od_w_in'], i),
                                     P['od_w_low'][i], P['od_gate_w_up'][i], P['od_gate_b'][i],
                                     ROW_TILE)
            zeros = jnp.zeros((b_p, C_HEADS, C_DK_HEAD, C_DV_HEAD), F32)
            gain = P['od_o_gain'][i]
            mix_p, s_p = _gla_mixer(*proj_p, zeros, gain, b_p, t_p, t_p, GLA_CHUNK, GLA_TILE, 1)
            mix_s, s_s = _gla_mixer(*proj_s, past[3][i], gain, b_s, t_s, SAMPLE_PAD, SAMPLE_PAD,
                                    SAMPLE_PAD, SAMPLE_SEQS)
            w_mix = [(P['od_w_out'], i)]
            new_p['s'].append(s_p)
            new_s['s'].append(s_s)
        x_p, x_s = _ffn(x_p, x_s, P['norm_ffn2'][layer], P['ffn2_w_in'], P['ffn2_w_out'], layer,
                        FFN_TILE, mix_p, mix_s, w_mix)
    stacked = lambda new: tuple(jnp.stack(new[name]) for name in 'kvus')
    return (x_p.reshape(b_p, t_p, D_MODEL), x_s.reshape(b_s, t_s, D_MODEL),
            *stacked(new_p), *stacked(new_s))


def kernel(x_prompt, x_sample, cache_k, cache_v, cache_conv, state_gla, rel_bias, norm_ffn1, ffn1_w_in, ffn1_w_out, norm_mix, norm_ffn2, ffn2_w_in, ffn2_w_out, ev_w_in, ev_q_gain, ev_k_gain, ev_conv_w, ev_conv_b, ev_conv_ln_g, ev_conv_ln_b, ev_w_out, od_w_in, od_gate_w_up, od_gate_b, od_o_gain, od_w_out):
    n_even = ev_w_in.shape[0]
    n_odd = od_w_in.shape[0]
    main = 2 * C_DK + 2 * C_DV
    head_ids = np.arange(MXU_DIM) // A_HEAD_DIM
    per = lambda n, f: [f(j) for j in range(n)]
    row = lambda z: z[None, :]
    P = {
        'norm_ffn1': per(DEPTH, lambda j: row(norm_ffn1[j])),
        'norm_mix': per(DEPTH, lambda j: row(norm_mix[j])),
        'norm_ffn2': per(DEPTH, lambda j: row(norm_ffn2[j])),
        'ffn1_w_in': ffn1_w_in, 'ffn1_w_out': ffn1_w_out,
        'ffn2_w_in': ffn2_w_in, 'ffn2_w_out': ffn2_w_out,
        'ev_w_in': ev_w_in, 'ev_w_out': ev_w_out, 'od_w_out': od_w_out,
        'od_w_in': od_w_in.transpose(0, 2, 1),
        'ev_q_gain': per(n_even, lambda j: row(jnp.tile(ev_q_gain[j], A_HEADS))),
        'ev_k_gain': per(n_even, lambda j: row(jnp.tile(ev_k_gain[j], A_HEADS))),
        'head_ones': jnp.asarray(head_ids[:, None] == head_ids[None, :], BF16),
        'ev_conv_w': per(n_even, lambda j: jnp.broadcast_to(
            ev_conv_w[j][:, None, :], (CONV_WIDTH, SUBLANES, CONV_CH))),
        'ev_conv_b': per(n_even, lambda j: row(ev_conv_b[j])),
        'ev_conv_ln_g': per(n_even, lambda j: row(ev_conv_ln_g[j])),
        'ev_conv_ln_b': per(n_even, lambda j: row(ev_conv_ln_b[j])),
        'od_w_low': per(n_odd, lambda j: jnp.pad(od_w_in[j, :, main:],
                                                 ((0, 0), (0, LANES - GATE_RANK))).astype(BF16)),
        'od_gate_w_up': per(n_odd, lambda j: jnp.pad(od_gate_w_up[j],
                                                     ((0, LANES - GATE_RANK), (0, 0))).astype(BF16)),
        'od_gate_b': per(n_odd, lambda j: row(od_gate_b[j])),
        'od_o_gain': per(n_odd, lambda j: row(od_o_gain[j])),
        'band_vectors': _band_vectors(rel_bias),
        'decode_tables': _decode_tables(rel_bias, x_sample.shape[1]),
    }
    P, x_prompt, x_sample = lax.optimization_barrier((P, x_prompt, x_sample))
    return _trunks(x_prompt, x_sample, (cache_k, cache_v, cache_conv, state_gla), P)
```
